```python
import math
import jax, jax.numpy as jnp
from jax import lax
import numpy as np

D_MODEL = 1024
BATCH = 8
SEQ = 4096
DEPTH = 4

CHUNK = 64
N_MIXERS = 2
SSM_GROUP = 16
SSM_GROUPS = D_MODEL // SSM_GROUP
SSM_STATE = 64
DT_MIN = 1e-3
DT_MAX = 1e-1
POOL_WINDOWS = (2, 4, 8, 16)
POOL_GROUP = D_MODEL // len(POOL_WINDOWS)
D_FF = ((8 * D_MODEL // 3 + 127) // 128) * 128
N_EXPERTS = 8
TOP_K = 2
D_FF_EXPERT = 7 * D_MODEL // 2
N_EVEN = (DEPTH + 1) // 2
N_ODD = DEPTH // 2
EPS = 1e-6

kernel_name = 'hybrid_s5_pool_moe_adaln'


def _rmsnorm(x, g):
    xf = x.astype(jnp.float32)
    y = xf * lax.rsqrt(jnp.mean(xf * xf, axis=-1, keepdims=True) + EPS)
    return (y * g.astype(jnp.float32)).astype(x.dtype)


def _modulate(h, shift, scale):
    return h * (1 + scale[:, None, :]) + shift[:, None, :]


def _scan_combine(e1, e2):
    a1r, a1i, b1r, b1i = e1
    a2r, a2i, b2r, b2i = e2
    return (a2r * a1r - a2i * a1i,
            a2r * a1i + a2i * a1r,
            a2r * b1r - a2i * b1i + b2r,
            a2r * b1i + a2i * b1r + b2i)


def _s5_mixer(h, w_in, log_dt, lam_re, lam_im, b_re, b_im, c_re, c_im, d_skip, w_glu, w_out):
    f32 = jnp.float32
    bsz, seq, _ = h.shape
    n_chunks = seq // CHUNK
    u = (h @ w_in).astype(f32).reshape(bsz, n_chunks, CHUNK, SSM_GROUPS, SSM_GROUP)
    dt = jnp.exp(log_dt.astype(f32))[:, None]
    lr = lam_re.astype(f32)
    li = lam_im.astype(f32)
    mag = jnp.exp(lr * dt)
    a_re = mag * jnp.cos(li * dt)
    a_im = mag * jnp.sin(li * dt)
    den = lr * lr + li * li
    nr = a_re - 1
    coef_re = (nr * lr + a_im * li) / den
    coef_im = (a_im * lr - nr * li) / den
    br = b_re.astype(f32)
    bi = b_im.astype(f32)
    bbar_re = coef_re[..., None] * br - coef_im[..., None] * bi
    bbar_im = coef_re[..., None] * bi + coef_im[..., None] * br
    steps = jnp.arange(1, CHUNK + 1, dtype=f32)[:, None, None]
    pmag = jnp.exp(lr * dt * steps)
    pow_re = pmag * jnp.cos(li * dt * steps)
    pow_im = pmag * jnp.sin(li * dt * steps)
    cr = c_re.astype(f32)
    ci = c_im.astype(f32)
    dd = d_skip.astype(f32)

    def chunk_step(carry, u_c):
        s_re, s_im = carry
        bu_re = jnp.einsum('bkgh,gph->bkgp', u_c, bbar_re)
        bu_im = jnp.einsum('bkgh,gph->bkgp', u_c, bbar_im)
        ar = jnp.broadcast_to(a_re, bu_re.shape)
        ai = jnp.broadcast_to(a_im, bu_re.shape)
        _, _, x_re, x_im = lax.associative_scan(_scan_combine, (ar, ai, bu_re, bu_im), axis=1)
        x_re = x_re + pow_re * s_re[:, None] - pow_im * s_im[:, None]
        x_im = x_im + pow_re * s_im[:, None] + pow_im * s_re[:, None]
        y_c = (jnp.einsum('bkgp,ghp->bkgh', x_re, cr)
               - jnp.einsum('bkgp,ghp->bkgh', x_im, ci)
               + dd * u_c)
        return (x_re[:, -1], x_im[:, -1]), y_c

    init = (jnp.zeros((bsz, SSM_GROUPS, SSM_STATE), f32),
            jnp.zeros((bsz, SSM_GROUPS, SSM_STATE), f32))
    _, ys = lax.scan(chunk_step, init, jnp.moveaxis(u, 1, 0))
    y = jnp.moveaxis(ys, 0, 1).reshape(bsz, seq, D_MODEL)
    z = jax.nn.gelu(y)
    z = z * jax.nn.sigmoid(z @ w_glu.astype(f32))
    return z.astype(h.dtype) @ w_out


def _pool_mixer(h, w_in, w_mix, scale, w_out):
    f32 = jnp.float32
    bsz, seq, _ = h.shape
    u = (h @ w_in).astype(f32)
    cs = jnp.cumsum(u, axis=1)
    t = jnp.arange(1, seq + 1, dtype=f32)[:, None]
    outs = []
    for g, w in enumerate(POOL_WINDOWS):
        sl = slice(g * POOL_GROUP, (g + 1) * POOL_GROUP)
        csg = cs[..., sl]
        lagged = jnp.pad(csg, ((0, 0), (w, 0), (0, 0)))[:, :seq]
        mean = (csg - lagged) / jnp.minimum(t, w)
        outs.append((mean - u[..., sl]) @ w_mix[g].astype(f32))
    z = jnp.concatenate(outs, axis=-1) * scale.astype(f32)
    return z.astype(h.dtype) @ w_out


def _swiglu(h, w1, w3, w2):
    return (jax.nn.silu(h @ w1) * (h @ w3)) @ w2


def _moe(h, router_w, router_b, w1, w3, w2):
    f32 = jnp.float32
    bsz, seq, d = h.shape
    tok = h.reshape(bsz * seq, d)
    logits = (tok @ router_w).astype(f32) + router_b.astype(f32)
    top_v, top_i = lax.top_k(logits, TOP_K)
    gates = jax.nn.softmax(top_v, axis=-1)
    weight = jnp.sum(jax.nn.one_hot(top_i, N_EXPERTS, dtype=f32) * gates[..., None], axis=1)
    out = jnp.zeros((bsz * seq, d), f32)
    for e in range(N_EXPERTS):
        out = out + weight[:, e:e + 1] * _swiglu(tok, w1[e], w3[e], w2[e]).astype(f32)
    return out.astype(h.dtype).reshape(bsz, seq, d)


def setup_inputs(seed: int = 0) -> dict:
    key = jax.random.key(seed)
    ks = jax.random.split(key, 29)
    f32 = jnp.float32
    D = D_MODEL
    G, P, H = SSM_GROUPS, SSM_STATE, SSM_GROUP

    def nrm(k, shape, s):
        return jax.random.normal(k, shape, f32) * s

    n_idx = jnp.arange(P, dtype=f32)
    return {
        'x': nrm(ks[0], (BATCH, SEQ, D), 1.0),
        'c': nrm(ks[1], (BATCH, D), 1.0),
        'ada_w': nrm(ks[2], (DEPTH, D, 6 * D), 0.5 * D ** -0.5),
        'ada_b': nrm(ks[3], (DEPTH, 6 * D), 0.02),
        'norm_g': 1.0 + nrm(ks[4], (DEPTH, 2, D), 0.02),
        'ssm_in': nrm(ks[5], (N_EVEN, D, D), D ** -0.5),
        'ssm_log_dt': jax.random.uniform(ks[6], (N_EVEN, G), f32, math.log(DT_MIN), math.log(DT_MAX)),
        'ssm_lam_re': -0.5 + nrm(ks[7], (N_EVEN, G, P), 0.01),
        'ssm_lam_im': math.pi * n_idx + nrm(ks[8], (N_EVEN, G, P), 0.01),
        'ssm_b_re': nrm(ks[9], (N_EVEN, G, P, H), (2 * H) ** -0.5),
        'ssm_b_im': nrm(ks[10], (N_EVEN, G, P, H), (2 * H) ** -0.5),
        'ssm_c_re': nrm(ks[11], (N_EVEN, G, H, P), (2 * P) ** -0.5),
        'ssm_c_im': nrm(ks[12], (N_EVEN, G, H, P), (2 * P) ** -0.5),
        'ssm_d': nrm(ks[13], (N_EVEN, G, H), 1.0),
        'ssm_glu': nrm(ks[14], (N_EVEN, D, D), D ** -0.5),
        'ssm_out': nrm(ks[15], (N_EVEN, D, D), D ** -0.5),
        'pool_in': nrm(ks[16], (N_ODD, D, D), D ** -0.5),
        'pool_mix': nrm(ks[17], (N_ODD, len(POOL_WINDOWS), POOL_GROUP, POOL_GROUP), POOL_GROUP ** -0.5),
        'pool_scale': 1.0 + nrm(ks[18], (N_ODD, D), 0.1),
        'pool_out': nrm(ks[19], (N_ODD, D, D), D ** -0.5),
        'ffn_w1': nrm(ks[20], (N_EVEN, D, D_FF), D ** -0.5),
        'ffn_w3': nrm(ks[21], (N_EVEN, D, D_FF), D ** -0.5),
        'ffn_w2': nrm(ks[22], (N_EVEN, D_FF, D), D_FF ** -0.5),
        'router_w': nrm(ks[23], (N_ODD, D, N_EXPERTS), D ** -0.5),
        'router_b': nrm(ks[24], (N_ODD, N_EXPERTS), 0.01),
        'moe_w1': nrm(ks[25], (N_ODD, N_EXPERTS, D, D_FF_EXPERT), D ** -0.5),
        'moe_w3': nrm(ks[26], (N_ODD, N_EXPERTS, D, D_FF_EXPERT), D ** -0.5),
        'moe_w2': nrm(ks[27], (N_ODD, N_EXPERTS, D_FF_EXPERT, D), D_FF_EXPERT ** -0.5),
        'final_g': 1.0 + nrm(ks[28], (D,), 0.02),
    }


def reference(x, c, ada_w, ada_b, norm_g, ssm_in, ssm_log_dt, ssm_lam_re, ssm_lam_im,
              ssm_b_re, ssm_b_im, ssm_c_re, ssm_c_im, ssm_d, ssm_glu, ssm_out,
              pool_in, pool_mix, pool_scale, pool_out, ffn_w1, ffn_w3, ffn_w2,
              router_w, router_b, moe_w1, moe_w3, moe_w2, final_g):
    cond = jax.nn.silu(c)
    for i in range(DEPTH):
        j = i // 2
        mod = cond @ ada_w[i] + ada_b[i]
        sh1, sc1, g1, sh2, sc2, g2 = jnp.split(mod, 6, axis=-1)
        h = _modulate(_rmsnorm(x, norm_g[i, 0]), sh1, sc1)
        if i % N_MIXERS == 0:
            y = _s5_mixer(h, ssm_in[j], ssm_log_dt[j], ssm_lam_re[j], ssm_lam_im[j],
                          ssm_b_re[j], ssm_b_im[j], ssm_c_re[j], ssm_c_im[j], ssm_d[j],
                          ssm_glu[j], ssm_out[j])
        else:
            y = _pool_mixer(h, pool_in[j], pool_mix[j], pool_scale[j], pool_out[j])
        x = x + g1[:, None, :] * y
        h = _modulate(_rmsnorm(x, norm_g[i, 1]), sh2, sc2)
        if i % 2 == 0:
            y = _swiglu(h, ffn_w1[j], ffn_w3[j], ffn_w2[j])
        else:
            y = _moe(h, router_w[j], router_b[j], moe_w1[j], moe_w3[j], moe_w2[j])
        x = x + g2[:, None, :] * y
    return _rmsnorm(x, final_g)
```

```python
import functools
import math

import jax
import jax.numpy as jnp
from jax import lax
from jax.experimental import pallas as pl
from jax.experimental.pallas import tpu as pltpu

F32 = jnp.float32
BF16 = jnp.bfloat16
EPS = 1e-6
POOL_WINDOWS = (2, 4, 8, 16)
POOL_HALO = 16
SSM_T = 16
TOP_K = 2
VMEM_LIMIT = 56 * 1024 * 1024


def _cparams(*sem):
    return pltpu.CompilerParams(dimension_semantics=sem, vmem_limit_bytes=VMEM_LIMIT)


def _normmod(x, g, shift, scale):
    ms = jnp.mean(x * x, axis=-1, keepdims=True)
    y = x * lax.rsqrt(ms + EPS)
    return (y * g) * (1.0 + scale) + shift


def _bdot(a, b):
    return jnp.dot(a, b, preferred_element_type=F32)


def _ada_kernel(c_ref, w_ref, b_ref, o_ref):
    c = c_ref[...]
    cond = c * jax.nn.sigmoid(c)
    o_ref[0] = jnp.dot(cond, w_ref[0], precision=lax.Precision.HIGHEST,
                       preferred_element_type=F32) + b_ref[0]


def _ada(c, ada_w, ada_b):
    depth, d, d6 = ada_w.shape
    bsz = c.shape[0]
    tn = d6 // 4
    return pl.pallas_call(
        _ada_kernel,
        grid=(depth, d6 // tn),
        in_specs=[pl.BlockSpec((bsz, d), lambda l, j: (0, 0)),
                  pl.BlockSpec((1, d, tn), lambda l, j: (l, 0, j)),
                  pl.BlockSpec((1, 1, tn), lambda l, j: (l, 0, j))],
        out_specs=pl.BlockSpec((1, bsz, tn), lambda l, j: (l, 0, j)),
        out_shape=jax.ShapeDtypeStruct((depth, bsz, d6), F32),
        compiler_params=_cparams("arbitrary", "arbitrary"),
        name="ada",
    )(c, ada_w, ada_b.reshape(depth, 1, d6))


def _in_kernel(x_ref, mod_ref, g_ref, w_ref, o_ref):
    m = mod_ref[0]
    h = _normmod(x_ref[...], g_ref[...], m[0:1], m[1:2])
    o_ref[...] = _bdot(h.astype(BF16), w_ref[...]).astype(o_ref.dtype)


def _in_proj(x, mod, g, w, seq, tm, out_dtype):
    n, d = x.shape
    tps = seq // tm
    return pl.pallas_call(
        _in_kernel,
        grid=(n // tm,),
        in_specs=[pl.BlockSpec((tm, d), lambda i: (i, 0)),
                  pl.BlockSpec((1, 6, d), lambda i: (i // tps, 0, 0)),
                  pl.BlockSpec((1, d), lambda i: (0, 0)),
                  pl.BlockSpec((d, w.shape[1]), lambda i: (0, 0))],
        out_specs=pl.BlockSpec((tm, w.shape[1]), lambda i: (i, 0)),
        out_shape=jax.ShapeDtypeStruct((n, w.shape[1]), out_dtype),
        compiler_params=_cparams("arbitrary"),
        name="in_proj",
    )(x, mod, g, w)


def _ssm_operators(log_dt, lam_re, lam_im, b_re, b_im, c_re, c_im, d_skip, t):
    hp = lax.Precision.HIGHEST
    g_, p_, h_ = b_re.shape
    dt = jnp.exp(log_dt.astype(F32))[:, None]
    lr = lam_re.astype(F32)
    li = lam_im.astype(F32)
    mag = jnp.exp(lr * dt)
    a_re = mag * jnp.cos(li * dt)
    a_im = mag * jnp.sin(li * dt)
    den = lr * lr + li * li
    nr = a_re - 1
    coef_re = (nr * lr + a_im * li) / den
    coef_im = (a_im * lr - nr * li) / den
    br = b_re.astype(F32)
    bi = b_im.astype(F32)
    bbar_re = coef_re[..., None] * br - coef_im[..., None] * bi
    bbar_im = coef_re[..., None] * bi + coef_im[..., None] * br
    steps = jnp.arange(0, t + 1, dtype=F32)[:, None, None]
    pmag = jnp.exp(lr * dt * steps)
    pw_re = pmag * jnp.cos(li * dt * steps)
    pw_im = pmag * jnp.sin(li * dt * steps)
    cr = c_re.astype(F32)
    ci = c_im.astype(F32)
    er = cr[None] * pw_re[:, :, None, :] - ci[None] * pw_im[:, :, None, :]
    ei = cr[None] * pw_im[:, :, None, :] + ci[None] * pw_re[:, :, None, :]
    kk = (jnp.einsum('mgap,gph->mgah', er[:t], bbar_re, precision=hp)
          - jnp.einsum('mgap,gph->mgah', ei[:t], bbar_im, precision=hp))
    kk = jnp.transpose(kk, (1, 0, 3, 2))
    jj = jnp.arange(t)[:, None]
    kq = jnp.arange(t)[None, :]
    lag = kq - jj
    mm = kk[:, jnp.maximum(lag, 0)]
    mm = jnp.where((lag >= 0)[None, :, :, None, None], mm, 0.0)
    mm = jnp.transpose(mm, (0, 1, 3, 2, 4)).reshape(g_, t * h_, t * h_)
    rev_re = pw_re[:t][::-1]
    rev_im = pw_im[:t][::-1]
    w_re = rev_re[..., None] * bbar_re[None] - rev_im[..., None] * bbar_im[None]
    w_im = rev_re[..., None] * bbar_im[None] + rev_im[..., None] * bbar_re[None]
    w_re = jnp.transpose(w_re, (1, 0, 3, 2)).reshape(g_, t * h_, p_)
    w_im = jnp.transpose(w_im, (1, 0, 3, 2)).reshape(g_, t * h_, p_)
    w4 = jnp.concatenate([w_re, w_im, w_im, w_re], axis=-1)
    v_re = jnp.transpose(er[1:], (1, 3, 0, 2)).reshape(g_, p_, t * h_)
    v_im = -jnp.transpose(ei[1:], (1, 3, 0, 2)).reshape(g_, p_, t * h_)
    vv = jnp.concatenate([v_re, v_im], axis=1)
    at_re, at_im = pw_re[t], pw_im[t]
    c1 = jnp.concatenate([at_re, at_re], axis=-1)
    c2 = jnp.concatenate([-at_im, at_im], axis=-1)
    coef = jnp.zeros((g_, 8, 2 * p_), F32).at[:, 0].set(c1).at[:, 1].set(c2)
    dk = jnp.tile(d_skip.astype(F32)[:, None, :], (1, t, 1)).reshape(g_, 1, t * h_)
    return mm.astype(BF16), w4.astype(BF16), vv.astype(BF16), coef, dk


def _ssm_kernel(uf_ref, m_ref, w4_ref, v_ref, coef_ref, dk_ref, y_ref, z_scr, s_scr,
                *, gb, nchunks, bsz):
    p2 = s_scr.shape[-1]
    for gi in range(gb):
        z_scr[gi] = _bdot(uf_ref[gi], w4_ref[gi])
    c1 = [jnp.broadcast_to(coef_ref[gi, 0:1, :], (bsz, p2)) for gi in range(gb)]
    c2 = [jnp.broadcast_to(coef_ref[gi, 1:2, :], (bsz, p2)) for gi in range(gb)]

    def body(c, carry):
        r = pl.multiple_of(c * bsz, bsz)
        new = []
        for gi in range(gb):
            s1, s2 = carry[gi]
            s_scr[gi, pl.ds(r, bsz), :] = s1
            z = z_scr[gi, pl.ds(r, bsz), :]
            n1 = c1[gi] * s1 + c2[gi] * s2 + z[:, :p2]
            n2 = c1[gi] * s2 - c2[gi] * s1 + z[:, p2:]
            new.append((n1, n2))
        return tuple(new)

    zero = jnp.zeros((bsz, p2), F32)
    lax.fori_loop(0, nchunks, body, tuple((zero, zero) for _ in range(gb)))
    for gi in range(gb):
        u = uf_ref[gi]
        y = (_bdot(u, m_ref[gi]) + _bdot(s_scr[gi].astype(BF16), v_ref[gi])
             + dk_ref[gi] * u.astype(F32))
        y_ref[gi] = y.astype(y_ref.dtype)


def _ssm(uf, mm, w4, vv, coef, dk, bsz, gb):
    g_, r, th = uf.shape
    p2 = vv.shape[1]
    kern = functools.partial(_ssm_kernel, gb=gb, nchunks=r // bsz, bsz=bsz)
    spec3 = lambda a, b: pl.BlockSpec((gb, a, b), lambda i: (i, 0, 0))
    return pl.pallas_call(
        kern,
        grid=(g_ // gb,),
        in_specs=[spec3(r, th), spec3(th, th), spec3(th, 2 * p2), spec3(p2, th),
                  spec3(8, p2), spec3(1, th)],
        out_specs=spec3(r, th),
        out_shape=jax.ShapeDtypeStruct((g_, r, th), BF16),
        scratch_shapes=[pltpu.VMEM((gb, r, 2 * p2), F32), pltpu.VMEM((gb, r, p2), F32)],
        compiler_params=_cparams("arbitrary"),
        name="ssm",
    )(uf, mm, w4, vv, coef, dk)


def _gelu_tanh(y):
    return y * (0.5 * (1.0 + jnp.tanh(math.sqrt(2.0 / math.pi) * (y + 0.044715 * (y * y * y)))))


def _glu_out_kernel(y_ref, x_ref, mod_ref, wg_ref, wo_ref, o_ref):
    m = mod_ref[0]
    z = _gelu_tanh(y_ref[...].astype(F32))
    z = z * jax.nn.sigmoid(_bdot(z.astype(BF16), wg_ref[...]))
    o_ref[...] = x_ref[...] + m[2:3] * _bdot(z.astype(BF16), wo_ref[...])


def _glu_out(y, x, mod, wg, wo, seq, tm):
    n, d = x.shape
    tps = seq // tm
    return pl.pallas_call(
        _glu_out_kernel,
        grid=(n // tm,),
        in_specs=[pl.BlockSpec((tm, d), lambda i: (i, 0)),
                  pl.BlockSpec((tm, d), lambda i: (i, 0)),
                  pl.BlockSpec((1, 6, d), lambda i: (i // tps, 0, 0)),
                  pl.BlockSpec((d, d), lambda i: (0, 0)),
                  pl.BlockSpec((d, d), lambda i: (0, 0))],
        out_specs=pl.BlockSpec((tm, d), lambda i: (i, 0)),
        out_shape=jax.ShapeDtypeStruct((n, d), F32),
        compiler_params=_cparams("arbitrary"),
        name="glu_out",
    )(y, x, mod, wg, wo)


def _chunks(total, step):
    return [(s, min(step, total - s)) for s in range(0, total, step)]


def _swiglu_acc(h, w1_ref, w3_ref, w2_ref, acc_ref, sub):
    for s, n in _chunks(w1_ref.shape[-1], sub):
        a = _bdot(h, w1_ref[:, s:s + n])
        b = _bdot(h, w3_ref[:, s:s + n])
        act = (a * jax.nn.sigmoid(a) * b).astype(BF16)
        acc_ref[...] += _bdot(act, w2_ref[s:s + n, :])


def _ffn_kernel(x_ref, mod_ref, g_ref, w1_ref, w3_ref, w2_ref, o_ref, acc_ref, *, sub):
    m = mod_ref[0]
    x = x_ref[...]
    h = _normmod(x, g_ref[...], m[3:4], m[4:5]).astype(BF16)
    acc_ref[...] = jnp.zeros_like(acc_ref)
    _swiglu_acc(h, w1_ref, w3_ref, w2_ref, acc_ref, sub)
    o_ref[...] = x + m[5:6] * acc_ref[...]


def _ffn(x, mod, g, w1, w3, w2, seq, tm):
    n, d = x.shape
    dff = w1.shape[1]
    tps = seq // tm
    const = lambda i: (0, 0)
    return pl.pallas_call(
        functools.partial(_ffn_kernel, sub=512),
        grid=(n // tm,),
        in_specs=[pl.BlockSpec((tm, d), lambda i: (i, 0)),
                  pl.BlockSpec((1, 6, d), lambda i: (i // tps, 0, 0)),
                  pl.BlockSpec((1, d), const),
                  pl.BlockSpec((d, dff), const, pipeline_mode=pl.Buffered(1)),
                  pl.BlockSpec((d, dff), const, pipeline_mode=pl.Buffered(1)),
                  pl.BlockSpec((dff, d), const, pipeline_mode=pl.Buffered(1))],
        out_specs=pl.BlockSpec((tm, d), lambda i: (i, 0)),
        out_shape=jax.ShapeDtypeStruct((n, d), F32),
        scratch_shapes=[pltpu.VMEM((tm, d), F32)],
        compiler_params=_cparams("arbitrary"),
        name="ffn",
    )(x, mod, g, w1, w3, w2)


def _pool_kernel(u_ref, halo_ref, x_ref, mod_ref, wmix_ref, scale_ref, wo_ref, o_ref,
                 ext_ref, z_ref, *, tps):
    tm, d = u_ref.shape
    pg = d // len(POOL_WINDOWS)
    m = mod_ref[0]
    it = pl.program_id(0) % tps
    ext_ref[0:POOL_HALO, :] = jnp.where(it == 0, 0.0, halo_ref[...])
    ext_ref[POOL_HALO:, :] = u_ref[...]
    tpos = (it * tm + 1 + lax.broadcasted_iota(jnp.int32, (tm, 1), 0)).astype(F32)
    for gi, w in enumerate(POOL_WINDOWS):
        cols = slice(gi * pg, (gi + 1) * pg)
        s = ext_ref[:, cols]
        span = 1
        while span < w:
            s = s + pltpu.roll(s, span, 0)
            span *= 2
        mean = s[POOL_HALO:, :] * (1.0 / jnp.minimum(tpos, float(w)))
        dlt = (mean - u_ref[:, cols]).astype(BF16)
        z_ref[:, cols] = (_bdot(dlt, wmix_ref[gi]) * scale_ref[:, cols]).astype(BF16)
    o_ref[...] = x_ref[...] + m[2:3] * _bdot(z_ref[...], wo_ref[...])


def _pool(u, x, mod, wmix, scale, wo, seq, tm):
    n, d = x.shape
    tps = seq // tm
    hb = tm // POOL_HALO
    ng, pg, _ = wmix.shape
    return pl.pallas_call(
        functools.partial(_pool_kernel, tps=tps),
        grid=(n // tm,),
        in_specs=[pl.BlockSpec((tm, d), lambda i: (i, 0)),
                  pl.BlockSpec((POOL_HALO, d), lambda i: (jnp.maximum(i * hb - 1, 0), 0)),
                  pl.BlockSpec((tm, d), lambda i: (i, 0)),
                  pl.BlockSpec((1, 6, d), lambda i: (i // tps, 0, 0)),
                  pl.BlockSpec((ng, pg, pg), lambda i: (0, 0, 0)),
                  pl.BlockSpec((1, d), lambda i: (0, 0)),
                  pl.BlockSpec((d, d), lambda i: (0, 0))],
        out_specs=pl.BlockSpec((tm, d), lambda i: (i, 0)),
        out_shape=jax.ShapeDtypeStruct((n, d), F32),
        scratch_shapes=[pltpu.VMEM((tm + POOL_HALO, d), F32), pltpu.VMEM((tm, d), BF16)],
        compiler_params=_cparams("arbitrary"),
        name="pool",
    )(u, u, x, mod, wmix, scale, wo)


def _router_kernel(x_ref, mod_ref, g_ref, rw_ref, rb_ref, h_ref, idx_ref, gate_ref):
    m = mod_ref[0]
    h = _normmod(x_ref[...], g_ref[...], m[3:4], m[4:5])
    h_ref[...] = h.astype(BF16)
    logits = jnp.dot(h, rw_ref[...], precision=lax.Precision.HIGHEST,
                     preferred_element_type=F32) + rb_ref[...]
    ne = float(logits.shape[1])
    lane = lax.broadcasted_iota(jnp.int32, logits.shape, 1).astype(F32)
    v1 = jnp.max(logits, axis=1, keepdims=True)
    i1 = jnp.min(jnp.where(logits == v1, lane, ne), axis=1, keepdims=True)
    rest = jnp.where(lane == i1, -jnp.inf, logits)
    v2 = jnp.max(rest, axis=1, keepdims=True)
    i2 = jnp.min(jnp.where(rest == v2, lane, ne), axis=1, keepdims=True)
    e2 = jnp.exp(v2 - v1)
    den = 1.0 + e2
    two = lax.broadcasted_iota(jnp.int32, idx_ref.shape, 1)
    idx_ref[...] = jnp.where(two == 0, i1, i2).astype(jnp.int32)
    gate_ref[...] = jnp.where(two == 0, 1.0 / den, e2 / den)


def _router(x, mod, g, rw, rb, seq, tm):
    n, d = x.shape
    ne = rw.shape[1]
    tps = seq // tm
    return pl.pallas_call(
        _router_kernel,
        grid=(n // tm,),
        in_specs=[pl.BlockSpec((tm, d), lambda i: (i, 0)),
                  pl.BlockSpec((1, 6, d), lambda i: (i // tps, 0, 0)),
                  pl.BlockSpec((1, d), lambda i: (0, 0)),
                  pl.BlockSpec((d, ne), lambda i: (0, 0)),
                  pl.BlockSpec((1, ne), lambda i: (0, 0))],
        out_specs=[pl.BlockSpec((tm, d), lambda i: (i, 0)),
                   pl.BlockSpec((tm, TOP_K), lambda i: (i, 0)),
                   pl.BlockSpec((tm, TOP_K), lambda i: (i, 0))],
        out_shape=[jax.ShapeDtypeStruct((n, d), BF16),
                   jax.ShapeDtypeStruct((n, TOP_K), jnp.int32),
                   jax.ShapeDtypeStruct((n, TOP_K), F32)],
        compiler_params=_cparams("arbitrary"),
        name="router",
    )(x, mod, g, rw, rb)


def _routing_tables(idx, ne, tile, ts):
    n = idx.shape[0]
    nch = n // ts
    e_flat = idx.reshape(-1)
    onehot = (e_flat[:, None] == jnp.arange(ne, dtype=jnp.int32)[None, :]).astype(jnp.int32)
    csum = jnp.cumsum(onehot, axis=0)
    cnt = csum[-1]
    rank = jnp.take_along_axis(csum, e_flat[:, None], axis=1)[:, 0] - 1
    padded = ((cnt + tile - 1) // tile) * tile
    off = jnp.cumsum(padded) - padded
    pos = (off[e_flat] + rank).astype(jnp.int32).reshape(n, TOP_K)
    n_tiles = (TOP_K * n) // tile + ne
    used = jnp.sum(padded) // tile
    tile_ids = jnp.arange(n_tiles, dtype=jnp.int32)
    tile_expert = jnp.searchsorted(jnp.cumsum(padded), tile_ids * tile, side='right')
    last_used = jnp.maximum(used - 1, 0)
    tile_valid = (tile_ids < used).astype(jnp.int32)
    tile_src = jnp.minimum(tile_ids, last_used).astype(jnp.int32)
    tile_expert = jnp.minimum(tile_expert, ne - 1).astype(jnp.int32)
    tile_expert = jnp.where(tile_valid == 1, tile_expert, tile_expert[last_used])
    cend = csum[(jnp.arange(nch) + 1) * (TOP_K * ts) - 1]
    cbeg = jnp.concatenate([jnp.zeros((1, ne), jnp.int32), cend[:-1]], axis=0)
    seg_len = cend - cbeg
    seg_lo = (off[None, :] + cbeg) // tile
    seg_hi = (off[None, :] + cend - 1) // tile
    seg_n = jnp.where(seg_len > 0, seg_hi - seg_lo + 1, 0)
    n_visits = ne * nch + n_tiles

    def enumerate_visits(order_ce):
        cc, ee = order_ce[:, 0], order_ce[:, 1]
        sn = seg_n[cc, ee]
        cum = jnp.cumsum(sn)
        total = cum[-1]
        v = jnp.arange(n_visits, dtype=jnp.int32)
        vv = jnp.minimum(v, total - 1)
        seg = jnp.searchsorted(cum, vv, side='right')
        start = cum[seg] - sn[seg]
        v_tile = (seg_lo[cc, ee][seg] + (vv - start)).astype(jnp.int32)
        v_chunk = cc[seg].astype(jnp.int32)
        valid = (v < total).astype(jnp.int32)
        return v_tile, v_chunk, valid

    cg, eg = jnp.meshgrid(jnp.arange(nch), jnp.arange(ne), indexing='ij')
    by_chunk = jnp.stack([cg.reshape(-1), eg.reshape(-1)], axis=1)
    by_expert = jnp.stack([cg.T.reshape(-1), eg.T.reshape(-1)], axis=1)
    d_tile, d_chunk, d_valid = enumerate_visits(by_expert)
    c_tile, c_chunk, c_valid = enumerate_visits(by_chunk)

    def edges(key, valid):
        prev = jnp.concatenate([jnp.full((1,), -1, jnp.int32), key[:-1]])
        nxt = jnp.concatenate([key[1:], jnp.full((1,), -1, jnp.int32)])
        nvalid = jnp.concatenate([valid[1:], jnp.zeros((1,), jnp.int32)])
        first = ((key != prev) & (valid == 1)).astype(jnp.int32)
        last = (((key != nxt) | (nvalid == 0)) & (valid == 1)).astype(jnp.int32)
        return first, last

    d_first, d_last = edges(d_tile, d_valid)
    c_first, c_last = edges(c_chunk, c_valid)
    return dict(pos=pos, tile_expert=tile_expert, tile_valid=tile_valid, tile_src=tile_src,
                d_tile=d_tile, d_chunk=d_chunk, d_valid=d_valid, d_first=d_first,
                d_last=d_last,
                c_tile=c_tile, c_chunk=c_chunk, c_valid=c_valid, c_first=c_first,
                c_last=c_last, n_tiles=n_tiles, n_visits=n_visits)


def _dispatch_kernel(vt_ref, vc_ref, vv_ref, vf_ref, vl_ref, h_ref, post_ref, gatet_ref,
                     xs_ref, gs_ref, acc_ref, gacc_ref):
    v = pl.program_id(0)
    tile = acc_ref.shape[0]

    @pl.when(vf_ref[v] == 1)
    def _():
        acc_ref[...] = jnp.zeros_like(acc_ref)
        gacc_ref[...] = jnp.zeros_like(gacc_ref)

    @pl.when(vv_ref[v] == 1)
    def _():
        ts = h_ref.shape[0]
        row = vt_ref[v] * tile + lax.broadcasted_iota(jnp.int32, (tile, ts), 0)
        ma = post_ref[0:1, :] == row
        mb = post_ref[1:2, :] == row
        sel = jnp.where(ma | mb, 1.0, 0.0).astype(BF16)
        acc_ref[...] += _bdot(sel, h_ref[...])
        gsel = jnp.where(ma, gatet_ref[0:1, :], 0.0) + jnp.where(mb, gatet_ref[1:2, :], 0.0)
        gacc_ref[...] += jnp.sum(gsel, axis=1, keepdims=True)

    @pl.when(vl_ref[v] == 1)
    def _():
        xs_ref[...] = acc_ref[...].astype(BF16)
        gs_ref[...] = gacc_ref[...]


def _dispatch(h, pos_t, gate_t, rt, tile, ts):
    n, d = h.shape
    rows = rt['n_tiles'] * tile
    im = lambda f: (lambda v, vt, vc, vv, vf, vl: f(v, vt, vc))
    return pl.pallas_call(
        _dispatch_kernel,
        grid_spec=pltpu.PrefetchScalarGridSpec(
            num_scalar_prefetch=5,
            grid=(rt['n_visits'],),
            in_specs=[pl.BlockSpec((ts, d), im(lambda v, vt, vc: (vc[v], 0))),
                      pl.BlockSpec((TOP_K, ts), im(lambda v, vt, vc: (0, vc[v]))),
                      pl.BlockSpec((TOP_K, ts), im(lambda v, vt, vc: (0, vc[v])))],
            out_specs=[pl.BlockSpec((tile, d), im(lambda v, vt, vc: (vt[v], 0))),
                       pl.BlockSpec((tile, 1), im(lambda v, vt, vc: (vt[v], 0)))],
            scratch_shapes=[pltpu.VMEM((tile, d), F32), pltpu.VMEM((tile, 1), F32)]),
        out_shape=[jax.ShapeDtypeStruct((rows, d), BF16),
                   jax.ShapeDtypeStruct((rows, 1), F32)],
        compiler_params=_cparams("arbitrary"),
        name="dispatch",
    )(rt['d_tile'], rt['d_chunk'], rt['d_valid'], rt['d_first'], rt['d_last'],
      h, pos_t, gate_t)


def _expert_kernel(te_ref, tv_ref, tsrc_ref, xs_ref, gs_ref, w1_ref, w3_ref, w2_ref,
                   ys_ref, acc_ref, *, sub):
    j = pl.program_id(0)
    k = pl.program_id(1)

    @pl.when(tv_ref[j] == 1)
    def _():
        @pl.when(k == 0)
        def _():
            acc_ref[...] = jnp.zeros_like(acc_ref)

        _swiglu_acc(xs_ref[...], w1_ref.at[0], w3_ref.at[0], w2_ref.at[0], acc_ref, sub)

        @pl.when(k == pl.num_programs(1) - 1)
        def _():
            ys_ref[...] = (acc_ref[...] * gs_ref[...]).astype(BF16)


def _experts(xs, gs, w1, w3, w2, rt, tile, kc):
    rows, d = xs.shape
    ne, _, dff = w1.shape
    fc = dff // kc
    kcol = lambda j, k, te, tv, tsrc: jnp.where(tv[j] == 1, k, kc - 1)
    return pl.pallas_call(
        functools.partial(_expert_kernel, sub=512),
        grid_spec=pltpu.PrefetchScalarGridSpec(
            num_scalar_prefetch=3,
            grid=(rt['n_tiles'], kc),
            in_specs=[pl.BlockSpec((tile, d), lambda j, k, te, tv, tsrc: (tsrc[j], 0)),
                      pl.BlockSpec((tile, 1), lambda j, k, te, tv, tsrc: (tsrc[j], 0)),
                      pl.BlockSpec((1, d, fc), lambda j, k, te, tv, tsrc: (te[j], 0, kcol(j, k, te, tv, tsrc))),
                      pl.BlockSpec((1, d, fc), lambda j, k, te, tv, tsrc: (te[j], 0, kcol(j, k, te, tv, tsrc))),
                      pl.BlockSpec((1, fc, d), lambda j, k, te, tv, tsrc: (te[j], kcol(j, k, te, tv, tsrc), 0))],
            out_specs=pl.BlockSpec((tile, d), lambda j, k, te, tv, tsrc: (tsrc[j], 0)),
            scratch_shapes=[pltpu.VMEM((tile, d), F32)]),
        out_shape=jax.ShapeDtypeStruct((rows, d), BF16),
        compiler_params=_cparams("arbitrary", "arbitrary"),
        name="experts",
    )(rt['tile_expert'], rt['tile_valid'], rt['tile_src'], xs, gs, w1, w3, w2)


def _combine_kernel(vt_ref, vc_ref, vv_ref, vf_ref, vl_ref, ys_ref, pos_ref, x_ref, mod_ref,
                    fg_ref, o_ref, acc_ref, *, final_norm):
    v = pl.program_id(0)

    @pl.when(vf_ref[v] == 1)
    def _():
        acc_ref[...] = jnp.zeros_like(acc_ref)

    @pl.when(vv_ref[v] == 1)
    def _():
        tile = ys_ref.shape[0]
        ts = acc_ref.shape[0]
        row = vt_ref[v] * tile + lax.broadcasted_iota(jnp.int32, (ts, tile), 1)
        sel = (pos_ref[:, 0:1] == row) | (pos_ref[:, 1:2] == row)
        acc_ref[...] += _bdot(jnp.where(sel, 1.0, 0.0).astype(BF16), ys_ref[...])

    @pl.when(vl_ref[v] == 1)
    def _():
        m = mod_ref[0]
        xn = x_ref[...] + m[5:6] * acc_ref[...]
        if final_norm:
            ms = jnp.mean(xn * xn, axis=-1, keepdims=True)
            xn = (xn * lax.rsqrt(ms + EPS)) * fg_ref[...]
        o_ref[...] = xn


def _combine(ys, pos, x, mod, fg, rt, tile, ts, seq, final_norm):
    n, d = x.shape
    cps = seq // ts
    im = lambda f: (lambda v, vt, vc, vv, vf, vl: f(v, vt, vc))
    return pl.pallas_call(
        functools.partial(_combine_kernel, final_norm=final_norm),
        grid_spec=pltpu.PrefetchScalarGridSpec(
            num_scalar_prefetch=5,
            grid=(rt['n_visits'],),
            in_specs=[pl.BlockSpec((tile, d), im(lambda v, vt, vc: (vt[v], 0))),
                      pl.BlockSpec((ts, TOP_K), im(lambda v, vt, vc: (vc[v], 0))),
                      pl.BlockSpec((ts, d), im(lambda v, vt, vc: (vc[v], 0))),
                      pl.BlockSpec((1, 6, d), im(lambda v, vt, vc: (vc[v] // cps, 0, 0))),
                      pl.BlockSpec((1, d), im(lambda v, vt, vc: (0, 0)))],
            out_specs=pl.BlockSpec((ts, d), im(lambda v, vt, vc: (vc[v], 0))),
            scratch_shapes=[pltpu.VMEM((ts, d), F32)]),
        out_shape=jax.ShapeDtypeStruct((n, d), F32),
        compiler_params=_cparams("arbitrary"),
        name="combine",
    )(rt['c_tile'], rt['c_chunk'], rt['c_valid'], rt['c_first'], rt['c_last'],
      ys, pos, x, mod, fg)


def _pick(n, prefs):
    for p in prefs:
        if n % p == 0:
            return p
    return n


def kernel(x, c, ada_w, ada_b, norm_g, ssm_in, ssm_log_dt, ssm_lam_re, ssm_lam_im,
           ssm_b_re, ssm_b_im, ssm_c_re, ssm_c_im, ssm_d, ssm_glu, ssm_out,
           pool_in, pool_mix, pool_scale, pool_out, ffn_w1, ffn_w3, ffn_w2,
           router_w, router_b, moe_w1, moe_w3, moe_w2, final_g):
    bsz, seq, d = x.shape
    depth = ada_w.shape[0]
    n = bsz * seq
    ne = router_w.shape[-1]
    g_, p_, h_ = ssm_b_re.shape[1:]
    t = SSM_T
    nc = seq // t
    if depth % 2 != 0:
        raise NotImplementedError("the final RMSNorm is fused into an expert layer's combine")
    tm = _pick(seq, (512, 256, 128, 64, 32, 16))
    tile = _pick(TOP_K * n, (512, 256, 128, 64, 32, 16))
    ts = _pick(seq, (512, 256, 128, 64, 32, 16))
    kc = 2 if moe_w1.shape[-1] % 256 == 0 else 1
    gb = _pick(g_, (4, 2, 1))

    mod = _ada(c, ada_w, ada_b).reshape(depth, bsz, 6, d)
    xf = x.reshape(n, d)
    fg = final_g.reshape(1, d)
    for i in range(depth):
        j = i // 2
        mod_i = mod[i]
        g_a = norm_g[i, 0].reshape(1, d)
        g_b = norm_g[i, 1].reshape(1, d)
        if i % 2 == 0:
            u = _in_proj(xf, mod_i, g_a, ssm_in[j].astype(BF16), seq, tm, BF16)
            ops = _ssm_operators(ssm_log_dt[j], ssm_lam_re[j], ssm_lam_im[j], ssm_b_re[j],
                                 ssm_b_im[j], ssm_c_re[j], ssm_c_im[j], ssm_d[j], t)
            uf = jnp.transpose(u.reshape(bsz, nc, t, g_, h_), (3, 1, 0, 2, 4))
            uf = uf.reshape(g_, nc * bsz, t * h_)
            yf = _ssm(uf, *ops, bsz, gb)
            y = jnp.transpose(yf.reshape(g_, nc, bsz, t, h_), (2, 1, 3, 0, 4)).reshape(n, d)
            xf = _glu_out(y, xf, mod_i, ssm_glu[j].astype(BF16), ssm_out[j].astype(BF16),
                          seq, tm)
            xf = _ffn(xf, mod_i, g_b, ffn_w1[j].astype(BF16), ffn_w3[j].astype(BF16),
                      ffn_w2[j].astype(BF16), seq, tm)
        else:
            u = _in_proj(xf, mod_i, g_a, pool_in[j].astype(BF16), seq, tm, F32)
            xf = _pool(u, xf, mod_i, pool_mix[j].astype(BF16), pool_scale[j].reshape(1, d),
                       pool_out[j].astype(BF16), seq, tm)
            h, idx, gates = _router(xf, mod_i, g_b, router_w[j], router_b[j].reshape(1, ne),
                                    seq, tm)
            rt = _routing_tables(idx, ne, tile, ts)
            xs, gs = _dispatch(h, rt['pos'].T, gates.T, rt, tile, ts)
            ys = _experts(xs, gs, moe_w1[j].astype(BF16), moe_w3[j].astype(BF16),
                          moe_w2[j].astype(BF16), rt, tile, kc)
            xf = _combine(ys, rt['pos'], xf, mod_i, fg, rt, tile, ts, seq,
                          final_norm=(i == depth - 1))
    return xf.reshape(bsz, seq, d)
```

```python
import functools
import math

import jax
import jax.numpy as jnp
from jax import lax
from jax.experimental import pallas as pl
from jax.experimental.pallas import tpu as pltpu

F32 = jnp.float32
BF16 = jnp.bfloat16
EPS = 1e-6
POOL_WINDOWS = (2, 4, 8, 16)
POOL_HALO = 16
SSM_T = 16
LANES = 128
TOP_K = 2
VMEM_LIMIT = 56 * 1024 * 1024


def _cparams(*sem):
    return pltpu.CompilerParams(dimension_semantics=sem, vmem_limit_bytes=VMEM_LIMIT)


def _normmod(x, g, shift, scale):
    ms = jnp.mean(x * x, axis=-1, keepdims=True)
    y = x * lax.rsqrt(ms + EPS)
    return (y * g) * (1.0 + scale) + shift


def _bdot(a, b):
    return jnp.dot(a, b, preferred_element_type=F32)


def _ada_kernel(c_ref, w_ref, b_ref, o_ref):
    c = c_ref[...]
    cond = c * jax.nn.sigmoid(c)
    o_ref[0] = jnp.dot(cond, w_ref[0], precision=lax.Precision.HIGHEST,
                       preferred_element_type=F32) + b_ref[0]


def _ada(c, ada_w, ada_b):
    depth, d, d6 = ada_w.shape
    bsz = c.shape[0]
    tn = d6 // 4
    return pl.pallas_call(
        _ada_kernel,
        grid=(depth, d6 // tn),
        in_specs=[pl.BlockSpec((bsz, d), lambda l, j: (0, 0)),
                  pl.BlockSpec((1, d, tn), lambda l, j: (l, 0, j)),
                  pl.BlockSpec((1, 1, tn), lambda l, j: (l, 0, j))],
        out_specs=pl.BlockSpec((1, bsz, tn), lambda l, j: (l, 0, j)),
        out_shape=jax.ShapeDtypeStruct((depth, bsz, d6), F32),
        compiler_params=_cparams("arbitrary", "arbitrary"),
        name="ada",
    )(c, ada_w, ada_b.reshape(depth, 1, d6))


def _in_kernel(x_ref, mod_ref, g_ref, w_ref, o_ref):
    m = mod_ref[0]
    h = _normmod(x_ref[...], g_ref[...], m[0:1], m[1:2])
    o_ref[...] = _bdot(h.astype(BF16), w_ref[...]).astype(o_ref.dtype)


def _in_proj(x, mod, g, w, seq, tm, out_dtype):
    n, d = x.shape
    tps = seq // tm
    return pl.pallas_call(
        _in_kernel,
        grid=(n // tm,),
        in_specs=[pl.BlockSpec((tm, d), lambda i: (i, 0)),
                  pl.BlockSpec((1, 6, d), lambda i: (i // tps, 0, 0)),
                  pl.BlockSpec((1, d), lambda i: (0, 0)),
                  pl.BlockSpec((d, w.shape[1]), lambda i: (0, 0))],
        out_specs=pl.BlockSpec((tm, w.shape[1]), lambda i: (i, 0)),
        out_shape=jax.ShapeDtypeStruct((n, w.shape[1]), out_dtype),
        compiler_params=_cparams("arbitrary"),
        name="in_proj",
    )(x, mod, g, w)


def _ssm_operators(log_dt, lam_re, lam_im, b_re, b_im, c_re, c_im, d_skip, t):
    hp = lax.Precision.HIGHEST
    g_, p_, h_ = b_re.shape
    dt = jnp.exp(log_dt.astype(F32))[:, None]
    lr = lam_re.astype(F32)
    li = lam_im.astype(F32)
    mag = jnp.exp(lr * dt)
    a_re = mag * jnp.cos(li * dt)
    a_im = mag * jnp.sin(li * dt)
    den = lr * lr + li * li
    nr = a_re - 1
    coef_re = (nr * lr + a_im * li) / den
    coef_im = (a_im * lr - nr * li) / den
    br = b_re.astype(F32)
    bi = b_im.astype(F32)
    bbar_re = coef_re[..., None] * br - coef_im[..., None] * bi
    bbar_im = coef_re[..., None] * bi + coef_im[..., None] * br
    steps = jnp.arange(0, t + 1, dtype=F32)[:, None, None]
    pmag = jnp.exp(lr * dt * steps)
    pw_re = pmag * jnp.cos(li * dt * steps)
    pw_im = pmag * jnp.sin(li * dt * steps)
    cr = c_re.astype(F32)
    ci = c_im.astype(F32)
    er = cr[None] * pw_re[:, :, None, :] - ci[None] * pw_im[:, :, None, :]
    ei = cr[None] * pw_im[:, :, None, :] + ci[None] * pw_re[:, :, None, :]
    kk = (jnp.einsum('mgap,gph->mgah', er[:t], bbar_re, precision=hp)
          - jnp.einsum('mgap,gph->mgah', ei[:t], bbar_im, precision=hp))
    kk = jnp.transpose(kk, (1, 0, 3, 2))
    jj = jnp.arange(t)[:, None]
    kq = jnp.arange(t)[None, :]
    lag = kq - jj
    mm = kk[:, jnp.maximum(lag, 0)]
    mm = jnp.where((lag >= 0)[None, :, :, None, None], mm, 0.0)
    mm = jnp.transpose(mm, (0, 1, 3, 2, 4)).reshape(g_, t * h_, t * h_)
    rev_re = pw_re[:t][::-1]
    rev_im = pw_im[:t][::-1]
    w_re = rev_re[..., None] * bbar_re[None] - rev_im[..., None] * bbar_im[None]
    w_im = rev_re[..., None] * bbar_im[None] + rev_im[..., None] * bbar_re[None]
    w_re = jnp.transpose(w_re, (1, 0, 3, 2)).reshape(g_, t * h_, p_)
    w_im = jnp.transpose(w_im, (1, 0, 3, 2)).reshape(g_, t * h_, p_)
    w4 = jnp.concatenate([w_re, w_im, w_im, w_re], axis=-1)
    v_re = jnp.transpose(er[1:], (1, 3, 0, 2)).reshape(g_, p_, t * h_)
    v_im = -jnp.transpose(ei[1:], (1, 3, 0, 2)).reshape(g_, p_, t * h_)
    vv = jnp.concatenate([v_re, v_im], axis=1)
    at_re, at_im = pw_re[t], pw_im[t]
    c1 = jnp.concatenate([at_re, at_re], axis=-1)
    c2 = jnp.concatenate([-at_im, at_im], axis=-1)
    coef = jnp.zeros((g_, 8, 2 * p_), F32).at[:, 0].set(c1).at[:, 1].set(c2)
    dk = jnp.tile(d_skip.astype(F32)[:, None, :], (1, t, 1)).reshape(g_, t * h_, 1)
    tr = lambda a: jnp.swapaxes(a, 1, 2).astype(BF16)
    return tr(mm), w4.astype(BF16), tr(vv), coef, dk


def _ssm_in_kernel(x_ref, mod_ref, g_ref, w_ref, ut_ref, u_scr, *, t, sub):
    q = pl.program_id(1)
    m = mod_ref[0]
    h = _normmod(x_ref[...], g_ref[...], m[0:1], m[1:2])
    u = _bdot(h.astype(BF16), w_ref[...])
    nlt = u_scr.shape[0]
    rows = pl.ds(pl.multiple_of(q * sub, sub), sub)
    for l in range(nlt):
        u_scr[l, rows, :] = u[:, l * LANES:(l + 1) * LANES]

    @pl.when(q == pl.num_programs(1) - 1)
    def _():
        ng, th, nch = ut_ref.shape
        hh = th // t
        gl = LANES // hh
        for l in range(nlt):
            for k in range(t):
                uk = u_scr[l, pl.ds(k, nch, stride=t), :]
                ut_ref[l * gl:(l + 1) * gl, k * hh:(k + 1) * hh, :] = (
                    uk.T.reshape(gl, hh, nch).astype(BF16))


def _ssm_in(x, mod, g, w, seq, t, hh, tb, sub):
    n, d = x.shape
    ng = d // hh
    nsub = tb // sub
    sps = seq // sub
    return pl.pallas_call(
        functools.partial(_ssm_in_kernel, t=t, sub=sub),
        grid=(n // tb, nsub),
        in_specs=[pl.BlockSpec((sub, d), lambda o, q: (o * nsub + q, 0)),
                  pl.BlockSpec((1, 6, d), lambda o, q: ((o * nsub + q) // sps, 0, 0)),
                  pl.BlockSpec((1, d), lambda o, q: (0, 0)),
                  pl.BlockSpec((d, d), lambda o, q: (0, 0))],
        out_specs=pl.BlockSpec((ng, t * hh, tb // t), lambda o, q: (0, 0, o)),
        out_shape=jax.ShapeDtypeStruct((ng, t * hh, n // t), BF16),
        scratch_shapes=[pltpu.VMEM((d // LANES, tb, LANES), F32)],
        compiler_params=_cparams("arbitrary", "arbitrary"),
        name="ssm_in",
    )(x, mod, g, w)


def _ssm_kernel(ut_ref, mt_ref, w4_ref, vt_ref, coef_ref, dk_ref, yt_ref, z_scr, s_scr,
                *, gb, nchunks, bsz):
    p2 = s_scr.shape[-1]
    for gi in range(gb):
        u = ut_ref[gi].astype(F32).T.astype(BF16)
        z = _bdot(u, w4_ref[gi])
        z_scr[gi, 0] = z[:, :p2]
        z_scr[gi, 1] = z[:, p2:]
    c1 = [jnp.broadcast_to(coef_ref[gi, 0:1, :], (bsz, p2)) for gi in range(gb)]
    c2 = [jnp.broadcast_to(coef_ref[gi, 1:2, :], (bsz, p2)) for gi in range(gb)]

    def body(c, carry):
        new = []
        for gi in range(gb):
            s1, s2 = carry[gi]
            rows = pl.ds(c, bsz, stride=nchunks)
            s_scr[gi, rows, :] = s1
            n1 = c1[gi] * s1 + c2[gi] * s2 + z_scr[gi, 0, rows, :]
            n2 = c1[gi] * s2 - c2[gi] * s1 + z_scr[gi, 1, rows, :]
            new.append((n1, n2))
        return tuple(new)

    zero = jnp.zeros((bsz, p2), F32)
    lax.fori_loop(0, nchunks, body, tuple((zero, zero) for _ in range(gb)))
    for gi in range(gb):
        ut = ut_ref[gi]
        st = _bdot(vt_ref[gi], s_scr[gi].T.astype(BF16))
        yt = _bdot(mt_ref[gi], ut) + st + dk_ref[gi] * ut.astype(F32)
        yt_ref[gi] = yt.astype(yt_ref.dtype)


def _ssm(ut, mt, w4t, vt, coef, dk, bsz, gb):
    g_, th, nch = ut.shape
    p2 = vt.shape[2]
    kern = functools.partial(_ssm_kernel, gb=gb, nchunks=nch // bsz, bsz=bsz)
    spec3 = lambda a, b: pl.BlockSpec((gb, a, b), lambda i: (i, 0, 0))
    return pl.pallas_call(
        kern,
        grid=(g_ // gb,),
        in_specs=[spec3(th, nch), spec3(th, th), spec3(th, 2 * p2), spec3(th, p2),
                  spec3(8, p2), spec3(th, 1)],
        out_specs=spec3(th, nch),
        out_shape=jax.ShapeDtypeStruct((g_, th, nch), BF16),
        scratch_shapes=[pltpu.VMEM((gb, 2, nch, p2), F32), pltpu.VMEM((gb, nch, p2), F32)],
        compiler_params=_cparams("arbitrary"),
        name="ssm",
    )(ut, mt, w4t, vt, coef, dk)


def _gelu_tanh(y):
    return y * (0.5 * (1.0 + jnp.tanh(math.sqrt(2.0 / math.pi) * (y + 0.044715 * (y * y * y)))))


def _glu_out_kernel(yt_ref, x_ref, mod_ref, wg_ref, wo_ref, o_ref, y_scr, *, t, sub):
    q = pl.program_id(1)

    @pl.when(q == 0)
    def _():
        ng, th, nch = yt_ref.shape
        hh = th // t
        gl = LANES // hh
        for l in range(y_scr.shape[0]):
            for k in range(t):
                yk = yt_ref[l * gl:(l + 1) * gl, k * hh:(k + 1) * hh, :].astype(F32)
                y_scr[l, pl.ds(k, nch, stride=t), :] = yk.reshape(LANES, nch).T

    m = mod_ref[0]
    rows = pl.ds(pl.multiple_of(q * sub, sub), sub)
    y = jnp.concatenate([y_scr[l, rows, :] for l in range(y_scr.shape[0])], axis=1)
    z = _gelu_tanh(y)
    z = z * jax.nn.sigmoid(_bdot(z.astype(BF16), wg_ref[...]))
    o_ref[...] = x_ref[...] + m[2:3] * _bdot(z.astype(BF16), wo_ref[...])


def _glu_out(yt, x, mod, wg, wo, seq, t, tb, sub):
    n, d = x.shape
    ng, th, _ = yt.shape
    nsub = tb // sub
    sps = seq // sub
    return pl.pallas_call(
        functools.partial(_glu_out_kernel, t=t, sub=sub),
        grid=(n // tb, nsub),
        in_specs=[pl.BlockSpec((ng, th, tb // t), lambda o, q: (0, 0, o)),
                  pl.BlockSpec((sub, d), lambda o, q: (o * nsub + q, 0)),
                  pl.BlockSpec((1, 6, d), lambda o, q: ((o * nsub + q) // sps, 0, 0)),
                  pl.BlockSpec((d, d), lambda o, q: (0, 0)),
                  pl.BlockSpec((d, d), lambda o, q: (0, 0))],
        out_specs=pl.BlockSpec((sub, d), lambda o, q: (o * nsub + q, 0)),
        out_shape=jax.ShapeDtypeStruct((n, d), F32),
        scratch_shapes=[pltpu.VMEM((d // LANES, tb, LANES), F32)],
        compiler_params=_cparams("arbitrary", "arbitrary"),
        name="glu_out",
    )(yt, x, mod, wg, wo)


def _chunks(total, step):
    return [(s, min(step, total - s)) for s in range(0, total, step)]


def _swiglu_acc(h, w1_ref, w3_ref, w2_ref, acc_ref, sub):
    for s, n in _chunks(w1_ref.shape[-1], sub):
        a = _bdot(h, w1_ref[:, s:s + n])
        b = _bdot(h, w3_ref[:, s:s + n])
        act = (a * jax.nn.sigmoid(a) * b).astype(BF16)
        acc_ref[...] += _bdot(act, w2_ref[s:s + n, :])


def _ffn_kernel(x_ref, mod_ref, g_ref, w1_ref, w3_ref, w2_ref, o_ref, acc_ref, *, sub):
    m = mod_ref[0]
    x = x_ref[...]
    h = _normmod(x, g_ref[...], m[3:4], m[4:5]).astype(BF16)
    acc_ref[...] = jnp.zeros_like(acc_ref)
    _swiglu_acc(h, w1_ref, w3_ref, w2_ref, acc_ref, sub)
    o_ref[...] = x + m[5:6] * acc_ref[...]


def _ffn(x, mod, g, w1, w3, w2, seq, tm):
    n, d = x.shape
    dff = w1.shape[1]
    tps = seq // tm
    const = lambda i: (0, 0)
    return pl.pallas_call(
        functools.partial(_ffn_kernel, sub=512),
        grid=(n // tm,),
        in_specs=[pl.BlockSpec((tm, d), lambda i: (i, 0)),
                  pl.BlockSpec((1, 6, d), lambda i: (i // tps, 0, 0)),
                  pl.BlockSpec((1, d), const),
                  pl.BlockSpec((d, dff), const, pipeline_mode=pl.Buffered(1)),
                  pl.BlockSpec((d, dff), const, pipeline_mode=pl.Buffered(1)),
                  pl.BlockSpec((dff, d), const, pipeline_mode=pl.Buffered(1))],
        out_specs=pl.BlockSpec((tm, d), lambda i: (i, 0)),
        out_shape=jax.ShapeDtypeStruct((n, d), F32),
        scratch_shapes=[pltpu.VMEM((tm, d), F32)],
        compiler_params=_cparams("arbitrary"),
        name="ffn",
    )(x, mod, g, w1, w3, w2)


def _pool_kernel(u_ref, halo_ref, x_ref, mod_ref, wmix_ref, scale_ref, wo_ref, o_ref,
                 ext_ref, z_ref, *, tps):
    tm, d = u_ref.shape
    pg = d // len(POOL_WINDOWS)
    m = mod_ref[0]
    it = pl.program_id(0) % tps
    ext_ref[0:POOL_HALO, :] = jnp.where(it == 0, 0.0, halo_ref[...])
    ext_ref[POOL_HALO:, :] = u_ref[...]
    tpos = (it * tm + 1 + lax.broadcasted_iota(jnp.int32, (tm, 1), 0)).astype(F32)
    for gi, w in enumerate(POOL_WINDOWS):
        cols = slice(gi * pg, (gi + 1) * pg)
        s = ext_ref[:, cols]
        span = 1
        while span < w:
            s = s + pltpu.roll(s, span, 0)
            span *= 2
        mean = s[POOL_HALO:, :] * (1.0 / jnp.minimum(tpos, float(w)))
        dlt = (mean - u_ref[:, cols]).astype(BF16)
        z_ref[:, cols] = (_bdot(dlt, wmix_ref[gi]) * scale_ref[:, cols]).astype(BF16)
    o_ref[...] = x_ref[...] + m[2:3] * _bdot(z_ref[...], wo_ref[...])


def _pool(u, x, mod, wmix, scale, wo, seq, tm):
    n, d = x.shape
    tps = seq // tm
    hb = tm // POOL_HALO
    ng, pg, _ = wmix.shape
    return pl.pallas_call(
        functools.partial(_pool_kernel, tps=tps),
        grid=(n // tm,),
        in_specs=[pl.BlockSpec((tm, d), lambda i: (i, 0)),
                  pl.BlockSpec((POOL_HALO, d), lambda i: (jnp.maximum(i * hb - 1, 0), 0)),
                  pl.BlockSpec((tm, d), lambda i: (i, 0)),
                  pl.BlockSpec((1, 6, d), lambda i: (i // tps, 0, 0)),
                  pl.BlockSpec((ng, pg, pg), lambda i: (0, 0, 0)),
                  pl.BlockSpec((1, d), lambda i: (0, 0)),
                  pl.BlockSpec((d, d), lambda i: (0, 0))],
        out_specs=pl.BlockSpec((tm, d), lambda i: (i, 0)),
        out_shape=jax.ShapeDtypeStruct((n, d), F32),
        scratch_shapes=[pltpu.VMEM((tm + POOL_HALO, d), F32), pltpu.VMEM((tm, d), BF16)],
        compiler_params=_cparams("arbitrary"),
        name="pool",
    )(u, u, x, mod, wmix, scale, wo)


def _router_kernel(x_ref, mod_ref, g_ref, rw_ref, rb_ref, h_ref, idx_ref, gate_ref):
    m = mod_ref[0]
    h = _normmod(x_ref[...], g_ref[...], m[3:4], m[4:5])
    h_ref[...] = h.astype(BF16)
    logits = jnp.dot(h, rw_ref[...], precision=lax.Precision.HIGHEST,
                     preferred_element_type=F32) + rb_ref[...]
    ne = float(logits.shape[1])
    lane = lax.broadcasted_iota(jnp.int32, logits.shape, 1).astype(F32)
    v1 = jnp.max(logits, axis=1, keepdims=True)
    i1 = jnp.min(jnp.where(logits == v1, lane, ne), axis=1, keepdims=True)
    rest = jnp.where(lane == i1, -jnp.inf, logits)
    v2 = jnp.max(rest, axis=1, keepdims=True)
    i2 = jnp.min(jnp.where(rest == v2, lane, ne), axis=1, keepdims=True)
    e2 = jnp.exp(v2 - v1)
    den = 1.0 + e2
    two = lax.broadcasted_iota(jnp.int32, idx_ref.shape, 1)
    idx_ref[...] = jnp.where(two == 0, i1, i2).astype(jnp.int32)
    gate_ref[...] = jnp.where(two == 0, 1.0 / den, e2 / den)


def _router(x, mod, g, rw, rb, seq, tm):
    n, d = x.shape
    ne = rw.shape[1]
    tps = seq // tm
    return pl.pallas_call(
        _router_kernel,
        grid=(n // tm,),
        in_specs=[pl.BlockSpec((tm, d), lambda i: (i, 0)),
                  pl.BlockSpec((1, 6, d), lambda i: (i // tps, 0, 0)),
                  pl.BlockSpec((1, d), lambda i: (0, 0)),
                  pl.BlockSpec((d, ne), lambda i: (0, 0)),
                  pl.BlockSpec((1, ne), lambda i: (0, 0))],
        out_specs=[pl.BlockSpec((tm, d), lambda i: (i, 0)),
                   pl.BlockSpec((tm, TOP_K), lambda i: (i, 0)),
                   pl.BlockSpec((tm, TOP_K), lambda i: (i, 0))],
        out_shape=[jax.ShapeDtypeStruct((n, d), BF16),
                   jax.ShapeDtypeStruct((n, TOP_K), jnp.int32),
                   jax.ShapeDtypeStruct((n, TOP_K), F32)],
        compiler_params=_cparams("arbitrary"),
        name="router",
    )(x, mod, g, rw, rb)


def _count_le(sorted_vals, queries):
    return jnp.sum((sorted_vals[None, :] <= queries[:, None]).astype(jnp.int32), axis=1)


def _routing_tables(idx, ne, tile, ts):
    n = idx.shape[0]
    nch = n // ts
    e_flat = idx.reshape(-1)
    onehot = (e_flat[:, None] == jnp.arange(ne, dtype=jnp.int32)[None, :]).astype(jnp.int32)
    csum = jnp.cumsum(onehot, axis=0)
    cnt = csum[-1]
    rank = jnp.take_along_axis(csum, e_flat[:, None], axis=1)[:, 0] - 1
    padded = ((cnt + tile - 1) // tile) * tile
    off = jnp.cumsum(padded) - padded
    pos = (off[e_flat] + rank).astype(jnp.int32).reshape(n, TOP_K)
    n_tiles = (TOP_K * n) // tile + ne
    used = jnp.sum(padded) // tile
    tile_ids = jnp.arange(n_tiles, dtype=jnp.int32)
    tile_expert = _count_le(jnp.cumsum(padded), tile_ids * tile)
    last_used = jnp.maximum(used - 1, 0)
    tile_valid = (tile_ids < used).astype(jnp.int32)
    tile_src = jnp.minimum(tile_ids, last_used).astype(jnp.int32)
    tile_expert = jnp.minimum(tile_expert, ne - 1).astype(jnp.int32)
    tile_expert = jnp.where(tile_valid == 1, tile_expert, tile_expert[last_used])
    cend = csum[(jnp.arange(nch) + 1) * (TOP_K * ts) - 1]
    cbeg = jnp.concatenate([jnp.zeros((1, ne), jnp.int32), cend[:-1]], axis=0)
    seg_len = cend - cbeg
    seg_lo = (off[None, :] + cbeg) // tile
    seg_hi = (off[None, :] + cend - 1) // tile
    seg_n = jnp.where(seg_len > 0, seg_hi - seg_lo + 1, 0)
    n_visits = ne * nch + n_tiles

    def enumerate_visits(order_ce):
        cc, ee = order_ce[:, 0], order_ce[:, 1]
        sn = seg_n[cc, ee]
        cum = jnp.cumsum(sn)
        total = cum[-1]
        v = jnp.arange(n_visits, dtype=jnp.int32)
        vv = jnp.minimum(v, total - 1)
        seg = _count_le(cum, vv)
        start = cum[seg] - sn[seg]
        v_tile = (seg_lo[cc, ee][seg] + (vv - start)).astype(jnp.int32)
        v_chunk = cc[seg].astype(jnp.int32)
        valid = (v < total).astype(jnp.int32)
        return v_tile, v_chunk, valid

    cg, eg = jnp.meshgrid(jnp.arange(nch), jnp.arange(ne), indexing='ij')
    by_chunk = jnp.stack([cg.reshape(-1), eg.reshape(-1)], axis=1)
    by_expert = jnp.stack([cg.T.reshape(-1), eg.T.reshape(-1)], axis=1)
    d_tile, d_chunk, d_valid = enumerate_visits(by_expert)
    c_tile, c_chunk, c_valid = enumerate_visits(by_chunk)

    def edges(key, valid):
        prev = jnp.concatenate([jnp.full((1,), -1, jnp.int32), key[:-1]])
        nxt = jnp.concatenate([key[1:], jnp.full((1,), -1, jnp.int32)])
        nvalid = jnp.concatenate([valid[1:], jnp.zeros((1,), jnp.int32)])
        first = ((key != prev) & (valid == 1)).astype(jnp.int32)
        last = (((key != nxt) | (nvalid == 0)) & (valid == 1)).astype(jnp.int32)
        return first, last

    d_first, d_last = edges(d_tile, d_valid)
    c_first, c_last = edges(c_chunk, c_valid)
    return dict(pos=pos, tile_expert=tile_expert, tile_valid=tile_valid, tile_src=tile_src,
                d_tile=d_tile, d_chunk=d_chunk, d_valid=d_valid, d_first=d_first,
                d_last=d_last,
                c_tile=c_tile, c_chunk=c_chunk, c_valid=c_valid, c_first=c_first,
                c_last=c_last, n_tiles=n_tiles, n_visits=n_visits)


def _dispatch_kernel(vt_ref, vc_ref, vv_ref, vf_ref, vl_ref, h_ref, post_ref, gatet_ref,
                     xs_ref, gs_ref, acc_ref, gacc_ref):
    v = pl.program_id(0)
    tile = acc_ref.shape[0]

    @pl.when(vf_ref[v] == 1)
    def _():
        acc_ref[...] = jnp.zeros_like(acc_ref)
        gacc_ref[...] = jnp.zeros_like(gacc_ref)

    @pl.when(vv_ref[v] == 1)
    def _():
        ts = h_ref.shape[0]
        row = vt_ref[v] * tile + lax.broadcasted_iota(jnp.int32, (tile, ts), 0)
        ma = post_ref[0:1, :] == row
        mb = post_ref[1:2, :] == row
        sel = jnp.where(ma | mb, 1.0, 0.0).astype(BF16)
        acc_ref[...] += _bdot(sel, h_ref[...])
        gsel = jnp.where(ma, gatet_ref[0:1, :], 0.0) + jnp.where(mb, gatet_ref[1:2, :], 0.0)
        gacc_ref[...] += jnp.sum(gsel, axis=1, keepdims=True)

    @pl.when(vl_ref[v] == 1)
    def _():
        xs_ref[...] = acc_ref[...].astype(BF16)
        gs_ref[...] = gacc_ref[...]


def _dispatch(h, pos_t, gate_t, rt, tile, ts):
    n, d = h.shape
    rows = rt['n_tiles'] * tile
    im = lambda f: (lambda v, vt, vc, vv, vf, vl: f(v, vt, vc))
    return pl.pallas_call(
        _dispatch_kernel,
        grid_spec=pltpu.PrefetchScalarGridSpec(
            num_scalar_prefetch=5,
            grid=(rt['n_visits'],),
            in_specs=[pl.BlockSpec((ts, d), im(lambda v, vt, vc: (vc[v], 0))),
                      pl.BlockSpec((TOP_K, ts), im(lambda v, vt, vc: (0, vc[v]))),
                      pl.BlockSpec((TOP_K, ts), im(lambda v, vt, vc: (0, vc[v])))],
            out_specs=[pl.BlockSpec((tile, d), im(lambda v, vt, vc: (vt[v], 0))),
                       pl.BlockSpec((tile, 1), im(lambda v, vt, vc: (vt[v], 0)))],
            scratch_shapes=[pltpu.VMEM((tile, d), F32), pltpu.VMEM((tile, 1), F32)]),
        out_shape=[jax.ShapeDtypeStruct((rows, d), BF16),
                   jax.ShapeDtypeStruct((rows, 1), F32)],
        compiler_params=_cparams("arbitrary"),
        name="dispatch",
    )(rt['d_tile'], rt['d_chunk'], rt['d_valid'], rt['d_first'], rt['d_last'],
      h, pos_t, gate_t)


def _expert_kernel(te_ref, tv_ref, tsrc_ref, xs_ref, gs_ref, w1_ref, w3_ref, w2_ref,
                   ys_ref, acc_ref, *, sub):
    j = pl.program_id(0)
    k = pl.program_id(1)

    @pl.when(tv_ref[j] == 1)
    def _():
        @pl.when(k == 0)
        def _():
            acc_ref[...] = jnp.zeros_like(acc_ref)

        _swiglu_acc(xs_ref[...], w1_ref.at[0], w3_ref.at[0], w2_ref.at[0], acc_ref, sub)

        @pl.when(k == pl.num_programs(1) - 1)
        def _():
            ys_ref[...] = (acc_ref[...] * gs_ref[...]).astype(BF16)


def _experts(xs, gs, w1, w3, w2, rt, tile, kc):
    rows, d = xs.shape
    ne, _, dff = w1.shape
    fc = dff // kc
    kcol = lambda j, k, te, tv, tsrc: jnp.where(tv[j] == 1, k, kc - 1)
    return pl.pallas_call(
        functools.partial(_expert_kernel, sub=512),
        grid_spec=pltpu.PrefetchScalarGridSpec(
            num_scalar_prefetch=3,
            grid=(rt['n_tiles'], kc),
            in_specs=[pl.BlockSpec((tile, d), lambda j, k, te, tv, tsrc: (tsrc[j], 0)),
                      pl.BlockSpec((tile, 1), lambda j, k, te, tv, tsrc: (tsrc[j], 0)),
                      pl.BlockSpec((1, d, fc), lambda j, k, te, tv, tsrc: (te[j], 0, kcol(j, k, te, tv, tsrc))),
                      pl.BlockSpec((1, d, fc), lambda j, k, te, tv, tsrc: (te[j], 0, kcol(j, k, te, tv, tsrc))),
                      pl.BlockSpec((1, fc, d), lambda j, k, te, tv, tsrc: (te[j], kcol(j, k, te, tv, tsrc), 0))],
            out_specs=pl.BlockSpec((tile, d), lambda j, k, te, tv, tsrc: (tsrc[j], 0)),
            scratch_shapes=[pltpu.VMEM((tile, d), F32)]),
        out_shape=jax.ShapeDtypeStruct((rows, d), BF16),
        compiler_params=_cparams("arbitrary", "arbitrary"),
        name="experts",
    )(rt['tile_expert'], rt['tile_valid'], rt['tile_src'], xs, gs, w1, w3, w2)


def _combine_kernel(vt_ref, vc_ref, vv_ref, vf_ref, vl_ref, ys_ref, pos_ref, x_ref, mod_ref,
                    fg_ref, o_ref, acc_ref, *, final_norm):
    v = pl.program_id(0)

    @pl.when(vf_ref[v] == 1)
    def _():
        acc_ref[...] = jnp.zeros_like(acc_ref)

    @pl.when(vv_ref[v] == 1)
    def _():
        tile = ys_ref.shape[0]
        ts = acc_ref.shape[0]
        row = vt_ref[v] * tile + lax.broadcasted_iota(jnp.int32, (ts, tile), 1)
        sel = (pos_ref[:, 0:1] == row) | (pos_ref[:, 1:2] == row)
        acc_ref[...] += _bdot(jnp.where(sel, 1.0, 0.0).astype(BF16), ys_ref[...])

    @pl.when(vl_ref[v] == 1)
    def _():
        m = mod_ref[0]
        xn = x_ref[...] + m[5:6] * acc_ref[...]
        if final_norm:
            ms = jnp.mean(xn * xn, axis=-1, keepdims=True)
            xn = (xn * lax.rsqrt(ms + EPS)) * fg_ref[...]
        o_ref[...] = xn


def _combine(ys, pos, x, mod, fg, rt, tile, ts, seq, final_norm):
    n, d = x.shape
    cps = seq // ts
    im = lambda f: (lambda v, vt, vc, vv, vf, vl: f(v, vt, vc))
    return pl.pallas_call(
        functools.partial(_combine_kernel, final_norm=final_norm),
        grid_spec=pltpu.PrefetchScalarGridSpec(
            num_scalar_prefetch=5,
            grid=(rt['n_visits'],),
            in_specs=[pl.BlockSpec((tile, d), im(lambda v, vt, vc: (vt[v], 0))),
                      pl.BlockSpec((ts, TOP_K), im(lambda v, vt, vc: (vc[v], 0))),
                      pl.BlockSpec((ts, d), im(lambda v, vt, vc: (vc[v], 0))),
                      pl.BlockSpec((1, 6, d), im(lambda v, vt, vc: (vc[v] // cps, 0, 0))),
                      pl.BlockSpec((1, d), im(lambda v, vt, vc: (0, 0)))],
            out_specs=pl.BlockSpec((ts, d), im(lambda v, vt, vc: (vc[v], 0))),
            scratch_shapes=[pltpu.VMEM((ts, d), F32)]),
        out_shape=jax.ShapeDtypeStruct((n, d), F32),
        compiler_params=_cparams("arbitrary"),
        name="combine",
    )(rt['c_tile'], rt['c_chunk'], rt['c_valid'], rt['c_first'], rt['c_last'],
      ys, pos, x, mod, fg)


def _pick(n, prefs):
    for p in prefs:
        if n % p == 0:
            return p
    return n


def kernel(x, c, ada_w, ada_b, norm_g, ssm_in, ssm_log_dt, ssm_lam_re, ssm_lam_im,
           ssm_b_re, ssm_b_im, ssm_c_re, ssm_c_im, ssm_d, ssm_glu, ssm_out,
           pool_in, pool_mix, pool_scale, pool_out, ffn_w1, ffn_w3, ffn_w2,
           router_w, router_b, moe_w1, moe_w3, moe_w2, final_g):
    bsz, seq, d = x.shape
    depth = ada_w.shape[0]
    n = bsz * seq
    ne = router_w.shape[-1]
    g_, p_, h_ = ssm_b_re.shape[1:]
    t = SSM_T
    tb = 128 * t
    if seq % tb != 0 or bsz != 8:
        raise NotImplementedError("state-space kernels assume batch 8 and seq % 2048 == 0")
    if depth % 2 != 0:
        raise NotImplementedError("the final RMSNorm is fused into an expert layer's combine")
    tm = _pick(seq, (512, 256, 128, 64, 32, 16))
    tile = _pick(TOP_K * n, (512, 256, 128, 64, 32, 16))
    ts = _pick(seq, (512, 256, 128, 64, 32, 16))
    kc = 2 if moe_w1.shape[-1] % 256 == 0 else 1
    gb = _pick(g_, (4, 2, 1))

    mod = _ada(c, ada_w, ada_b).reshape(depth, bsz, 6, d)
    xf = x.reshape(n, d)
    fg = final_g.reshape(1, d)
    for i in range(depth):
        j = i // 2
        mod_i = mod[i]
        g_a = norm_g[i, 0].reshape(1, d)
        g_b = norm_g[i, 1].reshape(1, d)
        if i % 2 == 0:
            ops = _ssm_operators(ssm_log_dt[j], ssm_lam_re[j], ssm_lam_im[j], ssm_b_re[j],
                                 ssm_b_im[j], ssm_c_re[j], ssm_c_im[j], ssm_d[j], t)
            ut = _ssm_in(xf, mod_i, g_a, ssm_in[j].astype(BF16), seq, t, h_, tb, tm)
            yt = _ssm(ut, *ops, bsz, gb)
            xf = _glu_out(yt, xf, mod_i, ssm_glu[j].astype(BF16), ssm_out[j].astype(BF16),
                          seq, t, tb, tm)
            xf = _ffn(xf, mod_i, g_b, ffn_w1[j].astype(BF16), ffn_w3[j].astype(BF16),
                      ffn_w2[j].astype(BF16), seq, tm)
        else:
            u = _in_proj(xf, mod_i, g_a, pool_in[j].astype(BF16), seq, tm, F32)
            xf = _pool(u, xf, mod_i, pool_mix[j].astype(BF16), pool_scale[j].reshape(1, d),
                       pool_out[j].astype(BF16), seq, tm)
            h, idx, gates = _router(xf, mod_i, g_b, router_w[j], router_b[j].reshape(1, ne),
                                    seq, tm)
            rt = _routing_tables(idx, ne, tile, ts)
            xs, gs = _dispatch(h, rt['pos'].T, gates.T, rt, tile, ts)
            ys = _experts(xs, gs, moe_w1[j].astype(BF16), moe_w3[j].astype(BF16),
                          moe_w2[j].astype(BF16), rt, tile, kc)
            xf = _combine(ys, rt['pos'], xf, mod_i, fg, rt, tile, ts, seq,
                          final_norm=(i == depth - 1))
    return xf.reshape(bsz, seq, d)
```

```python
import functools
import math

import jax
import jax.numpy as jnp
from jax import lax
from jax.experimental import pallas as pl
from jax.experimental.pallas import tpu as pltpu
from jax.experimental.pallas import tpu_sc as plsc

F32 = jnp.float32
BF16 = jnp.bfloat16
EPS = 1e-6
POOL_WINDOWS = (2, 4, 8, 16)
POOL_HALO = 16
SSM_T = 16
LANES = 128
TOP_K = 2
SC_CORES, SC_SUBCORES = 2, 16
SC_WORKERS = SC_CORES * SC_SUBCORES
SC_ROWS = 64
VMEM_LIMIT = 56 * 1024 * 1024


def _cparams(*sem):
    return pltpu.CompilerParams(dimension_semantics=sem, vmem_limit_bytes=VMEM_LIMIT)


def _normmod(x, g, shift, scale):
    ms = jnp.mean(x * x, axis=-1, keepdims=True)
    y = x * lax.rsqrt(ms + EPS)
    return (y * g) * (1.0 + scale) + shift


def _bdot(a, b):
    return jnp.dot(a, b, preferred_element_type=F32)


def _ada_kernel(c_ref, w_ref, b_ref, o_ref):
    c = c_ref[...]
    cond = c * jax.nn.sigmoid(c)
    o_ref[0] = jnp.dot(cond, w_ref[0], precision=lax.Precision.HIGHEST,
                       preferred_element_type=F32) + b_ref[0]


def _ada(c, ada_w, ada_b):
    depth, d, d6 = ada_w.shape
    bsz = c.shape[0]
    tn = d6 // 4
    return pl.pallas_call(
        _ada_kernel,
        grid=(depth, d6 // tn),
        in_specs=[pl.BlockSpec((bsz, d), lambda l, j: (0, 0)),
                  pl.BlockSpec((1, d, tn), lambda l, j: (l, 0, j)),
                  pl.BlockSpec((1, 1, tn), lambda l, j: (l, 0, j))],
        out_specs=pl.BlockSpec((1, bsz, tn), lambda l, j: (l, 0, j)),
        out_shape=jax.ShapeDtypeStruct((depth, bsz, d6), F32),
        compiler_params=_cparams("arbitrary", "arbitrary"),
        name="ada",
    )(c, ada_w, ada_b.reshape(depth, 1, d6))


def _in_kernel(x_ref, mod_ref, g_ref, w_ref, o_ref):
    m = mod_ref[0]
    h = _normmod(x_ref[...], g_ref[...], m[0:1], m[1:2])
    o_ref[...] = _bdot(h.astype(BF16), w_ref[...]).astype(o_ref.dtype)


def _in_proj(x, mod, g, w, seq, tm, out_dtype):
    n, d = x.shape
    tps = seq // tm
    return pl.pallas_call(
        _in_kernel,
        grid=(n // tm,),
        in_specs=[pl.BlockSpec((tm, d), lambda i: (i, 0)),
                  pl.BlockSpec((1, 6, d), lambda i: (i // tps, 0, 0)),
                  pl.BlockSpec((1, d), lambda i: (0, 0)),
                  pl.BlockSpec((d, w.shape[1]), lambda i: (0, 0))],
        out_specs=pl.BlockSpec((tm, w.shape[1]), lambda i: (i, 0)),
        out_shape=jax.ShapeDtypeStruct((n, w.shape[1]), out_dtype),
        compiler_params=_cparams("arbitrary"),
        name="in_proj",
    )(x, mod, g, w)


def _ssm_operators(log_dt, lam_re, lam_im, b_re, b_im, c_re, c_im, d_skip, t):
    hp = lax.Precision.HIGHEST
    g_, p_, h_ = b_re.shape
    dt = jnp.exp(log_dt.astype(F32))[:, None]
    lr = lam_re.astype(F32)
    li = lam_im.astype(F32)
    mag = jnp.exp(lr * dt)
    a_re = mag * jnp.cos(li * dt)
    a_im = mag * jnp.sin(li * dt)
    den = lr * lr + li * li
    nr = a_re - 1
    coef_re = (nr * lr + a_im * li) / den
    coef_im = (a_im * lr - nr * li) / den
    br = b_re.astype(F32)
    bi = b_im.astype(F32)
    bbar_re = coef_re[..., None] * br - coef_im[..., None] * bi
    bbar_im = coef_re[..., None] * bi + coef_im[..., None] * br
    steps = jnp.arange(0, t + 1, dtype=F32)[:, None, None]
    pmag = jnp.exp(lr * dt * steps)
    pw_re = pmag * jnp.cos(li * dt * steps)
    pw_im = pmag * jnp.sin(li * dt * steps)
    cr = c_re.astype(F32)
    ci = c_im.astype(F32)
    er = cr[None] * pw_re[:, :, None, :] - ci[None] * pw_im[:, :, None, :]
    ei = cr[None] * pw_im[:, :, None, :] + ci[None] * pw_re[:, :, None, :]
    kk = (jnp.einsum('mgap,gph->mgah', er[:t], bbar_re, precision=hp)
          - jnp.einsum('mgap,gph->mgah', ei[:t], bbar_im, precision=hp))
    kk = jnp.transpose(kk, (1, 0, 3, 2))
    jj = jnp.arange(t)[:, None]
    kq = jnp.arange(t)[None, :]
    lag = kq - jj
    mm = kk[:, jnp.maximum(lag, 0)]
    mm = jnp.where((lag >= 0)[None, :, :, None, None], mm, 0.0)
    mm = jnp.transpose(mm, (0, 1, 3, 2, 4)).reshape(g_, t * h_, t * h_)
    rev_re = pw_re[:t][::-1]
    rev_im = pw_im[:t][::-1]
    w_re = rev_re[..., None] * bbar_re[None] - rev_im[..., None] * bbar_im[None]
    w_im = rev_re[..., None] * bbar_im[None] + rev_im[..., None] * bbar_re[None]
    w_re = jnp.transpose(w_re, (1, 0, 3, 2)).reshape(g_, t * h_, p_)
    w_im = jnp.transpose(w_im, (1, 0, 3, 2)).reshape(g_, t * h_, p_)
    w4 = jnp.concatenate([w_re, w_im, w_im, w_re], axis=-1)
    v_re = jnp.transpose(er[1:], (1, 3, 0, 2)).reshape(g_, p_, t * h_)
    v_im = -jnp.transpose(ei[1:], (1, 3, 0, 2)).reshape(g_, p_, t * h_)
    vv = jnp.concatenate([v_re, v_im], axis=1)
    at_re, at_im = pw_re[t], pw_im[t]
    c1 = jnp.concatenate([at_re, at_re], axis=-1)
    c2 = jnp.concatenate([-at_im, at_im], axis=-1)
    coef = jnp.zeros((g_, 8, 2 * p_), F32).at[:, 0].set(c1).at[:, 1].set(c2)
    dk = jnp.tile(d_skip.astype(F32)[:, None, :], (1, t, 1)).reshape(g_, t * h_, 1)
    tr = lambda a: jnp.swapaxes(a, 1, 2).astype(BF16)
    return tr(mm), w4.astype(BF16), tr(vv), coef, dk


def _ssm_in_kernel(x_ref, mod_ref, g_ref, w_ref, ut_ref, u_scr, *, t, sub):
    q = pl.program_id(1)
    m = mod_ref[0]
    h = _normmod(x_ref[...], g_ref[...], m[0:1], m[1:2])
    u = _bdot(h.astype(BF16), w_ref[...])
    nlt = u_scr.shape[0]
    rows = pl.ds(pl.multiple_of(q * sub, sub), sub)
    for l in range(nlt):
        u_scr[l, rows, :] = u[:, l * LANES:(l + 1) * LANES]

    @pl.when(q == pl.num_programs(1) - 1)
    def _():
        ng, th, nch = ut_ref.shape
        hh = th // t
        gl = LANES // hh
        for l in range(nlt):
            for k in range(t):
                uk = u_scr[l, pl.ds(k, nch, stride=t), :]
                ut_ref[l * gl:(l + 1) * gl, k * hh:(k + 1) * hh, :] = (
                    uk.T.reshape(gl, hh, nch).astype(BF16))


def _ssm_in(x, mod, g, w, seq, t, hh, tb, sub):
    n, d = x.shape
    ng = d // hh
    nsub = tb // sub
    sps = seq // sub
    return pl.pallas_call(
        functools.partial(_ssm_in_kernel, t=t, sub=sub),
        grid=(n // tb, nsub),
        in_specs=[pl.BlockSpec((sub, d), lambda o, q: (o * nsub + q, 0)),
                  pl.BlockSpec((1, 6, d), lambda o, q: ((o * nsub + q) // sps, 0, 0)),
                  pl.BlockSpec((1, d), lambda o, q: (0, 0)),
                  pl.BlockSpec((d, d), lambda o, q: (0, 0))],
        out_specs=pl.BlockSpec((ng, t * hh, tb // t), lambda o, q: (0, 0, o)),
        out_shape=jax.ShapeDtypeStruct((ng, t * hh, n // t), BF16),
        scratch_shapes=[pltpu.VMEM((d // LANES, tb, LANES), F32)],
        compiler_params=_cparams("arbitrary", "arbitrary"),
        name="ssm_in",
    )(x, mod, g, w)


def _ssm_kernel(ut_ref, mt_ref, w4_ref, vt_ref, coef_ref, dk_ref, yt_ref, z_scr, s_scr,
                *, gb, nchunks, bsz):
    p2 = s_scr.shape[-1]
    for gi in range(gb):
        u = ut_ref[gi].astype(F32).T.astype(BF16)
        z = _bdot(u, w4_ref[gi])
        z_scr[gi, 0] = z[:, :p2]
        z_scr[gi, 1] = z[:, p2:]
    c1 = [jnp.broadcast_to(coef_ref[gi, 0:1, :], (bsz, p2)) for gi in range(gb)]
    c2 = [jnp.broadcast_to(coef_ref[gi, 1:2, :], (bsz, p2)) for gi in range(gb)]

    def body(c, carry):
        new = []
        for gi in range(gb):
            s1, s2 = carry[gi]
            rows = pl.ds(c, bsz, stride=nchunks)
            s_scr[gi, rows, :] = s1
            n1 = c1[gi] * s1 + c2[gi] * s2 + z_scr[gi, 0, rows, :]
            n2 = c1[gi] * s2 - c2[gi] * s1 + z_scr[gi, 1, rows, :]
            new.append((n1, n2))
        return tuple(new)

    zero = jnp.zeros((bsz, p2), F32)
    lax.fori_loop(0, nchunks, body, tuple((zero, zero) for _ in range(gb)))
    for gi in range(gb):
        ut = ut_ref[gi]
        st = _bdot(vt_ref[gi], s_scr[gi].T.astype(BF16))
        yt = _bdot(mt_ref[gi], ut) + st + dk_ref[gi] * ut.astype(F32)
        yt_ref[gi] = yt.astype(yt_ref.dtype)


def _ssm(ut, mt, w4t, vt, coef, dk, bsz, gb):
    g_, th, nch = ut.shape
    p2 = vt.shape[2]
    kern = functools.partial(_ssm_kernel, gb=gb, nchunks=nch // bsz, bsz=bsz)
    spec3 = lambda a, b: pl.BlockSpec((gb, a, b), lambda i: (i, 0, 0))
    return pl.pallas_call(
        kern,
        grid=(g_ // gb,),
        in_specs=[spec3(th, nch), spec3(th, th), spec3(th, 2 * p2), spec3(th, p2),
                  spec3(8, p2), spec3(th, 1)],
        out_specs=spec3(th, nch),
        out_shape=jax.ShapeDtypeStruct((g_, th, nch), BF16),
        scratch_shapes=[pltpu.VMEM((gb, 2, nch, p2), F32), pltpu.VMEM((gb, nch, p2), F32)],
        compiler_params=_cparams("arbitrary"),
        name="ssm",
    )(ut, mt, w4t, vt, coef, dk)


def _gelu_tanh(y):
    return y * (0.5 * (1.0 + jnp.tanh(math.sqrt(2.0 / math.pi) * (y + 0.044715 * (y * y * y)))))


def _glu_out_kernel(yt_ref, x_ref, mod_ref, wg_ref, wo_ref, o_ref, y_scr, *, t, sub):
    q = pl.program_id(1)

    @pl.when(q == 0)
    def _():
        ng, th, nch = yt_ref.shape
        hh = th // t
        gl = LANES // hh
        for l in range(y_scr.shape[0]):
            for k in range(t):
                yk = yt_ref[l * gl:(l + 1) * gl, k * hh:(k + 1) * hh, :].astype(F32)
                y_scr[l, pl.ds(k, nch, stride=t), :] = yk.reshape(LANES, nch).T

    m = mod_ref[0]
    rows = pl.ds(pl.multiple_of(q * sub, sub), sub)
    y = jnp.concatenate([y_scr[l, rows, :] for l in range(y_scr.shape[0])], axis=1)
    z = _gelu_tanh(y)
    z = z * jax.nn.sigmoid(_bdot(z.astype(BF16), wg_ref[...]))
    o_ref[...] = x_ref[...] + m[2:3] * _bdot(z.astype(BF16), wo_ref[...])


def _glu_out(yt, x, mod, wg, wo, seq, t, tb, sub):
    n, d = x.shape
    ng, th, _ = yt.shape
    nsub = tb // sub
    sps = seq // sub
    return pl.pallas_call(
        functools.partial(_glu_out_kernel, t=t, sub=sub),
        grid=(n // tb, nsub),
        in_specs=[pl.BlockSpec((ng, th, tb // t), lambda o, q: (0, 0, o)),
                  pl.BlockSpec((sub, d), lambda o, q: (o * nsub + q, 0)),
                  pl.BlockSpec((1, 6, d), lambda o, q: ((o * nsub + q) // sps, 0, 0)),
                  pl.BlockSpec((d, d), lambda o, q: (0, 0)),
                  pl.BlockSpec((d, d), lambda o, q: (0, 0))],
        out_specs=pl.BlockSpec((sub, d), lambda o, q: (o * nsub + q, 0)),
        out_shape=jax.ShapeDtypeStruct((n, d), F32),
        scratch_shapes=[pltpu.VMEM((d // LANES, tb, LANES), F32)],
        compiler_params=_cparams("arbitrary", "arbitrary"),
        name="glu_out",
    )(yt, x, mod, wg, wo)


def _chunks(total, step):
    return [(s, min(step, total - s)) for s in range(0, total, step)]


def _swiglu_acc(h, w1_ref, w3_ref, w2_ref, acc_ref, sub):
    for s, n in _chunks(w1_ref.shape[-1], sub):
        a = _bdot(h, w1_ref[:, s:s + n])
        b = _bdot(h, w3_ref[:, s:s + n])
        act = (a * jax.nn.sigmoid(a) * b).astype(BF16)
        acc_ref[...] += _bdot(act, w2_ref[s:s + n, :])


def _ffn_kernel(x_ref, mod_ref, g_ref, w1_ref, w3_ref, w2_ref, o_ref, acc_ref, *, sub):
    m = mod_ref[0]
    x = x_ref[...]
    h = _normmod(x, g_ref[...], m[3:4], m[4:5]).astype(BF16)
    acc_ref[...] = jnp.zeros_like(acc_ref)
    _swiglu_acc(h, w1_ref, w3_ref, w2_ref, acc_ref, sub)
    o_ref[...] = x + m[5:6] * acc_ref[...]


def _ffn(x, mod, g, w1, w3, w2, seq, tm):
    n, d = x.shape
    dff = w1.shape[1]
    tps = seq // tm
    const = lambda i: (0, 0)
    return pl.pallas_call(
        functools.partial(_ffn_kernel, sub=512),
        grid=(n // tm,),
        in_specs=[pl.BlockSpec((tm, d), lambda i: (i, 0)),
                  pl.BlockSpec((1, 6, d), lambda i: (i // tps, 0, 0)),
                  pl.BlockSpec((1, d), const),
                  pl.BlockSpec((d, dff), const, pipeline_mode=pl.Buffered(1)),
                  pl.BlockSpec((d, dff), const, pipeline_mode=pl.Buffered(1)),
                  pl.BlockSpec((dff, d), const, pipeline_mode=pl.Buffered(1))],
        out_specs=pl.BlockSpec((tm, d), lambda i: (i, 0)),
        out_shape=jax.ShapeDtypeStruct((n, d), F32),
        scratch_shapes=[pltpu.VMEM((tm, d), F32)],
        compiler_params=_cparams("arbitrary"),
        name="ffn",
    )(x, mod, g, w1, w3, w2)


def _pool_kernel(u_ref, halo_ref, x_ref, mod_ref, wmix_ref, scale_ref, wo_ref, o_ref,
                 ext_ref, z_ref, *, tps):
    tm, d = u_ref.shape
    pg = d // len(POOL_WINDOWS)
    m = mod_ref[0]
    it = pl.program_id(0) % tps
    ext_ref[0:POOL_HALO, :] = jnp.where(it == 0, 0.0, halo_ref[...])
    ext_ref[POOL_HALO:, :] = u_ref[...]
    tpos = (it * tm + 1 + lax.broadcasted_iota(jnp.int32, (tm, 1), 0)).astype(F32)
    for gi, w in enumerate(POOL_WINDOWS):
        cols = slice(gi * pg, (gi + 1) * pg)
        s = ext_ref[:, cols]
        span = 1
        while span < w:
            s = s + pltpu.roll(s, span, 0)
            span *= 2
        mean = s[POOL_HALO:, :] * (1.0 / jnp.minimum(tpos, float(w)))
        dlt = (mean - u_ref[:, cols]).astype(BF16)
        z_ref[:, cols] = (_bdot(dlt, wmix_ref[gi]) * scale_ref[:, cols]).astype(BF16)
    o_ref[...] = x_ref[...] + m[2:3] * _bdot(z_ref[...], wo_ref[...])


def _pool(u, x, mod, wmix, scale, wo, seq, tm):
    n, d = x.shape
    tps = seq // tm
    hb = tm // POOL_HALO
    ng, pg, _ = wmix.shape
    return pl.pallas_call(
        functools.partial(_pool_kernel, tps=tps),
        grid=(n // tm,),
        in_specs=[pl.BlockSpec((tm, d), lambda i: (i, 0)),
                  pl.BlockSpec((POOL_HALO, d), lambda i: (jnp.maximum(i * hb - 1, 0), 0)),
                  pl.BlockSpec((tm, d), lambda i: (i, 0)),
                  pl.BlockSpec((1, 6, d), lambda i: (i // tps, 0, 0)),
                  pl.BlockSpec((ng, pg, pg), lambda i: (0, 0, 0)),
                  pl.BlockSpec((1, d), lambda i: (0, 0)),
                  pl.BlockSpec((d, d), lambda i: (0, 0))],
        out_specs=pl.BlockSpec((tm, d), lambda i: (i, 0)),
        out_shape=jax.ShapeDtypeStruct((n, d), F32),
        scratch_shapes=[pltpu.VMEM((tm + POOL_HALO, d), F32), pltpu.VMEM((tm, d), BF16)],
        compiler_params=_cparams("arbitrary"),
        name="pool",
    )(u, u, x, mod, wmix, scale, wo)


def _pack_bf16_pairs(v):
    w = v.shape[1] // 2
    bits = lambda a: lax.bitcast_convert_type(a.astype(BF16).astype(F32), jnp.int32)
    return bits(v[:, :w]) | lax.shift_right_logical(bits(v[:, w:]), 16)


def _unpack_bf16_pairs(p):
    hi = lax.bitcast_convert_type(p & jnp.int32(-65536), F32)
    lo = lax.bitcast_convert_type(lax.shift_left(p, 16), F32)
    return jnp.concatenate([hi, lo], axis=1)


def _router_kernel(x_ref, mod_ref, g_ref, rw_ref, rb_ref, h_ref, idx_ref, gate_ref):
    m = mod_ref[0]
    h = _normmod(x_ref[...], g_ref[...], m[3:4], m[4:5])
    h_ref[...] = _pack_bf16_pairs(h)
    logits = jnp.dot(h, rw_ref[...], precision=lax.Precision.HIGHEST,
                     preferred_element_type=F32) + rb_ref[...]
    ne = float(logits.shape[1])
    lane = lax.broadcasted_iota(jnp.int32, logits.shape, 1).astype(F32)
    v1 = jnp.max(logits, axis=1, keepdims=True)
    i1 = jnp.min(jnp.where(logits == v1, lane, ne), axis=1, keepdims=True)
    rest = jnp.where(lane == i1, -jnp.inf, logits)
    v2 = jnp.max(rest, axis=1, keepdims=True)
    i2 = jnp.min(jnp.where(rest == v2, lane, ne), axis=1, keepdims=True)
    e2 = jnp.exp(v2 - v1)
    den = 1.0 + e2
    two = lax.broadcasted_iota(jnp.int32, idx_ref.shape, 1)
    idx_ref[...] = jnp.where(two == 0, i1, i2).astype(jnp.int32)
    gate_ref[...] = jnp.where(two == 0, 1.0 / den, e2 / den)


def _router(x, mod, g, rw, rb, seq, tm):
    n, d = x.shape
    ne = rw.shape[1]
    tps = seq // tm
    return pl.pallas_call(
        _router_kernel,
        grid=(n // tm,),
        in_specs=[pl.BlockSpec((tm, d), lambda i: (i, 0)),
                  pl.BlockSpec((1, 6, d), lambda i: (i // tps, 0, 0)),
                  pl.BlockSpec((1, d), lambda i: (0, 0)),
                  pl.BlockSpec((d, ne), lambda i: (0, 0)),
                  pl.BlockSpec((1, ne), lambda i: (0, 0))],
        out_specs=[pl.BlockSpec((tm, d // 2), lambda i: (i, 0)),
                   pl.BlockSpec((tm, TOP_K), lambda i: (i, 0)),
                   pl.BlockSpec((tm, TOP_K), lambda i: (i, 0))],
        out_shape=[jax.ShapeDtypeStruct((n, d // 2), jnp.int32),
                   jax.ShapeDtypeStruct((n, TOP_K), jnp.int32),
                   jax.ShapeDtypeStruct((n, TOP_K), F32)],
        compiler_params=_cparams("arbitrary"),
        name="router",
    )(x, mod, g, rw, rb)


def _count_le(sorted_vals, queries):
    return jnp.sum((sorted_vals[None, :] <= queries[:, None]).astype(jnp.int32), axis=1)


def _routing_tables(idx, ne, tile):
    n = idx.shape[0]
    e_flat = idx.reshape(-1)
    onehot = (e_flat[:, None] == jnp.arange(ne, dtype=jnp.int32)[None, :]).astype(jnp.int32)
    csum = jnp.cumsum(onehot, axis=0)
    cnt = csum[-1]
    rank = jnp.take_along_axis(csum, e_flat[:, None], axis=1)[:, 0] - 1
    padded = ((cnt + tile - 1) // tile) * tile
    off = jnp.cumsum(padded) - padded
    pos = (off[e_flat] + rank).astype(jnp.int32).reshape(n, TOP_K)
    n_tiles = (TOP_K * n) // tile + ne
    used = jnp.sum(padded) // tile
    tile_ids = jnp.arange(n_tiles, dtype=jnp.int32)
    tile_expert = _count_le(jnp.cumsum(padded), tile_ids * tile)
    last_used = jnp.maximum(used - 1, 0)
    tile_valid = (tile_ids < used).astype(jnp.int32)
    tile_src = jnp.minimum(tile_ids, last_used).astype(jnp.int32)
    tile_expert = jnp.minimum(tile_expert, ne - 1).astype(jnp.int32)
    tile_expert = jnp.where(tile_valid == 1, tile_expert, tile_expert[last_used])
    tile_rows = jnp.clip(cnt[tile_expert] - (tile_ids * tile - off[tile_expert]), 0, tile)
    tile_rows = jnp.where(tile_valid == 1, tile_rows, 0).astype(jnp.int32)
    return dict(pos=pos, tile_expert=tile_expert, tile_valid=tile_valid, tile_src=tile_src,
                tile_rows=tile_rows, n_tiles=n_tiles)


def _sc_mesh():
    return plsc.VectorSubcoreMesh(core_axis_name="c", subcore_axis_name="s",
                                  num_cores=SC_CORES, num_subcores=SC_SUBCORES)


def _dispatch(hp, pos0, pos1, rows_out):
    n, w = hp.shape
    per_w = n // SC_WORKERS
    ch = SC_ROWS
    nch = per_w // ch
    assert per_w * SC_WORKERS == n and nch * ch == per_w and nch % 2 == 0

    @functools.partial(
        pl.kernel, mesh=_sc_mesh(),
        out_type=jax.ShapeDtypeStruct((rows_out, w), hp.dtype),
        scratch_types=[pltpu.VMEM((per_w,), jnp.int32), pltpu.VMEM((per_w,), jnp.int32),
                       pltpu.VMEM((ch, w), hp.dtype), pltpu.VMEM((ch, w), hp.dtype)]
        + [pltpu.SemaphoreType.DMA] * 4,
        name="dispatch",
    )
    def k(hp_hbm, p0_hbm, p1_hbm, out_hbm, i0_v, i1_v, r0_v, r1_v, l0, l1, s0, s1):
        base = (lax.axis_index("s") * SC_CORES + lax.axis_index("c")) * per_w
        pltpu.sync_copy(p0_hbm.at[pl.ds(base, per_w)], i0_v)
        pltpu.sync_copy(p1_hbm.at[pl.ds(base, per_w)], i1_v)
        bufs, lsem, ssem = (r0_v, r1_v), (l0, l1), (s0, s1)

        def load(j, b):
            return pltpu.make_async_copy(hp_hbm.at[pl.ds(base + j * ch, ch)], bufs[b], lsem[b])

        def scatter(j, b, idx):
            return pltpu.make_async_copy(bufs[b], out_hbm.at[idx.at[pl.ds(j * ch, ch)]], ssem[b])

        load(0, 0).start()

        @pl.loop(0, nch, step=2)
        def _(j):
            for b in range(2):
                jj = j + b
                load(jj, b).wait()

                @pl.when(jj + 1 < nch)
                def _():
                    load(jj + 1, 1 - b).start()

                c0 = scatter(jj, b, i0_v)
                c1 = scatter(jj, b, i1_v)
                c0.start()
                c1.start()
                c0.wait()
                c1.wait()

    return k(hp, pos0, pos1)


def _gather_pairs(ys, pos0, pos1):
    n = pos0.shape[0]
    w = ys.shape[1]
    per_w = n // SC_WORKERS
    ch = SC_ROWS
    nch = per_w // ch
    assert per_w * SC_WORKERS == n and nch * ch == per_w
    out = jax.ShapeDtypeStruct((n, w), ys.dtype)

    @functools.partial(
        pl.kernel, mesh=_sc_mesh(), out_type=(out, out),
        scratch_types=[pltpu.VMEM((per_w,), jnp.int32), pltpu.VMEM((per_w,), jnp.int32),
                       pltpu.VMEM((ch, w), ys.dtype), pltpu.VMEM((ch, w), ys.dtype)]
        + [pltpu.SemaphoreType.DMA] * 4,
        name="gather_pairs",
    )
    def k(ys_hbm, p0_hbm, p1_hbm, a_hbm, b_hbm, i0_v, i1_v, ra_v, rb_v, ga, gb, wa, wb):
        base = (lax.axis_index("s") * SC_CORES + lax.axis_index("c")) * per_w
        pltpu.sync_copy(p0_hbm.at[pl.ds(base, per_w)], i0_v)
        pltpu.sync_copy(p1_hbm.at[pl.ds(base, per_w)], i1_v)

        def gather(j, idx, buf, sem):
            return pltpu.make_async_copy(ys_hbm.at[idx.at[pl.ds(j * ch, ch)]], buf, sem)

        def write(j, buf, dst, sem):
            return pltpu.make_async_copy(buf, dst.at[pl.ds(base + j * ch, ch)], sem)

        gather(0, i0_v, ra_v, ga).start()
        gather(0, i1_v, rb_v, gb).start()

        @pl.loop(0, nch)
        def _(j):
            gather(j, i0_v, ra_v, ga).wait()
            write(j, ra_v, a_hbm, wa).start()
            gather(j, i1_v, rb_v, gb).wait()
            write(j, rb_v, b_hbm, wb).start()
            write(j, ra_v, a_hbm, wa).wait()

            @pl.when(j + 1 < nch)
            def _():
                gather(j + 1, i0_v, ra_v, ga).start()

            write(j, rb_v, b_hbm, wb).wait()

            @pl.when(j + 1 < nch)
            def _():
                gather(j + 1, i1_v, rb_v, gb).start()

    return k(ys, pos0, pos1)


def _expert_kernel(te_ref, tv_ref, tsrc_ref, tr_ref, xs_ref, w1_ref, w3_ref, w2_ref,
                   ys_ref, x_scr, acc_ref, *, sub):
    j = pl.program_id(0)
    k = pl.program_id(1)
    last = pl.num_programs(1) - 1

    @pl.when(tv_ref[j] == 1)
    def _():
        @pl.when(k == 0)
        def _():
            row = lax.broadcasted_iota(jnp.int32, x_scr.shape, 0)
            x = _unpack_bf16_pairs(xs_ref[...])
            x_scr[...] = jnp.where(row < tr_ref[j], x, 0.0).astype(BF16)
            acc_ref[...] = jnp.zeros_like(acc_ref)

        _swiglu_acc(x_scr[...], w1_ref.at[0], w3_ref.at[0], w2_ref.at[0], acc_ref, sub)

        @pl.when(k == last)
        def _():
            ys_ref[...] = _pack_bf16_pairs(acc_ref[...])

    @pl.when((tv_ref[j] == 0) & (k == last))
    def _():
        ys_ref[...] = jnp.zeros_like(ys_ref)


def _experts(xs, w1, w3, w2, rt, tile, kc):
    rows, wp = xs.shape
    ne, d, dff = w1.shape
    fc = dff // kc
    kcol = lambda j, k, tv: jnp.where(tv[j] == 1, k, kc - 1)
    return pl.pallas_call(
        functools.partial(_expert_kernel, sub=512),
        grid_spec=pltpu.PrefetchScalarGridSpec(
            num_scalar_prefetch=4,
            grid=(rt['n_tiles'], kc),
            in_specs=[pl.BlockSpec((tile, wp), lambda j, k, te, tv, tsrc, tr: (tsrc[j], 0)),
                      pl.BlockSpec((1, d, fc), lambda j, k, te, tv, tsrc, tr: (te[j], 0, kcol(j, k, tv))),
                      pl.BlockSpec((1, d, fc), lambda j, k, te, tv, tsrc, tr: (te[j], 0, kcol(j, k, tv))),
                      pl.BlockSpec((1, fc, d), lambda j, k, te, tv, tsrc, tr: (te[j], kcol(j, k, tv), 0))],
            out_specs=pl.BlockSpec((tile, wp), lambda j, k, te, tv, tsrc, tr: (j, 0)),
            scratch_shapes=[pltpu.VMEM((tile, d), BF16), pltpu.VMEM((tile, d), F32)]),
        out_shape=jax.ShapeDtypeStruct((rows, wp), jnp.int32),
        compiler_params=_cparams("arbitrary", "arbitrary"),
        name="experts",
    )(rt['tile_expert'], rt['tile_valid'], rt['tile_src'], rt['tile_rows'], xs, w1, w3, w2)


def _combine_kernel(a_ref, b_ref, gate_ref, x_ref, mod_ref, fg_ref, o_ref, *, final_norm):
    m = mod_ref[0]
    g = gate_ref[...]
    y = g[:, 0:1] * _unpack_bf16_pairs(a_ref[...]) + g[:, 1:2] * _unpack_bf16_pairs(b_ref[...])
    xn = x_ref[...] + m[5:6] * y
    if final_norm:
        ms = jnp.mean(xn * xn, axis=-1, keepdims=True)
        xn = (xn * lax.rsqrt(ms + EPS)) * fg_ref[...]
    o_ref[...] = xn


def _combine(a, b, gates, x, mod, fg, seq, tm, final_norm):
    n, d = x.shape
    tps = seq // tm
    row = lambda w: pl.BlockSpec((tm, w), lambda i: (i, 0))
    return pl.pallas_call(
        functools.partial(_combine_kernel, final_norm=final_norm),
        grid=(n // tm,),
        in_specs=[row(d // 2), row(d // 2), row(TOP_K), row(d),
                  pl.BlockSpec((1, 6, d), lambda i: (i // tps, 0, 0)),
                  pl.BlockSpec((1, d), lambda i: (0, 0))],
        out_specs=row(d),
        out_shape=jax.ShapeDtypeStruct((n, d), F32),
        compiler_params=_cparams("arbitrary"),
        name="combine",
    )(a, b, gates, x, mod, fg)


def _pick(n, prefs):
    for p in prefs:
        if n % p == 0:
            return p
    return n


def kernel(x, c, ada_w, ada_b, norm_g, ssm_in, ssm_log_dt, ssm_lam_re, ssm_lam_im,
           ssm_b_re, ssm_b_im, ssm_c_re, ssm_c_im, ssm_d, ssm_glu, ssm_out,
           pool_in, pool_mix, pool_scale, pool_out, ffn_w1, ffn_w3, ffn_w2,
           router_w, router_b, moe_w1, moe_w3, moe_w2, final_g):
    bsz, seq, d = x.shape
    depth = ada_w.shape[0]
    n = bsz * seq
    ne = router_w.shape[-1]
    g_, p_, h_ = ssm_b_re.shape[1:]
    t = SSM_T
    tb = 128 * t
    if seq % tb != 0 or bsz != 8:
        raise NotImplementedError("state-space kernels assume batch 8 and seq % 2048 == 0")
    if depth % 2 != 0:
        raise NotImplementedError("the final RMSNorm is fused into an expert layer's combine")
    tm = _pick(seq, (512, 256, 128, 64, 32, 16))
    tile = _pick(TOP_K * n, (512, 256, 128, 64, 32, 16))
    kc = 2 if moe_w1.shape[-1] % 256 == 0 else 1
    gb = _pick(g_, (4, 2, 1))

    mod = _ada(c, ada_w, ada_b).reshape(depth, bsz, 6, d)
    xf = x.reshape(n, d)
    fg = final_g.reshape(1, d)
    for i in range(depth):
        j = i // 2
        mod_i = mod[i]
        g_a = norm_g[i, 0].reshape(1, d)
        g_b = norm_g[i, 1].reshape(1, d)
        if i % 2 == 0:
            ops = _ssm_operators(ssm_log_dt[j], ssm_lam_re[j], ssm_lam_im[j], ssm_b_re[j],
                                 ssm_b_im[j], ssm_c_re[j], ssm_c_im[j], ssm_d[j], t)
            ut = _ssm_in(xf, mod_i, g_a, ssm_in[j].astype(BF16), seq, t, h_, tb, tm)
            yt = _ssm(ut, *ops, bsz, gb)
            xf = _glu_out(yt, xf, mod_i, ssm_glu[j].astype(BF16), ssm_out[j].astype(BF16),
                          seq, t, tb, tm)
            xf = _ffn(xf, mod_i, g_b, ffn_w1[j].astype(BF16), ffn_w3[j].astype(BF16),
                      ffn_w2[j].astype(BF16), seq, tm)
        else:
            u = _in_proj(xf, mod_i, g_a, pool_in[j].astype(BF16), seq, tm, F32)
            xf = _pool(u, xf, mod_i, pool_mix[j].astype(BF16), pool_scale[j].reshape(1, d),
                       pool_out[j].astype(BF16), seq, tm)
            h, idx, gates = _router(xf, mod_i, g_b, router_w[j], router_b[j].reshape(1, ne),
                                    seq, tm)
            rt = _routing_tables(idx, ne, tile)
            pos0, pos1 = rt['pos'][:, 0], rt['pos'][:, 1]
            xs = _dispatch(h, pos0, pos1, rt['n_tiles'] * tile)
            ys = _experts(xs, moe_w1[j].astype(BF16), moe_w3[j].astype(BF16),
                          moe_w2[j].astype(BF16), rt, tile, kc)
            ya, yb = _gather_pairs(ys, pos0, pos1)
            xf = _combine(ya, yb, gates, xf, mod_i, fg, seq, tm,
                          final_norm=(i == depth - 1))
    return xf.reshape(bsz, seq, d)
```

```python
import functools
import math

import jax
import jax.numpy as jnp
from jax import lax
from jax.experimental import pallas as pl
from jax.experimental.pallas import tpu as pltpu
from jax.experimental.pallas import tpu_sc as plsc

F32 = jnp.float32
BF16 = jnp.bfloat16
EPS = 1e-6
POOL_WINDOWS = (2, 4, 8, 16)
POOL_HALO = 16
SSM_T = 16
LANES = 128
TOP_K = 2
SC_CORES, SC_SUBCORES = 2, 16
SC_WORKERS = SC_CORES * SC_SUBCORES
SC_ROWS = 64
VMEM_LIMIT = 56 * 1024 * 1024


def _cparams(*sem):
    return pltpu.CompilerParams(dimension_semantics=sem, vmem_limit_bytes=VMEM_LIMIT)


def _normmod(x, g, shift, scale):
    ms = jnp.mean(x * x, axis=-1, keepdims=True)
    y = x * lax.rsqrt(ms + EPS)
    return (y * g) * (1.0 + scale) + shift


def _bdot(a, b):
    return jnp.dot(a, b, preferred_element_type=F32)


def _ada_kernel(c_ref, w_ref, b_ref, o_ref):
    c = c_ref[...]
    cond = c * jax.nn.sigmoid(c)
    o_ref[0] = jnp.dot(cond, w_ref[0], precision=lax.Precision.HIGHEST,
                       preferred_element_type=F32) + b_ref[0]


def _ada(c, ada_w, ada_b):
    depth, d, d6 = ada_w.shape
    bsz = c.shape[0]
    tn = d6 // 4
    return pl.pallas_call(
        _ada_kernel,
        grid=(depth, d6 // tn),
        in_specs=[pl.BlockSpec((bsz, d), lambda l, j: (0, 0)),
                  pl.BlockSpec((1, d, tn), lambda l, j: (l, 0, j)),
                  pl.BlockSpec((1, 1, tn), lambda l, j: (l, 0, j))],
        out_specs=pl.BlockSpec((1, bsz, tn), lambda l, j: (l, 0, j)),
        out_shape=jax.ShapeDtypeStruct((depth, bsz, d6), F32),
        compiler_params=_cparams("arbitrary", "arbitrary"),
        name="ada",
    )(c, ada_w, ada_b.reshape(depth, 1, d6))


def _ssm_operators(log_dt, lam_re, lam_im, b_re, b_im, c_re, c_im, d_skip, t):
    g_, p_, h_ = b_re.shape
    dt = jnp.exp(log_dt.astype(F32))[:, None]
    lr = lam_re.astype(F32)
    li = lam_im.astype(F32)
    mag = jnp.exp(lr * dt)
    a_re = mag * jnp.cos(li * dt)
    a_im = mag * jnp.sin(li * dt)
    den = lr * lr + li * li
    nr = a_re - 1
    coef_re = (nr * lr + a_im * li) / den
    coef_im = (a_im * lr - nr * li) / den
    br = b_re.astype(F32)
    bi = b_im.astype(F32)
    bbar_re = coef_re[..., None] * br - coef_im[..., None] * bi
    bbar_im = coef_re[..., None] * bi + coef_im[..., None] * br
    steps = jnp.arange(0, t + 1, dtype=F32)[:, None, None]
    pmag = jnp.exp(lr * dt * steps)
    pw_re = pmag * jnp.cos(li * dt * steps)
    pw_im = pmag * jnp.sin(li * dt * steps)
    cr = c_re.astype(F32)
    ci = c_im.astype(F32)
    er = cr[None] * pw_re[:, :, None, :] - ci[None] * pw_im[:, :, None, :]
    ei = cr[None] * pw_im[:, :, None, :] + ci[None] * pw_re[:, :, None, :]
    er_l = jnp.transpose(er[:t], (0, 2, 3, 1))[..., None]
    ei_l = jnp.transpose(ei[:t], (0, 2, 3, 1))[..., None]
    br_l = jnp.transpose(bbar_re, (1, 0, 2))[None, None]
    bi_l = jnp.transpose(bbar_im, (1, 0, 2))[None, None]
    kk = jnp.sum(er_l * br_l - ei_l * bi_l, axis=2)
    kk = jnp.transpose(kk, (2, 0, 1, 3))
    cols = [jnp.pad(kk[:, :t - j], ((0, 0), (j, 0), (0, 0), (0, 0))) for j in range(t)]
    mt = jnp.stack(cols, axis=3).reshape(g_, t * h_, t * h_)
    rev_re = jnp.transpose(pw_re[:t][::-1], (1, 0, 2))[:, :, None, :]
    rev_im = jnp.transpose(pw_im[:t][::-1], (1, 0, 2))[:, :, None, :]
    bt_re = jnp.transpose(bbar_re, (0, 2, 1))[:, None]
    bt_im = jnp.transpose(bbar_im, (0, 2, 1))[:, None]
    w_re = (rev_re * bt_re - rev_im * bt_im).reshape(g_, t * h_, p_)
    w_im = (rev_re * bt_im + rev_im * bt_re).reshape(g_, t * h_, p_)
    w4 = jnp.concatenate([w_re, w_im, w_im, w_re], axis=-1)
    vt_re = jnp.transpose(er[1:], (1, 0, 2, 3)).reshape(g_, t * h_, p_)
    vt_im = -jnp.transpose(ei[1:], (1, 0, 2, 3)).reshape(g_, t * h_, p_)
    vt = jnp.concatenate([vt_re, vt_im], axis=-1)
    at_re, at_im = pw_re[t], pw_im[t]
    c1 = jnp.concatenate([at_re, at_re], axis=-1)
    c2 = jnp.concatenate([-at_im, at_im], axis=-1)
    coef = jnp.zeros((g_, 8, 2 * p_), F32).at[:, 0].set(c1).at[:, 1].set(c2)
    dk = jnp.tile(d_skip.astype(F32)[:, None, :], (1, t, 1)).reshape(g_, t * h_, 1)
    return mt.astype(BF16), w4.astype(BF16), vt.astype(BF16), coef, dk


def _chunk_rows(c, b, bsz, t):
    return pl.ds(pl.multiple_of((c * bsz + b) * t, t), t)


def _ssm_in_kernel(x_ref, mod_ref, g_ref, w_ref, ut_ref, u_scr, *, t, bsz):
    q = pl.program_id(1)
    nb, cs, _ = x_ref.shape
    nlt = u_scr.shape[0]
    for bi in range(nb):
        m = mod_ref[bi]
        h = _normmod(x_ref[bi], g_ref[...], m[0:1], m[1:2])
        u = _bdot(h.astype(BF16), w_ref[...])
        for c in range(cs // t):
            rows = _chunk_rows(c, q * nb + bi, bsz, t)
            for l in range(nlt):
                u_scr[l, rows, :] = u[c * t:(c + 1) * t, l * LANES:(l + 1) * LANES]

    @pl.when(q == pl.num_programs(1) - 1)
    def _():
        ng, th, nch = ut_ref.shape
        hh = th // t
        gl = LANES // hh
        for l in range(nlt):
            for k in range(t):
                uk = u_scr[l, pl.ds(k, nch, stride=t), :]
                ut_ref[l * gl:(l + 1) * gl, k * hh:(k + 1) * hh, :] = (
                    uk.T.reshape(gl, hh, nch).astype(BF16))


def _ssm_in(x, mod, g, w, t, hh, nb):
    bsz, seq, d = x.shape
    ng = d // hh
    cs = LANES * t // bsz
    return pl.pallas_call(
        functools.partial(_ssm_in_kernel, t=t, bsz=bsz),
        grid=(seq // cs, bsz // nb),
        in_specs=[pl.BlockSpec((nb, cs, d), lambda o, q: (q, o, 0)),
                  pl.BlockSpec((nb, 6, d), lambda o, q: (q, 0, 0)),
                  pl.BlockSpec((1, d), lambda o, q: (0, 0)),
                  pl.BlockSpec((d, d), lambda o, q: (0, 0))],
        out_specs=pl.BlockSpec((ng, t * hh, LANES), lambda o, q: (0, 0, o)),
        out_shape=jax.ShapeDtypeStruct((ng, t * hh, bsz * seq // t), BF16),
        scratch_shapes=[pltpu.VMEM((d // LANES, LANES * t, LANES), F32)],
        compiler_params=_cparams("arbitrary", "arbitrary"),
        name="ssm_in",
    )(x, mod, g, w)


def _ssm_kernel(ut_ref, mt_ref, w4_ref, vt_ref, coef_ref, dk_ref, yt_ref, z_scr, s_scr,
                *, gb, nchunks, bsz):
    p2 = s_scr.shape[-1]
    for gi in range(gb):
        u = ut_ref[gi].astype(F32).T.astype(BF16)
        z = _bdot(u, w4_ref[gi])
        z_scr[gi, 0] = z[:, :p2]
        z_scr[gi, 1] = z[:, p2:]
    c1 = [jnp.broadcast_to(coef_ref[gi, 0:1, :], (bsz, p2)) for gi in range(gb)]
    c2 = [jnp.broadcast_to(coef_ref[gi, 1:2, :], (bsz, p2)) for gi in range(gb)]

    def body(c, carry):
        new = []
        for gi in range(gb):
            s1, s2 = carry[gi]
            rows = pl.ds(pl.multiple_of(c * bsz, bsz), bsz)
            s_scr[gi, rows, :] = s1
            n1 = c1[gi] * s1 + c2[gi] * s2 + z_scr[gi, 0, rows, :]
            n2 = c1[gi] * s2 - c2[gi] * s1 + z_scr[gi, 1, rows, :]
            new.append((n1, n2))
        return tuple(new)

    zero = jnp.zeros((bsz, p2), F32)
    lax.fori_loop(0, nchunks, body, tuple((zero, zero) for _ in range(gb)))
    for gi in range(gb):
        ut = ut_ref[gi]
        st = _bdot(vt_ref[gi], s_scr[gi].T.astype(BF16))
        yt = _bdot(mt_ref[gi], ut) + st + dk_ref[gi] * ut.astype(F32)
        yt_ref[gi] = yt.astype(yt_ref.dtype)


def _ssm(ut, mt, w4t, vt, coef, dk, bsz, gb):
    g_, th, nch = ut.shape
    p2 = vt.shape[2]
    kern = functools.partial(_ssm_kernel, gb=gb, nchunks=nch // bsz, bsz=bsz)
    spec3 = lambda a, b: pl.BlockSpec((gb, a, b), lambda i: (i, 0, 0))
    return pl.pallas_call(
        kern,
        grid=(g_ // gb,),
        in_specs=[spec3(th, nch), spec3(th, th), spec3(th, 2 * p2), spec3(th, p2),
                  spec3(8, p2), spec3(th, 1)],
        out_specs=spec3(th, nch),
        out_shape=jax.ShapeDtypeStruct((g_, th, nch), BF16),
        scratch_shapes=[pltpu.VMEM((gb, 2, nch, p2), F32), pltpu.VMEM((gb, nch, p2), F32)],
        compiler_params=_cparams("arbitrary"),
        name="ssm",
    )(ut, mt, w4t, vt, coef, dk)


def _gelu_tanh(y):
    return y * (0.5 * (1.0 + jnp.tanh(math.sqrt(2.0 / math.pi) * (y + 0.044715 * (y * y * y)))))


def _glu_out_kernel(yt_ref, x_ref, mod_ref, wg_ref, wo_ref, o_ref, y_scr, yb_scr, *, t, bsz):
    q = pl.program_id(1)
    nb, cs, _ = x_ref.shape
    nlt = y_scr.shape[0]

    @pl.when(q == 0)
    def _():
        ng, th, nch = yt_ref.shape
        hh = th // t
        gl = LANES // hh
        for l in range(nlt):
            for k in range(t):
                yk = yt_ref[l * gl:(l + 1) * gl, k * hh:(k + 1) * hh, :].astype(F32)
                y_scr[l, pl.ds(k, nch, stride=t), :] = yk.reshape(LANES, nch).T

    for bi in range(nb):
        for c in range(cs // t):
            rows = _chunk_rows(c, q * nb + bi, bsz, t)
            for l in range(nlt):
                yb_scr[bi * cs + c * t:bi * cs + (c + 1) * t, l * LANES:(l + 1) * LANES] = (
                    y_scr[l, rows, :])
    z = _gelu_tanh(yb_scr[...])
    z = z * jax.nn.sigmoid(_bdot(z.astype(BF16), wg_ref[...]))
    o = _bdot(z.astype(BF16), wo_ref[...])
    for bi in range(nb):
        o_ref[bi] = x_ref[bi] + mod_ref[bi][2:3] * o[bi * cs:(bi + 1) * cs]


def _glu_out(yt, x, mod, wg, wo, t, nb):
    bsz, seq, d = x.shape
    ng, th, _ = yt.shape
    cs = LANES * t // bsz
    return pl.pallas_call(
        functools.partial(_glu_out_kernel, t=t, bsz=bsz),
        grid=(seq // cs, bsz // nb),
        in_specs=[pl.BlockSpec((ng, th, LANES), lambda o, q: (0, 0, o)),
                  pl.BlockSpec((nb, cs, d), lambda o, q: (q, o, 0)),
                  pl.BlockSpec((nb, 6, d), lambda o, q: (q, 0, 0)),
                  pl.BlockSpec((d, d), lambda o, q: (0, 0)),
                  pl.BlockSpec((d, d), lambda o, q: (0, 0))],
        out_specs=pl.BlockSpec((nb, cs, d), lambda o, q: (q, o, 0)),
        out_shape=jax.ShapeDtypeStruct((bsz, seq, d), F32),
        scratch_shapes=[pltpu.VMEM((d // LANES, LANES * t, LANES), F32),
                        pltpu.VMEM((nb * cs, d), F32)],
        compiler_params=_cparams("arbitrary", "arbitrary"),
        name="glu_out",
    )(yt, x, mod, wg, wo)


def _chunks(total, step):
    return [(s, min(step, total - s)) for s in range(0, total, step)]


def _swiglu_acc(h, w1_ref, w3_ref, w2_ref, acc_ref, sub):
    for s, n in _chunks(w1_ref.shape[-1], sub):
        a = _bdot(h, w1_ref[:, s:s + n])
        b = _bdot(h, w3_ref[:, s:s + n])
        act = (a * jax.nn.sigmoid(a) * b).astype(BF16)
        acc_ref[...] += _bdot(act, w2_ref[s:s + n, :])


def _ffn_kernel(x_ref, mod_ref, g_ref, w1_ref, w3_ref, w2_ref, o_ref, acc_ref, *, sub):
    m = mod_ref[0]
    x = x_ref[...]
    h = _normmod(x, g_ref[...], m[3:4], m[4:5]).astype(BF16)
    acc_ref[...] = jnp.zeros_like(acc_ref)
    _swiglu_acc(h, w1_ref, w3_ref, w2_ref, acc_ref, sub)
    o_ref[...] = x + m[5:6] * acc_ref[...]


def _ffn(x, mod, g, w1, w3, w2, seq, tm):
    n, d = x.shape
    dff = w1.shape[1]
    tps = seq // tm
    const = lambda i: (0, 0)
    return pl.pallas_call(
        functools.partial(_ffn_kernel, sub=512),
        grid=(n // tm,),
        in_specs=[pl.BlockSpec((tm, d), lambda i: (i, 0)),
                  pl.BlockSpec((1, 6, d), lambda i: (i // tps, 0, 0)),
                  pl.BlockSpec((1, d), const),
                  pl.BlockSpec((d, dff), const, pipeline_mode=pl.Buffered(1)),
                  pl.BlockSpec((d, dff), const, pipeline_mode=pl.Buffered(1)),
                  pl.BlockSpec((dff, d), const, pipeline_mode=pl.Buffered(1))],
        out_specs=pl.BlockSpec((tm, d), lambda i: (i, 0)),
        out_shape=jax.ShapeDtypeStruct((n, d), F32),
        scratch_shapes=[pltpu.VMEM((tm, d), F32)],
        compiler_params=_cparams("arbitrary"),
        name="ffn",
    )(x, mod, g, w1, w3, w2)


def _pool_kernel(x_ref, halo_ref, mod_ref, g_ref, win_ref, wmix_ref, scale_ref, wo_ref, o_ref,
                 ext_ref, z_ref, *, tps):
    tm, d = x_ref.shape
    pg = d // len(POOL_WINDOWS)
    m = mod_ref[0]
    it = pl.program_id(0) % tps

    def project(rows):
        h = _normmod(rows, g_ref[...], m[0:1], m[1:2])
        return _bdot(h.astype(BF16), win_ref[...])

    ext_ref[0:POOL_HALO, :] = jnp.where(it == 0, 0.0, project(halo_ref[...]))
    ext_ref[POOL_HALO:, :] = project(x_ref[...])
    tpos = (it * tm + 1 + lax.broadcasted_iota(jnp.int32, (tm, 1), 0)).astype(F32)
    for gi, w in enumerate(POOL_WINDOWS):
        cols = slice(gi * pg, (gi + 1) * pg)
        s = ext_ref[:, cols]
        span = 1
        while span < w:
            s = s + pltpu.roll(s, span, 0)
            span *= 2
        mean = s[POOL_HALO:, :] * (1.0 / jnp.minimum(tpos, float(w)))
        dlt = (mean - ext_ref[POOL_HALO:, cols]).astype(BF16)
        z_ref[:, cols] = (_bdot(dlt, wmix_ref[gi]) * scale_ref[:, cols]).astype(BF16)
    o_ref[...] = x_ref[...] + m[2:3] * _bdot(z_ref[...], wo_ref[...])


def _pool(x, mod, g, win, wmix, scale, wo, seq, tm):
    n, d = x.shape
    tps = seq // tm
    hb = tm // POOL_HALO
    ng, pg, _ = wmix.shape
    return pl.pallas_call(
        functools.partial(_pool_kernel, tps=tps),
        grid=(n // tm,),
        in_specs=[pl.BlockSpec((tm, d), lambda i: (i, 0)),
                  pl.BlockSpec((POOL_HALO, d), lambda i: (jnp.maximum(i * hb - 1, 0), 0)),
                  pl.BlockSpec((1, 6, d), lambda i: (i // tps, 0, 0)),
                  pl.BlockSpec((1, d), lambda i: (0, 0)),
                  pl.BlockSpec((d, d), lambda i: (0, 0)),
                  pl.BlockSpec((ng, pg, pg), lambda i: (0, 0, 0)),
                  pl.BlockSpec((1, d), lambda i: (0, 0)),
                  pl.BlockSpec((d, d), lambda i: (0, 0))],
        out_specs=pl.BlockSpec((tm, d), lambda i: (i, 0)),
        out_shape=jax.ShapeDtypeStruct((n, d), F32),
        scratch_shapes=[pltpu.VMEM((tm + POOL_HALO, d), F32), pltpu.VMEM((tm, d), BF16)],
        compiler_params=_cparams("arbitrary"),
        name="pool",
    )(x, x, mod, g, win, wmix, scale, wo)


def _pack_bf16_pairs(v):
    w = v.shape[1] // 2
    bits = lambda a: lax.bitcast_convert_type(a.astype(BF16).astype(F32), jnp.int32)
    return bits(v[:, :w]) | lax.shift_right_logical(bits(v[:, w:]), 16)


def _unpack_bf16_pairs(p):
    hi = lax.bitcast_convert_type(p & jnp.int32(-65536), F32)
    lo = lax.bitcast_convert_type(lax.shift_left(p, 16), F32)
    return jnp.concatenate([hi, lo], axis=1)


def _router_kernel(x_ref, mod_ref, g_ref, rw_ref, rb_ref, h_ref, idx_ref, gate_ref):
    m = mod_ref[0]
    h = _normmod(x_ref[...], g_ref[...], m[3:4], m[4:5])
    h_ref[...] = _pack_bf16_pairs(h)
    nexp = rb_ref.shape[1]
    h_hi = h.astype(BF16)
    h_lo = (h - h_hi.astype(F32)).astype(BF16)
    p_hi = _bdot(h_hi, rw_ref[...])
    logits = (p_hi[:, :nexp] + p_hi[:, nexp:] + _bdot(h_lo, rw_ref[:, :nexp])) + rb_ref[...]
    ne = float(logits.shape[1])
    lane = lax.broadcasted_iota(jnp.int32, logits.shape, 1).astype(F32)
    v1 = jnp.max(logits, axis=1, keepdims=True)
    i1 = jnp.min(jnp.where(logits == v1, lane, ne), axis=1, keepdims=True)
    rest = jnp.where(lane == i1, -jnp.inf, logits)
    v2 = jnp.max(rest, axis=1, keepdims=True)
    i2 = jnp.min(jnp.where(rest == v2, lane, ne), axis=1, keepdims=True)
    e2 = jnp.exp(v2 - v1)
    den = 1.0 + e2
    two = lax.broadcasted_iota(jnp.int32, idx_ref.shape, 1)
    idx_ref[...] = jnp.where(two == 0, i1, i2).astype(jnp.int32)
    gate_ref[...] = jnp.where(two == 0, 1.0 / den, e2 / den)


def _router(x, mod, g, rw, rb, seq, tm):
    n, d = x.shape
    ne = rw.shape[1]
    tps = seq // tm
    rw_hi = rw.astype(BF16)
    rw_lo = (rw - rw_hi.astype(F32)).astype(BF16)
    rw = jnp.concatenate([rw_hi, rw_lo], axis=1)
    return pl.pallas_call(
        _router_kernel,
        grid=(n // tm,),
        in_specs=[pl.BlockSpec((tm, d), lambda i: (i, 0)),
                  pl.BlockSpec((1, 6, d), lambda i: (i // tps, 0, 0)),
                  pl.BlockSpec((1, d), lambda i: (0, 0)),
                  pl.BlockSpec((d, 2 * ne), lambda i: (0, 0)),
                  pl.BlockSpec((1, ne), lambda i: (0, 0))],
        out_specs=[pl.BlockSpec((tm, d // 2), lambda i: (i, 0)),
                   pl.BlockSpec((tm, TOP_K), lambda i: (i, 0)),
                   pl.BlockSpec((tm, TOP_K), lambda i: (i, 0))],
        out_shape=[jax.ShapeDtypeStruct((n, d // 2), jnp.int32),
                   jax.ShapeDtypeStruct((n, TOP_K), jnp.int32),
                   jax.ShapeDtypeStruct((n, TOP_K), F32)],
        compiler_params=_cparams("arbitrary"),
        name="router",
    )(x, mod, g, rw, rb)


def _count_le(sorted_vals, queries):
    return jnp.sum((sorted_vals[None, :] <= queries[:, None]).astype(jnp.int32), axis=1)


def _routing_tables(idx, ne, tile):
    n = idx.shape[0]
    e_flat = idx.reshape(-1)
    onehot = (e_flat[:, None] == jnp.arange(ne, dtype=jnp.int32)[None, :]).astype(jnp.int32)
    csum = jnp.cumsum(onehot, axis=0)
    cnt = csum[-1]
    rank = jnp.take_along_axis(csum, e_flat[:, None], axis=1)[:, 0] - 1
    padded = ((cnt + tile - 1) // tile) * tile
    off = jnp.cumsum(padded) - padded
    pos = (off[e_flat] + rank).astype(jnp.int32).reshape(n, TOP_K)
    n_tiles = (TOP_K * n) // tile + ne
    used = jnp.sum(padded) // tile
    tile_ids = jnp.arange(n_tiles, dtype=jnp.int32)
    tile_expert = _count_le(jnp.cumsum(padded), tile_ids * tile)
    last_used = jnp.maximum(used - 1, 0)
    tile_valid = (tile_ids < used).astype(jnp.int32)
    tile_src = jnp.minimum(tile_ids, last_used).astype(jnp.int32)
    tile_expert = jnp.minimum(tile_expert, ne - 1).astype(jnp.int32)
    tile_expert = jnp.where(tile_valid == 1, tile_expert, tile_expert[last_used])
    tile_rows = jnp.clip(cnt[tile_expert] - (tile_ids * tile - off[tile_expert]), 0, tile)
    tile_rows = jnp.where(tile_valid == 1, tile_rows, 0).astype(jnp.int32)
    return dict(pos=pos, tile_expert=tile_expert, tile_valid=tile_valid, tile_src=tile_src,
                tile_rows=tile_rows, n_tiles=n_tiles)


def _sc_mesh():
    return plsc.VectorSubcoreMesh(core_axis_name="c", subcore_axis_name="s",
                                  num_cores=SC_CORES, num_subcores=SC_SUBCORES)


def _dispatch(hp, pos0, pos1, rows_out):
    n, w = hp.shape
    per_w = n // SC_WORKERS
    ch = SC_ROWS
    nch = per_w // ch
    assert per_w * SC_WORKERS == n and nch * ch == per_w and nch % 2 == 0

    @functools.partial(
        pl.kernel, mesh=_sc_mesh(),
        out_type=jax.ShapeDtypeStruct((rows_out, w), hp.dtype),
        scratch_types=[pltpu.VMEM((per_w,), jnp.int32), pltpu.VMEM((per_w,), jnp.int32),
                       pltpu.VMEM((ch, w), hp.dtype), pltpu.VMEM((ch, w), hp.dtype)]
        + [pltpu.SemaphoreType.DMA] * 4,
        name="dispatch",
    )
    def k(hp_hbm, p0_hbm, p1_hbm, out_hbm, i0_v, i1_v, r0_v, r1_v, l0, l1, s0, s1):
        base = (lax.axis_index("s") * SC_CORES + lax.axis_index("c")) * per_w
        pltpu.sync_copy(p0_hbm.at[pl.ds(base, per_w)], i0_v)
        pltpu.sync_copy(p1_hbm.at[pl.ds(base, per_w)], i1_v)
        bufs, lsem, ssem = (r0_v, r1_v), (l0, l1), (s0, s1)

        def load(j, b):
            return pltpu.make_async_copy(hp_hbm.at[pl.ds(base + j * ch, ch)], bufs[b], lsem[b])

        def scatter(j, b, idx):
            return pltpu.make_async_copy(bufs[b], out_hbm.at[idx.at[pl.ds(j * ch, ch)]], ssem[b])

        load(0, 0).start()

        @pl.loop(0, nch, step=2)
        def _(j):
            for b in range(2):
                jj = j + b
                load(jj, b).wait()

                @pl.when(jj + 1 < nch)
                def _():
                    load(jj + 1, 1 - b).start()

                c0 = scatter(jj, b, i0_v)
                c1 = scatter(jj, b, i1_v)
                c0.start()
                c1.start()
                c0.wait()
                c1.wait()

    return k(hp, pos0, pos1)


def _gather_pairs(ys, pos0, pos1):
    n = pos0.shape[0]
    w = ys.shape[1]
    per_w = n // SC_WORKERS
    ch = SC_ROWS
    nch = per_w // ch
    assert per_w * SC_WORKERS == n and nch * ch == per_w
    out = jax.ShapeDtypeStruct((n, w), ys.dtype)

    @functools.partial(
        pl.kernel, mesh=_sc_mesh(), out_type=(out, out),
        scratch_types=[pltpu.VMEM((per_w,), jnp.int32), pltpu.VMEM((per_w,), jnp.int32),
                       pltpu.VMEM((ch, w), ys.dtype), pltpu.VMEM((ch, w), ys.dtype)]
        + [pltpu.SemaphoreType.DMA] * 4,
        name="gather_pairs",
    )
    def k(ys_hbm, p0_hbm, p1_hbm, a_hbm, b_hbm, i0_v, i1_v, ra_v, rb_v, ga, gb, wa, wb):
        base = (lax.axis_index("s") * SC_CORES + lax.axis_index("c")) * per_w
        pltpu.sync_copy(p0_hbm.at[pl.ds(base, per_w)], i0_v)
        pltpu.sync_copy(p1_hbm.at[pl.ds(base, per_w)], i1_v)

        def gather(j, idx, buf, sem):
            return pltpu.make_async_copy(ys_hbm.at[idx.at[pl.ds(j * ch, ch)]], buf, sem)

        def write(j, buf, dst, sem):
            return pltpu.make_async_copy(buf, dst.at[pl.ds(base + j * ch, ch)], sem)

        gather(0, i0_v, ra_v, ga).start()
        gather(0, i1_v, rb_v, gb).start()

        @pl.loop(0, nch)
        def _(j):
            gather(j, i0_v, ra_v, ga).wait()
            write(j, ra_v, a_hbm, wa).start()
            gather(j, i1_v, rb_v, gb).wait()
            write(j, rb_v, b_hbm, wb).start()
            write(j, ra_v, a_hbm, wa).wait()

            @pl.when(j + 1 < nch)
            def _():
                gather(j + 1, i0_v, ra_v, ga).start()

            write(j, rb_v, b_hbm, wb).wait()

            @pl.when(j + 1 < nch)
            def _():
                gather(j + 1, i1_v, rb_v, gb).start()

    return k(ys, pos0, pos1)


def _expert_kernel(te_ref, tv_ref, tsrc_ref, tr_ref, xs_ref, w1_ref, w3_ref, w2_ref,
                   ys_ref, x_scr, acc_ref, *, sub):
    j = pl.program_id(0)
    k = pl.program_id(1)
    last = pl.num_programs(1) - 1

    @pl.when(tv_ref[j] == 1)
    def _():
        @pl.when(k == 0)
        def _():
            row = lax.broadcasted_iota(jnp.int32, x_scr.shape, 0)
            x = _unpack_bf16_pairs(xs_ref[...])
            x_scr[...] = jnp.where(row < tr_ref[j], x, 0.0).astype(BF16)
            acc_ref[...] = jnp.zeros_like(acc_ref)

        _swiglu_acc(x_scr[...], w1_ref.at[0], w3_ref.at[0], w2_ref.at[0], acc_ref, sub)

        @pl.when(k == last)
        def _():
            ys_ref[...] = _pack_bf16_pairs(acc_ref[...])

    @pl.when((tv_ref[j] == 0) & (k == last))
    def _():
        ys_ref[...] = jnp.zeros_like(ys_ref)


def _experts(xs, w1, w3, w2, rt, tile, kc):
    rows, wp = xs.shape
    ne, d, dff = w1.shape
    fc = dff // kc
    kcol = lambda j, k, tv: jnp.where(tv[j] == 1, k, kc - 1)
    return pl.pallas_call(
        functools.partial(_expert_kernel, sub=512),
        grid_spec=pltpu.PrefetchScalarGridSpec(
            num_scalar_prefetch=4,
            grid=(rt['n_tiles'], kc),
            in_specs=[pl.BlockSpec((tile, wp), lambda j, k, te, tv, tsrc, tr: (tsrc[j], 0)),
                      pl.BlockSpec((1, d, fc), lambda j, k, te, tv, tsrc, tr: (te[j], 0, kcol(j, k, tv))),
                      pl.BlockSpec((1, d, fc), lambda j, k, te, tv, tsrc, tr: (te[j], 0, kcol(j, k, tv))),
                      pl.BlockSpec((1, fc, d), lambda j, k, te, tv, tsrc, tr: (te[j], kcol(j, k, tv), 0))],
            out_specs=pl.BlockSpec((tile, wp), lambda j, k, te, tv, tsrc, tr: (j, 0)),
            scratch_shapes=[pltpu.VMEM((tile, d), BF16), pltpu.VMEM((tile, d), F32)]),
        out_shape=jax.ShapeDtypeStruct((rows, wp), jnp.int32),
        compiler_params=_cparams("arbitrary", "arbitrary"),
        name="experts",
    )(rt['tile_expert'], rt['tile_valid'], rt['tile_src'], rt['tile_rows'], xs, w1, w3, w2)


def _combine_kernel(a_ref, b_ref, gate_ref, x_ref, mod_ref, fg_ref, o_ref, *, final_norm):
    m = mod_ref[0]
    g = gate_ref[...]
    y = g[:, 0:1] * _unpack_bf16_pairs(a_ref[...]) + g[:, 1:2] * _unpack_bf16_pairs(b_ref[...])
    xn = x_ref[...] + m[5:6] * y
    if final_norm:
        ms = jnp.mean(xn * xn, axis=-1, keepdims=True)
        xn = (xn * lax.rsqrt(ms + EPS)) * fg_ref[...]
    o_ref[...] = xn


def _combine(a, b, gates, x, mod, fg, seq, tm, final_norm):
    n, d = x.shape
    tps = seq // tm
    row = lambda w: pl.BlockSpec((tm, w), lambda i: (i, 0))
    return pl.pallas_call(
        functools.partial(_combine_kernel, final_norm=final_norm),
        grid=(n // tm,),
        in_specs=[row(d // 2), row(d // 2), row(TOP_K), row(d),
                  pl.BlockSpec((1, 6, d), lambda i: (i // tps, 0, 0)),
                  pl.BlockSpec((1, d), lambda i: (0, 0))],
        out_specs=row(d),
        out_shape=jax.ShapeDtypeStruct((n, d), F32),
        compiler_params=_cparams("arbitrary"),
        name="combine",
    )(a, b, gates, x, mod, fg)


def _pick(n, prefs):
    for p in prefs:
        if n % p == 0:
            return p
    return n


def kernel(x, c, ada_w, ada_b, norm_g, ssm_in, ssm_log_dt, ssm_lam_re, ssm_lam_im,
           ssm_b_re, ssm_b_im, ssm_c_re, ssm_c_im, ssm_d, ssm_glu, ssm_out,
           pool_in, pool_mix, pool_scale, pool_out, ffn_w1, ffn_w3, ffn_w2,
           router_w, router_b, moe_w1, moe_w3, moe_w2, final_g):
    bsz, seq, d = x.shape
    depth = ada_w.shape[0]
    n = bsz * seq
    ne = router_w.shape[-1]
    g_, p_, h_ = ssm_b_re.shape[1:]
    t = SSM_T
    if seq % (LANES * t // bsz) != 0 or bsz != 8:
        raise NotImplementedError("state-space kernels assume batch 8 and seq % 256 == 0")
    if depth % 2 != 0:
        raise NotImplementedError("the final RMSNorm is fused into an expert layer's combine")
    tm = _pick(seq, (512, 256, 128, 64, 32, 16))
    tile = _pick(TOP_K * n, (512, 256, 128, 64, 32, 16))
    kc = 2 if moe_w1.shape[-1] % 256 == 0 else 1
    gb = _pick(g_, (4, 2, 1))

    stack = lambda w: w.astype(BF16).reshape((-1,) + w.shape[2:])
    ew1, ew3, ew2 = stack(moe_w1), stack(moe_w3), stack(moe_w2)
    mod = _ada(c, ada_w, ada_b).reshape(depth, bsz, 6, d)
    xf = x.reshape(n, d)
    fg = final_g.reshape(1, d)
    for i in range(depth):
        j = i // 2
        mod_i = mod[i]
        g_a = norm_g[i, 0].reshape(1, d)
        g_b = norm_g[i, 1].reshape(1, d)
        if i % 2 == 0:
            ops = _ssm_operators(ssm_log_dt[j], ssm_lam_re[j], ssm_lam_im[j], ssm_b_re[j],
                                 ssm_b_im[j], ssm_c_re[j], ssm_c_im[j], ssm_d[j], t)
            ut = _ssm_in(xf.reshape(bsz, seq, d), mod_i, g_a, ssm_in[j].astype(BF16), t, h_, 2)
            yt = _ssm(ut, *ops, bsz, gb)
            xf = _glu_out(yt, xf.reshape(bsz, seq, d), mod_i, ssm_glu[j].astype(BF16),
                          ssm_out[j].astype(BF16), t, 2).reshape(n, d)
            xf = _ffn(xf, mod_i, g_b, ffn_w1[j].astype(BF16), ffn_w3[j].astype(BF16),
                      ffn_w2[j].astype(BF16), seq, tm)
        else:
            xf = _pool(xf, mod_i, g_a, pool_in[j].astype(BF16), pool_mix[j].astype(BF16),
                       pool_scale[j].reshape(1, d), pool_out[j].astype(BF16), seq, tm)
            h, idx, gates = _router(xf, mod_i, g_b, router_w[j], router_b[j].reshape(1, ne),
                                    seq, tm)
            rt = _routing_tables(idx, ne, tile)
            pos0, pos1 = rt['pos'][:, 0], rt['pos'][:, 1]
            xs = _dispatch(h, pos0, pos1, rt['n_tiles'] * tile)
            rt['tile_expert'] = rt['tile_expert'] + j * ne
            ys = _experts(xs, ew1, ew3, ew2, rt, tile, kc)
            ya, yb = _gather_pairs(ys, pos0, pos1)
            xf = _combine(ya, yb, gates, xf, mod_i, fg, seq, tm,
                          final_norm=(i == depth - 1))
    return xf.reshape(bsz, seq, d)
```

```python
import functools
import math

import jax
import jax.numpy as jnp
from jax import lax
from jax.experimental import pallas as pl
from jax.experimental.pallas import tpu as pltpu
from jax.experimental.pallas import tpu_sc as plsc

F32 = jnp.float32
BF16 = jnp.bfloat16
EPS = 1e-6
POOL_WINDOWS = (2, 4, 8, 16)
POOL_HALO = 16
SSM_T = 16
LANES = 128
TOP_K = 2
SC_CORES, SC_SUBCORES = 2, 16
SC_WORKERS = SC_CORES * SC_SUBCORES
SC_ROWS = 64
COMBINE_PIECES = 4
VMEM_LIMIT = 56 * 1024 * 1024


def _cparams(*sem):
    return pltpu.CompilerParams(dimension_semantics=sem, vmem_limit_bytes=VMEM_LIMIT)


def _normmod(x, g, shift, scale):
    ms = jnp.mean(x * x, axis=-1, keepdims=True)
    y = x * lax.rsqrt(ms + EPS)
    return (y * g) * (1.0 + scale) + shift


def _bdot(a, b):
    return jnp.dot(a, b, preferred_element_type=F32)


def _ada_kernel(c_ref, w_ref, b_ref, o_ref):
    c = c_ref[...]
    cond = c * jax.nn.sigmoid(c)
    o_ref[0] = jnp.dot(cond, w_ref[0], precision=lax.Precision.HIGHEST,
                       preferred_element_type=F32) + b_ref[0]


def _ada(c, ada_w, ada_b):
    depth, d, d6 = ada_w.shape
    bsz = c.shape[0]
    tn = d6 // 4
    return pl.pallas_call(
        _ada_kernel,
        grid=(depth, d6 // tn),
        in_specs=[pl.BlockSpec((bsz, d), lambda l, j: (0, 0)),
                  pl.BlockSpec((1, d, tn), lambda l, j: (l, 0, j)),
                  pl.BlockSpec((1, 1, tn), lambda l, j: (l, 0, j))],
        out_specs=pl.BlockSpec((1, bsz, tn), lambda l, j: (l, 0, j)),
        out_shape=jax.ShapeDtypeStruct((depth, bsz, d6), F32),
        compiler_params=_cparams("arbitrary", "arbitrary"),
        name="ada",
    )(c, ada_w, ada_b.reshape(depth, 1, d6))


def _ssm_operators(log_dt, lam_re, lam_im, b_re, b_im, c_re, c_im, d_skip, t):
    g_, p_, h_ = b_re.shape
    dt = jnp.exp(log_dt.astype(F32))[:, None]
    lr = lam_re.astype(F32)
    li = lam_im.astype(F32)
    mag = jnp.exp(lr * dt)
    a_re = mag * jnp.cos(li * dt)
    a_im = mag * jnp.sin(li * dt)
    den = lr * lr + li * li
    nr = a_re - 1
    coef_re = (nr * lr + a_im * li) / den
    coef_im = (a_im * lr - nr * li) / den
    br = b_re.astype(F32)
    bi = b_im.astype(F32)
    bbar_re = coef_re[..., None] * br - coef_im[..., None] * bi
    bbar_im = coef_re[..., None] * bi + coef_im[..., None] * br
    steps = jnp.arange(0, t + 1, dtype=F32)[:, None, None]
    pmag = jnp.exp(lr * dt * steps)
    pw_re = pmag * jnp.cos(li * dt * steps)
    pw_im = pmag * jnp.sin(li * dt * steps)
    cr = c_re.astype(F32)
    ci = c_im.astype(F32)
    er = cr[None] * pw_re[:, :, None, :] - ci[None] * pw_im[:, :, None, :]
    ei = cr[None] * pw_im[:, :, None, :] + ci[None] * pw_re[:, :, None, :]
    er_l = jnp.transpose(er[:t], (0, 2, 3, 1))[..., None]
    ei_l = jnp.transpose(ei[:t], (0, 2, 3, 1))[..., None]
    br_l = jnp.transpose(bbar_re, (1, 0, 2))[None, None]
    bi_l = jnp.transpose(bbar_im, (1, 0, 2))[None, None]
    kk = jnp.sum(er_l * br_l - ei_l * bi_l, axis=2)
    kk = jnp.transpose(kk, (2, 0, 1, 3))
    cols = [jnp.pad(kk[:, :t - j], ((0, 0), (j, 0), (0, 0), (0, 0))) for j in range(t)]
    mt = jnp.stack(cols, axis=3).reshape(g_, t * h_, t * h_)
    rev_re = jnp.transpose(pw_re[:t][::-1], (1, 0, 2))[:, :, None, :]
    rev_im = jnp.transpose(pw_im[:t][::-1], (1, 0, 2))[:, :, None, :]
    bt_re = jnp.transpose(bbar_re, (0, 2, 1))[:, None]
    bt_im = jnp.transpose(bbar_im, (0, 2, 1))[:, None]
    w_re = (rev_re * bt_re - rev_im * bt_im).reshape(g_, t * h_, p_)
    w_im = (rev_re * bt_im + rev_im * bt_re).reshape(g_, t * h_, p_)
    w4 = jnp.concatenate([w_re, w_im, w_im, w_re], axis=-1)
    vt_re = jnp.transpose(er[1:], (1, 0, 2, 3)).reshape(g_, t * h_, p_)
    vt_im = -jnp.transpose(ei[1:], (1, 0, 2, 3)).reshape(g_, t * h_, p_)
    vt = jnp.concatenate([vt_re, vt_im], axis=-1)
    at_re, at_im = pw_re[t], pw_im[t]
    c1 = jnp.concatenate([at_re, at_re], axis=-1)
    c2 = jnp.concatenate([-at_im, at_im], axis=-1)
    coef = jnp.zeros((g_, 8, 2 * p_), F32).at[:, 0].set(c1).at[:, 1].set(c2)
    dk = jnp.tile(d_skip.astype(F32)[:, None, :], (1, t, 1)).reshape(g_, t * h_, 1)
    return mt.astype(BF16), w4.astype(BF16), vt.astype(BF16), coef, dk


def _chunk_rows(c, b, bsz, t):
    return pl.ds(pl.multiple_of((c * bsz + b) * t, t), t)


def _ssm_in_kernel(x_ref, mod_ref, g_ref, w_ref, ut_ref, u_scr, *, t, bsz):
    q = pl.program_id(1)
    nb, cs, _ = x_ref.shape
    nlt = u_scr.shape[0]
    for bi in range(nb):
        m = mod_ref[bi]
        h = _normmod(x_ref[bi], g_ref[...], m[0:1], m[1:2])
        u = _bdot(h.astype(BF16), w_ref[...])
        for c in range(cs // t):
            rows = _chunk_rows(c, q * nb + bi, bsz, t)
            for l in range(nlt):
                u_scr[l, rows, :] = u[c * t:(c + 1) * t, l * LANES:(l + 1) * LANES]

    @pl.when(q == pl.num_programs(1) - 1)
    def _():
        ng, th, nch = ut_ref.shape
        hh = th // t
        gl = LANES // hh
        for l in range(nlt):
            for k in range(t):
                uk = u_scr[l, pl.ds(k, nch, stride=t), :]
                ut_ref[l * gl:(l + 1) * gl, k * hh:(k + 1) * hh, :] = (
                    uk.T.reshape(gl, hh, nch).astype(BF16))


def _ssm_in(x, mod, g, w, t, hh, nb):
    bsz, seq, d = x.shape
    ng = d // hh
    cs = LANES * t // bsz
    return pl.pallas_call(
        functools.partial(_ssm_in_kernel, t=t, bsz=bsz),
        grid=(seq // cs, bsz // nb),
        in_specs=[pl.BlockSpec((nb, cs, d), lambda o, q: (q, o, 0)),
                  pl.BlockSpec((nb, 6, d), lambda o, q: (q, 0, 0)),
                  pl.BlockSpec((1, d), lambda o, q: (0, 0)),
                  pl.BlockSpec((d, d), lambda o, q: (0, 0))],
        out_specs=pl.BlockSpec((ng, t * hh, LANES), lambda o, q: (0, 0, o)),
        out_shape=jax.ShapeDtypeStruct((ng, t * hh, bsz * seq // t), BF16),
        scratch_shapes=[pltpu.VMEM((d // LANES, LANES * t, LANES), F32)],
        compiler_params=_cparams("arbitrary", "arbitrary"),
        name="ssm_in",
    )(x, mod, g, w)


def _ssm_kernel(ut_ref, mt_ref, w4_ref, vt_ref, coef_ref, dk_ref, yt_ref, z_scr, s_scr,
                *, gb, nchunks, bsz):
    p2 = s_scr.shape[-1]
    for gi in range(gb):
        u = ut_ref[gi].astype(F32).T.astype(BF16)
        z = _bdot(u, w4_ref[gi])
        z_scr[gi, 0] = z[:, :p2]
        z_scr[gi, 1] = z[:, p2:]
    c1 = [jnp.broadcast_to(coef_ref[gi, 0:1, :], (bsz, p2)) for gi in range(gb)]
    c2 = [jnp.broadcast_to(coef_ref[gi, 1:2, :], (bsz, p2)) for gi in range(gb)]

    def body(c, carry):
        new = []
        for gi in range(gb):
            s1, s2 = carry[gi]
            rows = pl.ds(pl.multiple_of(c * bsz, bsz), bsz)
            s_scr[gi, rows, :] = s1
            n1 = c1[gi] * s1 + c2[gi] * s2 + z_scr[gi, 0, rows, :]
            n2 = c1[gi] * s2 - c2[gi] * s1 + z_scr[gi, 1, rows, :]
            new.append((n1, n2))
        return tuple(new)

    zero = jnp.zeros((bsz, p2), F32)
    lax.fori_loop(0, nchunks, body, tuple((zero, zero) for _ in range(gb)))
    for gi in range(gb):
        ut = ut_ref[gi]
        st = _bdot(vt_ref[gi], s_scr[gi].T.astype(BF16))
        yt = _bdot(mt_ref[gi], ut) + st + dk_ref[gi] * ut.astype(F32)
        yt_ref[gi] = yt.astype(yt_ref.dtype)


def _ssm(ut, mt, w4t, vt, coef, dk, bsz, gb):
    g_, th, nch = ut.shape
    p2 = vt.shape[2]
    kern = functools.partial(_ssm_kernel, gb=gb, nchunks=nch // bsz, bsz=bsz)
    spec3 = lambda a, b: pl.BlockSpec((gb, a, b), lambda i: (i, 0, 0))
    return pl.pallas_call(
        kern,
        grid=(g_ // gb,),
        in_specs=[spec3(th, nch), spec3(th, th), spec3(th, 2 * p2), spec3(th, p2),
                  spec3(8, p2), spec3(th, 1)],
        out_specs=spec3(th, nch),
        out_shape=jax.ShapeDtypeStruct((g_, th, nch), BF16),
        scratch_shapes=[pltpu.VMEM((gb, 2, nch, p2), F32), pltpu.VMEM((gb, nch, p2), F32)],
        compiler_params=_cparams("arbitrary"),
        name="ssm",
    )(ut, mt, w4t, vt, coef, dk)


def _gelu_tanh(y):
    return y * (0.5 * (1.0 + jnp.tanh(math.sqrt(2.0 / math.pi) * (y + 0.044715 * (y * y * y)))))


def _glu_out_kernel(yt_ref, x_ref, mod_ref, wg_ref, wo_ref, o_ref, y_scr, yb_scr, *, t, bsz):
    q = pl.program_id(1)
    nb, cs, _ = x_ref.shape
    nlt = y_scr.shape[0]

    @pl.when(q == 0)
    def _():
        ng, th, nch = yt_ref.shape
        hh = th // t
        gl = LANES // hh
        for l in range(nlt):
            for k in range(t):
                yk = yt_ref[l * gl:(l + 1) * gl, k * hh:(k + 1) * hh, :].astype(F32)
                y_scr[l, pl.ds(k, nch, stride=t), :] = yk.reshape(LANES, nch).T

    for bi in range(nb):
        for c in range(cs // t):
            rows = _chunk_rows(c, q * nb + bi, bsz, t)
            for l in range(nlt):
                yb_scr[bi * cs + c * t:bi * cs + (c + 1) * t, l * LANES:(l + 1) * LANES] = (
                    y_scr[l, rows, :])
    z = _gelu_tanh(yb_scr[...])
    z = z * jax.nn.sigmoid(_bdot(z.astype(BF16), wg_ref[...]))
    o = _bdot(z.astype(BF16), wo_ref[...])
    for bi in range(nb):
        o_ref[bi] = x_ref[bi] + mod_ref[bi][2:3] * o[bi * cs:(bi + 1) * cs]


def _glu_out(yt, x, mod, wg, wo, t, nb):
    bsz, seq, d = x.shape
    ng, th, _ = yt.shape
    cs = LANES * t // bsz
    return pl.pallas_call(
        functools.partial(_glu_out_kernel, t=t, bsz=bsz),
        grid=(seq // cs, bsz // nb),
        in_specs=[pl.BlockSpec((ng, th, LANES), lambda o, q: (0, 0, o)),
                  pl.BlockSpec((nb, cs, d), lambda o, q: (q, o, 0)),
                  pl.BlockSpec((nb, 6, d), lambda o, q: (q, 0, 0)),
                  pl.BlockSpec((d, d), lambda o, q: (0, 0)),
                  pl.BlockSpec((d, d), lambda o, q: (0, 0))],
        out_specs=pl.BlockSpec((nb, cs, d), lambda o, q: (q, o, 0)),
        out_shape=jax.ShapeDtypeStruct((bsz, seq, d), F32),
        scratch_shapes=[pltpu.VMEM((d // LANES, LANES * t, LANES), F32),
                        pltpu.VMEM((nb * cs, d), F32)],
        compiler_params=_cparams("arbitrary", "arbitrary"),
        name="glu_out",
    )(yt, x, mod, wg, wo)


def _chunks(total, step):
    return [(s, min(step, total - s)) for s in range(0, total, step)]


def _swiglu_acc(h, w1_ref, w3_ref, w2_ref, acc_ref, sub):
    for s, n in _chunks(w1_ref.shape[-1], sub):
        a = _bdot(h, w1_ref[:, s:s + n])
        b = _bdot(h, w3_ref[:, s:s + n])
        act = (a * jax.nn.sigmoid(a) * b).astype(BF16)
        acc_ref[...] += _bdot(act, w2_ref[s:s + n, :])


def _ffn_kernel(x_ref, mod_ref, g_ref, w1_ref, w3_ref, w2_ref, o_ref, acc_ref, *, sub):
    m = mod_ref[0]
    x = x_ref[...]
    h = _normmod(x, g_ref[...], m[3:4], m[4:5]).astype(BF16)
    acc_ref[...] = jnp.zeros_like(acc_ref)
    _swiglu_acc(h, w1_ref, w3_ref, w2_ref, acc_ref, sub)
    o_ref[...] = x + m[5:6] * acc_ref[...]


def _ffn(x, mod, g, w1, w3, w2, seq, tm):
    n, d = x.shape
    dff = w1.shape[1]
    tps = seq // tm
    const = lambda i: (0, 0)
    return pl.pallas_call(
        functools.partial(_ffn_kernel, sub=512),
        grid=(n // tm,),
        in_specs=[pl.BlockSpec((tm, d), lambda i: (i, 0)),
                  pl.BlockSpec((1, 6, d), lambda i: (i // tps, 0, 0)),
                  pl.BlockSpec((1, d), const),
                  pl.BlockSpec((d, dff), const, pipeline_mode=pl.Buffered(1)),
                  pl.BlockSpec((d, dff), const, pipeline_mode=pl.Buffered(1)),
                  pl.BlockSpec((dff, d), const, pipeline_mode=pl.Buffered(1))],
        out_specs=pl.BlockSpec((tm, d), lambda i: (i, 0)),
        out_shape=jax.ShapeDtypeStruct((n, d), F32),
        scratch_shapes=[pltpu.VMEM((tm, d), F32)],
        compiler_params=_cparams("arbitrary"),
        name="ffn",
    )(x, mod, g, w1, w3, w2)


def _pool_kernel(x_ref, halo_ref, mod_ref, g_ref, win_ref, wmix_ref, scale_ref, wo_ref, o_ref,
                 ext_ref, z_ref, *, tps):
    tm, d = x_ref.shape
    pg = d // len(POOL_WINDOWS)
    m = mod_ref[0]
    it = pl.program_id(0) % tps

    def project(rows):
        h = _normmod(rows, g_ref[...], m[0:1], m[1:2])
        return _bdot(h.astype(BF16), win_ref[...])

    ext_ref[0:POOL_HALO, :] = jnp.where(it == 0, 0.0, project(halo_ref[...]))
    ext_ref[POOL_HALO:, :] = project(x_ref[...])
    tpos = (it * tm + 1 + lax.broadcasted_iota(jnp.int32, (tm, 1), 0)).astype(F32)
    for gi, w in enumerate(POOL_WINDOWS):
        cols = slice(gi * pg, (gi + 1) * pg)
        s = ext_ref[:, cols]
        span = 1
        while span < w:
            s = s + pltpu.roll(s, span, 0)
            span *= 2
        mean = s[POOL_HALO:, :] * (1.0 / jnp.minimum(tpos, float(w)))
        dlt = (mean - ext_ref[POOL_HALO:, cols]).astype(BF16)
        z_ref[:, cols] = (_bdot(dlt, wmix_ref[gi]) * scale_ref[:, cols]).astype(BF16)
    o_ref[...] = x_ref[...] + m[2:3] * _bdot(z_ref[...], wo_ref[...])


def _pool(x, mod, g, win, wmix, scale, wo, seq, tm):
    n, d = x.shape
    tps = seq // tm
    hb = tm // POOL_HALO
    ng, pg, _ = wmix.shape
    return pl.pallas_call(
        functools.partial(_pool_kernel, tps=tps),
        grid=(n // tm,),
        in_specs=[pl.BlockSpec((tm, d), lambda i: (i, 0)),
                  pl.BlockSpec((POOL_HALO, d), lambda i: (jnp.maximum(i * hb - 1, 0), 0)),
                  pl.BlockSpec((1, 6, d), lambda i: (i // tps, 0, 0)),
                  pl.BlockSpec((1, d), lambda i: (0, 0)),
                  pl.BlockSpec((d, d), lambda i: (0, 0)),
                  pl.BlockSpec((ng, pg, pg), lambda i: (0, 0, 0)),
                  pl.BlockSpec((1, d), lambda i: (0, 0)),
                  pl.BlockSpec((d, d), lambda i: (0, 0))],
        out_specs=pl.BlockSpec((tm, d), lambda i: (i, 0)),
        out_shape=jax.ShapeDtypeStruct((n, d), F32),
        scratch_shapes=[pltpu.VMEM((tm + POOL_HALO, d), F32), pltpu.VMEM((tm, d), BF16)],
        compiler_params=_cparams("arbitrary"),
        name="pool",
    )(x, x, mod, g, win, wmix, scale, wo)


def _pack_bf16_pairs(v):
    w = v.shape[1] // 2
    bits = lambda a: lax.bitcast_convert_type(a.astype(BF16).astype(F32), jnp.int32)
    return bits(v[:, :w]) | lax.shift_right_logical(bits(v[:, w:]), 16)


def _unpack_bf16_pairs(p):
    hi = lax.bitcast_convert_type(p & jnp.int32(-65536), F32)
    lo = lax.bitcast_convert_type(lax.shift_left(p, 16), F32)
    return jnp.concatenate([hi, lo], axis=1)


def _router_kernel(x_ref, mod_ref, g_ref, rw_ref, rb_ref, h_ref, idx_ref, gate_ref, rank_ref,
                   cnt_ref, base_scr):
    @pl.when(pl.program_id(0) == 0)
    def _():
        base_scr[...] = jnp.zeros_like(base_scr)

    m = mod_ref[0]
    h = _normmod(x_ref[...], g_ref[...], m[3:4], m[4:5])
    h_ref[...] = _pack_bf16_pairs(h)
    split = lambda a: (a.astype(BF16), (a - a.astype(BF16).astype(F32)).astype(BF16))
    h_hi, h_lo = split(h)
    w_hi, w_lo = split(rw_ref[...])
    logits = (_bdot(h_hi, w_hi) + _bdot(h_hi, w_lo) + _bdot(h_lo, w_hi)) + rb_ref[...]
    ne = float(logits.shape[1])
    lane = lax.broadcasted_iota(jnp.int32, logits.shape, 1).astype(F32)
    v1 = jnp.max(logits, axis=1, keepdims=True)
    i1 = jnp.min(jnp.where(logits == v1, lane, ne), axis=1, keepdims=True)
    rest = jnp.where(lane == i1, -jnp.inf, logits)
    v2 = jnp.max(rest, axis=1, keepdims=True)
    i2 = jnp.min(jnp.where(rest == v2, lane, ne), axis=1, keepdims=True)
    e2 = jnp.exp(v2 - v1)
    den = 1.0 + e2
    two = lax.broadcasted_iota(jnp.int32, idx_ref.shape, 1)
    idx_ref[...] = jnp.where(two == 0, i1, i2).astype(jnp.int32)
    gate_ref[...] = jnp.where(two == 0, 1.0 / den, e2 / den)
    tm = logits.shape[0]
    oh1 = jnp.where(lane == i1, 1.0, 0.0)
    oh2 = jnp.where(lane == i2, 1.0, 0.0)
    both = oh1 + oh2
    before = (lax.broadcasted_iota(jnp.int32, (tm, tm), 1)
              < lax.broadcasted_iota(jnp.int32, (tm, tm), 0))
    prior = _bdot(jnp.where(before, 1.0, 0.0).astype(BF16), both.astype(BF16)) + base_scr[...]
    r1 = jnp.sum(oh1 * prior, axis=1, keepdims=True)
    r2 = jnp.sum(oh2 * prior, axis=1, keepdims=True)
    rank_ref[...] = jnp.where(two == 0, r1, r2).astype(jnp.int32)
    total = base_scr[...] + jnp.sum(both, axis=0, keepdims=True)
    base_scr[...] = total
    cnt_ref[...] = total.astype(jnp.int32)


def _router(x, mod, g, rw, rb, seq, tm):
    n, d = x.shape
    ne = rw.shape[1]
    tps = seq // tm
    return pl.pallas_call(
        _router_kernel,
        grid=(n // tm,),
        in_specs=[pl.BlockSpec((tm, d), lambda i: (i, 0)),
                  pl.BlockSpec((1, 6, d), lambda i: (i // tps, 0, 0)),
                  pl.BlockSpec((1, d), lambda i: (0, 0)),
                  pl.BlockSpec((d, ne), lambda i: (0, 0)),
                  pl.BlockSpec((1, ne), lambda i: (0, 0))],
        out_specs=[pl.BlockSpec((tm, d // 2), lambda i: (i, 0)),
                   pl.BlockSpec((tm, TOP_K), lambda i: (i, 0)),
                   pl.BlockSpec((tm, TOP_K), lambda i: (i, 0)),
                   pl.BlockSpec((tm, TOP_K), lambda i: (i, 0)),
                   pl.BlockSpec((1, ne), lambda i: (0, 0))],
        out_shape=[jax.ShapeDtypeStruct((n, d // 2), jnp.int32),
                   jax.ShapeDtypeStruct((n, TOP_K), jnp.int32),
                   jax.ShapeDtypeStruct((n, TOP_K), F32),
                   jax.ShapeDtypeStruct((n, TOP_K), jnp.int32),
                   jax.ShapeDtypeStruct((1, ne), jnp.int32)],
        scratch_shapes=[pltpu.VMEM((1, ne), F32)],
        compiler_params=_cparams("arbitrary"),
        name="router",
    )(x, mod, g, rw, rb)


def _count_le(sorted_vals, queries):
    return jnp.sum((sorted_vals[None, :] <= queries[:, None]).astype(jnp.int32), axis=1)


def _routing_tables(idx, rank, cnt, ne, tile):
    n = idx.shape[0]
    padded = ((cnt + tile - 1) // tile) * tile
    off = jnp.cumsum(padded) - padded
    onehot = idx[:, :, None] == jnp.arange(ne, dtype=jnp.int32)[None, None, :]
    pos = rank + jnp.sum(jnp.where(onehot, off[None, None, :], 0), axis=2)
    n_tiles = (TOP_K * n) // tile + ne
    used = jnp.sum(padded) // tile
    tile_ids = jnp.arange(n_tiles, dtype=jnp.int32)
    tile_expert = _count_le(jnp.cumsum(padded), tile_ids * tile)
    last_used = jnp.maximum(used - 1, 0)
    tile_valid = (tile_ids < used).astype(jnp.int32)
    tile_src = jnp.minimum(tile_ids, last_used).astype(jnp.int32)
    tile_expert = jnp.minimum(tile_expert, ne - 1).astype(jnp.int32)
    tile_expert = jnp.where(tile_valid == 1, tile_expert, tile_expert[last_used])
    tile_rows = jnp.clip(cnt[tile_expert] - (tile_ids * tile - off[tile_expert]), 0, tile)
    tile_rows = jnp.where(tile_valid == 1, tile_rows, 0).astype(jnp.int32)
    return dict(pos=pos, tile_expert=tile_expert, tile_valid=tile_valid, tile_src=tile_src,
                tile_rows=tile_rows, n_tiles=n_tiles)


def _sc_mesh():
    return plsc.VectorSubcoreMesh(core_axis_name="c", subcore_axis_name="s",
                                  num_cores=SC_CORES, num_subcores=SC_SUBCORES)


def _dispatch(hp, pos0, pos1, rows_out):
    n, w = hp.shape
    per_w = n // SC_WORKERS
    ch = SC_ROWS
    nch = per_w // ch
    assert per_w * SC_WORKERS == n and nch * ch == per_w and nch % 2 == 0

    @functools.partial(
        pl.kernel, mesh=_sc_mesh(),
        out_type=jax.ShapeDtypeStruct((rows_out, w), hp.dtype),
        scratch_types=[pltpu.VMEM((per_w,), jnp.int32), pltpu.VMEM((per_w,), jnp.int32),
                       pltpu.VMEM((ch, w), hp.dtype), pltpu.VMEM((ch, w), hp.dtype)]
        + [pltpu.SemaphoreType.DMA] * 4,
        name="dispatch",
    )
    def k(hp_hbm, p0_hbm, p1_hbm, out_hbm, i0_v, i1_v, r0_v, r1_v, l0, l1, s0, s1):
        base = (lax.axis_index("s") * SC_CORES + lax.axis_index("c")) * per_w
        pltpu.sync_copy(p0_hbm.at[pl.ds(base, per_w)], i0_v)
        pltpu.sync_copy(p1_hbm.at[pl.ds(base, per_w)], i1_v)
        bufs, lsem, ssem = (r0_v, r1_v), (l0, l1), (s0, s1)

        def load(j, b):
            return pltpu.make_async_copy(hp_hbm.at[pl.ds(base + j * ch, ch)], bufs[b], lsem[b])

        def scatter(j, b, idx):
            return pltpu.make_async_copy(bufs[b], out_hbm.at[idx.at[pl.ds(j * ch, ch)]], ssem[b])

        load(0, 0).start()

        @pl.loop(0, nch, step=2)
        def _(j):
            for b in range(2):
                jj = j + b
                load(jj, b).wait()

                @pl.when(jj + 1 < nch)
                def _():
                    load(jj + 1, 1 - b).start()

                c0 = scatter(jj, b, i0_v)
                c1 = scatter(jj, b, i1_v)
                c0.start()
                c1.start()
                c0.wait()
                c1.wait()

    return k(hp, pos0, pos1)


def _gather_pairs(ys, pos0, pos1):
    n = pos0.shape[0]
    w = ys.shape[1]
    per_w = n // SC_WORKERS
    ch = SC_ROWS
    nch = per_w // ch
    assert per_w * SC_WORKERS == n and nch * ch == per_w
    out = jax.ShapeDtypeStruct((n, w), ys.dtype)

    @functools.partial(
        pl.kernel, mesh=_sc_mesh(), out_type=(out, out),
        scratch_types=[pltpu.VMEM((per_w,), jnp.int32), pltpu.VMEM((per_w,), jnp.int32),
                       pltpu.VMEM((ch, w), ys.dtype), pltpu.VMEM((ch, w), ys.dtype)]
        + [pltpu.SemaphoreType.DMA] * 4,
        name="gather_pairs",
    )
    def k(ys_hbm, p0_hbm, p1_hbm, a_hbm, b_hbm, i0_v, i1_v, ra_v, rb_v, ga, gb, wa, wb):
        base = (lax.axis_index("s") * SC_CORES + lax.axis_index("c")) * per_w
        pltpu.sync_copy(p0_hbm.at[pl.ds(base, per_w)], i0_v)
        pltpu.sync_copy(p1_hbm.at[pl.ds(base, per_w)], i1_v)

        def gather(j, idx, buf, sem):
            return pltpu.make_async_copy(ys_hbm.at[idx.at[pl.ds(j * ch, ch)]], buf, sem)

        def write(j, buf, dst, sem):
            return pltpu.make_async_copy(buf, dst.at[pl.ds(base + j * ch, ch)], sem)

        gather(0, i0_v, ra_v, ga).start()
        gather(0, i1_v, rb_v, gb).start()

        @pl.loop(0, nch)
        def _(j):
            gather(j, i0_v, ra_v, ga).wait()
            write(j, ra_v, a_hbm, wa).start()
            gather(j, i1_v, rb_v, gb).wait()
            write(j, rb_v, b_hbm, wb).start()
            write(j, ra_v, a_hbm, wa).wait()

            @pl.when(j + 1 < nch)
            def _():
                gather(j + 1, i0_v, ra_v, ga).start()

            write(j, rb_v, b_hbm, wb).wait()

            @pl.when(j + 1 < nch)
            def _():
                gather(j + 1, i1_v, rb_v, gb).start()

    return k(ys, pos0, pos1)


def _expert_kernel(te_ref, tv_ref, tsrc_ref, tr_ref, xs_ref, w1_ref, w3_ref, w2_ref,
                   ys_ref, x_scr, acc_ref, *, sub):
    j = pl.program_id(0)
    k = pl.program_id(1)
    last = pl.num_programs(1) - 1

    @pl.when(tv_ref[j] == 1)
    def _():
        @pl.when(k == 0)
        def _():
            row = lax.broadcasted_iota(jnp.int32, x_scr.shape, 0)
            x = _unpack_bf16_pairs(xs_ref[...])
            x_scr[...] = jnp.where(row < tr_ref[j], x, 0.0).astype(BF16)
            acc_ref[...] = jnp.zeros_like(acc_ref)

        _swiglu_acc(x_scr[...], w1_ref.at[0], w3_ref.at[0], w2_ref.at[0], acc_ref, sub)

        @pl.when(k == last)
        def _():
            ys_ref[...] = _pack_bf16_pairs(acc_ref[...])

    @pl.when((tv_ref[j] == 0) & (k == last))
    def _():
        ys_ref[...] = jnp.zeros_like(ys_ref)


def _experts(xs, w1, w3, w2, rt, tile, kc):
    rows, wp = xs.shape
    ne, d, dff = w1.shape
    fc = dff // kc
    kcol = lambda j, k, tv: jnp.where(tv[j] == 1, k, kc - 1)
    return pl.pallas_call(
        functools.partial(_expert_kernel, sub=512),
        grid_spec=pltpu.PrefetchScalarGridSpec(
            num_scalar_prefetch=4,
            grid=(rt['n_tiles'], kc),
            in_specs=[pl.BlockSpec((tile, wp), lambda j, k, te, tv, tsrc, tr: (tsrc[j], 0)),
                      pl.BlockSpec((1, d, fc), lambda j, k, te, tv, tsrc, tr: (te[j], 0, kcol(j, k, tv))),
                      pl.BlockSpec((1, d, fc), lambda j, k, te, tv, tsrc, tr: (te[j], 0, kcol(j, k, tv))),
                      pl.BlockSpec((1, fc, d), lambda j, k, te, tv, tsrc, tr: (te[j], kcol(j, k, tv), 0))],
            out_specs=pl.BlockSpec((tile, wp), lambda j, k, te, tv, tsrc, tr: (j, 0)),
            scratch_shapes=[pltpu.VMEM((tile, d), BF16), pltpu.VMEM((tile, d), F32)]),
        out_shape=jax.ShapeDtypeStruct((rows, wp), jnp.int32),
        compiler_params=_cparams("arbitrary", "arbitrary"),
        name="experts",
    )(rt['tile_expert'], rt['tile_valid'], rt['tile_src'], rt['tile_rows'], xs, w1, w3, w2)


def _combine_kernel(a_ref, b_ref, gate_ref, x_ref, mod_ref, fg_ref, o_ref, *, final_norm):
    m = mod_ref[0]
    g = gate_ref[...]
    y = g[:, 0:1] * _unpack_bf16_pairs(a_ref[...]) + g[:, 1:2] * _unpack_bf16_pairs(b_ref[...])
    xn = x_ref[...] + m[5:6] * y
    if final_norm:
        ms = jnp.mean(xn * xn, axis=-1, keepdims=True)
        xn = (xn * lax.rsqrt(ms + EPS)) * fg_ref[...]
    o_ref[...] = xn


def _combine(a, b, gates, x, mod, fg, seq, tm, final_norm, tile0):
    n, d = x.shape
    tps = seq // tm
    part = lambda w: pl.BlockSpec((tm, w), lambda i: (i, 0))
    full = lambda w: pl.BlockSpec((tm, w), lambda i: (i + tile0, 0))
    return pl.pallas_call(
        functools.partial(_combine_kernel, final_norm=final_norm),
        grid=(a.shape[0] // tm,),
        in_specs=[part(d // 2), part(d // 2), full(TOP_K), full(d),
                  pl.BlockSpec((1, 6, d), lambda i: ((i + tile0) // tps, 0, 0)),
                  pl.BlockSpec((1, d), lambda i: (0, 0))],
        out_specs=full(d),
        out_shape=jax.ShapeDtypeStruct((n, d), F32),
        input_output_aliases={3: 0},
        compiler_params=_cparams("arbitrary"),
        name="combine",
    )(a, b, gates, x, mod, fg)


def _pick(n, prefs):
    for p in prefs:
        if n % p == 0:
            return p
    return n


def kernel(x, c, ada_w, ada_b, norm_g, ssm_in, ssm_log_dt, ssm_lam_re, ssm_lam_im,
           ssm_b_re, ssm_b_im, ssm_c_re, ssm_c_im, ssm_d, ssm_glu, ssm_out,
           pool_in, pool_mix, pool_scale, pool_out, ffn_w1, ffn_w3, ffn_w2,
           router_w, router_b, moe_w1, moe_w3, moe_w2, final_g):
    bsz, seq, d = x.shape
    depth = ada_w.shape[0]
    n = bsz * seq
    ne = router_w.shape[-1]
    g_, p_, h_ = ssm_b_re.shape[1:]
    t = SSM_T
    if seq % (LANES * t // bsz) != 0 or bsz != 8:
        raise NotImplementedError("state-space kernels assume batch 8 and seq % 256 == 0")
    if depth % 2 != 0:
        raise NotImplementedError("the final RMSNorm is fused into an expert layer's combine")
    tm = _pick(seq, (512, 256, 128, 64, 32, 16))
    tile = _pick(TOP_K * n, (512, 256, 128, 64, 32, 16))
    kc = 2 if moe_w1.shape[-1] % 256 == 0 else 1
    gb = _pick(g_, (4, 2, 1))

    stack = lambda w: w.astype(BF16).reshape((-1,) + w.shape[2:])
    ew1, ew3, ew2 = stack(moe_w1), stack(moe_w3), stack(moe_w2)
    mod = _ada(c, ada_w, ada_b).reshape(depth, bsz, 6, d)
    xf = x.reshape(n, d)
    fg = final_g.reshape(1, d)
    for i in range(depth):
        j = i // 2
        mod_i = mod[i]
        g_a = norm_g[i, 0].reshape(1, d)
        g_b = norm_g[i, 1].reshape(1, d)
        if i % 2 == 0:
            ops = _ssm_operators(ssm_log_dt[j], ssm_lam_re[j], ssm_lam_im[j], ssm_b_re[j],
                                 ssm_b_im[j], ssm_c_re[j], ssm_c_im[j], ssm_d[j], t)
            ut = _ssm_in(xf.reshape(bsz, seq, d), mod_i, g_a, ssm_in[j].astype(BF16), t, h_, 2)
            yt = _ssm(ut, *ops, bsz, gb)
            xf = _glu_out(yt, xf.reshape(bsz, seq, d), mod_i, ssm_glu[j].astype(BF16),
                          ssm_out[j].astype(BF16), t, 2).reshape(n, d)
            xf = _ffn(xf, mod_i, g_b, ffn_w1[j].astype(BF16), ffn_w3[j].astype(BF16),
                      ffn_w2[j].astype(BF16), seq, tm)
        else:
            xf = _pool(xf, mod_i, g_a, pool_in[j].astype(BF16), pool_mix[j].astype(BF16),
                       pool_scale[j].reshape(1, d), pool_out[j].astype(BF16), seq, tm)
            h, idx, gates, rank, cnt = _router(xf, mod_i, g_b, router_w[j],
                                               router_b[j].reshape(1, ne), seq, tm)
            rt = _routing_tables(idx, rank, cnt[0], ne, tile)
            pos0, pos1 = rt['pos'][:, 0], rt['pos'][:, 1]
            xs = _dispatch(h, pos0, pos1, rt['n_tiles'] * tile)
            rt['tile_expert'] = rt['tile_expert'] + j * ne
            ys = _experts(xs, ew1, ew3, ew2, rt, tile, kc)
            npc = n // COMBINE_PIECES
            for q in range(COMBINE_PIECES):
                piece = slice(q * npc, (q + 1) * npc)
                ya, yb = _gather_pairs(ys, pos0[piece], pos1[piece])
                xf = _combine(ya, yb, gates, xf, mod_i, fg, seq, tm,
                              final_norm=(i == depth - 1), tile0=q * (npc // tm))
    return xf.reshape(bsz, seq, d)
```

```python
import functools
import math

import jax
import jax.numpy as jnp
from jax import lax
from jax.experimental import pallas as pl
from jax.experimental.pallas import tpu as pltpu
from jax.experimental.pallas import tpu_sc as plsc

F32 = jnp.float32
BF16 = jnp.bfloat16
EPS = 1e-6
POOL_WINDOWS = (2, 4, 8, 16)
POOL_HALO = 16
SSM_T = 16
LANES = 128
TOP_K = 2
SC_CORES, SC_SUBCORES = 2, 16
SC_WORKERS = SC_CORES * SC_SUBCORES
SC_ROWS = 64
COMBINE_PIECES = 4
VMEM_LIMIT = 56 * 1024 * 1024


def _cparams(*sem):
    return pltpu.CompilerParams(dimension_semantics=sem, vmem_limit_bytes=VMEM_LIMIT)


def _normmod(x, g, shift, scale):
    ms = jnp.mean(x * x, axis=-1, keepdims=True)
    y = x * lax.rsqrt(ms + EPS)
    return (y * g) * (1.0 + scale) + shift


def _bdot(a, b):
    return jnp.dot(a, b, preferred_element_type=F32)


def _ada_kernel(c_ref, w_ref, b_ref, o_ref):
    c = c_ref[...]
    cond = c * jax.nn.sigmoid(c)
    o_ref[0] = jnp.dot(cond, w_ref[0], precision=lax.Precision.HIGHEST,
                       preferred_element_type=F32) + b_ref[0]


def _ada(c, ada_w, ada_b):
    depth, d, d6 = ada_w.shape
    bsz = c.shape[0]
    tn = d6 // 4
    return pl.pallas_call(
        _ada_kernel,
        grid=(depth, d6 // tn),
        in_specs=[pl.BlockSpec((bsz, d), lambda l, j: (0, 0)),
                  pl.BlockSpec((1, d, tn), lambda l, j: (l, 0, j)),
                  pl.BlockSpec((1, 1, tn), lambda l, j: (l, 0, j))],
        out_specs=pl.BlockSpec((1, bsz, tn), lambda l, j: (l, 0, j)),
        out_shape=jax.ShapeDtypeStruct((depth, bsz, d6), F32),
        compiler_params=_cparams("arbitrary", "arbitrary"),
        name="ada",
    )(c, ada_w, ada_b.reshape(depth, 1, d6))


def _ssm_operators(log_dt, lam_re, lam_im, b_re, b_im, c_re, c_im, d_skip, t):
    g_, p_, h_ = b_re.shape
    dt = jnp.exp(log_dt.astype(F32))[:, None]
    lr = lam_re.astype(F32)
    li = lam_im.astype(F32)
    mag = jnp.exp(lr * dt)
    a_re = mag * jnp.cos(li * dt)
    a_im = mag * jnp.sin(li * dt)
    den = lr * lr + li * li
    nr = a_re - 1
    coef_re = (nr * lr + a_im * li) / den
    coef_im = (a_im * lr - nr * li) / den
    br = b_re.astype(F32)
    bi = b_im.astype(F32)
    bbar_re = coef_re[..., None] * br - coef_im[..., None] * bi
    bbar_im = coef_re[..., None] * bi + coef_im[..., None] * br
    steps = jnp.arange(0, t + 1, dtype=F32)[:, None, None]
    pmag = jnp.exp(lr * dt * steps)
    pw_re = pmag * jnp.cos(li * dt * steps)
    pw_im = pmag * jnp.sin(li * dt * steps)
    cr = c_re.astype(F32)
    ci = c_im.astype(F32)
    er = cr[None] * pw_re[:, :, None, :] - ci[None] * pw_im[:, :, None, :]
    ei = cr[None] * pw_im[:, :, None, :] + ci[None] * pw_re[:, :, None, :]
    er_l = jnp.transpose(er[:t], (0, 2, 3, 1))[..., None]
    ei_l = jnp.transpose(ei[:t], (0, 2, 3, 1))[..., None]
    br_l = jnp.transpose(bbar_re, (1, 0, 2))[None, None]
    bi_l = jnp.transpose(bbar_im, (1, 0, 2))[None, None]
    kk = jnp.sum(er_l * br_l - ei_l * bi_l, axis=2)
    kk = jnp.transpose(kk, (2, 0, 1, 3))
    cols = [jnp.pad(kk[:, :t - j], ((0, 0), (j, 0), (0, 0), (0, 0))) for j in range(t)]
    mt = jnp.stack(cols, axis=3).reshape(g_, t * h_, t * h_)
    rev_re = jnp.transpose(pw_re[:t][::-1], (1, 0, 2))[:, :, None, :]
    rev_im = jnp.transpose(pw_im[:t][::-1], (1, 0, 2))[:, :, None, :]
    bt_re = jnp.transpose(bbar_re, (0, 2, 1))[:, None]
    bt_im = jnp.transpose(bbar_im, (0, 2, 1))[:, None]
    w_re = (rev_re * bt_re - rev_im * bt_im).reshape(g_, t * h_, p_)
    w_im = (rev_re * bt_im + rev_im * bt_re).reshape(g_, t * h_, p_)
    w4 = jnp.concatenate([w_re, w_im, w_im, w_re], axis=-1)
    vt_re = jnp.transpose(er[1:], (1, 0, 2, 3)).reshape(g_, t * h_, p_)
    vt_im = -jnp.transpose(ei[1:], (1, 0, 2, 3)).reshape(g_, t * h_, p_)
    vt = jnp.concatenate([vt_re, vt_im], axis=-1)
    at_re, at_im = pw_re[t], pw_im[t]
    c1 = jnp.concatenate([at_re, at_re], axis=-1)
    c2 = jnp.concatenate([-at_im, at_im], axis=-1)
    coef = jnp.zeros((g_, 8, 2 * p_), F32).at[:, 0].set(c1).at[:, 1].set(c2)
    dk = jnp.tile(d_skip.astype(F32)[:, None, :], (1, t, 1)).reshape(g_, t * h_, 1)
    return mt.astype(BF16), w4.astype(BF16), vt.astype(BF16), coef, dk


def _chunk_rows(c, b, bsz, t):
    return pl.ds(pl.multiple_of((c * bsz + b) * t, t), t)


def _ssm_in_kernel(x_ref, mod_ref, g_ref, w_ref, ut_ref, u_scr, *, t, bsz):
    q = pl.program_id(1)
    nb, cs, _ = x_ref.shape
    nlt = u_scr.shape[0]
    for bi in range(nb):
        m = mod_ref[bi]
        h = _normmod(x_ref[bi], g_ref[...], m[0:1], m[1:2])
        u = _bdot(h.astype(BF16), w_ref[...])
        for c in range(cs // t):
            rows = _chunk_rows(c, q * nb + bi, bsz, t)
            for l in range(nlt):
                u_scr[l, rows, :] = u[c * t:(c + 1) * t, l * LANES:(l + 1) * LANES]

    @pl.when(q == pl.num_programs(1) - 1)
    def _():
        ng, th, nch = ut_ref.shape
        hh = th // t
        gl = LANES // hh
        for l in range(nlt):
            for k in range(t):
                uk = u_scr[l, pl.ds(k, nch, stride=t), :]
                ut_ref[l * gl:(l + 1) * gl, k * hh:(k + 1) * hh, :] = (
                    uk.T.reshape(gl, hh, nch).astype(BF16))


def _ssm_in(x, mod, g, w, t, hh, nb):
    bsz, seq, d = x.shape
    ng = d // hh
    cs = LANES * t // bsz
    return pl.pallas_call(
        functools.partial(_ssm_in_kernel, t=t, bsz=bsz),
        grid=(seq // cs, bsz // nb),
        in_specs=[pl.BlockSpec((nb, cs, d), lambda o, q: (q, o, 0)),
                  pl.BlockSpec((nb, 6, d), lambda o, q: (q, 0, 0)),
                  pl.BlockSpec((1, d), lambda o, q: (0, 0)),
                  pl.BlockSpec((d, d), lambda o, q: (0, 0))],
        out_specs=pl.BlockSpec((ng, t * hh, LANES), lambda o, q: (0, 0, o)),
        out_shape=jax.ShapeDtypeStruct((ng, t * hh, bsz * seq // t), BF16),
        scratch_shapes=[pltpu.VMEM((d // LANES, LANES * t, LANES), F32)],
        compiler_params=_cparams("arbitrary", "arbitrary"),
        name="ssm_in",
    )(x, mod, g, w)


def _ssm_kernel(ut_ref, mt_ref, w4_ref, vt_ref, coef_ref, dk_ref, yt_ref, z_scr, s_scr,
                *, gb, nchunks, bsz):
    p2 = s_scr.shape[-1]
    for gi in range(gb):
        u = ut_ref[gi].astype(F32).T.astype(BF16)
        z = _bdot(u, w4_ref[gi])
        z_scr[gi, 0] = z[:, :p2]
        z_scr[gi, 1] = z[:, p2:]
    c1 = [jnp.broadcast_to(coef_ref[gi, 0:1, :], (bsz, p2)) for gi in range(gb)]
    c2 = [jnp.broadcast_to(coef_ref[gi, 1:2, :], (bsz, p2)) for gi in range(gb)]

    def body(c, carry):
        new = []
        for gi in range(gb):
            s1, s2 = carry[gi]
            rows = pl.ds(pl.multiple_of(c * bsz, bsz), bsz)
            s_scr[gi, rows, :] = s1
            n1 = c1[gi] * s1 + c2[gi] * s2 + z_scr[gi, 0, rows, :]
            n2 = c1[gi] * s2 - c2[gi] * s1 + z_scr[gi, 1, rows, :]
            new.append((n1, n2))
        return tuple(new)

    zero = jnp.zeros((bsz, p2), F32)
    lax.fori_loop(0, nchunks, body, tuple((zero, zero) for _ in range(gb)))
    for gi in range(gb):
        ut = ut_ref[gi]
        st = _bdot(vt_ref[gi], s_scr[gi].T.astype(BF16))
        yt = _bdot(mt_ref[gi], ut) + st + dk_ref[gi] * ut.astype(F32)
        yt_ref[gi] = yt.astype(yt_ref.dtype)


def _ssm(ut, mt, w4, vt, coef, dk, bsz, gb, layer):
    g_, th, nch = ut.shape
    p2 = vt.shape[2]
    kern = functools.partial(_ssm_kernel, gb=gb, nchunks=nch // bsz, bsz=bsz)
    spec3 = lambda a, b: pl.BlockSpec((gb, a, b), lambda i: (i, 0, 0))
    ops3 = lambda a, b: pl.BlockSpec((gb, a, b), lambda i: (i + layer * (g_ // gb), 0, 0))
    return pl.pallas_call(
        kern,
        grid=(g_ // gb,),
        in_specs=[spec3(th, nch), ops3(th, th), ops3(th, 2 * p2), ops3(th, p2),
                  ops3(8, p2), ops3(th, 1)],
        out_specs=spec3(th, nch),
        out_shape=jax.ShapeDtypeStruct((g_, th, nch), BF16),
        scratch_shapes=[pltpu.VMEM((gb, 2, nch, p2), F32), pltpu.VMEM((gb, nch, p2), F32)],
        compiler_params=_cparams("arbitrary"),
        name="ssm",
    )(ut, mt, w4, vt, coef, dk)


def _gelu_tanh(y):
    return y * (0.5 * (1.0 + jnp.tanh(math.sqrt(2.0 / math.pi) * (y + 0.044715 * (y * y * y)))))


def _glu_out_kernel(yt_ref, x_ref, mod_ref, wg_ref, wo_ref, o_ref, y_scr, yb_scr, *, t, bsz):
    q = pl.program_id(1)
    nb, cs, _ = x_ref.shape
    nlt = y_scr.shape[0]

    @pl.when(q == 0)
    def _():
        ng, th, nch = yt_ref.shape
        hh = th // t
        gl = LANES // hh
        for l in range(nlt):
            for k in range(t):
                yk = yt_ref[l * gl:(l + 1) * gl, k * hh:(k + 1) * hh, :].astype(F32)
                y_scr[l, pl.ds(k, nch, stride=t), :] = yk.reshape(LANES, nch).T

    for bi in range(nb):
        for c in range(cs // t):
            rows = _chunk_rows(c, q * nb + bi, bsz, t)
            for l in range(nlt):
                yb_scr[bi * cs + c * t:bi * cs + (c + 1) * t, l * LANES:(l + 1) * LANES] = (
                    y_scr[l, rows, :])
    z = _gelu_tanh(yb_scr[...])
    z = z * jax.nn.sigmoid(_bdot(z.astype(BF16), wg_ref[...]))
    o = _bdot(z.astype(BF16), wo_ref[...])
    for bi in range(nb):
        o_ref[bi] = x_ref[bi] + mod_ref[bi][2:3] * o[bi * cs:(bi + 1) * cs]


def _glu_out(yt, x, mod, wg, wo, t, nb):
    bsz, seq, d = x.shape
    ng, th, _ = yt.shape
    cs = LANES * t // bsz
    return pl.pallas_call(
        functools.partial(_glu_out_kernel, t=t, bsz=bsz),
        grid=(seq // cs, bsz // nb),
        in_specs=[pl.BlockSpec((ng, th, LANES), lambda o, q: (0, 0, o)),
                  pl.BlockSpec((nb, cs, d), lambda o, q: (q, o, 0)),
                  pl.BlockSpec((nb, 6, d), lambda o, q: (q, 0, 0)),
                  pl.BlockSpec((d, d), lambda o, q: (0, 0)),
                  pl.BlockSpec((d, d), lambda o, q: (0, 0))],
        out_specs=pl.BlockSpec((nb, cs, d), lambda o, q: (q, o, 0)),
        out_shape=jax.ShapeDtypeStruct((bsz, seq, d), F32),
        scratch_shapes=[pltpu.VMEM((d // LANES, LANES * t, LANES), F32),
                        pltpu.VMEM((nb * cs, d), F32)],
        compiler_params=_cparams("arbitrary", "arbitrary"),
        name="glu_out",
    )(yt, x, mod, wg, wo)


def _chunks(total, step):
    return [(s, min(step, total - s)) for s in range(0, total, step)]


def _swiglu(h, w1_ref, w3_ref, w2_ref, sub):
    total = None
    for s, n in _chunks(w1_ref.shape[-1], sub):
        a = _bdot(h, w1_ref[:, s:s + n])
        b = _bdot(h, w3_ref[:, s:s + n])
        act = (a * jax.nn.sigmoid(a) * b).astype(BF16)
        y = _bdot(act, w2_ref[s:s + n, :])
        total = y if total is None else total + y
    return total


def _ffn_kernel(x_ref, mod_ref, g_ref, w1_ref, w3_ref, w2_ref, o_ref, *, sub):
    m = mod_ref[0]
    x = x_ref[...]
    h = _normmod(x, g_ref[...], m[3:4], m[4:5]).astype(BF16)
    o_ref[...] = x + m[5:6] * _swiglu(h, w1_ref.at[0], w3_ref.at[0], w2_ref.at[0], sub)


def _ffn(x, mod, g, w1, w3, w2, seq, tm, layer):
    n, d = x.shape
    dff = w1.shape[2]
    tps = seq // tm
    const = lambda i: (0, 0)
    lay = lambda i: (layer, 0, 0)
    return pl.pallas_call(
        functools.partial(_ffn_kernel, sub=512),
        grid=(n // tm,),
        in_specs=[pl.BlockSpec((tm, d), lambda i: (i, 0)),
                  pl.BlockSpec((1, 6, d), lambda i: (i // tps, 0, 0)),
                  pl.BlockSpec((1, d), const),
                  pl.BlockSpec((1, d, dff), lay, pipeline_mode=pl.Buffered(1)),
                  pl.BlockSpec((1, d, dff), lay, pipeline_mode=pl.Buffered(1)),
                  pl.BlockSpec((1, dff, d), lay, pipeline_mode=pl.Buffered(1))],
        out_specs=pl.BlockSpec((tm, d), lambda i: (i, 0)),
        out_shape=jax.ShapeDtypeStruct((n, d), F32),
        compiler_params=_cparams("arbitrary"),
        name="ffn",
    )(x, mod, g, w1, w3, w2)


def _pool_kernel(x_ref, halo_ref, mod_ref, g_ref, win_ref, wmix_ref, scale_ref, wo_ref, o_ref,
                 ext_ref, z_ref, *, tps):
    tm, d = x_ref.shape
    pg = d // len(POOL_WINDOWS)
    m = mod_ref[0]
    it = pl.program_id(0) % tps

    def project(rows):
        h = _normmod(rows, g_ref[...], m[0:1], m[1:2])
        return _bdot(h.astype(BF16), win_ref[...])

    ext_ref[0:POOL_HALO, :] = jnp.where(it == 0, 0.0, project(halo_ref[...]))
    ext_ref[POOL_HALO:, :] = project(x_ref[...])
    tpos = (it * tm + 1 + lax.broadcasted_iota(jnp.int32, (tm, 1), 0)).astype(F32)
    for gi, w in enumerate(POOL_WINDOWS):
        cols = slice(gi * pg, (gi + 1) * pg)
        s = ext_ref[:, cols]
        span = 1
        while span < w:
            s = s + pltpu.roll(s, span, 0)
            span *= 2
        mean = s[POOL_HALO:, :] * (1.0 / jnp.minimum(tpos, float(w)))
        dlt = (mean - ext_ref[POOL_HALO:, cols]).astype(BF16)
        z_ref[:, cols] = (_bdot(dlt, wmix_ref[gi]) * scale_ref[:, cols]).astype(BF16)
    o_ref[...] = x_ref[...] + m[2:3] * _bdot(z_ref[...], wo_ref[...])


def _pool(x, mod, g, win, wmix, scale, wo, seq, tm):
    n, d = x.shape
    tps = seq // tm
    hb = tm // POOL_HALO
    ng, pg, _ = wmix.shape
    return pl.pallas_call(
        functools.partial(_pool_kernel, tps=tps),
        grid=(n // tm,),
        in_specs=[pl.BlockSpec((tm, d), lambda i: (i, 0)),
                  pl.BlockSpec((POOL_HALO, d), lambda i: (jnp.maximum(i * hb - 1, 0), 0)),
                  pl.BlockSpec((1, 6, d), lambda i: (i // tps, 0, 0)),
                  pl.BlockSpec((1, d), lambda i: (0, 0)),
                  pl.BlockSpec((d, d), lambda i: (0, 0)),
                  pl.BlockSpec((ng, pg, pg), lambda i: (0, 0, 0)),
                  pl.BlockSpec((1, d), lambda i: (0, 0)),
                  pl.BlockSpec((d, d), lambda i: (0, 0))],
        out_specs=pl.BlockSpec((tm, d), lambda i: (i, 0)),
        out_shape=jax.ShapeDtypeStruct((n, d), F32),
        scratch_shapes=[pltpu.VMEM((tm + POOL_HALO, d), F32), pltpu.VMEM((tm, d), BF16)],
        compiler_params=_cparams("arbitrary"),
        name="pool",
    )(x, x, mod, g, win, wmix, scale, wo)


def _pack_bf16_pairs(v):
    w = v.shape[1] // 2
    bits = lambda a: lax.bitcast_convert_type(a.astype(BF16).astype(F32), jnp.int32)
    return bits(v[:, :w]) | lax.shift_right_logical(bits(v[:, w:]), 16)


def _unpack_bf16_pairs(p):
    hi = lax.bitcast_convert_type(p & jnp.int32(-65536), F32)
    lo = lax.bitcast_convert_type(lax.shift_left(p, 16), F32)
    return jnp.concatenate([hi, lo], axis=1)


def _router_kernel(x_ref, mod_ref, g_ref, rw_ref, rb_ref, h_ref, idx_ref, gate_ref, rank_ref,
                   cnt_ref, base_scr):
    @pl.when(pl.program_id(0) == 0)
    def _():
        base_scr[...] = jnp.zeros_like(base_scr)

    m = mod_ref[0]
    h = _normmod(x_ref[...], g_ref[...], m[3:4], m[4:5])
    h_ref[...] = _pack_bf16_pairs(h)
    split = lambda a: (a.astype(BF16), (a - a.astype(BF16).astype(F32)).astype(BF16))
    h_hi, h_lo = split(h)
    w_hi, w_lo = split(rw_ref[...])
    logits = (_bdot(h_hi, w_hi) + _bdot(h_hi, w_lo) + _bdot(h_lo, w_hi)) + rb_ref[...]
    ne = float(logits.shape[1])
    lane = lax.broadcasted_iota(jnp.int32, logits.shape, 1).astype(F32)
    v1 = jnp.max(logits, axis=1, keepdims=True)
    i1 = jnp.min(jnp.where(logits == v1, lane, ne), axis=1, keepdims=True)
    rest = jnp.where(lane == i1, -jnp.inf, logits)
    v2 = jnp.max(rest, axis=1, keepdims=True)
    i2 = jnp.min(jnp.where(rest == v2, lane, ne), axis=1, keepdims=True)
    e2 = jnp.exp(v2 - v1)
    den = 1.0 + e2
    two = lax.broadcasted_iota(jnp.int32, idx_ref.shape, 1)
    idx_ref[...] = jnp.where(two == 0, i1, i2).astype(jnp.int32)
    gate_ref[...] = jnp.where(two == 0, 1.0 / den, e2 / den)
    tm = logits.shape[0]
    oh1 = jnp.where(lane == i1, 1.0, 0.0)
    oh2 = jnp.where(lane == i2, 1.0, 0.0)
    both = oh1 + oh2
    before = (lax.broadcasted_iota(jnp.int32, (tm, tm), 1)
              < lax.broadcasted_iota(jnp.int32, (tm, tm), 0))
    prior = _bdot(jnp.where(before, 1.0, 0.0).astype(BF16), both.astype(BF16)) + base_scr[...]
    r1 = jnp.sum(oh1 * prior, axis=1, keepdims=True)
    r2 = jnp.sum(oh2 * prior, axis=1, keepdims=True)
    rank_ref[...] = jnp.where(two == 0, r1, r2).astype(jnp.int32)
    total = base_scr[...] + jnp.sum(both, axis=0, keepdims=True)
    base_scr[...] = total
    cnt_ref[...] = total.astype(jnp.int32)


def _router(x, mod, g, rw, rb, seq, tm):
    n, d = x.shape
    ne = rw.shape[1]
    tps = seq // tm
    return pl.pallas_call(
        _router_kernel,
        grid=(n // tm,),
        in_specs=[pl.BlockSpec((tm, d), lambda i: (i, 0)),
                  pl.BlockSpec((1, 6, d), lambda i: (i // tps, 0, 0)),
                  pl.BlockSpec((1, d), lambda i: (0, 0)),
                  pl.BlockSpec((d, ne), lambda i: (0, 0)),
                  pl.BlockSpec((1, ne), lambda i: (0, 0))],
        out_specs=[pl.BlockSpec((tm, d // 2), lambda i: (i, 0)),
                   pl.BlockSpec((tm, TOP_K), lambda i: (i, 0)),
                   pl.BlockSpec((tm, TOP_K), lambda i: (i, 0)),
                   pl.BlockSpec((tm, TOP_K), lambda i: (i, 0)),
                   pl.BlockSpec((1, ne), lambda i: (0, 0))],
        out_shape=[jax.ShapeDtypeStruct((n, d // 2), jnp.int32),
                   jax.ShapeDtypeStruct((n, TOP_K), jnp.int32),
                   jax.ShapeDtypeStruct((n, TOP_K), F32),
                   jax.ShapeDtypeStruct((n, TOP_K), jnp.int32),
                   jax.ShapeDtypeStruct((1, ne), jnp.int32)],
        scratch_shapes=[pltpu.VMEM((1, ne), F32)],
        compiler_params=_cparams("arbitrary"),
        name="router",
    )(x, mod, g, rw, rb)


def _count_le(sorted_vals, queries):
    return jnp.sum((sorted_vals[None, :] <= queries[:, None]).astype(jnp.int32), axis=1)


def _routing_tables(idx, rank, cnt, ne, tile):
    n = idx.shape[0]
    padded = ((cnt + tile - 1) // tile) * tile
    off = jnp.cumsum(padded) - padded
    onehot = idx[:, :, None] == jnp.arange(ne, dtype=jnp.int32)[None, None, :]
    pos = rank + jnp.sum(jnp.where(onehot, off[None, None, :], 0), axis=2)
    n_tiles = (TOP_K * n) // tile + ne
    used = jnp.sum(padded) // tile
    tile_ids = jnp.arange(n_tiles, dtype=jnp.int32)
    tile_expert = _count_le(jnp.cumsum(padded), tile_ids * tile)
    last_used = jnp.maximum(used - 1, 0)
    tile_valid = (tile_ids < used).astype(jnp.int32)
    tile_src = jnp.minimum(tile_ids, last_used).astype(jnp.int32)
    tile_expert = jnp.minimum(tile_expert, ne - 1).astype(jnp.int32)
    tile_expert = jnp.where(tile_valid == 1, tile_expert, tile_expert[last_used])
    tile_rows = jnp.clip(cnt[tile_expert] - (tile_ids * tile - off[tile_expert]), 0, tile)
    tile_rows = jnp.where(tile_valid == 1, tile_rows, 0).astype(jnp.int32)
    return dict(pos=pos, tile_expert=tile_expert, tile_valid=tile_valid, tile_src=tile_src,
                tile_rows=tile_rows, n_tiles=n_tiles)


def _sc_mesh():
    return plsc.VectorSubcoreMesh(core_axis_name="c", subcore_axis_name="s",
                                  num_cores=SC_CORES, num_subcores=SC_SUBCORES)


def _dispatch(hp, pos0, pos1, rows_out):
    n, w = hp.shape
    per_w = n // SC_WORKERS
    ch = SC_ROWS
    nch = per_w // ch
    assert per_w * SC_WORKERS == n and nch * ch == per_w and nch % 2 == 0

    @functools.partial(
        pl.kernel, mesh=_sc_mesh(),
        out_type=jax.ShapeDtypeStruct((rows_out, w), hp.dtype),
        scratch_types=[pltpu.VMEM((per_w,), jnp.int32), pltpu.VMEM((per_w,), jnp.int32),
                       pltpu.VMEM((ch, w), hp.dtype), pltpu.VMEM((ch, w), hp.dtype)]
        + [pltpu.SemaphoreType.DMA] * 4,
        name="dispatch",
    )
    def k(hp_hbm, p0_hbm, p1_hbm, out_hbm, i0_v, i1_v, r0_v, r1_v, l0, l1, s0, s1):
        base = (lax.axis_index("s") * SC_CORES + lax.axis_index("c")) * per_w
        pltpu.sync_copy(p0_hbm.at[pl.ds(base, per_w)], i0_v)
        pltpu.sync_copy(p1_hbm.at[pl.ds(base, per_w)], i1_v)
        bufs, lsem, ssem = (r0_v, r1_v), (l0, l1), (s0, s1)

        def load(j, b):
            return pltpu.make_async_copy(hp_hbm.at[pl.ds(base + j * ch, ch)], bufs[b], lsem[b])

        def scatter(j, b, idx):
            return pltpu.make_async_copy(bufs[b], out_hbm.at[idx.at[pl.ds(j * ch, ch)]], ssem[b])

        load(0, 0).start()

        @pl.loop(0, nch, step=2)
        def _(j):
            for b in range(2):
                jj = j + b
                load(jj, b).wait()

                @pl.when(jj + 1 < nch)
                def _():
                    load(jj + 1, 1 - b).start()

                c0 = scatter(jj, b, i0_v)
                c1 = scatter(jj, b, i1_v)
                c0.start()
                c1.start()
                c0.wait()
                c1.wait()

    return k(hp, pos0, pos1)


def _gather_pairs(ys, pos0, pos1):
    n = pos0.shape[0]
    w = ys.shape[1]
    per_w = n // SC_WORKERS
    ch = SC_ROWS
    nch = per_w // ch
    assert per_w * SC_WORKERS == n and nch * ch == per_w
    out = jax.ShapeDtypeStruct((n, w), ys.dtype)

    @functools.partial(
        pl.kernel, mesh=_sc_mesh(), out_type=(out, out),
        scratch_types=[pltpu.VMEM((per_w,), jnp.int32), pltpu.VMEM((per_w,), jnp.int32),
                       pltpu.VMEM((ch, w), ys.dtype), pltpu.VMEM((ch, w), ys.dtype)]
        + [pltpu.SemaphoreType.DMA] * 4,
        name="gather_pairs",
    )
    def k(ys_hbm, p0_hbm, p1_hbm, a_hbm, b_hbm, i0_v, i1_v, ra_v, rb_v, ga, gb, wa, wb):
        base = (lax.axis_index("s") * SC_CORES + lax.axis_index("c")) * per_w
        pltpu.sync_copy(p0_hbm.at[pl.ds(base, per_w)], i0_v)
        pltpu.sync_copy(p1_hbm.at[pl.ds(base, per_w)], i1_v)

        def gather(j, idx, buf, sem):
            return pltpu.make_async_copy(ys_hbm.at[idx.at[pl.ds(j * ch, ch)]], buf, sem)

        def write(j, buf, dst, sem):
            return pltpu.make_async_copy(buf, dst.at[pl.ds(base + j * ch, ch)], sem)

        gather(0, i0_v, ra_v, ga).start()
        gather(0, i1_v, rb_v, gb).start()

        @pl.loop(0, nch)
        def _(j):
            gather(j, i0_v, ra_v, ga).wait()
            write(j, ra_v, a_hbm, wa).start()
            gather(j, i1_v, rb_v, gb).wait()
            write(j, rb_v, b_hbm, wb).start()
            write(j, ra_v, a_hbm, wa).wait()

            @pl.when(j + 1 < nch)
            def _():
                gather(j + 1, i0_v, ra_v, ga).start()

            write(j, rb_v, b_hbm, wb).wait()

            @pl.when(j + 1 < nch)
            def _():
                gather(j + 1, i1_v, rb_v, gb).start()

    return k(ys, pos0, pos1)


def _expert_kernel(te_ref, tv_ref, tsrc_ref, tr_ref, xs_ref, w1_ref, w3_ref, w2_ref,
                   ys_ref, x_scr, acc_ref, *, sub, kc):
    j = pl.program_id(0)
    k = pl.program_id(1)
    last = kc - 1

    @pl.when(tv_ref[j] == 1)
    def _():
        @pl.when(k == 0)
        def _():
            row = lax.broadcasted_iota(jnp.int32, x_scr.shape, 0)
            x = _unpack_bf16_pairs(xs_ref[...])
            x_scr[...] = jnp.where(row < tr_ref[j], x, 0.0).astype(BF16)

        part = _swiglu(x_scr[...], w1_ref.at[0], w3_ref.at[0], w2_ref.at[0], sub)

        if kc == 1:
            ys_ref[...] = _pack_bf16_pairs(part)
        else:
            @pl.when(k == 0)
            def _():
                acc_ref[...] = part

            @pl.when((k > 0) & (k < last))
            def _():
                acc_ref[...] += part

            @pl.when(k == last)
            def _():
                ys_ref[...] = _pack_bf16_pairs(acc_ref[...] + part)

    @pl.when((tv_ref[j] == 0) & (k == last))
    def _():
        ys_ref[...] = jnp.zeros_like(ys_ref)


def _experts(xs, w1, w3, w2, rt, tile, kc):
    rows, wp = xs.shape
    ne, d, dff = w1.shape
    fc = dff // kc
    kcol = lambda j, k, tv: jnp.where(tv[j] == 1, jnp.where(j % 2 == 0, k, kc - 1 - k), kc - 1)
    return pl.pallas_call(
        functools.partial(_expert_kernel, sub=512, kc=kc),
        grid_spec=pltpu.PrefetchScalarGridSpec(
            num_scalar_prefetch=4,
            grid=(rt['n_tiles'], kc),
            in_specs=[pl.BlockSpec((tile, wp), lambda j, k, te, tv, tsrc, tr: (tsrc[j], 0)),
                      pl.BlockSpec((1, d, fc), lambda j, k, te, tv, tsrc, tr: (te[j], 0, kcol(j, k, tv))),
                      pl.BlockSpec((1, d, fc), lambda j, k, te, tv, tsrc, tr: (te[j], 0, kcol(j, k, tv))),
                      pl.BlockSpec((1, fc, d), lambda j, k, te, tv, tsrc, tr: (te[j], kcol(j, k, tv), 0))],
            out_specs=pl.BlockSpec((tile, wp), lambda j, k, te, tv, tsrc, tr: (j, 0)),
            scratch_shapes=[pltpu.VMEM((tile, d), BF16), pltpu.VMEM((tile, d), F32)]),
        out_shape=jax.ShapeDtypeStruct((rows, wp), jnp.int32),
        compiler_params=_cparams("arbitrary", "arbitrary"),
        name="experts",
    )(rt['tile_expert'], rt['tile_valid'], rt['tile_src'], rt['tile_rows'], xs, w1, w3, w2)


def _combine_kernel(a_ref, b_ref, gate_ref, x_ref, mod_ref, fg_ref, o_ref, *, final_norm):
    m = mod_ref[0]
    g = gate_ref[...]
    y = g[:, 0:1] * _unpack_bf16_pairs(a_ref[...]) + g[:, 1:2] * _unpack_bf16_pairs(b_ref[...])
    xn = x_ref[...] + m[5:6] * y
    if final_norm:
        ms = jnp.mean(xn * xn, axis=-1, keepdims=True)
        xn = (xn * lax.rsqrt(ms + EPS)) * fg_ref[...]
    o_ref[...] = xn


def _combine(a, b, gates, x, mod, fg, seq, tm, final_norm, tile0):
    n, d = x.shape
    tps = seq // tm
    part = lambda w: pl.BlockSpec((tm, w), lambda i: (i, 0))
    full = lambda w: pl.BlockSpec((tm, w), lambda i: (i + tile0, 0))
    return pl.pallas_call(
        functools.partial(_combine_kernel, final_norm=final_norm),
        grid=(a.shape[0] // tm,),
        in_specs=[part(d // 2), part(d // 2), full(TOP_K), full(d),
                  pl.BlockSpec((1, 6, d), lambda i: ((i + tile0) // tps, 0, 0)),
                  pl.BlockSpec((1, d), lambda i: (0, 0))],
        out_specs=full(d),
        out_shape=jax.ShapeDtypeStruct((n, d), F32),
        input_output_aliases={3: 0},
        compiler_params=_cparams("arbitrary"),
        name="combine",
    )(a, b, gates, x, mod, fg)


def _pick(n, prefs):
    for p in prefs:
        if n % p == 0:
            return p
    return n


def kernel(x, c, ada_w, ada_b, norm_g, ssm_in, ssm_log_dt, ssm_lam_re, ssm_lam_im,
           ssm_b_re, ssm_b_im, ssm_c_re, ssm_c_im, ssm_d, ssm_glu, ssm_out,
           pool_in, pool_mix, pool_scale, pool_out, ffn_w1, ffn_w3, ffn_w2,
           router_w, router_b, moe_w1, moe_w3, moe_w2, final_g):
    bsz, seq, d = x.shape
    depth = ada_w.shape[0]
    n = bsz * seq
    ne = router_w.shape[-1]
    g_, p_, h_ = ssm_b_re.shape[1:]
    t = SSM_T
    if seq % (LANES * t // bsz) != 0 or bsz != 8:
        raise NotImplementedError("state-space kernels assume batch 8 and seq % 256 == 0")
    if depth % 2 != 0:
        raise NotImplementedError("the final RMSNorm is fused into an expert layer's combine")
    tm = _pick(seq, (512, 256, 128, 64, 32, 16))
    tile = _pick(TOP_K * n, (512, 256, 128, 64, 32, 16))
    kc = 2 if moe_w1.shape[-1] % 256 == 0 else 1
    gb = _pick(g_, (4, 2, 1))

    stack = lambda w: w.astype(BF16).reshape((-1,) + w.shape[2:])
    ew1, ew3, ew2 = stack(moe_w1), stack(moe_w3), stack(moe_w2)
    fw1, fw3, fw2 = ffn_w1.astype(BF16), ffn_w3.astype(BF16), ffn_w2.astype(BF16)
    ops = jax.vmap(functools.partial(_ssm_operators, t=t))(
        ssm_log_dt, ssm_lam_re, ssm_lam_im, ssm_b_re, ssm_b_im, ssm_c_re, ssm_c_im, ssm_d)
    ops = tuple(o.reshape((-1,) + o.shape[2:]) for o in ops)
    mod = _ada(c, ada_w, ada_b).reshape(depth, bsz, 6, d)
    xf = x.reshape(n, d)
    fg = final_g.reshape(1, d)
    for i in range(depth):
        j = i // 2
        mod_i = mod[i]
        g_a = norm_g[i, 0].reshape(1, d)
        g_b = norm_g[i, 1].reshape(1, d)
        if i % 2 == 0:
            ut = _ssm_in(xf.reshape(bsz, seq, d), mod_i, g_a, ssm_in[j].astype(BF16), t, h_, 2)
            yt = _ssm(ut, *ops, bsz, gb, layer=j)
            xf = _glu_out(yt, xf.reshape(bsz, seq, d), mod_i, ssm_glu[j].astype(BF16),
                          ssm_out[j].astype(BF16), t, 2).reshape(n, d)
            xf = _ffn(xf, mod_i, g_b, fw1, fw3, fw2, seq, tm, layer=j)
        else:
            xf = _pool(xf, mod_i, g_a, pool_in[j].astype(BF16), pool_mix[j].astype(BF16),
                       pool_scale[j].reshape(1, d), pool_out[j].astype(BF16), seq, tm)
            h, idx, gates, rank, cnt = _router(xf, mod_i, g_b, router_w[j],
                                               router_b[j].reshape(1, ne), seq, tm)
            rt = _routing_tables(idx, rank, cnt[0], ne, tile)
            pos0, pos1 = rt['pos'][:, 0], rt['pos'][:, 1]
            xs = _dispatch(h, pos0, pos1, rt['n_tiles'] * tile)
            rt['tile_expert'] = rt['tile_expert'] + j * ne
            ys = _experts(xs, ew1, ew3, ew2, rt, tile, kc)
            npc = n // COMBINE_PIECES
            for q in range(COMBINE_PIECES):
                piece = slice(q * npc, (q + 1) * npc)
                ya, yb = _gather_pairs(ys, pos0[piece], pos1[piece])
                xf = _combine(ya, yb, gates, xf, mod_i, fg, seq, tm,
                              final_norm=(i == depth - 1), tile0=q * (npc // tm))
    return xf.reshape(bsz, seq, d)
```

```python
import functools
import math

import jax
import jax.numpy as jnp
from jax import lax
from jax.experimental import pallas as pl
from jax.experimental.pallas import tpu as pltpu
from jax.experimental.pallas import tpu_sc as plsc

F32 = jnp.float32
BF16 = jnp.bfloat16
EPS = 1e-6
POOL_WINDOWS = (2, 4, 8, 16)
POOL_HALO = 16
SSM_T = 16
LANES = 128
TOP_K = 2
SC_CORES, SC_SUBCORES = 2, 16
SC_WORKERS = SC_CORES * SC_SUBCORES
SC_ROWS = 64
COMBINE_PIECES = 4
VMEM_LIMIT = 58 * 1024 * 1024


def _cparams(*sem):
    return pltpu.CompilerParams(dimension_semantics=sem, vmem_limit_bytes=VMEM_LIMIT)


def _normmod(x, g, shift, scale):
    ms = jnp.mean(x * x, axis=-1, keepdims=True)
    y = x * lax.rsqrt(ms + EPS)
    return (y * g) * (1.0 + scale) + shift


def _bdot(a, b):
    return jnp.dot(a, b, preferred_element_type=F32)


def _ada_kernel(c_ref, w_ref, b_ref, o_ref):
    c = c_ref[...]
    cond = c * jax.nn.sigmoid(c)
    o_ref[0] = jnp.dot(cond, w_ref[0], precision=lax.Precision.HIGHEST,
                       preferred_element_type=F32) + b_ref[0]


def _ada(c, ada_w, ada_b):
    depth, d, d6 = ada_w.shape
    bsz = c.shape[0]
    tn = d6 // 4
    return pl.pallas_call(
        _ada_kernel,
        grid=(depth, d6 // tn),
        in_specs=[pl.BlockSpec((bsz, d), lambda l, j: (0, 0)),
                  pl.BlockSpec((1, d, tn), lambda l, j: (l, 0, j)),
                  pl.BlockSpec((1, 1, tn), lambda l, j: (l, 0, j))],
        out_specs=pl.BlockSpec((1, bsz, tn), lambda l, j: (l, 0, j)),
        out_shape=jax.ShapeDtypeStruct((depth, bsz, d6), F32),
        compiler_params=_cparams("arbitrary", "arbitrary"),
        name="ada",
    )(c, ada_w, ada_b.reshape(depth, 1, d6))


def _ssm_operators(log_dt, lam_re, lam_im, b_re, b_im, c_re, c_im, d_skip, t):
    g_, p_, h_ = b_re.shape
    dt = jnp.exp(log_dt.astype(F32))[:, None]
    lr = lam_re.astype(F32)
    li = lam_im.astype(F32)
    mag = jnp.exp(lr * dt)
    a_re = mag * jnp.cos(li * dt)
    a_im = mag * jnp.sin(li * dt)
    den = lr * lr + li * li
    nr = a_re - 1
    coef_re = (nr * lr + a_im * li) / den
    coef_im = (a_im * lr - nr * li) / den
    br = b_re.astype(F32)
    bi = b_im.astype(F32)
    bbar_re = coef_re[..., None] * br - coef_im[..., None] * bi
    bbar_im = coef_re[..., None] * bi + coef_im[..., None] * br
    steps = jnp.arange(0, t + 1, dtype=F32)[:, None, None]
    pmag = jnp.exp(lr * dt * steps)
    pw_re = pmag * jnp.cos(li * dt * steps)
    pw_im = pmag * jnp.sin(li * dt * steps)
    cr = c_re.astype(F32)
    ci = c_im.astype(F32)
    er = cr[None] * pw_re[:, :, None, :] - ci[None] * pw_im[:, :, None, :]
    ei = cr[None] * pw_im[:, :, None, :] + ci[None] * pw_re[:, :, None, :]
    er_l = jnp.transpose(er[:t], (0, 2, 3, 1))[..., None]
    ei_l = jnp.transpose(ei[:t], (0, 2, 3, 1))[..., None]
    br_l = jnp.transpose(bbar_re, (1, 0, 2))[None, None]
    bi_l = jnp.transpose(bbar_im, (1, 0, 2))[None, None]
    kk = jnp.sum(er_l * br_l - ei_l * bi_l, axis=2)
    kk = jnp.transpose(kk, (2, 0, 1, 3))
    cols = [jnp.pad(kk[:, :t - j], ((0, 0), (j, 0), (0, 0), (0, 0))) for j in range(t)]
    mt = jnp.stack(cols, axis=3).reshape(g_, t * h_, t * h_)
    rev_re = jnp.transpose(pw_re[:t][::-1], (1, 0, 2))[:, :, None, :]
    rev_im = jnp.transpose(pw_im[:t][::-1], (1, 0, 2))[:, :, None, :]
    bt_re = jnp.transpose(bbar_re, (0, 2, 1))[:, None]
    bt_im = jnp.transpose(bbar_im, (0, 2, 1))[:, None]
    w_re = (rev_re * bt_re - rev_im * bt_im).reshape(g_, t * h_, p_)
    w_im = (rev_re * bt_im + rev_im * bt_re).reshape(g_, t * h_, p_)
    w4 = jnp.concatenate([w_re, w_im, w_im, w_re], axis=-1)
    vt_re = jnp.transpose(er[1:], (1, 0, 2, 3)).reshape(g_, t * h_, p_)
    vt_im = -jnp.transpose(ei[1:], (1, 0, 2, 3)).reshape(g_, t * h_, p_)
    vt = jnp.concatenate([vt_re, vt_im], axis=-1)
    at_re, at_im = pw_re[t], pw_im[t]
    c1 = jnp.concatenate([at_re, at_re], axis=-1)
    c2 = jnp.concatenate([-at_im, at_im], axis=-1)
    coef = jnp.zeros((g_, 8, 2 * p_), F32).at[:, 0].set(c1).at[:, 1].set(c2)
    dk = jnp.tile(d_skip.astype(F32)[:, None, :], (1, t, 1)).reshape(g_, t * h_, 1)
    return mt.astype(BF16), w4.astype(BF16), vt.astype(BF16), coef, dk


def _chunk_rows(c, b, bsz, t):
    return pl.ds(pl.multiple_of((c * bsz + b) * t, t), t)


def _ssm_in_kernel(x_ref, mod_ref, g_ref, w_ref, ut_ref, u_scr, *, t, bsz):
    q = pl.program_id(1)
    nb, cs, _ = x_ref.shape
    nlt = u_scr.shape[0]
    for bi in range(nb):
        m = mod_ref[bi]
        h = _normmod(x_ref[bi], g_ref[...], m[0:1], m[1:2])
        u = _bdot(h.astype(BF16), w_ref[...])
        for c in range(cs // t):
            rows = _chunk_rows(c, q * nb + bi, bsz, t)
            for l in range(nlt):
                u_scr[l, rows, :] = u[c * t:(c + 1) * t, l * LANES:(l + 1) * LANES]

    @pl.when(q == pl.num_programs(1) - 1)
    def _():
        ng, th, nch = ut_ref.shape
        hh = th // t
        gl = LANES // hh
        for l in range(nlt):
            for k in range(t):
                uk = u_scr[l, pl.ds(k, nch, stride=t), :]
                ut_ref[l * gl:(l + 1) * gl, k * hh:(k + 1) * hh, :] = (
                    uk.T.reshape(gl, hh, nch).astype(BF16))


def _ssm_in(x, mod, g, w, t, hh, nb):
    bsz, seq, d = x.shape
    ng = d // hh
    cs = LANES * t // bsz
    return pl.pallas_call(
        functools.partial(_ssm_in_kernel, t=t, bsz=bsz),
        grid=(seq // cs, bsz // nb),
        in_specs=[pl.BlockSpec((nb, cs, d), lambda o, q: (q, o, 0)),
                  pl.BlockSpec((nb, 6, d), lambda o, q: (q, 0, 0)),
                  pl.BlockSpec((1, d), lambda o, q: (0, 0)),
                  pl.BlockSpec((d, d), lambda o, q: (0, 0))],
        out_specs=pl.BlockSpec((ng, t * hh, LANES), lambda o, q: (0, 0, o)),
        out_shape=jax.ShapeDtypeStruct((ng, t * hh, bsz * seq // t), BF16),
        scratch_shapes=[pltpu.VMEM((d // LANES, LANES * t, LANES), F32)],
        compiler_params=_cparams("arbitrary", "arbitrary"),
        name="ssm_in",
    )(x, mod, g, w)


def _ssm_kernel(ut_ref, mt_ref, w4_ref, vt_ref, coef_ref, dk_ref, yt_ref, z_scr, s_scr,
                *, gb, nchunks, bsz):
    p2 = s_scr.shape[-1]
    for gi in range(gb):
        u = ut_ref[gi].astype(F32).T.astype(BF16)
        z = _bdot(u, w4_ref[gi])
        z_scr[gi, 0] = z[:, :p2]
        z_scr[gi, 1] = z[:, p2:]
    c1 = [jnp.broadcast_to(coef_ref[gi, 0:1, :], (bsz, p2)) for gi in range(gb)]
    c2 = [jnp.broadcast_to(coef_ref[gi, 1:2, :], (bsz, p2)) for gi in range(gb)]

    def body(c, carry):
        new = []
        for gi in range(gb):
            s1, s2 = carry[gi]
            rows = pl.ds(pl.multiple_of(c * bsz, bsz), bsz)
            s_scr[gi, rows, :] = s1
            n1 = c1[gi] * s1 + c2[gi] * s2 + z_scr[gi, 0, rows, :]
            n2 = c1[gi] * s2 - c2[gi] * s1 + z_scr[gi, 1, rows, :]
            new.append((n1, n2))
        return tuple(new)

    zero = jnp.zeros((bsz, p2), F32)
    lax.fori_loop(0, nchunks, body, tuple((zero, zero) for _ in range(gb)))
    for gi in range(gb):
        ut = ut_ref[gi]
        st = _bdot(vt_ref[gi], s_scr[gi].T.astype(BF16))
        yt = _bdot(mt_ref[gi], ut) + st + dk_ref[gi] * ut.astype(F32)
        yt_ref[gi] = yt.astype(yt_ref.dtype)


def _ssm(ut, mt, w4, vt, coef, dk, bsz, gb):
    g_, th, nch = ut.shape
    p2 = vt.shape[2]
    kern = functools.partial(_ssm_kernel, gb=gb, nchunks=nch // bsz, bsz=bsz)
    spec3 = lambda a, b: pl.BlockSpec((gb, a, b), lambda i: (i, 0, 0))
    return pl.pallas_call(
        kern,
        grid=(g_ // gb,),
        in_specs=[spec3(th, nch), spec3(th, th), spec3(th, 2 * p2), spec3(th, p2),
                  spec3(8, p2), spec3(th, 1)],
        out_specs=spec3(th, nch),
        out_shape=jax.ShapeDtypeStruct((g_, th, nch), BF16),
        scratch_shapes=[pltpu.VMEM((gb, 2, nch, p2), F32), pltpu.VMEM((gb, nch, p2), F32)],
        compiler_params=_cparams("arbitrary"),
        name="ssm",
    )(ut, mt, w4, vt, coef, dk)


def _gelu_tanh(y):
    return y * (0.5 * (1.0 + jnp.tanh(math.sqrt(2.0 / math.pi) * (y + 0.044715 * (y * y * y)))))


def _glu_out_kernel(yt_ref, x_ref, mod_ref, wg_ref, wo_ref, o_ref, y_scr, yb_scr, *, t, bsz):
    q = pl.program_id(1)
    nb, cs, _ = x_ref.shape
    nlt = y_scr.shape[0]

    @pl.when(q == 0)
    def _():
        ng, th, nch = yt_ref.shape
        hh = th // t
        gl = LANES // hh
        for l in range(nlt):
            for k in range(t):
                yk = yt_ref[l * gl:(l + 1) * gl, k * hh:(k + 1) * hh, :].astype(F32)
                y_scr[l, pl.ds(k, nch, stride=t), :] = yk.reshape(LANES, nch).T

    for bi in range(nb):
        for c in range(cs // t):
            rows = _chunk_rows(c, q * nb + bi, bsz, t)
            for l in range(nlt):
                yb_scr[bi * cs + c * t:bi * cs + (c + 1) * t, l * LANES:(l + 1) * LANES] = (
                    y_scr[l, rows, :])
    z = _gelu_tanh(yb_scr[...])
    z = z * jax.nn.sigmoid(_bdot(z.astype(BF16), wg_ref[...]))
    o = _bdot(z.astype(BF16), wo_ref[...])
    for bi in range(nb):
        o_ref[bi] = x_ref[bi] + mod_ref[bi][2:3] * o[bi * cs:(bi + 1) * cs]


def _glu_out(yt, x, mod, wg, wo, t, nb):
    bsz, seq, d = x.shape
    ng, th, _ = yt.shape
    cs = LANES * t // bsz
    return pl.pallas_call(
        functools.partial(_glu_out_kernel, t=t, bsz=bsz),
        grid=(seq // cs, bsz // nb),
        in_specs=[pl.BlockSpec((ng, th, LANES), lambda o, q: (0, 0, o)),
                  pl.BlockSpec((nb, cs, d), lambda o, q: (q, o, 0)),
                  pl.BlockSpec((nb, 6, d), lambda o, q: (q, 0, 0)),
                  pl.BlockSpec((d, d), lambda o, q: (0, 0)),
                  pl.BlockSpec((d, d), lambda o, q: (0, 0))],
        out_specs=pl.BlockSpec((nb, cs, d), lambda o, q: (q, o, 0)),
        out_shape=jax.ShapeDtypeStruct((bsz, seq, d), F32),
        scratch_shapes=[pltpu.VMEM((d // LANES, LANES * t, LANES), F32),
                        pltpu.VMEM((nb * cs, d), F32)],
        compiler_params=_cparams("arbitrary", "arbitrary"),
        name="glu_out",
    )(yt, x, mod, wg, wo)


def _chunks(total, step):
    return [(s, min(step, total - s)) for s in range(0, total, step)]


def _swiglu(h, w1_ref, w3_ref, w2_ref, sub):
    total = None
    for s, n in _chunks(w1_ref.shape[-1], sub):
        a = _bdot(h, w1_ref[:, s:s + n])
        b = _bdot(h, w3_ref[:, s:s + n])
        act = (a * jax.nn.sigmoid(a) * b).astype(BF16)
        y = _bdot(act, w2_ref[s:s + n, :])
        total = y if total is None else total + y
    return total


def _ffn_kernel(x_ref, mod_ref, g_ref, w1_ref, w3_ref, w2_ref, o_ref, *, sub):
    m = mod_ref[0]
    x = x_ref[...]
    h = _normmod(x, g_ref[...], m[3:4], m[4:5]).astype(BF16)
    o_ref[...] = x + m[5:6] * _swiglu(h, w1_ref.at[0], w3_ref.at[0], w2_ref.at[0], sub)


def _ffn(x, mod, g, w1, w3, w2, seq, tm, layer):
    n, d = x.shape
    dff = w1.shape[2]
    tps = seq // tm
    const = lambda i: (0, 0)
    lay = lambda i: (layer, 0, 0)
    return pl.pallas_call(
        functools.partial(_ffn_kernel, sub=512),
        grid=(n // tm,),
        in_specs=[pl.BlockSpec((tm, d), lambda i: (i, 0)),
                  pl.BlockSpec((1, 6, d), lambda i: (i // tps, 0, 0)),
                  pl.BlockSpec((1, d), const),
                  pl.BlockSpec((1, d, dff), lay, pipeline_mode=pl.Buffered(1)),
                  pl.BlockSpec((1, d, dff), lay, pipeline_mode=pl.Buffered(1)),
                  pl.BlockSpec((1, dff, d), lay, pipeline_mode=pl.Buffered(1))],
        out_specs=pl.BlockSpec((tm, d), lambda i: (i, 0)),
        out_shape=jax.ShapeDtypeStruct((n, d), F32),
        compiler_params=_cparams("arbitrary"),
        name="ffn",
    )(x, mod, g, w1, w3, w2)


def _pool_kernel(x_ref, halo_ref, mod_ref, g_ref, win_ref, wmix_ref, scale_ref, wo_ref, o_ref,
                 ext_ref, z_ref, *, tps):
    tm, d = x_ref.shape
    pg = d // len(POOL_WINDOWS)
    m = mod_ref[0]
    it = pl.program_id(0) % tps

    def project(rows):
        h = _normmod(rows, g_ref[...], m[0:1], m[1:2])
        return _bdot(h.astype(BF16), win_ref[...])

    ext_ref[0:POOL_HALO, :] = jnp.where(it == 0, 0.0, project(halo_ref[...]))
    ext_ref[POOL_HALO:, :] = project(x_ref[...])
    tpos = (it * tm + 1 + lax.broadcasted_iota(jnp.int32, (tm, 1), 0)).astype(F32)
    for gi, w in enumerate(POOL_WINDOWS):
        cols = slice(gi * pg, (gi + 1) * pg)
        s = ext_ref[:, cols]
        span = 1
        while span < w:
            s = s + pltpu.roll(s, span, 0)
            span *= 2
        mean = s[POOL_HALO:, :] * (1.0 / jnp.minimum(tpos, float(w)))
        dlt = (mean - ext_ref[POOL_HALO:, cols]).astype(BF16)
        z_ref[:, cols] = (_bdot(dlt, wmix_ref[gi]) * scale_ref[:, cols]).astype(BF16)
    o_ref[...] = x_ref[...] + m[2:3] * _bdot(z_ref[...], wo_ref[...])


def _pool(x, mod, g, win, wmix, scale, wo, seq, tm):
    n, d = x.shape
    tps = seq // tm
    hb = tm // POOL_HALO
    ng, pg, _ = wmix.shape
    return pl.pallas_call(
        functools.partial(_pool_kernel, tps=tps),
        grid=(n // tm,),
        in_specs=[pl.BlockSpec((tm, d), lambda i: (i, 0)),
                  pl.BlockSpec((POOL_HALO, d), lambda i: (jnp.maximum(i * hb - 1, 0), 0)),
                  pl.BlockSpec((1, 6, d), lambda i: (i // tps, 0, 0)),
                  pl.BlockSpec((1, d), lambda i: (0, 0)),
                  pl.BlockSpec((d, d), lambda i: (0, 0)),
                  pl.BlockSpec((ng, pg, pg), lambda i: (0, 0, 0)),
                  pl.BlockSpec((1, d), lambda i: (0, 0)),
                  pl.BlockSpec((d, d), lambda i: (0, 0))],
        out_specs=pl.BlockSpec((tm, d), lambda i: (i, 0)),
        out_shape=jax.ShapeDtypeStruct((n, d), F32),
        scratch_shapes=[pltpu.VMEM((tm + POOL_HALO, d), F32), pltpu.VMEM((tm, d), BF16)],
        compiler_params=_cparams("arbitrary"),
        name="pool",
    )(x, x, mod, g, win, wmix, scale, wo)


def _pack_bf16_pairs(v):
    w = v.shape[1] // 2
    bits = lambda a: lax.bitcast_convert_type(a.astype(BF16).astype(F32), jnp.int32)
    return bits(v[:, :w]) | lax.shift_right_logical(bits(v[:, w:]), 16)


def _unpack_bf16_pairs(p):
    hi = lax.bitcast_convert_type(p & jnp.int32(-65536), F32)
    lo = lax.bitcast_convert_type(lax.shift_left(p, 16), F32)
    return jnp.concatenate([hi, lo], axis=1)


def _router_kernel(x_ref, mod_ref, g_ref, rw_ref, rb_ref, h_ref, idx_ref, gate_ref, rank_ref,
                   cnt_ref, base_scr):
    @pl.when(pl.program_id(0) == 0)
    def _():
        base_scr[...] = jnp.zeros_like(base_scr)

    m = mod_ref[0]
    h = _normmod(x_ref[...], g_ref[...], m[3:4], m[4:5])
    h_ref[...] = _pack_bf16_pairs(h)
    split = lambda a: (a.astype(BF16), (a - a.astype(BF16).astype(F32)).astype(BF16))
    h_hi, h_lo = split(h)
    w_hi, w_lo = split(rw_ref[...])
    logits = (_bdot(h_hi, w_hi) + _bdot(h_hi, w_lo) + _bdot(h_lo, w_hi)) + rb_ref[...]
    ne = float(logits.shape[1])
    lane = lax.broadcasted_iota(jnp.int32, logits.shape, 1).astype(F32)
    v1 = jnp.max(logits, axis=1, keepdims=True)
    i1 = jnp.min(jnp.where(logits == v1, lane, ne), axis=1, keepdims=True)
    rest = jnp.where(lane == i1, -jnp.inf, logits)
    v2 = jnp.max(rest, axis=1, keepdims=True)
    i2 = jnp.min(jnp.where(rest == v2, lane, ne), axis=1, keepdims=True)
    e2 = jnp.exp(v2 - v1)
    den = 1.0 + e2
    two = lax.broadcasted_iota(jnp.int32, idx_ref.shape, 1)
    idx_ref[...] = jnp.where(two == 0, i1, i2).astype(jnp.int32)
    gate_ref[...] = jnp.where(two == 0, 1.0 / den, e2 / den)
    tm = logits.shape[0]
    oh1 = jnp.where(lane == i1, 1.0, 0.0)
    oh2 = jnp.where(lane == i2, 1.0, 0.0)
    both = oh1 + oh2
    before = (lax.broadcasted_iota(jnp.int32, (tm, tm), 1)
              < lax.broadcasted_iota(jnp.int32, (tm, tm), 0))
    prior = _bdot(jnp.where(before, 1.0, 0.0).astype(BF16), both.astype(BF16)) + base_scr[...]
    r1 = jnp.sum(oh1 * prior, axis=1, keepdims=True)
    r2 = jnp.sum(oh2 * prior, axis=1, keepdims=True)
    rank_ref[...] = jnp.where(two == 0, r1, r2).astype(jnp.int32)
    total = base_scr[...] + jnp.sum(both, axis=0, keepdims=True)
    base_scr[...] = total
    cnt_ref[...] = total.astype(jnp.int32)


def _router(x, mod, g, rw, rb, seq, tm):
    n, d = x.shape
    ne = rw.shape[1]
    tps = seq // tm
    return pl.pallas_call(
        _router_kernel,
        grid=(n // tm,),
        in_specs=[pl.BlockSpec((tm, d), lambda i: (i, 0)),
                  pl.BlockSpec((1, 6, d), lambda i: (i // tps, 0, 0)),
                  pl.BlockSpec((1, d), lambda i: (0, 0)),
                  pl.BlockSpec((d, ne), lambda i: (0, 0)),
                  pl.BlockSpec((1, ne), lambda i: (0, 0))],
        out_specs=[pl.BlockSpec((tm, d // 2), lambda i: (i, 0)),
                   pl.BlockSpec((tm, TOP_K), lambda i: (i, 0)),
                   pl.BlockSpec((tm, TOP_K), lambda i: (i, 0)),
                   pl.BlockSpec((tm, TOP_K), lambda i: (i, 0)),
                   pl.BlockSpec((1, ne), lambda i: (0, 0))],
        out_shape=[jax.ShapeDtypeStruct((n, d // 2), jnp.int32),
                   jax.ShapeDtypeStruct((n, TOP_K), jnp.int32),
                   jax.ShapeDtypeStruct((n, TOP_K), F32),
                   jax.ShapeDtypeStruct((n, TOP_K), jnp.int32),
                   jax.ShapeDtypeStruct((1, ne), jnp.int32)],
        scratch_shapes=[pltpu.VMEM((1, ne), F32)],
        compiler_params=_cparams("arbitrary"),
        name="router",
    )(x, mod, g, rw, rb)


def _count_le(sorted_vals, queries):
    return jnp.sum((sorted_vals[None, :] <= queries[:, None]).astype(jnp.int32), axis=1)


def _routing_tables(idx, rank, cnt, ne, tile):
    n = idx.shape[0]
    padded = ((cnt + tile - 1) // tile) * tile
    off = jnp.cumsum(padded) - padded
    onehot = idx[:, :, None] == jnp.arange(ne, dtype=jnp.int32)[None, None, :]
    pos = rank + jnp.sum(jnp.where(onehot, off[None, None, :], 0), axis=2)
    n_tiles = (TOP_K * n) // tile + ne
    used = jnp.sum(padded) // tile
    tile_ids = jnp.arange(n_tiles, dtype=jnp.int32)
    tile_expert = _count_le(jnp.cumsum(padded), tile_ids * tile)
    last_used = jnp.maximum(used - 1, 0)
    tile_valid = (tile_ids < used).astype(jnp.int32)
    tile_src = jnp.minimum(tile_ids, last_used).astype(jnp.int32)
    tile_expert = jnp.minimum(tile_expert, ne - 1).astype(jnp.int32)
    tile_expert = jnp.where(tile_valid == 1, tile_expert, tile_expert[last_used])
    tile_rows = jnp.clip(cnt[tile_expert] - (tile_ids * tile - off[tile_expert]), 0, tile)
    tile_rows = jnp.where(tile_valid == 1, tile_rows, 0).astype(jnp.int32)
    return dict(pos=pos, tile_expert=tile_expert, tile_valid=tile_valid, tile_src=tile_src,
                tile_rows=tile_rows, n_tiles=n_tiles)


def _sc_mesh():
    return plsc.VectorSubcoreMesh(core_axis_name="c", subcore_axis_name="s",
                                  num_cores=SC_CORES, num_subcores=SC_SUBCORES)


def _dispatch(hp, pos0, pos1, rows_out):
    n, w = hp.shape
    per_w = n // SC_WORKERS
    ch = SC_ROWS
    nch = per_w // ch
    assert per_w * SC_WORKERS == n and nch * ch == per_w and nch % 2 == 0

    @functools.partial(
        pl.kernel, mesh=_sc_mesh(),
        out_type=jax.ShapeDtypeStruct((rows_out, w), hp.dtype),
        scratch_types=[pltpu.VMEM((per_w,), jnp.int32), pltpu.VMEM((per_w,), jnp.int32),
                       pltpu.VMEM((ch, w), hp.dtype), pltpu.VMEM((ch, w), hp.dtype)]
        + [pltpu.SemaphoreType.DMA] * 4,
        name="dispatch",
    )
    def k(hp_hbm, p0_hbm, p1_hbm, out_hbm, i0_v, i1_v, r0_v, r1_v, l0, l1, s0, s1):
        base = (lax.axis_index("s") * SC_CORES + lax.axis_index("c")) * per_w
        pltpu.sync_copy(p0_hbm.at[pl.ds(base, per_w)], i0_v)
        pltpu.sync_copy(p1_hbm.at[pl.ds(base, per_w)], i1_v)
        bufs, lsem, ssem = (r0_v, r1_v), (l0, l1), (s0, s1)

        def load(j, b):
            return pltpu.make_async_copy(hp_hbm.at[pl.ds(base + j * ch, ch)], bufs[b], lsem[b])

        def scatter(j, b, idx):
            return pltpu.make_async_copy(bufs[b], out_hbm.at[idx.at[pl.ds(j * ch, ch)]], ssem[b])

        load(0, 0).start()

        @pl.loop(0, nch, step=2)
        def _(j):
            for b in range(2):
                jj = j + b
                load(jj, b).wait()

                @pl.when(jj + 1 < nch)
                def _():
                    load(jj + 1, 1 - b).start()

                c0 = scatter(jj, b, i0_v)
                c1 = scatter(jj, b, i1_v)
                c0.start()
                c1.start()
                c0.wait()
                c1.wait()

    return k(hp, pos0, pos1)


def _gather_pairs(ys, pos0, pos1):
    n = pos0.shape[0]
    w = ys.shape[1]
    per_w = n // SC_WORKERS
    ch = SC_ROWS
    nch = per_w // ch
    assert per_w * SC_WORKERS == n and nch * ch == per_w
    out = jax.ShapeDtypeStruct((n, w), ys.dtype)

    @functools.partial(
        pl.kernel, mesh=_sc_mesh(), out_type=(out, out),
        scratch_types=[pltpu.VMEM((per_w,), jnp.int32), pltpu.VMEM((per_w,), jnp.int32),
                       pltpu.VMEM((ch, w), ys.dtype), pltpu.VMEM((ch, w), ys.dtype)]
        + [pltpu.SemaphoreType.DMA] * 4,
        name="gather_pairs",
    )
    def k(ys_hbm, p0_hbm, p1_hbm, a_hbm, b_hbm, i0_v, i1_v, ra_v, rb_v, ga, gb, wa, wb):
        base = (lax.axis_index("s") * SC_CORES + lax.axis_index("c")) * per_w
        pltpu.sync_copy(p0_hbm.at[pl.ds(base, per_w)], i0_v)
        pltpu.sync_copy(p1_hbm.at[pl.ds(base, per_w)], i1_v)

        def gather(j, idx, buf, sem):
            return pltpu.make_async_copy(ys_hbm.at[idx.at[pl.ds(j * ch, ch)]], buf, sem)

        def write(j, buf, dst, sem):
            return pltpu.make_async_copy(buf, dst.at[pl.ds(base + j * ch, ch)], sem)

        gather(0, i0_v, ra_v, ga).start()
        gather(0, i1_v, rb_v, gb).start()

        @pl.loop(0, nch)
        def _(j):
            gather(j, i0_v, ra_v, ga).wait()
            write(j, ra_v, a_hbm, wa).start()
            gather(j, i1_v, rb_v, gb).wait()
            write(j, rb_v, b_hbm, wb).start()
            write(j, ra_v, a_hbm, wa).wait()

            @pl.when(j + 1 < nch)
            def _():
                gather(j + 1, i0_v, ra_v, ga).start()

            write(j, rb_v, b_hbm, wb).wait()

            @pl.when(j + 1 < nch)
            def _():
                gather(j + 1, i1_v, rb_v, gb).start()

    return k(ys, pos0, pos1)


def _expert_kernel(te_ref, tv_ref, tsrc_ref, tr_ref, xs_ref, w1_ref, w3_ref, w2_ref,
                   ys_ref, *, sub):
    j = pl.program_id(0)

    @pl.when(tv_ref[j] == 1)
    def _():
        x = _unpack_bf16_pairs(xs_ref[...])
        row = lax.broadcasted_iota(jnp.int32, x.shape, 0)
        x = jnp.where(row < tr_ref[j], x, 0.0).astype(BF16)
        ys_ref[...] = _pack_bf16_pairs(_swiglu(x, w1_ref.at[0], w3_ref.at[0], w2_ref.at[0], sub))

    @pl.when(tv_ref[j] == 0)
    def _():
        ys_ref[...] = jnp.zeros_like(ys_ref)


def _experts(xs, w1, w3, w2, rt, tile):
    rows, wp = xs.shape
    ne, d, dff = w1.shape
    return pl.pallas_call(
        functools.partial(_expert_kernel, sub=512),
        grid_spec=pltpu.PrefetchScalarGridSpec(
            num_scalar_prefetch=4,
            grid=(rt['n_tiles'],),
            in_specs=[pl.BlockSpec((tile, wp), lambda j, te, tv, tsrc, tr: (tsrc[j], 0)),
                      pl.BlockSpec((1, d, dff), lambda j, te, tv, tsrc, tr: (te[j], 0, 0)),
                      pl.BlockSpec((1, d, dff), lambda j, te, tv, tsrc, tr: (te[j], 0, 0)),
                      pl.BlockSpec((1, dff, d), lambda j, te, tv, tsrc, tr: (te[j], 0, 0))],
            out_specs=pl.BlockSpec((tile, wp), lambda j, te, tv, tsrc, tr: (j, 0))),
        out_shape=jax.ShapeDtypeStruct((rows, wp), jnp.int32),
        compiler_params=_cparams("arbitrary"),
        name="experts",
    )(rt['tile_expert'], rt['tile_valid'], rt['tile_src'], rt['tile_rows'], xs, w1, w3, w2)


def _combine_kernel(a_ref, b_ref, gate_ref, x_ref, mod_ref, fg_ref, o_ref, *, final_norm):
    m = mod_ref[0]
    g = gate_ref[...]
    y = g[:, 0:1] * _unpack_bf16_pairs(a_ref[...]) + g[:, 1:2] * _unpack_bf16_pairs(b_ref[...])
    xn = x_ref[...] + m[5:6] * y
    if final_norm:
        ms = jnp.mean(xn * xn, axis=-1, keepdims=True)
        xn = (xn * lax.rsqrt(ms + EPS)) * fg_ref[...]
    o_ref[...] = xn


def _combine(a, b, gates, x, mod, fg, seq, tm, final_norm, tile0):
    n, d = x.shape
    tps = seq // tm
    part = lambda w: pl.BlockSpec((tm, w), lambda i: (i, 0))
    full = lambda w: pl.BlockSpec((tm, w), lambda i: (i + tile0, 0))
    return pl.pallas_call(
        functools.partial(_combine_kernel, final_norm=final_norm),
        grid=(a.shape[0] // tm,),
        in_specs=[part(d // 2), part(d // 2), full(TOP_K), full(d),
                  pl.BlockSpec((1, 6, d), lambda i: ((i + tile0) // tps, 0, 0)),
                  pl.BlockSpec((1, d), lambda i: (0, 0))],
        out_specs=full(d),
        out_shape=jax.ShapeDtypeStruct((n, d), F32),
        input_output_aliases={3: 0},
        compiler_params=_cparams("arbitrary"),
        name="combine",
    )(a, b, gates, x, mod, fg)


def _pick(n, prefs):
    for p in prefs:
        if n % p == 0:
            return p
    return n


def kernel(x, c, ada_w, ada_b, norm_g, ssm_in, ssm_log_dt, ssm_lam_re, ssm_lam_im,
           ssm_b_re, ssm_b_im, ssm_c_re, ssm_c_im, ssm_d, ssm_glu, ssm_out,
           pool_in, pool_mix, pool_scale, pool_out, ffn_w1, ffn_w3, ffn_w2,
           router_w, router_b, moe_w1, moe_w3, moe_w2, final_g):
    bsz, seq, d = x.shape
    depth = ada_w.shape[0]
    n = bsz * seq
    ne = router_w.shape[-1]
    g_, p_, h_ = ssm_b_re.shape[1:]
    t = SSM_T
    if seq % (LANES * t // bsz) != 0 or bsz != 8:
        raise NotImplementedError("state-space kernels assume batch 8 and seq % 256 == 0")
    if depth % 2 != 0:
        raise NotImplementedError("the final RMSNorm is fused into an expert layer's combine")
    tm = _pick(seq, (512, 256, 128, 64, 32, 16))
    tile = _pick(TOP_K * n, (512, 256, 128, 64, 32, 16))
    gb = _pick(g_, (4, 2, 1))

    stack = lambda w: w.astype(BF16).reshape((-1,) + w.shape[2:])
    ew1, ew3, ew2 = stack(moe_w1), stack(moe_w3), stack(moe_w2)
    fw1, fw3, fw2 = ffn_w1.astype(BF16), ffn_w3.astype(BF16), ffn_w2.astype(BF16)
    mod = _ada(c, ada_w, ada_b).reshape(depth, bsz, 6, d)
    xf = x.reshape(n, d)
    fg = final_g.reshape(1, d)
    for i in range(depth):
        j = i // 2
        mod_i = mod[i]
        g_a = norm_g[i, 0].reshape(1, d)
        g_b = norm_g[i, 1].reshape(1, d)
        if i % 2 == 0:
            ops = _ssm_operators(ssm_log_dt[j], ssm_lam_re[j], ssm_lam_im[j], ssm_b_re[j],
                                 ssm_b_im[j], ssm_c_re[j], ssm_c_im[j], ssm_d[j], t)
            ut = _ssm_in(xf.reshape(bsz, seq, d), mod_i, g_a, ssm_in[j].astype(BF16), t, h_, 2)
            yt = _ssm(ut, *ops, bsz, gb)
            xf = _glu_out(yt, xf.reshape(bsz, seq, d), mod_i, ssm_glu[j].astype(BF16),
                          ssm_out[j].astype(BF16), t, 2).reshape(n, d)
            xf = _ffn(xf, mod_i, g_b, fw1, fw3, fw2, seq, tm, layer=j)
        else:
            xf = _pool(xf, mod_i, g_a, pool_in[j].astype(BF16), pool_mix[j].astype(BF16),
                       pool_scale[j].reshape(1, d), pool_out[j].astype(BF16), seq, tm)
            h, idx, gates, rank, cnt = _router(xf, mod_i, g_b, router_w[j],
                                               router_b[j].reshape(1, ne), seq, tm)
            rt = _routing_tables(idx, rank, cnt[0], ne, tile)
            pos0, pos1 = rt['pos'][:, 0], rt['pos'][:, 1]
            xs = _dispatch(h, pos0, pos1, rt['n_tiles'] * tile)
            rt['tile_expert'] = rt['tile_expert'] + j * ne
            ys = _experts(xs, ew1, ew3, ew2, rt, tile)
            npc = n // COMBINE_PIECES
            for q in range(COMBINE_PIECES):
                piece = slice(q * npc, (q + 1) * npc)
                ya, yb = _gather_pairs(ys, pos0[piece], pos1[piece])
                xf = _combine(ya, yb, gates, xf, mod_i, fg, seq, tm,
                              final_norm=(i == depth - 1), tile0=q * (npc // tm))
    return xf.reshape(bsz, seq, d)
```

```python
import functools
import math

import jax
import jax.numpy as jnp
from jax import lax
from jax.experimental import pallas as pl
from jax.experimental.pallas import tpu as pltpu
from jax.experimental.pallas import tpu_sc as plsc

F32 = jnp.float32
BF16 = jnp.bfloat16
EPS = 1e-6
POOL_WINDOWS = (2, 4, 8, 16)
POOL_HALO = 16
SSM_T = 16
LANES = 128
TOP_K = 2
SC_CORES, SC_SUBCORES = 2, 16
SC_WORKERS = SC_CORES * SC_SUBCORES
SC_ROWS = 64
COMBINE_PIECES = 4
VMEM_LIMIT = 58 * 1024 * 1024


def _cparams(*sem):
    return pltpu.CompilerParams(dimension_semantics=sem, vmem_limit_bytes=VMEM_LIMIT)


def _normmod(x, g, shift, scale):
    ms = jnp.mean(x * x, axis=-1, keepdims=True)
    y = x * lax.rsqrt(ms + EPS)
    return (y * g) * (1.0 + scale) + shift


def _bdot(a, b):
    return jnp.dot(a, b, preferred_element_type=F32)


def _ada_kernel(c_ref, w_ref, b_ref, o_ref):
    c = c_ref[...]
    cond = c * jax.nn.sigmoid(c)
    o_ref[0] = jnp.dot(cond, w_ref[0], precision=lax.Precision.HIGHEST,
                       preferred_element_type=F32) + b_ref[0]


def _ada(c, ada_w, ada_b):
    depth, d, d6 = ada_w.shape
    bsz = c.shape[0]
    tn = d6 // 4
    return pl.pallas_call(
        _ada_kernel,
        grid=(depth, d6 // tn),
        in_specs=[pl.BlockSpec((bsz, d), lambda l, j: (0, 0)),
                  pl.BlockSpec((1, d, tn), lambda l, j: (l, 0, j)),
                  pl.BlockSpec((1, 1, tn), lambda l, j: (l, 0, j))],
        out_specs=pl.BlockSpec((1, bsz, tn), lambda l, j: (l, 0, j)),
        out_shape=jax.ShapeDtypeStruct((depth, bsz, d6), F32),
        compiler_params=_cparams("arbitrary", "arbitrary"),
        name="ada",
    )(c, ada_w, ada_b.reshape(depth, 1, d6))


def _ssm_operators(log_dt, lam_re, lam_im, b_re, b_im, c_re, c_im, d_skip, t):
    g_, p_, h_ = b_re.shape
    dt = jnp.exp(log_dt.astype(F32))[:, None]
    lr = lam_re.astype(F32)
    li = lam_im.astype(F32)
    mag = jnp.exp(lr * dt)
    a_re = mag * jnp.cos(li * dt)
    a_im = mag * jnp.sin(li * dt)
    den = lr * lr + li * li
    nr = a_re - 1
    coef_re = (nr * lr + a_im * li) / den
    coef_im = (a_im * lr - nr * li) / den
    br = b_re.astype(F32)
    bi = b_im.astype(F32)
    bbar_re = coef_re[..., None] * br - coef_im[..., None] * bi
    bbar_im = coef_re[..., None] * bi + coef_im[..., None] * br
    steps = jnp.arange(0, t + 1, dtype=F32)[None, :, None]
    ang = (li * dt)[:, None, :] * steps
    pmag = jnp.exp((lr * dt)[:, None, :] * steps)
    pw_re = pmag * jnp.cos(ang)
    pw_im = pmag * jnp.sin(ang)
    cr = c_re.astype(F32)
    ci = c_im.astype(F32)
    er = cr[:, None] * pw_re[:, :, None, :] - ci[:, None] * pw_im[:, :, None, :]
    ei = cr[:, None] * pw_im[:, :, None, :] + ci[:, None] * pw_re[:, :, None, :]
    tg = lambda a: jnp.transpose(a, (1, 2, 0))
    pt_re, pt_im = tg(pw_re[:, :t])[:, None], tg(pw_im[:, :t])[:, None]
    ct_re, ct_im = tg(cr)[None], tg(ci)[None]
    er_l = (ct_re * pt_re - ct_im * pt_im)[..., None]
    ei_l = (ct_re * pt_im + ct_im * pt_re)[..., None]
    br_l = jnp.transpose(bbar_re, (1, 0, 2))[None, None]
    bi_l = jnp.transpose(bbar_im, (1, 0, 2))[None, None]
    kk = jnp.sum(er_l * br_l - ei_l * bi_l, axis=2)
    kk = jnp.transpose(kk, (2, 0, 1, 3))
    cols = [jnp.pad(kk[:, :t - j], ((0, 0), (j, 0), (0, 0), (0, 0))) for j in range(t)]
    mt = jnp.stack(cols, axis=3).reshape(g_, t * h_, t * h_)
    rev_re = pw_re[:, :t][:, ::-1, None, :]
    rev_im = pw_im[:, :t][:, ::-1, None, :]
    bt_re = jnp.transpose(bbar_re, (0, 2, 1))[:, None]
    bt_im = jnp.transpose(bbar_im, (0, 2, 1))[:, None]
    w_re = (rev_re * bt_re - rev_im * bt_im).reshape(g_, t * h_, p_)
    w_im = (rev_re * bt_im + rev_im * bt_re).reshape(g_, t * h_, p_)
    w4 = jnp.concatenate([w_re, w_im, w_im, w_re], axis=-1)
    vt = jnp.concatenate([er[:, 1:].reshape(g_, t * h_, p_),
                          -ei[:, 1:].reshape(g_, t * h_, p_)], axis=-1)
    at_re, at_im = pw_re[:, t], pw_im[:, t]
    c1 = jnp.concatenate([at_re, at_re], axis=-1)
    c2 = jnp.concatenate([-at_im, at_im], axis=-1)
    coef = jnp.concatenate([c1[:, None], c2[:, None], jnp.zeros((g_, 6, 2 * p_), F32)], axis=1)
    dk = jnp.tile(d_skip.astype(F32)[:, None, :], (1, t, 1)).reshape(g_, t * h_, 1)
    return mt.astype(BF16), w4.astype(BF16), vt.astype(BF16), coef, dk


def _ssm_in_kernel(x_ref, mod_ref, g_ref, w_ref, ut_ref, h_scr, u_scr, *, t):
    q = pl.program_id(1)
    bsz, cs, _ = x_ref.shape
    nlt = u_scr.shape[0]
    for b in range(bsz):
        m = mod_ref[b]
        h = _normmod(x_ref[b], g_ref[...], m[0:1], m[1:2]).astype(BF16)
        for c in range(cs // t):
            h_scr[(c * bsz + b) * t:(c * bsz + b + 1) * t, :] = h[c * t:(c + 1) * t, :]
    u = _bdot(h_scr[...], w_ref[...])
    rows = pl.ds(pl.multiple_of(q * (bsz * cs), bsz * cs), bsz * cs)
    for l in range(nlt):
        u_scr[l, rows, :] = u[:, l * LANES:(l + 1) * LANES]

    @pl.when(q == pl.num_programs(1) - 1)
    def _():
        ng, th, nch = ut_ref.shape
        hh = th // t
        gl = LANES // hh
        for l in range(nlt):
            for k in range(t):
                uk = u_scr[l, pl.ds(k, nch, stride=t), :]
                ut_ref[l * gl:(l + 1) * gl, k * hh:(k + 1) * hh, :] = (
                    uk.T.reshape(gl, hh, nch).astype(BF16))


def _ssm_in(x, mod, g, w, t, hh, nq):
    bsz, seq, d = x.shape
    ng = d // hh
    cs = LANES * t // bsz // nq
    return pl.pallas_call(
        functools.partial(_ssm_in_kernel, t=t),
        grid=(seq // (cs * nq), nq),
        in_specs=[pl.BlockSpec((bsz, cs, d), lambda o, q: (0, o * nq + q, 0)),
                  pl.BlockSpec((bsz, 6, d), lambda o, q: (0, 0, 0)),
                  pl.BlockSpec((1, d), lambda o, q: (0, 0)),
                  pl.BlockSpec((d, d), lambda o, q: (0, 0))],
        out_specs=pl.BlockSpec((ng, t * hh, LANES), lambda o, q: (0, 0, o)),
        out_shape=jax.ShapeDtypeStruct((ng, t * hh, bsz * seq // t), BF16),
        scratch_shapes=[pltpu.VMEM((bsz * cs, d), BF16),
                        pltpu.VMEM((d // LANES, LANES * t, LANES), F32)],
        compiler_params=_cparams("arbitrary", "arbitrary"),
        name="ssm_in",
    )(x, mod, g, w)


def _ssm_kernel(ut_ref, mt_ref, w4_ref, vt_ref, coef_ref, dk_ref, yt_ref, z_scr, s_scr,
                *, gb, nchunks, bsz):
    p2 = s_scr.shape[-1]
    for gi in range(gb):
        u = ut_ref[gi].astype(F32).T.astype(BF16)
        z = _bdot(u, w4_ref[gi])
        z_scr[gi, 0] = z[:, :p2]
        z_scr[gi, 1] = z[:, p2:]
    c1 = [jnp.broadcast_to(coef_ref[gi, 0:1, :], (bsz, p2)) for gi in range(gb)]
    c2 = [jnp.broadcast_to(coef_ref[gi, 1:2, :], (bsz, p2)) for gi in range(gb)]

    def body(c, carry):
        new = []
        for gi in range(gb):
            s1, s2 = carry[gi]
            rows = pl.ds(pl.multiple_of(c * bsz, bsz), bsz)
            s_scr[gi, rows, :] = s1
            n1 = c1[gi] * s1 + c2[gi] * s2 + z_scr[gi, 0, rows, :]
            n2 = c1[gi] * s2 - c2[gi] * s1 + z_scr[gi, 1, rows, :]
            new.append((n1, n2))
        return tuple(new)

    zero = jnp.zeros((bsz, p2), F32)
    lax.fori_loop(0, nchunks, body, tuple((zero, zero) for _ in range(gb)))
    for gi in range(gb):
        ut = ut_ref[gi]
        st = _bdot(vt_ref[gi], s_scr[gi].T.astype(BF16))
        yt = _bdot(mt_ref[gi], ut) + st + dk_ref[gi] * ut.astype(F32)
        yt_ref[gi] = yt.astype(yt_ref.dtype)


def _ssm(ut, mt, w4, vt, coef, dk, bsz, gb):
    g_, th, nch = ut.shape
    p2 = vt.shape[2]
    kern = functools.partial(_ssm_kernel, gb=gb, nchunks=nch // bsz, bsz=bsz)
    spec3 = lambda a, b: pl.BlockSpec((gb, a, b), lambda i: (i, 0, 0))
    return pl.pallas_call(
        kern,
        grid=(g_ // gb,),
        in_specs=[spec3(th, nch), spec3(th, th), spec3(th, 2 * p2), spec3(th, p2),
                  spec3(8, p2), spec3(th, 1)],
        out_specs=spec3(th, nch),
        out_shape=jax.ShapeDtypeStruct((g_, th, nch), BF16),
        scratch_shapes=[pltpu.VMEM((gb, 2, nch, p2), F32), pltpu.VMEM((gb, nch, p2), F32)],
        compiler_params=_cparams("arbitrary"),
        name="ssm",
    )(ut, mt, w4, vt, coef, dk)


def _gelu_tanh(y):
    return y * (0.5 * (1.0 + jnp.tanh(math.sqrt(2.0 / math.pi) * (y + 0.044715 * (y * y * y)))))


def _glu_out_kernel(yt_ref, x_ref, mod_ref, wg_ref, wo_ref, o_ref, y_scr, *, t):
    q = pl.program_id(1)
    bsz, cs, _ = x_ref.shape
    nlt = y_scr.shape[0]

    @pl.when(q == 0)
    def _():
        ng, th, nch = yt_ref.shape
        hh = th // t
        gl = LANES // hh
        for l in range(nlt):
            for k in range(t):
                yk = yt_ref[l * gl:(l + 1) * gl, k * hh:(k + 1) * hh, :].astype(F32)
                y_scr[l, pl.ds(k, nch, stride=t), :] = yk.reshape(LANES, nch).T

    rows = pl.ds(pl.multiple_of(q * (bsz * cs), bsz * cs), bsz * cs)
    y = jnp.concatenate([y_scr[l, rows, :] for l in range(nlt)], axis=1)
    z = _gelu_tanh(y)
    z = z * jax.nn.sigmoid(_bdot(z.astype(BF16), wg_ref[...]))
    o = _bdot(z.astype(BF16), wo_ref[...])
    for b in range(bsz):
        gate = mod_ref[b][2:3]
        for c in range(cs // t):
            tok = slice(c * t, (c + 1) * t)
            o_ref[b, tok, :] = x_ref[b, tok, :] + gate * o[(c * bsz + b) * t:(c * bsz + b + 1) * t]


def _glu_out(yt, x, mod, wg, wo, t, nq):
    bsz, seq, d = x.shape
    ng, th, _ = yt.shape
    cs = LANES * t // bsz // nq
    return pl.pallas_call(
        functools.partial(_glu_out_kernel, t=t),
        grid=(seq // (cs * nq), nq),
        in_specs=[pl.BlockSpec((ng, th, LANES), lambda o, q: (0, 0, o)),
                  pl.BlockSpec((bsz, cs, d), lambda o, q: (0, o * nq + q, 0)),
                  pl.BlockSpec((bsz, 6, d), lambda o, q: (0, 0, 0)),
                  pl.BlockSpec((d, d), lambda o, q: (0, 0)),
                  pl.BlockSpec((d, d), lambda o, q: (0, 0))],
        out_specs=pl.BlockSpec((bsz, cs, d), lambda o, q: (0, o * nq + q, 0)),
        out_shape=jax.ShapeDtypeStruct((bsz, seq, d), F32),
        scratch_shapes=[pltpu.VMEM((d // LANES, LANES * t, LANES), F32)],
        compiler_params=_cparams("arbitrary", "arbitrary"),
        name="glu_out",
    )(yt, x, mod, wg, wo)


def _chunks(total, step):
    return [(s, min(step, total - s)) for s in range(0, total, step)]


def _swiglu(h, w1_ref, w3_ref, w2_ref, sub):
    total = None
    for s, n in _chunks(w1_ref.shape[-1], sub):
        a = _bdot(h, w1_ref[:, s:s + n])
        b = _bdot(h, w3_ref[:, s:s + n])
        act = (a * jax.nn.sigmoid(a) * b).astype(BF16)
        y = _bdot(act, w2_ref[s:s + n, :])
        total = y if total is None else total + y
    return total


def _ffn_kernel(x_ref, mod_ref, g_ref, w1_ref, w3_ref, w2_ref, o_ref, *, sub):
    m = mod_ref[0]
    x = x_ref[...]
    h = _normmod(x, g_ref[...], m[3:4], m[4:5]).astype(BF16)
    o_ref[...] = x + m[5:6] * _swiglu(h, w1_ref.at[0], w3_ref.at[0], w2_ref.at[0], sub)


def _ffn(x, mod, g, w1, w3, w2, seq, tm, layer):
    n, d = x.shape
    dff = w1.shape[2]
    tps = seq // tm
    const = lambda i: (0, 0)
    lay = lambda i: (layer, 0, 0)
    return pl.pallas_call(
        functools.partial(_ffn_kernel, sub=512),
        grid=(n // tm,),
        in_specs=[pl.BlockSpec((tm, d), lambda i: (i, 0)),
                  pl.BlockSpec((1, 6, d), lambda i: (i // tps, 0, 0)),
                  pl.BlockSpec((1, d), const),
                  pl.BlockSpec((1, d, dff), lay, pipeline_mode=pl.Buffered(1)),
                  pl.BlockSpec((1, d, dff), lay, pipeline_mode=pl.Buffered(1)),
                  pl.BlockSpec((1, dff, d), lay, pipeline_mode=pl.Buffered(1))],
        out_specs=pl.BlockSpec((tm, d), lambda i: (i, 0)),
        out_shape=jax.ShapeDtypeStruct((n, d), F32),
        compiler_params=_cparams("arbitrary"),
        name="ffn",
    )(x, mod, g, w1, w3, w2)


def _pool_kernel(x_ref, halo_ref, mod_ref, g_ref, win_ref, wmix_ref, scale_ref, wo_ref, o_ref,
                 ext_ref, z_ref, *, tps):
    tm, d = x_ref.shape
    pg = d // len(POOL_WINDOWS)
    m = mod_ref[0]
    it = pl.program_id(0) % tps

    def project(rows):
        h = _normmod(rows, g_ref[...], m[0:1], m[1:2])
        return _bdot(h.astype(BF16), win_ref[...])

    ext_ref[0:POOL_HALO, :] = jnp.where(it == 0, 0.0, project(halo_ref[...]))
    ext_ref[POOL_HALO:, :] = project(x_ref[...])
    tpos = (it * tm + 1 + lax.broadcasted_iota(jnp.int32, (tm, 1), 0)).astype(F32)
    for gi, w in enumerate(POOL_WINDOWS):
        cols = slice(gi * pg, (gi + 1) * pg)
        s = ext_ref[:, cols]
        span = 1
        while span < w:
            s = s + pltpu.roll(s, span, 0)
            span *= 2
        mean = s[POOL_HALO:, :] * (1.0 / jnp.minimum(tpos, float(w)))
        dlt = (mean - ext_ref[POOL_HALO:, cols]).astype(BF16)
        z_ref[:, cols] = (_bdot(dlt, wmix_ref[gi]) * scale_ref[:, cols]).astype(BF16)
    o_ref[...] = x_ref[...] + m[2:3] * _bdot(z_ref[...], wo_ref[...])


def _pool(x, mod, g, win, wmix, scale, wo, seq, tm):
    n, d = x.shape
    tps = seq // tm
    hb = tm // POOL_HALO
    ng, pg, _ = wmix.shape
    return pl.pallas_call(
        functools.partial(_pool_kernel, tps=tps),
        grid=(n // tm,),
        in_specs=[pl.BlockSpec((tm, d), lambda i: (i, 0)),
                  pl.BlockSpec((POOL_HALO, d), lambda i: (jnp.maximum(i * hb - 1, 0), 0)),
                  pl.BlockSpec((1, 6, d), lambda i: (i // tps, 0, 0)),
                  pl.BlockSpec((1, d), lambda i: (0, 0)),
                  pl.BlockSpec((d, d), lambda i: (0, 0)),
                  pl.BlockSpec((ng, pg, pg), lambda i: (0, 0, 0)),
                  pl.BlockSpec((1, d), lambda i: (0, 0)),
                  pl.BlockSpec((d, d), lambda i: (0, 0))],
        out_specs=pl.BlockSpec((tm, d), lambda i: (i, 0)),
        out_shape=jax.ShapeDtypeStruct((n, d), F32),
        scratch_shapes=[pltpu.VMEM((tm + POOL_HALO, d), F32), pltpu.VMEM((tm, d), BF16)],
        compiler_params=_cparams("arbitrary"),
        name="pool",
    )(x, x, mod, g, win, wmix, scale, wo)


def _pack_bf16_pairs(v):
    w = v.shape[1] // 2
    bits = lambda a: lax.bitcast_convert_type(a.astype(BF16).astype(F32), jnp.int32)
    return bits(v[:, :w]) | lax.shift_right_logical(bits(v[:, w:]), 16)


def _unpack_bf16_pairs(p):
    hi = lax.bitcast_convert_type(p & jnp.int32(-65536), F32)
    lo = lax.bitcast_convert_type(lax.shift_left(p, 16), F32)
    return jnp.concatenate([hi, lo], axis=1)


def _router_kernel(x_ref, mod_ref, g_ref, rw_ref, rb_ref, h_ref, idx_ref, gate_ref, rank_ref,
                   cnt_ref, base_scr):
    @pl.when(pl.program_id(0) == 0)
    def _():
        base_scr[...] = jnp.zeros_like(base_scr)

    m = mod_ref[0]
    h = _normmod(x_ref[...], g_ref[...], m[3:4], m[4:5])
    h_ref[...] = _pack_bf16_pairs(h)
    split = lambda a: (a.astype(BF16), (a - a.astype(BF16).astype(F32)).astype(BF16))
    h_hi, h_lo = split(h)
    w_hi, w_lo = split(rw_ref[...])
    logits = (_bdot(h_hi, w_hi) + _bdot(h_hi, w_lo) + _bdot(h_lo, w_hi)) + rb_ref[...]
    ne = float(logits.shape[1])
    lane = lax.broadcasted_iota(jnp.int32, logits.shape, 1).astype(F32)
    v1 = jnp.max(logits, axis=1, keepdims=True)
    i1 = jnp.min(jnp.where(logits == v1, lane, ne), axis=1, keepdims=True)
    rest = jnp.where(lane == i1, -jnp.inf, logits)
    v2 = jnp.max(rest, axis=1, keepdims=True)
    i2 = jnp.min(jnp.where(rest == v2, lane, ne), axis=1, keepdims=True)
    e2 = jnp.exp(v2 - v1)
    den = 1.0 + e2
    two = lax.broadcasted_iota(jnp.int32, idx_ref.shape, 1)
    idx_ref[...] = jnp.where(two == 0, i1, i2).astype(jnp.int32)
    gate_ref[...] = jnp.where(two == 0, 1.0 / den, e2 / den)
    tm = logits.shape[0]
    oh1 = jnp.where(lane == i1, 1.0, 0.0)
    oh2 = jnp.where(lane == i2, 1.0, 0.0)
    both = oh1 + oh2
    before = (lax.broadcasted_iota(jnp.int32, (tm, tm), 1)
              < lax.broadcasted_iota(jnp.int32, (tm, tm), 0))
    prior = _bdot(jnp.where(before, 1.0, 0.0).astype(BF16), both.astype(BF16)) + base_scr[...]
    r1 = jnp.sum(oh1 * prior, axis=1, keepdims=True)
    r2 = jnp.sum(oh2 * prior, axis=1, keepdims=True)
    rank_ref[...] = jnp.where(two == 0, r1, r2).astype(jnp.int32)
    total = base_scr[...] + jnp.sum(both, axis=0, keepdims=True)
    base_scr[...] = total
    cnt_ref[...] = total.astype(jnp.int32)


def _router(x, mod, g, rw, rb, seq, tm):
    n, d = x.shape
    ne = rw.shape[1]
    tps = seq // tm
    return pl.pallas_call(
        _router_kernel,
        grid=(n // tm,),
        in_specs=[pl.BlockSpec((tm, d), lambda i: (i, 0)),
                  pl.BlockSpec((1, 6, d), lambda i: (i // tps, 0, 0)),
                  pl.BlockSpec((1, d), lambda i: (0, 0)),
                  pl.BlockSpec((d, ne), lambda i: (0, 0)),
                  pl.BlockSpec((1, ne), lambda i: (0, 0))],
        out_specs=[pl.BlockSpec((tm, d // 2), lambda i: (i, 0)),
                   pl.BlockSpec((tm, TOP_K), lambda i: (i, 0)),
                   pl.BlockSpec((tm, TOP_K), lambda i: (i, 0)),
                   pl.BlockSpec((tm, TOP_K), lambda i: (i, 0)),
                   pl.BlockSpec((1, ne), lambda i: (0, 0))],
        out_shape=[jax.ShapeDtypeStruct((n, d // 2), jnp.int32),
                   jax.ShapeDtypeStruct((n, TOP_K), jnp.int32),
                   jax.ShapeDtypeStruct((n, TOP_K), F32),
                   jax.ShapeDtypeStruct((n, TOP_K), jnp.int32),
                   jax.ShapeDtypeStruct((1, ne), jnp.int32)],
        scratch_shapes=[pltpu.VMEM((1, ne), F32)],
        compiler_params=_cparams("arbitrary"),
        name="router",
    )(x, mod, g, rw, rb)


def _count_le(sorted_vals, queries):
    return jnp.sum((sorted_vals[None, :] <= queries[:, None]).astype(jnp.int32), axis=1)


def _routing_tables(idx, rank, cnt, ne, tile):
    n = idx.shape[0]
    padded = ((cnt + tile - 1) // tile) * tile
    off = jnp.cumsum(padded) - padded
    onehot = idx[:, :, None] == jnp.arange(ne, dtype=jnp.int32)[None, None, :]
    pos = rank + jnp.sum(jnp.where(onehot, off[None, None, :], 0), axis=2)
    n_tiles = (TOP_K * n) // tile + ne
    used = jnp.sum(padded) // tile
    tile_ids = jnp.arange(n_tiles, dtype=jnp.int32)
    tile_expert = _count_le(jnp.cumsum(padded), tile_ids * tile)
    last_used = jnp.maximum(used - 1, 0)
    tile_valid = (tile_ids < used).astype(jnp.int32)
    tile_src = jnp.minimum(tile_ids, last_used).astype(jnp.int32)
    tile_expert = jnp.minimum(tile_expert, ne - 1).astype(jnp.int32)
    tile_expert = jnp.where(tile_valid == 1, tile_expert, tile_expert[last_used])
    tile_rows = jnp.clip(cnt[tile_expert] - (tile_ids * tile - off[tile_expert]), 0, tile)
    tile_rows = jnp.where(tile_valid == 1, tile_rows, 0).astype(jnp.int32)
    return dict(pos=pos, tile_expert=tile_expert, tile_valid=tile_valid, tile_src=tile_src,
                tile_rows=tile_rows, n_tiles=n_tiles)


def _sc_mesh():
    return plsc.VectorSubcoreMesh(core_axis_name="c", subcore_axis_name="s",
                                  num_cores=SC_CORES, num_subcores=SC_SUBCORES)


def _dispatch(hp, pos0, pos1, rows_out):
    n, w = hp.shape
    per_w = n // SC_WORKERS
    ch = SC_ROWS
    nch = per_w // ch
    assert per_w * SC_WORKERS == n and nch * ch == per_w and nch % 2 == 0

    @functools.partial(
        pl.kernel, mesh=_sc_mesh(),
        out_type=jax.ShapeDtypeStruct((rows_out, w), hp.dtype),
        scratch_types=[pltpu.VMEM((per_w,), jnp.int32), pltpu.VMEM((per_w,), jnp.int32),
                       pltpu.VMEM((ch, w), hp.dtype), pltpu.VMEM((ch, w), hp.dtype)]
        + [pltpu.SemaphoreType.DMA] * 4,
        name="dispatch",
    )
    def k(hp_hbm, p0_hbm, p1_hbm, out_hbm, i0_v, i1_v, r0_v, r1_v, l0, l1, s0, s1):
        base = (lax.axis_index("s") * SC_CORES + lax.axis_index("c")) * per_w
        pltpu.sync_copy(p0_hbm.at[pl.ds(base, per_w)], i0_v)
        pltpu.sync_copy(p1_hbm.at[pl.ds(base, per_w)], i1_v)
        bufs, lsem, ssem = (r0_v, r1_v), (l0, l1), (s0, s1)

        def load(j, b):
            return pltpu.make_async_copy(hp_hbm.at[pl.ds(base + j * ch, ch)], bufs[b], lsem[b])

        def scatter(j, b, idx):
            return pltpu.make_async_copy(bufs[b], out_hbm.at[idx.at[pl.ds(j * ch, ch)]], ssem[b])

        load(0, 0).start()

        @pl.loop(0, nch, step=2)
        def _(j):
            for b in range(2):
                jj = j + b
                load(jj, b).wait()

                @pl.when(jj + 1 < nch)
                def _():
                    load(jj + 1, 1 - b).start()

                c0 = scatter(jj, b, i0_v)
                c1 = scatter(jj, b, i1_v)
                c0.start()
                c1.start()
                c0.wait()
                c1.wait()

    return k(hp, pos0, pos1)


def _gather_pairs(ys, pos0, pos1):
    n = pos0.shape[0]
    w = ys.shape[1]
    per_w = n // SC_WORKERS
    ch = SC_ROWS
    nch = per_w // ch
    assert per_w * SC_WORKERS == n and nch * ch == per_w
    out = jax.ShapeDtypeStruct((n, w), ys.dtype)

    @functools.partial(
        pl.kernel, mesh=_sc_mesh(), out_type=(out, out),
        scratch_types=[pltpu.VMEM((per_w,), jnp.int32), pltpu.VMEM((per_w,), jnp.int32),
                       pltpu.VMEM((ch, w), ys.dtype), pltpu.VMEM((ch, w), ys.dtype)]
        + [pltpu.SemaphoreType.DMA] * 4,
        name="gather_pairs",
    )
    def k(ys_hbm, p0_hbm, p1_hbm, a_hbm, b_hbm, i0_v, i1_v, ra_v, rb_v, ga, gb, wa, wb):
        base = (lax.axis_index("s") * SC_CORES + lax.axis_index("c")) * per_w
        pltpu.sync_copy(p0_hbm.at[pl.ds(base, per_w)], i0_v)
        pltpu.sync_copy(p1_hbm.at[pl.ds(base, per_w)], i1_v)

        def gather(j, idx, buf, sem):
            return pltpu.make_async_copy(ys_hbm.at[idx.at[pl.ds(j * ch, ch)]], buf, sem)

        def write(j, buf, dst, sem):
            return pltpu.make_async_copy(buf, dst.at[pl.ds(base + j * ch, ch)], sem)

        gather(0, i0_v, ra_v, ga).start()
        gather(0, i1_v, rb_v, gb).start()

        @pl.loop(0, nch)
        def _(j):
            gather(j, i0_v, ra_v, ga).wait()
            write(j, ra_v, a_hbm, wa).start()
            gather(j, i1_v, rb_v, gb).wait()
            write(j, rb_v, b_hbm, wb).start()
            write(j, ra_v, a_hbm, wa).wait()

            @pl.when(j + 1 < nch)
            def _():
                gather(j + 1, i0_v, ra_v, ga).start()

            write(j, rb_v, b_hbm, wb).wait()

            @pl.when(j + 1 < nch)
            def _():
                gather(j + 1, i1_v, rb_v, gb).start()

    return k(ys, pos0, pos1)


def _expert_kernel(te_ref, tv_ref, tsrc_ref, tr_ref, xs_ref, w1_ref, w3_ref, w2_ref,
                   ys_ref, *, sub):
    j = pl.program_id(0)

    @pl.when(tv_ref[j] == 1)
    def _():
        x = _unpack_bf16_pairs(xs_ref[...])
        row = lax.broadcasted_iota(jnp.int32, x.shape, 0)
        x = jnp.where(row < tr_ref[j], x, 0.0).astype(BF16)
        ys_ref[...] = _pack_bf16_pairs(_swiglu(x, w1_ref.at[0], w3_ref.at[0], w2_ref.at[0], sub))

    @pl.when(tv_ref[j] == 0)
    def _():
        ys_ref[...] = jnp.zeros_like(ys_ref)


def _experts(xs, w1, w3, w2, rt, tile):
    rows, wp = xs.shape
    ne, d, dff = w1.shape
    return pl.pallas_call(
        functools.partial(_expert_kernel, sub=512),
        grid_spec=pltpu.PrefetchScalarGridSpec(
            num_scalar_prefetch=4,
            grid=(rt['n_tiles'],),
            in_specs=[pl.BlockSpec((tile, wp), lambda j, te, tv, tsrc, tr: (tsrc[j], 0)),
                      pl.BlockSpec((1, d, dff), lambda j, te, tv, tsrc, tr: (te[j], 0, 0)),
                      pl.BlockSpec((1, d, dff), lambda j, te, tv, tsrc, tr: (te[j], 0, 0)),
                      pl.BlockSpec((1, dff, d), lambda j, te, tv, tsrc, tr: (te[j], 0, 0))],
            out_specs=pl.BlockSpec((tile, wp), lambda j, te, tv, tsrc, tr: (j, 0))),
        out_shape=jax.ShapeDtypeStruct((rows, wp), jnp.int32),
        compiler_params=_cparams("arbitrary"),
        name="experts",
    )(rt['tile_expert'], rt['tile_valid'], rt['tile_src'], rt['tile_rows'], xs, w1, w3, w2)


def _combine_kernel(a_ref, b_ref, gate_ref, x_ref, mod_ref, fg_ref, o_ref, *, final_norm):
    m = mod_ref[0]
    g = gate_ref[...]
    y = g[:, 0:1] * _unpack_bf16_pairs(a_ref[...]) + g[:, 1:2] * _unpack_bf16_pairs(b_ref[...])
    xn = x_ref[...] + m[5:6] * y
    if final_norm:
        ms = jnp.mean(xn * xn, axis=-1, keepdims=True)
        xn = (xn * lax.rsqrt(ms + EPS)) * fg_ref[...]
    o_ref[...] = xn


def _combine(a, b, gates, x, mod, fg, seq, tm, final_norm, tile0):
    n, d = x.shape
    tps = seq // tm
    part = lambda w: pl.BlockSpec((tm, w), lambda i: (i, 0))
    full = lambda w: pl.BlockSpec((tm, w), lambda i: (i + tile0, 0))
    return pl.pallas_call(
        functools.partial(_combine_kernel, final_norm=final_norm),
        grid=(a.shape[0] // tm,),
        in_specs=[part(d // 2), part(d // 2), full(TOP_K), full(d),
                  pl.BlockSpec((1, 6, d), lambda i: ((i + tile0) // tps, 0, 0)),
                  pl.BlockSpec((1, d), lambda i: (0, 0))],
        out_specs=full(d),
        out_shape=jax.ShapeDtypeStruct((n, d), F32),
        input_output_aliases={3: 0},
        compiler_params=_cparams("arbitrary"),
        name="combine",
    )(a, b, gates, x, mod, fg)


def _pick(n, prefs):
    for p in prefs:
        if n % p == 0:
            return p
    return n


def kernel(x, c, ada_w, ada_b, norm_g, ssm_in, ssm_log_dt, ssm_lam_re, ssm_lam_im,
           ssm_b_re, ssm_b_im, ssm_c_re, ssm_c_im, ssm_d, ssm_glu, ssm_out,
           pool_in, pool_mix, pool_scale, pool_out, ffn_w1, ffn_w3, ffn_w2,
           router_w, router_b, moe_w1, moe_w3, moe_w2, final_g):
    bsz, seq, d = x.shape
    depth = ada_w.shape[0]
    n = bsz * seq
    ne = router_w.shape[-1]
    g_, p_, h_ = ssm_b_re.shape[1:]
    t = SSM_T
    if seq % (LANES * t // bsz) != 0 or bsz != 8:
        raise NotImplementedError("state-space kernels assume batch 8 and seq % 256 == 0")
    if depth % 2 != 0:
        raise NotImplementedError("the final RMSNorm is fused into an expert layer's combine")
    tm = _pick(seq, (512, 256, 128, 64, 32, 16))
    tile = _pick(TOP_K * n, (512, 256, 128, 64, 32, 16))
    gb = _pick(g_, (4, 2, 1))

    stack = lambda w: w.astype(BF16).reshape((-1,) + w.shape[2:])
    ew1, ew3, ew2 = stack(moe_w1), stack(moe_w3), stack(moe_w2)
    fw1, fw3, fw2 = ffn_w1.astype(BF16), ffn_w3.astype(BF16), ffn_w2.astype(BF16)
    mod = _ada(c, ada_w, ada_b).reshape(depth, bsz, 6, d)
    xf = x.reshape(n, d)
    fg = final_g.reshape(1, d)
    for i in range(depth):
        j = i // 2
        mod_i = mod[i]
        g_a = norm_g[i, 0].reshape(1, d)
        g_b = norm_g[i, 1].reshape(1, d)
        if i % 2 == 0:
            ops = _ssm_operators(ssm_log_dt[j], ssm_lam_re[j], ssm_lam_im[j], ssm_b_re[j],
                                 ssm_b_im[j], ssm_c_re[j], ssm_c_im[j], ssm_d[j], t)
            ut = _ssm_in(xf.reshape(bsz, seq, d), mod_i, g_a, ssm_in[j].astype(BF16), t, h_, 4)
            yt = _ssm(ut, *ops, bsz, gb)
            xf = _glu_out(yt, xf.reshape(bsz, seq, d), mod_i, ssm_glu[j].astype(BF16),
                          ssm_out[j].astype(BF16), t, 4).reshape(n, d)
            xf = _ffn(xf, mod_i, g_b, fw1, fw3, fw2, seq, tm, layer=j)
        else:
            xf = _pool(xf, mod_i, g_a, pool_in[j].astype(BF16), pool_mix[j].astype(BF16),
                       pool_scale[j].reshape(1, d), pool_out[j].astype(BF16), seq, 2 * tm)
            h, idx, gates, rank, cnt = _router(xf, mod_i, g_b, router_w[j],
                                               router_b[j].reshape(1, ne), seq, tm)
            rt = _routing_tables(idx, rank, cnt[0], ne, tile)
            pos0, pos1 = rt['pos'][:, 0], rt['pos'][:, 1]
            xs = _dispatch(h, pos0, pos1, rt['n_tiles'] * tile)
            rt['tile_expert'] = rt['tile_expert'] + j * ne
            ys = _experts(xs, ew1, ew3, ew2, rt, tile)
            npc = n // COMBINE_PIECES
            for q in range(COMBINE_PIECES):
                piece = slice(q * npc, (q + 1) * npc)
                ya, yb = _gather_pairs(ys, pos0[piece], pos1[piece])
                xf = _combine(ya, yb, gates, xf, mod_i, fg, seq, tm,
                              final_norm=(i == depth - 1), tile0=q * (npc // tm))
    return xf.reshape(bsz, seq, d)
```

```python
import functools
import math

import jax
import jax.numpy as jnp
from jax import lax
from jax.experimental import pallas as pl
from jax.experimental.pallas import tpu as pltpu
from jax.experimental.pallas import tpu_sc as plsc

F32 = jnp.float32
BF16 = jnp.bfloat16
EPS = 1e-6
POOL_WINDOWS = (2, 4, 8, 16)
POOL_HALO = 16
SSM_T = 16
LANES = 128
TOP_K = 2
SC_CORES, SC_SUBCORES = 2, 16
SC_WORKERS = SC_CORES * SC_SUBCORES
SC_ROWS = 64
COMBINE_PIECES = 4
VMEM_LIMIT = 58 * 1024 * 1024


def _cparams(*sem):
    return pltpu.CompilerParams(dimension_semantics=sem, vmem_limit_bytes=VMEM_LIMIT)


def _normmod(x, g, shift, scale):
    ms = jnp.mean(x * x, axis=-1, keepdims=True)
    y = x * lax.rsqrt(ms + EPS)
    return (y * g) * (1.0 + scale) + shift


def _bdot(a, b):
    return jnp.dot(a, b, preferred_element_type=F32)


def _ada_kernel(c_ref, w_ref, b_ref, o_ref):
    c = c_ref[...]
    cond = c * jax.nn.sigmoid(c)
    o_ref[0] = jnp.dot(cond, w_ref[0], precision=lax.Precision.HIGHEST,
                       preferred_element_type=F32) + b_ref[0]


def _ada(c, ada_w, ada_b):
    depth, d, d6 = ada_w.shape
    bsz = c.shape[0]
    tn = d6 // 4
    return pl.pallas_call(
        _ada_kernel,
        grid=(depth, d6 // tn),
        in_specs=[pl.BlockSpec((bsz, d), lambda l, j: (0, 0)),
                  pl.BlockSpec((1, d, tn), lambda l, j: (l, 0, j)),
                  pl.BlockSpec((1, 1, tn), lambda l, j: (l, 0, j))],
        out_specs=pl.BlockSpec((1, bsz, tn), lambda l, j: (l, 0, j)),
        out_shape=jax.ShapeDtypeStruct((depth, bsz, d6), F32),
        compiler_params=_cparams("arbitrary", "arbitrary"),
        name="ada",
    )(c, ada_w, ada_b.reshape(depth, 1, d6))


def _ssm_operators(log_dt, lam_re, lam_im, b_re, b_im, c_re, c_im, d_skip, t):
    g_, p_, h_ = b_re.shape
    dt = jnp.exp(log_dt.astype(F32))[:, None]
    lr = lam_re.astype(F32)
    li = lam_im.astype(F32)
    mag = jnp.exp(lr * dt)
    a_re = mag * jnp.cos(li * dt)
    a_im = mag * jnp.sin(li * dt)
    den = lr * lr + li * li
    nr = a_re - 1
    coef_re = (nr * lr + a_im * li) / den
    coef_im = (a_im * lr - nr * li) / den
    br = b_re.astype(F32)
    bi = b_im.astype(F32)
    bbar_re = coef_re[..., None] * br - coef_im[..., None] * bi
    bbar_im = coef_re[..., None] * bi + coef_im[..., None] * br
    steps = jnp.arange(0, t + 1, dtype=F32)[None, :, None]
    ang = (li * dt)[:, None, :] * steps
    pmag = jnp.exp((lr * dt)[:, None, :] * steps)
    pw_re = pmag * jnp.cos(ang)
    pw_im = pmag * jnp.sin(ang)
    cr = c_re.astype(F32)
    ci = c_im.astype(F32)
    er = cr[:, None] * pw_re[:, :, None, :] - ci[:, None] * pw_im[:, :, None, :]
    ei = cr[:, None] * pw_im[:, :, None, :] + ci[:, None] * pw_re[:, :, None, :]
    tg = lambda a: jnp.transpose(a, (1, 2, 0))
    pt_re, pt_im = tg(pw_re[:, :t])[:, None], tg(pw_im[:, :t])[:, None]
    ct_re, ct_im = tg(cr)[None], tg(ci)[None]
    er_l = (ct_re * pt_re - ct_im * pt_im)[..., None]
    ei_l = (ct_re * pt_im + ct_im * pt_re)[..., None]
    br_l = jnp.transpose(bbar_re, (1, 0, 2))[None, None]
    bi_l = jnp.transpose(bbar_im, (1, 0, 2))[None, None]
    kk = jnp.sum(er_l * br_l - ei_l * bi_l, axis=2)
    kk = jnp.transpose(kk, (2, 0, 1, 3))
    kz = jnp.pad(kk, ((0, 0), (t - 1, 0), (0, 0), (0, 0)))
    cols = [kz[:, t - 1 - j:2 * t - 1 - j] for j in range(t)]
    mt = jnp.stack(cols, axis=3).reshape(g_, t * h_, t * h_)
    rev_re = pw_re[:, :t][:, ::-1, None, :]
    rev_im = pw_im[:, :t][:, ::-1, None, :]
    bt_re = jnp.transpose(bbar_re, (0, 2, 1))[:, None]
    bt_im = jnp.transpose(bbar_im, (0, 2, 1))[:, None]
    w_re = (rev_re * bt_re - rev_im * bt_im).reshape(g_, t * h_, p_)
    w_im = (rev_re * bt_im + rev_im * bt_re).reshape(g_, t * h_, p_)
    w4 = jnp.concatenate([w_re, w_im, w_im, w_re], axis=-1)
    vt = jnp.concatenate([er[:, 1:].reshape(g_, t * h_, p_),
                          -ei[:, 1:].reshape(g_, t * h_, p_)], axis=-1)
    at_re, at_im = pw_re[:, t], pw_im[:, t]
    c1 = jnp.concatenate([at_re, at_re], axis=-1)
    c2 = jnp.concatenate([-at_im, at_im], axis=-1)
    coef = jnp.concatenate([c1[:, None], c2[:, None], jnp.zeros((g_, 6, 2 * p_), F32)], axis=1)
    dk = jnp.tile(d_skip.astype(F32)[:, None, :], (1, t, 1)).reshape(g_, t * h_, 1)
    return mt.astype(BF16), w4.astype(BF16), vt.astype(BF16), coef, dk


def _ssm_in_kernel(x_ref, mod_ref, g_ref, w_ref, ut_ref, h_scr, u_scr, *, t):
    q = pl.program_id(1)
    bsz, cs, _ = x_ref.shape
    nlt = u_scr.shape[0]
    for b in range(bsz):
        m = mod_ref[b]
        h = _normmod(x_ref[b], g_ref[...], m[0:1], m[1:2]).astype(BF16)
        for c in range(cs // t):
            h_scr[(c * bsz + b) * t:(c * bsz + b + 1) * t, :] = h[c * t:(c + 1) * t, :]
    u = _bdot(h_scr[...], w_ref[...])
    rows = pl.ds(pl.multiple_of(q * (bsz * cs), bsz * cs), bsz * cs)
    for l in range(nlt):
        u_scr[l, rows, :] = u[:, l * LANES:(l + 1) * LANES]

    @pl.when(q == pl.num_programs(1) - 1)
    def _():
        ng, th, nch = ut_ref.shape
        hh = th // t
        gl = LANES // hh
        for l in range(nlt):
            for k in range(t):
                uk = u_scr[l, pl.ds(k, nch, stride=t), :]
                ut_ref[l * gl:(l + 1) * gl, k * hh:(k + 1) * hh, :] = (
                    uk.T.reshape(gl, hh, nch).astype(BF16))


def _ssm_in(x, mod, g, w, t, hh, nq):
    bsz, seq, d = x.shape
    ng = d // hh
    cs = LANES * t // bsz // nq
    return pl.pallas_call(
        functools.partial(_ssm_in_kernel, t=t),
        grid=(seq // (cs * nq), nq),
        in_specs=[pl.BlockSpec((bsz, cs, d), lambda o, q: (0, o * nq + q, 0)),
                  pl.BlockSpec((bsz, 6, d), lambda o, q: (0, 0, 0)),
                  pl.BlockSpec((1, d), lambda o, q: (0, 0)),
                  pl.BlockSpec((d, d), lambda o, q: (0, 0))],
        out_specs=pl.BlockSpec((ng, t * hh, LANES), lambda o, q: (0, 0, o)),
        out_shape=jax.ShapeDtypeStruct((ng, t * hh, bsz * seq // t), BF16),
        scratch_shapes=[pltpu.VMEM((bsz * cs, d), BF16),
                        pltpu.VMEM((d // LANES, LANES * t, LANES), F32)],
        compiler_params=_cparams("arbitrary", "arbitrary"),
        name="ssm_in",
    )(x, mod, g, w)


def _ssm_kernel(ut_ref, mt_ref, w4_ref, vt_ref, coef_ref, dk_ref, yt_ref, z_scr, s_scr,
                *, gb, nchunks, bsz):
    p2 = s_scr.shape[-1]
    for gi in range(gb):
        u = ut_ref[gi].astype(F32).T.astype(BF16)
        z = _bdot(u, w4_ref[gi])
        z_scr[gi, 0] = z[:, :p2]
        z_scr[gi, 1] = z[:, p2:]
    c1 = [jnp.broadcast_to(coef_ref[gi, 0:1, :], (bsz, p2)) for gi in range(gb)]
    c2 = [jnp.broadcast_to(coef_ref[gi, 1:2, :], (bsz, p2)) for gi in range(gb)]

    def body(c, carry):
        new = []
        for gi in range(gb):
            s1, s2 = carry[gi]
            rows = pl.ds(pl.multiple_of(c * bsz, bsz), bsz)
            s_scr[gi, rows, :] = s1
            n1 = c1[gi] * s1 + c2[gi] * s2 + z_scr[gi, 0, rows, :]
            n2 = c1[gi] * s2 - c2[gi] * s1 + z_scr[gi, 1, rows, :]
            new.append((n1, n2))
        return tuple(new)

    zero = jnp.zeros((bsz, p2), F32)
    lax.fori_loop(0, nchunks, body, tuple((zero, zero) for _ in range(gb)))
    for gi in range(gb):
        ut = ut_ref[gi]
        st = _bdot(vt_ref[gi], s_scr[gi].T.astype(BF16))
        yt = _bdot(mt_ref[gi], ut) + st + dk_ref[gi] * ut.astype(F32)
        yt_ref[gi] = yt.astype(yt_ref.dtype)


def _ssm(ut, mt, w4, vt, coef, dk, bsz, gb):
    g_, th, nch = ut.shape
    p2 = vt.shape[2]
    kern = functools.partial(_ssm_kernel, gb=gb, nchunks=nch // bsz, bsz=bsz)
    spec3 = lambda a, b: pl.BlockSpec((gb, a, b), lambda i: (i, 0, 0))
    return pl.pallas_call(
        kern,
        grid=(g_ // gb,),
        in_specs=[spec3(th, nch), spec3(th, th), spec3(th, 2 * p2), spec3(th, p2),
                  spec3(8, p2), spec3(th, 1)],
        out_specs=spec3(th, nch),
        out_shape=jax.ShapeDtypeStruct((g_, th, nch), BF16),
        scratch_shapes=[pltpu.VMEM((gb, 2, nch, p2), F32), pltpu.VMEM((gb, nch, p2), F32)],
        compiler_params=_cparams("arbitrary"),
        name="ssm",
    )(ut, mt, w4, vt, coef, dk)


def _gelu_tanh(y):
    return y * (0.5 * (1.0 + jnp.tanh(math.sqrt(2.0 / math.pi) * (y + 0.044715 * (y * y * y)))))


def _glu_out_kernel(yt_ref, x_ref, mod_ref, wg_ref, wo_ref, o_ref, y_scr, *, t):
    q = pl.program_id(1)
    bsz, cs, _ = x_ref.shape
    nlt = y_scr.shape[0]

    @pl.when(q == 0)
    def _():
        ng, th, nch = yt_ref.shape
        hh = th // t
        gl = LANES // hh
        for l in range(nlt):
            for k in range(t):
                yk = yt_ref[l * gl:(l + 1) * gl, k * hh:(k + 1) * hh, :].astype(F32)
                y_scr[l, pl.ds(k, nch, stride=t), :] = yk.reshape(LANES, nch).T

    rows = pl.ds(pl.multiple_of(q * (bsz * cs), bsz * cs), bsz * cs)
    y = jnp.concatenate([y_scr[l, rows, :] for l in range(nlt)], axis=1)
    z = _gelu_tanh(y)
    z = z * jax.nn.sigmoid(_bdot(z.astype(BF16), wg_ref[...]))
    o = _bdot(z.astype(BF16), wo_ref[...])
    for b in range(bsz):
        gate = mod_ref[b][2:3]
        for c in range(cs // t):
            tok = slice(c * t, (c + 1) * t)
            o_ref[b, tok, :] = x_ref[b, tok, :] + gate * o[(c * bsz + b) * t:(c * bsz + b + 1) * t]


def _glu_out(yt, x, mod, wg, wo, t, nq):
    bsz, seq, d = x.shape
    ng, th, _ = yt.shape
    cs = LANES * t // bsz // nq
    return pl.pallas_call(
        functools.partial(_glu_out_kernel, t=t),
        grid=(seq // (cs * nq), nq),
        in_specs=[pl.BlockSpec((ng, th, LANES), lambda o, q: (0, 0, o)),
                  pl.BlockSpec((bsz, cs, d), lambda o, q: (0, o * nq + q, 0)),
                  pl.BlockSpec((bsz, 6, d), lambda o, q: (0, 0, 0)),
                  pl.BlockSpec((d, d), lambda o, q: (0, 0)),
                  pl.BlockSpec((d, d), lambda o, q: (0, 0))],
        out_specs=pl.BlockSpec((bsz, cs, d), lambda o, q: (0, o * nq + q, 0)),
        out_shape=jax.ShapeDtypeStruct((bsz, seq, d), F32),
        scratch_shapes=[pltpu.VMEM((d // LANES, LANES * t, LANES), F32)],
        compiler_params=_cparams("arbitrary", "arbitrary"),
        name="glu_out",
    )(yt, x, mod, wg, wo)


def _chunks(total, step):
    return [(s, min(step, total - s)) for s in range(0, total, step)]


def _swiglu(h, w1_ref, w3_ref, w2_ref, sub):
    total = None
    for s, n in _chunks(w1_ref.shape[-1], sub):
        a = _bdot(h, w1_ref[:, s:s + n])
        b = _bdot(h, w3_ref[:, s:s + n])
        act = (a * jax.nn.sigmoid(a) * b).astype(BF16)
        y = _bdot(act, w2_ref[s:s + n, :])
        total = y if total is None else total + y
    return total


def _ffn_kernel(x_ref, mod_ref, g_ref, w1_ref, w3_ref, w2_ref, o_ref, *, sub):
    m = mod_ref[0]
    x = x_ref[...]
    h = _normmod(x, g_ref[...], m[3:4], m[4:5]).astype(BF16)
    o_ref[...] = x + m[5:6] * _swiglu(h, w1_ref.at[0], w3_ref.at[0], w2_ref.at[0], sub)


def _ffn(x, mod, g, w1, w3, w2, seq, tm, layer):
    n, d = x.shape
    dff = w1.shape[2]
    tps = seq // tm
    const = lambda i: (0, 0)
    lay = lambda i: (layer, 0, 0)
    return pl.pallas_call(
        functools.partial(_ffn_kernel, sub=512),
        grid=(n // tm,),
        in_specs=[pl.BlockSpec((tm, d), lambda i: (i, 0)),
                  pl.BlockSpec((1, 6, d), lambda i: (i // tps, 0, 0)),
                  pl.BlockSpec((1, d), const),
                  pl.BlockSpec((1, d, dff), lay, pipeline_mode=pl.Buffered(1)),
                  pl.BlockSpec((1, d, dff), lay, pipeline_mode=pl.Buffered(1)),
                  pl.BlockSpec((1, dff, d), lay, pipeline_mode=pl.Buffered(1))],
        out_specs=pl.BlockSpec((tm, d), lambda i: (i, 0)),
        out_shape=jax.ShapeDtypeStruct((n, d), F32),
        compiler_params=_cparams("arbitrary"),
        name="ffn",
    )(x, mod, g, w1, w3, w2)


def _pool_kernel(x_ref, halo_ref, mod_ref, g_ref, win_ref, wmix_ref, scale_ref, wo_ref, o_ref,
                 ext_ref, z_ref, *, tps):
    tm, d = x_ref.shape
    pg = d // len(POOL_WINDOWS)
    m = mod_ref[0]
    it = pl.program_id(0) % tps

    def project(rows):
        h = _normmod(rows, g_ref[...], m[0:1], m[1:2])
        return _bdot(h.astype(BF16), win_ref[...])

    ext_ref[0:POOL_HALO, :] = jnp.where(it == 0, 0.0, project(halo_ref[...]))
    ext_ref[POOL_HALO:, :] = project(x_ref[...])
    tpos = (it * tm + 1 + lax.broadcasted_iota(jnp.int32, (tm, 1), 0)).astype(F32)
    for gi, w in enumerate(POOL_WINDOWS):
        cols = slice(gi * pg, (gi + 1) * pg)
        s = ext_ref[:, cols]
        span = 1
        while span < w:
            s = s + pltpu.roll(s, span, 0)
            span *= 2
        mean = s[POOL_HALO:, :] * (1.0 / jnp.minimum(tpos, float(w)))
        dlt = (mean - ext_ref[POOL_HALO:, cols]).astype(BF16)
        z_ref[:, cols] = (_bdot(dlt, wmix_ref[gi]) * scale_ref[:, cols]).astype(BF16)
    o_ref[...] = x_ref[...] + m[2:3] * _bdot(z_ref[...], wo_ref[...])


def _pool(x, mod, g, win, wmix, scale, wo, seq, tm):
    n, d = x.shape
    tps = seq // tm
    hb = tm // POOL_HALO
    ng, pg, _ = wmix.shape
    return pl.pallas_call(
        functools.partial(_pool_kernel, tps=tps),
        grid=(n // tm,),
        in_specs=[pl.BlockSpec((tm, d), lambda i: (i, 0)),
                  pl.BlockSpec((POOL_HALO, d), lambda i: (jnp.maximum(i * hb - 1, 0), 0)),
                  pl.BlockSpec((1, 6, d), lambda i: (i // tps, 0, 0)),
                  pl.BlockSpec((1, d), lambda i: (0, 0)),
                  pl.BlockSpec((d, d), lambda i: (0, 0)),
                  pl.BlockSpec((ng, pg, pg), lambda i: (0, 0, 0)),
                  pl.BlockSpec((1, d), lambda i: (0, 0)),
                  pl.BlockSpec((d, d), lambda i: (0, 0))],
        out_specs=pl.BlockSpec((tm, d), lambda i: (i, 0)),
        out_shape=jax.ShapeDtypeStruct((n, d), F32),
        scratch_shapes=[pltpu.VMEM((tm + POOL_HALO, d), F32), pltpu.VMEM((tm, d), BF16)],
        compiler_params=_cparams("arbitrary"),
        name="pool",
    )(x, x, mod, g, win, wmix, scale, wo)


def _pack_bf16_pairs(v):
    w = v.shape[1] // 2
    bits = lambda a: lax.bitcast_convert_type(a.astype(BF16).astype(F32), jnp.int32)
    return bits(v[:, :w]) | lax.shift_right_logical(bits(v[:, w:]), 16)


def _unpack_bf16_pairs(p):
    hi = lax.bitcast_convert_type(p & jnp.int32(-65536), F32)
    lo = lax.bitcast_convert_type(lax.shift_left(p, 16), F32)
    return jnp.concatenate([hi, lo], axis=1)


def _router_kernel(x_ref, mod_ref, g_ref, rw_ref, rb_ref, h_ref, idx_ref, gate_ref, rank_ref,
                   cnt_ref, base_scr, before_scr):
    @pl.when(pl.program_id(0) == 0)
    def _():
        base_scr[...] = jnp.zeros_like(base_scr)
        shape = before_scr.shape
        before = lax.broadcasted_iota(jnp.int32, shape, 1) < lax.broadcasted_iota(jnp.int32, shape, 0)
        before_scr[...] = jnp.where(before, 1.0, 0.0).astype(BF16)

    m = mod_ref[0]
    h = _normmod(x_ref[...], g_ref[...], m[3:4], m[4:5])
    h_ref[...] = _pack_bf16_pairs(h)
    split = lambda a: (a.astype(BF16), (a - a.astype(BF16).astype(F32)).astype(BF16))
    h_hi, h_lo = split(h)
    w_hi, w_lo = split(rw_ref[...])
    logits = (_bdot(h_hi, w_hi) + _bdot(h_hi, w_lo) + _bdot(h_lo, w_hi)) + rb_ref[...]
    ne = float(logits.shape[1])
    lane = lax.broadcasted_iota(jnp.int32, logits.shape, 1).astype(F32)
    v1 = jnp.max(logits, axis=1, keepdims=True)
    i1 = jnp.min(jnp.where(logits == v1, lane, ne), axis=1, keepdims=True)
    rest = jnp.where(lane == i1, -jnp.inf, logits)
    v2 = jnp.max(rest, axis=1, keepdims=True)
    i2 = jnp.min(jnp.where(rest == v2, lane, ne), axis=1, keepdims=True)
    e2 = jnp.exp(v2 - v1)
    den = 1.0 + e2
    two = lax.broadcasted_iota(jnp.int32, idx_ref.shape, 1)
    idx_ref[...] = jnp.where(two == 0, i1, i2).astype(jnp.int32)
    gate_ref[...] = jnp.where(two == 0, 1.0 / den, e2 / den)
    tm = logits.shape[0]
    oh1 = jnp.where(lane == i1, 1.0, 0.0)
    oh2 = jnp.where(lane == i2, 1.0, 0.0)
    both = oh1 + oh2
    prior = _bdot(before_scr[...], both.astype(BF16)) + base_scr[...]
    r1 = jnp.sum(oh1 * prior, axis=1, keepdims=True)
    r2 = jnp.sum(oh2 * prior, axis=1, keepdims=True)
    rank_ref[...] = jnp.where(two == 0, r1, r2).astype(jnp.int32)
    total = base_scr[...] + jnp.sum(both, axis=0, keepdims=True)
    base_scr[...] = total
    cnt_ref[...] = total.astype(jnp.int32)


def _router(x, mod, g, rw, rb, seq, tm):
    n, d = x.shape
    ne = rw.shape[1]
    tps = seq // tm
    return pl.pallas_call(
        _router_kernel,
        grid=(n // tm,),
        in_specs=[pl.BlockSpec((tm, d), lambda i: (i, 0)),
                  pl.BlockSpec((1, 6, d), lambda i: (i // tps, 0, 0)),
                  pl.BlockSpec((1, d), lambda i: (0, 0)),
                  pl.BlockSpec((d, ne), lambda i: (0, 0)),
                  pl.BlockSpec((1, ne), lambda i: (0, 0))],
        out_specs=[pl.BlockSpec((tm, d // 2), lambda i: (i, 0)),
                   pl.BlockSpec((tm, TOP_K), lambda i: (i, 0)),
                   pl.BlockSpec((tm, TOP_K), lambda i: (i, 0)),
                   pl.BlockSpec((tm, TOP_K), lambda i: (i, 0)),
                   pl.BlockSpec((1, ne), lambda i: (0, 0))],
        out_shape=[jax.ShapeDtypeStruct((n, d // 2), jnp.int32),
                   jax.ShapeDtypeStruct((n, TOP_K), jnp.int32),
                   jax.ShapeDtypeStruct((n, TOP_K), F32),
                   jax.ShapeDtypeStruct((n, TOP_K), jnp.int32),
                   jax.ShapeDtypeStruct((1, ne), jnp.int32)],
        scratch_shapes=[pltpu.VMEM((1, ne), F32), pltpu.VMEM((tm, tm), BF16)],
        compiler_params=_cparams("arbitrary"),
        name="router",
    )(x, mod, g, rw, rb)


def _count_le(sorted_vals, queries):
    return jnp.sum((sorted_vals[None, :] <= queries[:, None]).astype(jnp.int32), axis=1)


def _routing_tables(idx, rank, cnt, ne, tile):
    n = idx.shape[0]
    padded = ((cnt + tile - 1) // tile) * tile
    off = jnp.cumsum(padded) - padded
    onehot = idx[:, :, None] == jnp.arange(ne, dtype=jnp.int32)[None, None, :]
    pos = rank + jnp.sum(jnp.where(onehot, off[None, None, :], 0), axis=2)
    n_tiles = (TOP_K * n) // tile + ne
    used = jnp.sum(padded) // tile
    tile_ids = jnp.arange(n_tiles, dtype=jnp.int32)
    tile_expert = _count_le(jnp.cumsum(padded), tile_ids * tile)
    last_used = jnp.maximum(used - 1, 0)
    tile_valid = (tile_ids < used).astype(jnp.int32)
    tile_src = jnp.minimum(tile_ids, last_used).astype(jnp.int32)
    tile_expert = jnp.minimum(tile_expert, ne - 1).astype(jnp.int32)
    tile_expert = jnp.where(tile_valid == 1, tile_expert, tile_expert[last_used])
    tile_rows = jnp.clip(cnt[tile_expert] - (tile_ids * tile - off[tile_expert]), 0, tile)
    tile_rows = jnp.where(tile_valid == 1, tile_rows, 0).astype(jnp.int32)
    return dict(pos=pos, tile_expert=tile_expert, tile_valid=tile_valid, tile_src=tile_src,
                tile_rows=tile_rows, n_tiles=n_tiles)


def _sc_mesh():
    return plsc.VectorSubcoreMesh(core_axis_name="c", subcore_axis_name="s",
                                  num_cores=SC_CORES, num_subcores=SC_SUBCORES)


def _dispatch(hp, pos0, pos1, rows_out):
    n, w = hp.shape
    per_w = n // SC_WORKERS
    ch = SC_ROWS
    nch = per_w // ch
    assert per_w * SC_WORKERS == n and nch * ch == per_w and nch % 2 == 0

    @functools.partial(
        pl.kernel, mesh=_sc_mesh(),
        out_type=jax.ShapeDtypeStruct((rows_out, w), hp.dtype),
        scratch_types=[pltpu.VMEM((per_w,), jnp.int32), pltpu.VMEM((per_w,), jnp.int32),
                       pltpu.VMEM((ch, w), hp.dtype), pltpu.VMEM((ch, w), hp.dtype)]
        + [pltpu.SemaphoreType.DMA] * 4,
        name="dispatch",
    )
    def k(hp_hbm, p0_hbm, p1_hbm, out_hbm, i0_v, i1_v, r0_v, r1_v, l0, l1, s0, s1):
        base = (lax.axis_index("s") * SC_CORES + lax.axis_index("c")) * per_w
        pltpu.sync_copy(p0_hbm.at[pl.ds(base, per_w)], i0_v)
        pltpu.sync_copy(p1_hbm.at[pl.ds(base, per_w)], i1_v)
        bufs, lsem, ssem = (r0_v, r1_v), (l0, l1), (s0, s1)

        def load(j, b):
            return pltpu.make_async_copy(hp_hbm.at[pl.ds(base + j * ch, ch)], bufs[b], lsem[b])

        def scatter(j, b, idx):
            return pltpu.make_async_copy(bufs[b], out_hbm.at[idx.at[pl.ds(j * ch, ch)]], ssem[b])

        load(0, 0).start()

        @pl.loop(0, nch, step=2)
        def _(j):
            for b in range(2):
                jj = j + b
                load(jj, b).wait()

                @pl.when(jj + 1 < nch)
                def _():
                    load(jj + 1, 1 - b).start()

                c0 = scatter(jj, b, i0_v)
                c1 = scatter(jj, b, i1_v)
                c0.start()
                c1.start()
                c0.wait()
                c1.wait()

    return k(hp, pos0, pos1)


def _gather_pairs(ys, pos0, pos1):
    n = pos0.shape[0]
    w = ys.shape[1]
    per_w = n // SC_WORKERS
    ch = SC_ROWS
    nch = per_w // ch
    assert per_w * SC_WORKERS == n and nch * ch == per_w
    out = jax.ShapeDtypeStruct((n, w), ys.dtype)

    @functools.partial(
        pl.kernel, mesh=_sc_mesh(), out_type=(out, out),
        scratch_types=[pltpu.VMEM((per_w,), jnp.int32), pltpu.VMEM((per_w,), jnp.int32),
                       pltpu.VMEM((ch, w), ys.dtype), pltpu.VMEM((ch, w), ys.dtype)]
        + [pltpu.SemaphoreType.DMA] * 4,
        name="gather_pairs",
    )
    def k(ys_hbm, p0_hbm, p1_hbm, a_hbm, b_hbm, i0_v, i1_v, ra_v, rb_v, ga, gb, wa, wb):
        base = (lax.axis_index("s") * SC_CORES + lax.axis_index("c")) * per_w
        pltpu.sync_copy(p0_hbm.at[pl.ds(base, per_w)], i0_v)
        pltpu.sync_copy(p1_hbm.at[pl.ds(base, per_w)], i1_v)

        def gather(j, idx, buf, sem):
            return pltpu.make_async_copy(ys_hbm.at[idx.at[pl.ds(j * ch, ch)]], buf, sem)

        def write(j, buf, dst, sem):
            return pltpu.make_async_copy(buf, dst.at[pl.ds(base + j * ch, ch)], sem)

        gather(0, i0_v, ra_v, ga).start()
        gather(0, i1_v, rb_v, gb).start()

        @pl.loop(0, nch)
        def _(j):
            gather(j, i0_v, ra_v, ga).wait()
            write(j, ra_v, a_hbm, wa).start()
            gather(j, i1_v, rb_v, gb).wait()
            write(j, rb_v, b_hbm, wb).start()
            write(j, ra_v, a_hbm, wa).wait()

            @pl.when(j + 1 < nch)
            def _():
                gather(j + 1, i0_v, ra_v, ga).start()

            write(j, rb_v, b_hbm, wb).wait()

            @pl.when(j + 1 < nch)
            def _():
                gather(j + 1, i1_v, rb_v, gb).start()

    return k(ys, pos0, pos1)


def _expert_kernel(te_ref, tv_ref, tsrc_ref, tr_ref, xs_ref, w1_ref, w3_ref, w2_ref,
                   ys_ref, *, sub):
    j = pl.program_id(0)

    @pl.when(tv_ref[j] == 1)
    def _():
        x = _unpack_bf16_pairs(xs_ref[...])
        row = lax.broadcasted_iota(jnp.int32, x.shape, 0)
        x = jnp.where(row < tr_ref[j], x, 0.0).astype(BF16)
        ys_ref[...] = _pack_bf16_pairs(_swiglu(x, w1_ref.at[0], w3_ref.at[0], w2_ref.at[0], sub))

    @pl.when(tv_ref[j] == 0)
    def _():
        ys_ref[...] = jnp.zeros_like(ys_ref)


def _experts(xs, w1, w3, w2, rt, tile):
    rows, wp = xs.shape
    ne, d, dff = w1.shape
    return pl.pallas_call(
        functools.partial(_expert_kernel, sub=512),
        grid_spec=pltpu.PrefetchScalarGridSpec(
            num_scalar_prefetch=4,
            grid=(rt['n_tiles'],),
            in_specs=[pl.BlockSpec((tile, wp), lambda j, te, tv, tsrc, tr: (tsrc[j], 0)),
                      pl.BlockSpec((1, d, dff), lambda j, te, tv, tsrc, tr: (te[j], 0, 0)),
                      pl.BlockSpec((1, d, dff), lambda j, te, tv, tsrc, tr: (te[j], 0, 0)),
                      pl.BlockSpec((1, dff, d), lambda j, te, tv, tsrc, tr: (te[j], 0, 0))],
            out_specs=pl.BlockSpec((tile, wp), lambda j, te, tv, tsrc, tr: (j, 0))),
        out_shape=jax.ShapeDtypeStruct((rows, wp), jnp.int32),
        compiler_params=_cparams("arbitrary"),
        name="experts",
    )(rt['tile_expert'], rt['tile_valid'], rt['tile_src'], rt['tile_rows'], xs, w1, w3, w2)


def _combine_kernel(a_ref, b_ref, gate_ref, x_ref, mod_ref, fg_ref, o_ref, *, final_norm):
    m = mod_ref[0]
    g = gate_ref[...]
    y = g[:, 0:1] * _unpack_bf16_pairs(a_ref[...]) + g[:, 1:2] * _unpack_bf16_pairs(b_ref[...])
    xn = x_ref[...] + m[5:6] * y
    if final_norm:
        ms = jnp.mean(xn * xn, axis=-1, keepdims=True)
        xn = (xn * lax.rsqrt(ms + EPS)) * fg_ref[...]
    o_ref[...] = xn


def _combine(a, b, gates, x, mod, fg, seq, tm, final_norm, tile0):
    n, d = x.shape
    tps = seq // tm
    part = lambda w: pl.BlockSpec((tm, w), lambda i: (i, 0))
    full = lambda w: pl.BlockSpec((tm, w), lambda i: (i + tile0, 0))
    return pl.pallas_call(
        functools.partial(_combine_kernel, final_norm=final_norm),
        grid=(a.shape[0] // tm,),
        in_specs=[part(d // 2), part(d // 2), full(TOP_K), full(d),
                  pl.BlockSpec((1, 6, d), lambda i: ((i + tile0) // tps, 0, 0)),
                  pl.BlockSpec((1, d), lambda i: (0, 0))],
        out_specs=full(d),
        out_shape=jax.ShapeDtypeStruct((n, d), F32),
        input_output_aliases={3: 0},
        compiler_params=_cparams("arbitrary"),
        name="combine",
    )(a, b, gates, x, mod, fg)


def _pick(n, prefs):
    for p in prefs:
        if n % p == 0:
            return p
    return n


def kernel(x, c, ada_w, ada_b, norm_g, ssm_in, ssm_log_dt, ssm_lam_re, ssm_lam_im,
           ssm_b_re, ssm_b_im, ssm_c_re, ssm_c_im, ssm_d, ssm_glu, ssm_out,
           pool_in, pool_mix, pool_scale, pool_out, ffn_w1, ffn_w3, ffn_w2,
           router_w, router_b, moe_w1, moe_w3, moe_w2, final_g):
    bsz, seq, d = x.shape
    depth = ada_w.shape[0]
    n = bsz * seq
    ne = router_w.shape[-1]
    g_, p_, h_ = ssm_b_re.shape[1:]
    t = SSM_T
    if seq % (LANES * t // bsz) != 0 or bsz != 8:
        raise NotImplementedError("state-space kernels assume batch 8 and seq % 256 == 0")
    if depth % 2 != 0:
        raise NotImplementedError("the final RMSNorm is fused into an expert layer's combine")
    tm = _pick(seq, (512, 256, 128, 64, 32, 16))
    tile = _pick(TOP_K * n, (512, 256, 128, 64, 32, 16))
    gb = _pick(g_, (4, 2, 1))

    stack = lambda w: w.astype(BF16).reshape((-1,) + w.shape[2:])
    ew1, ew3, ew2 = stack(moe_w1), stack(moe_w3), stack(moe_w2)
    fw1, fw3, fw2 = ffn_w1.astype(BF16), ffn_w3.astype(BF16), ffn_w2.astype(BF16)
    mod = _ada(c, ada_w, ada_b).reshape(depth, bsz, 6, d)
    xf = x.reshape(n, d)
    fg = final_g.reshape(1, d)
    for i in range(depth):
        j = i // 2
        mod_i = mod[i]
        g_a = norm_g[i, 0].reshape(1, d)
        g_b = norm_g[i, 1].reshape(1, d)
        if i % 2 == 0:
            ops = _ssm_operators(ssm_log_dt[j], ssm_lam_re[j], ssm_lam_im[j], ssm_b_re[j],
                                 ssm_b_im[j], ssm_c_re[j], ssm_c_im[j], ssm_d[j], t)
            ut = _ssm_in(xf.reshape(bsz, seq, d), mod_i, g_a, ssm_in[j].astype(BF16), t, h_, 4)
            yt = _ssm(ut, *ops, bsz, gb)
            xf = _glu_out(yt, xf.reshape(bsz, seq, d), mod_i, ssm_glu[j].astype(BF16),
                          ssm_out[j].astype(BF16), t, 4).reshape(n, d)
            xf = _ffn(xf, mod_i, g_b, fw1, fw3, fw2, seq, tm, layer=j)
        else:
            xf = _pool(xf, mod_i, g_a, pool_in[j].astype(BF16), pool_mix[j].astype(BF16),
                       pool_scale[j].reshape(1, d), pool_out[j].astype(BF16), seq, 2 * tm)
            h, idx, gates, rank, cnt = _router(xf, mod_i, g_b, router_w[j],
                                               router_b[j].reshape(1, ne), seq, tm)
            rt = _routing_tables(idx, rank, cnt[0], ne, tile)
            pos0, pos1 = rt['pos'][:, 0], rt['pos'][:, 1]
            xs = _dispatch(h, pos0, pos1, rt['n_tiles'] * tile)
            rt['tile_expert'] = rt['tile_expert'] + j * ne
            ys = _experts(xs, ew1, ew3, ew2, rt, tile)
            npc = n // COMBINE_PIECES
            for q in range(COMBINE_PIECES):
                piece = slice(q * npc, (q + 1) * npc)
                ya, yb = _gather_pairs(ys, pos0[piece], pos1[piece])
                xf = _combine(ya, yb, gates, xf, mod_i, fg, seq, 2 * tm,
                              final_norm=(i == depth - 1), tile0=q * (npc // (2 * tm)))
    return xf.reshape(bsz, seq, d)
```

```python
import functools
import math

import jax
import jax.numpy as jnp
from jax import lax
from jax.experimental import pallas as pl
from jax.experimental.pallas import tpu as pltpu
from jax.experimental.pallas import tpu_sc as plsc

F32 = jnp.float32
BF16 = jnp.bfloat16
EPS = 1e-6
POOL_WINDOWS = (2, 4, 8, 16)
POOL_HALO = 16
SSM_T = 16
LANES = 128
TOP_K = 2
SC_CORES, SC_SUBCORES = 2, 16
SC_WORKERS = SC_CORES * SC_SUBCORES
SC_ROWS = 64
COMBINE_PIECES = 2
VMEM_LIMIT = 58 * 1024 * 1024


def _cparams(*sem):
    return pltpu.CompilerParams(dimension_semantics=sem, vmem_limit_bytes=VMEM_LIMIT)


def _normmod(x, g, shift, scale):
    ms = jnp.mean(x * x, axis=-1, keepdims=True)
    y = x * lax.rsqrt(ms + EPS)
    return (y * g) * (1.0 + scale) + shift


def _bdot(a, b):
    return jnp.dot(a, b, preferred_element_type=F32)


def _ada_kernel(c_ref, w_ref, b_ref, o_ref):
    c = c_ref[...]
    cond = c * jax.nn.sigmoid(c)
    o_ref[0] = jnp.dot(cond, w_ref[0], precision=lax.Precision.HIGHEST,
                       preferred_element_type=F32) + b_ref[0]


def _ada(c, ada_w, ada_b):
    depth, d, d6 = ada_w.shape
    bsz = c.shape[0]
    tn = d6 // 4
    return pl.pallas_call(
        _ada_kernel,
        grid=(depth, d6 // tn),
        in_specs=[pl.BlockSpec((bsz, d), lambda l, j: (0, 0)),
                  pl.BlockSpec((1, d, tn), lambda l, j: (l, 0, j)),
                  pl.BlockSpec((1, 1, tn), lambda l, j: (l, 0, j))],
        out_specs=pl.BlockSpec((1, bsz, tn), lambda l, j: (l, 0, j)),
        out_shape=jax.ShapeDtypeStruct((depth, bsz, d6), F32),
        compiler_params=_cparams("arbitrary", "arbitrary"),
        name="ada",
    )(c, ada_w, ada_b.reshape(depth, 1, d6))


def _ssm_operators(log_dt, lam_re, lam_im, b_re, b_im, c_re, c_im, d_skip, t):
    g_, p_, h_ = b_re.shape
    dt = jnp.exp(log_dt.astype(F32))[:, None]
    lr = lam_re.astype(F32)
    li = lam_im.astype(F32)
    mag = jnp.exp(lr * dt)
    a_re = mag * jnp.cos(li * dt)
    a_im = mag * jnp.sin(li * dt)
    den = lr * lr + li * li
    nr = a_re - 1
    coef_re = (nr * lr + a_im * li) / den
    coef_im = (a_im * lr - nr * li) / den
    br = b_re.astype(F32)
    bi = b_im.astype(F32)
    bbar_re = coef_re[..., None] * br - coef_im[..., None] * bi
    bbar_im = coef_re[..., None] * bi + coef_im[..., None] * br
    steps = jnp.arange(0, t + 1, dtype=F32)[None, :, None]
    ang = (li * dt)[:, None, :] * steps
    pmag = jnp.exp((lr * dt)[:, None, :] * steps)
    pw_re = pmag * jnp.cos(ang)
    pw_im = pmag * jnp.sin(ang)
    cr = c_re.astype(F32)
    ci = c_im.astype(F32)
    er = cr[:, None] * pw_re[:, :, None, :] - ci[:, None] * pw_im[:, :, None, :]
    ei = cr[:, None] * pw_im[:, :, None, :] + ci[:, None] * pw_re[:, :, None, :]
    tg = lambda a: jnp.transpose(a, (1, 2, 0))
    pt_re, pt_im = tg(pw_re[:, :t])[:, None], tg(pw_im[:, :t])[:, None]
    ct_re, ct_im = tg(cr)[None], tg(ci)[None]
    er_l = (ct_re * pt_re - ct_im * pt_im)[..., None]
    ei_l = (ct_re * pt_im + ct_im * pt_re)[..., None]
    br_l = jnp.transpose(bbar_re, (1, 0, 2))[None, None]
    bi_l = jnp.transpose(bbar_im, (1, 0, 2))[None, None]
    kk = jnp.sum(er_l * br_l - ei_l * bi_l, axis=2)
    kk = jnp.transpose(kk, (2, 0, 1, 3))
    kz = jnp.pad(kk, ((0, 0), (t - 1, 0), (0, 0), (0, 0)))
    cols = [kz[:, t - 1 - j:2 * t - 1 - j] for j in range(t)]
    mt = jnp.stack(cols, axis=3).reshape(g_, t * h_, t * h_)
    rev_re = pw_re[:, :t][:, ::-1, None, :]
    rev_im = pw_im[:, :t][:, ::-1, None, :]
    bt_re = jnp.transpose(bbar_re, (0, 2, 1))[:, None]
    bt_im = jnp.transpose(bbar_im, (0, 2, 1))[:, None]
    w_re = (rev_re * bt_re - rev_im * bt_im).reshape(g_, t * h_, p_)
    w_im = (rev_re * bt_im + rev_im * bt_re).reshape(g_, t * h_, p_)
    w4 = jnp.concatenate([w_re, w_im, w_im, w_re], axis=-1)
    vt = jnp.concatenate([er[:, 1:].reshape(g_, t * h_, p_),
                          -ei[:, 1:].reshape(g_, t * h_, p_)], axis=-1)
    at_re, at_im = pw_re[:, t], pw_im[:, t]
    c1 = jnp.concatenate([at_re, at_re], axis=-1)
    c2 = jnp.concatenate([-at_im, at_im], axis=-1)
    coef = jnp.concatenate([c1[:, None], c2[:, None], jnp.zeros((g_, 6, 2 * p_), F32)], axis=1)
    dk = jnp.tile(d_skip.astype(F32)[:, None, :], (1, t, 1)).reshape(g_, t * h_, 1)
    return mt.astype(BF16), w4.astype(BF16), vt.astype(BF16), coef, dk


def _ssm_in_kernel(x_ref, mod_ref, g_ref, w_ref, ut_ref, h_scr, u_scr, *, t):
    q = pl.program_id(1)
    bsz, cs, _ = x_ref.shape
    nlt = u_scr.shape[0]
    for b in range(bsz):
        m = mod_ref[b]
        h = _normmod(x_ref[b], g_ref[...], m[0:1], m[1:2]).astype(BF16)
        for c in range(cs // t):
            h_scr[(c * bsz + b) * t:(c * bsz + b + 1) * t, :] = h[c * t:(c + 1) * t, :]
    u = _bdot(h_scr[...], w_ref[...])
    rows = pl.ds(pl.multiple_of(q * (bsz * cs), bsz * cs), bsz * cs)
    for l in range(nlt):
        u_scr[l, rows, :] = u[:, l * LANES:(l + 1) * LANES]

    @pl.when(q == pl.num_programs(1) - 1)
    def _():
        ng, th, nch = ut_ref.shape
        hh = th // t
        gl = LANES // hh
        for l in range(nlt):
            for k in range(t):
                uk = u_scr[l, pl.ds(k, nch, stride=t), :]
                ut_ref[l * gl:(l + 1) * gl, k * hh:(k + 1) * hh, :] = (
                    uk.T.reshape(gl, hh, nch).astype(BF16))


def _ssm_in(x, mod, g, w, t, hh, nq):
    bsz, seq, d = x.shape
    ng = d // hh
    cs = LANES * t // bsz // nq
    return pl.pallas_call(
        functools.partial(_ssm_in_kernel, t=t),
        grid=(seq // (cs * nq), nq),
        in_specs=[pl.BlockSpec((bsz, cs, d), lambda o, q: (0, o * nq + q, 0)),
                  pl.BlockSpec((bsz, 6, d), lambda o, q: (0, 0, 0)),
                  pl.BlockSpec((1, d), lambda o, q: (0, 0)),
                  pl.BlockSpec((d, d), lambda o, q: (0, 0))],
        out_specs=pl.BlockSpec((ng, t * hh, LANES), lambda o, q: (0, 0, o)),
        out_shape=jax.ShapeDtypeStruct((ng, t * hh, bsz * seq // t), BF16),
        scratch_shapes=[pltpu.VMEM((bsz * cs, d), BF16),
                        pltpu.VMEM((d // LANES, LANES * t, LANES), F32)],
        compiler_params=_cparams("arbitrary", "arbitrary"),
        name="ssm_in",
    )(x, mod, g, w)


def _ssm_kernel(ut_ref, mt_ref, w4_ref, vt_ref, coef_ref, dk_ref, yt_ref, z_scr, s_scr,
                *, gb, nchunks, bsz):
    p2 = s_scr.shape[-1]
    for gi in range(gb):
        u = ut_ref[gi].astype(F32).T.astype(BF16)
        z = _bdot(u, w4_ref[gi])
        z_scr[gi, 0] = z[:, :p2]
        z_scr[gi, 1] = z[:, p2:]
    c1 = [jnp.broadcast_to(coef_ref[gi, 0:1, :], (bsz, p2)) for gi in range(gb)]
    c2 = [jnp.broadcast_to(coef_ref[gi, 1:2, :], (bsz, p2)) for gi in range(gb)]

    def body(c, carry):
        new = []
        for gi in range(gb):
            s1, s2 = carry[gi]
            rows = pl.ds(pl.multiple_of(c * bsz, bsz), bsz)
            s_scr[gi, rows, :] = s1
            n1 = c1[gi] * s1 + c2[gi] * s2 + z_scr[gi, 0, rows, :]
            n2 = c1[gi] * s2 - c2[gi] * s1 + z_scr[gi, 1, rows, :]
            new.append((n1, n2))
        return tuple(new)

    zero = jnp.zeros((bsz, p2), F32)
    lax.fori_loop(0, nchunks, body, tuple((zero, zero) for _ in range(gb)))
    for gi in range(gb):
        ut = ut_ref[gi]
        st = _bdot(vt_ref[gi], s_scr[gi].T.astype(BF16))
        yt = _bdot(mt_ref[gi], ut) + st + dk_ref[gi] * ut.astype(F32)
        yt_ref[gi] = yt.astype(yt_ref.dtype)


def _ssm(ut, mt, w4, vt, coef, dk, bsz, gb):
    g_, th, nch = ut.shape
    p2 = vt.shape[2]
    kern = functools.partial(_ssm_kernel, gb=gb, nchunks=nch // bsz, bsz=bsz)
    spec3 = lambda a, b: pl.BlockSpec((gb, a, b), lambda i: (i, 0, 0))
    return pl.pallas_call(
        kern,
        grid=(g_ // gb,),
        in_specs=[spec3(th, nch), spec3(th, th), spec3(th, 2 * p2), spec3(th, p2),
                  spec3(8, p2), spec3(th, 1)],
        out_specs=spec3(th, nch),
        out_shape=jax.ShapeDtypeStruct((g_, th, nch), BF16),
        scratch_shapes=[pltpu.VMEM((gb, 2, nch, p2), F32), pltpu.VMEM((gb, nch, p2), F32)],
        compiler_params=_cparams("arbitrary"),
        name="ssm",
    )(ut, mt, w4, vt, coef, dk)


def _gelu_tanh(y):
    return y * (0.5 * (1.0 + jnp.tanh(math.sqrt(2.0 / math.pi) * (y + 0.044715 * (y * y * y)))))


def _glu_out_kernel(yt_ref, x_ref, mod_ref, wg_ref, wo_ref, o_ref, y_scr, *, t):
    q = pl.program_id(1)
    bsz, cs, _ = x_ref.shape
    nlt = y_scr.shape[0]

    @pl.when(q == 0)
    def _():
        ng, th, nch = yt_ref.shape
        hh = th // t
        gl = LANES // hh
        for l in range(nlt):
            for k in range(t):
                yk = yt_ref[l * gl:(l + 1) * gl, k * hh:(k + 1) * hh, :].astype(F32)
                y_scr[l, pl.ds(k, nch, stride=t), :] = yk.reshape(LANES, nch).T

    rows = pl.ds(pl.multiple_of(q * (bsz * cs), bsz * cs), bsz * cs)
    y = jnp.concatenate([y_scr[l, rows, :] for l in range(nlt)], axis=1)
    z = _gelu_tanh(y)
    z = z * jax.nn.sigmoid(_bdot(z.astype(BF16), wg_ref[...]))
    o = _bdot(z.astype(BF16), wo_ref[...])
    for b in range(bsz):
        gate = mod_ref[b][2:3]
        for c in range(cs // t):
            tok = slice(c * t, (c + 1) * t)
            o_ref[b, tok, :] = x_ref[b, tok, :] + gate * o[(c * bsz + b) * t:(c * bsz + b + 1) * t]


def _glu_out(yt, x, mod, wg, wo, t, nq):
    bsz, seq, d = x.shape
    ng, th, _ = yt.shape
    cs = LANES * t // bsz // nq
    return pl.pallas_call(
        functools.partial(_glu_out_kernel, t=t),
        grid=(seq // (cs * nq), nq),
        in_specs=[pl.BlockSpec((ng, th, LANES), lambda o, q: (0, 0, o)),
                  pl.BlockSpec((bsz, cs, d), lambda o, q: (0, o * nq + q, 0)),
                  pl.BlockSpec((bsz, 6, d), lambda o, q: (0, 0, 0)),
                  pl.BlockSpec((d, d), lambda o, q: (0, 0)),
                  pl.BlockSpec((d, d), lambda o, q: (0, 0))],
        out_specs=pl.BlockSpec((bsz, cs, d), lambda o, q: (0, o * nq + q, 0)),
        out_shape=jax.ShapeDtypeStruct((bsz, seq, d), F32),
        scratch_shapes=[pltpu.VMEM((d // LANES, LANES * t, LANES), F32)],
        compiler_params=_cparams("arbitrary", "arbitrary"),
        name="glu_out",
    )(yt, x, mod, wg, wo)


def _chunks(total, step):
    return [(s, min(step, total - s)) for s in range(0, total, step)]


def _swiglu(h, w1_ref, w3_ref, w2_ref, sub):
    total = None
    for s, n in _chunks(w1_ref.shape[-1], sub):
        a = _bdot(h, w1_ref[:, s:s + n])
        b = _bdot(h, w3_ref[:, s:s + n])
        act = (a * jax.nn.sigmoid(a) * b).astype(BF16)
        y = _bdot(act, w2_ref[s:s + n, :])
        total = y if total is None else total + y
    return total


def _ffn_kernel(x_ref, mod_ref, g_ref, w1_ref, w3_ref, w2_ref, o_ref, *, sub):
    m = mod_ref[0]
    x = x_ref[...]
    h = _normmod(x, g_ref[...], m[3:4], m[4:5]).astype(BF16)
    o_ref[...] = x + m[5:6] * _swiglu(h, w1_ref.at[0], w3_ref.at[0], w2_ref.at[0], sub)


def _ffn(x, mod, g, w1, w3, w2, seq, tm, layer):
    n, d = x.shape
    dff = w1.shape[2]
    tps = seq // tm
    const = lambda i: (0, 0)
    lay = lambda i: (layer, 0, 0)
    return pl.pallas_call(
        functools.partial(_ffn_kernel, sub=512),
        grid=(n // tm,),
        in_specs=[pl.BlockSpec((tm, d), lambda i: (i, 0)),
                  pl.BlockSpec((1, 6, d), lambda i: (i // tps, 0, 0)),
                  pl.BlockSpec((1, d), const),
                  pl.BlockSpec((1, d, dff), lay, pipeline_mode=pl.Buffered(1)),
                  pl.BlockSpec((1, d, dff), lay, pipeline_mode=pl.Buffered(1)),
                  pl.BlockSpec((1, dff, d), lay, pipeline_mode=pl.Buffered(1))],
        out_specs=pl.BlockSpec((tm, d), lambda i: (i, 0)),
        out_shape=jax.ShapeDtypeStruct((n, d), F32),
        compiler_params=_cparams("arbitrary"),
        name="ffn",
    )(x, mod, g, w1, w3, w2)


def _pool_kernel(x_ref, halo_ref, mod_ref, g_ref, win_ref, wmix_ref, scale_ref, wo_ref, o_ref,
                 ext_ref, z_ref, *, tps):
    tm, d = x_ref.shape
    pg = d // len(POOL_WINDOWS)
    m = mod_ref[0]
    it = pl.program_id(0) % tps

    def project(rows):
        h = _normmod(rows, g_ref[...], m[0:1], m[1:2])
        return _bdot(h.astype(BF16), win_ref[...])

    ext_ref[0:POOL_HALO, :] = jnp.where(it == 0, 0.0, project(halo_ref[...]))
    ext_ref[POOL_HALO:, :] = project(x_ref[...])
    tpos = (it * tm + 1 + lax.broadcasted_iota(jnp.int32, (tm, 1), 0)).astype(F32)
    for gi, w in enumerate(POOL_WINDOWS):
        cols = slice(gi * pg, (gi + 1) * pg)
        s = ext_ref[:, cols]
        span = 1
        while span < w:
            s = s + pltpu.roll(s, span, 0)
            span *= 2
        mean = s[POOL_HALO:, :] * (1.0 / jnp.minimum(tpos, float(w)))
        dlt = (mean - ext_ref[POOL_HALO:, cols]).astype(BF16)
        z_ref[:, cols] = (_bdot(dlt, wmix_ref[gi]) * scale_ref[:, cols]).astype(BF16)
    o_ref[...] = x_ref[...] + m[2:3] * _bdot(z_ref[...], wo_ref[...])


def _pool(x, mod, g, win, wmix, scale, wo, seq, tm):
    n, d = x.shape
    tps = seq // tm
    hb = tm // POOL_HALO
    ng, pg, _ = wmix.shape
    return pl.pallas_call(
        functools.partial(_pool_kernel, tps=tps),
        grid=(n // tm,),
        in_specs=[pl.BlockSpec((tm, d), lambda i: (i, 0)),
                  pl.BlockSpec((POOL_HALO, d), lambda i: (jnp.maximum(i * hb - 1, 0), 0)),
                  pl.BlockSpec((1, 6, d), lambda i: (i // tps, 0, 0)),
                  pl.BlockSpec((1, d), lambda i: (0, 0)),
                  pl.BlockSpec((d, d), lambda i: (0, 0)),
                  pl.BlockSpec((ng, pg, pg), lambda i: (0, 0, 0)),
                  pl.BlockSpec((1, d), lambda i: (0, 0)),
                  pl.BlockSpec((d, d), lambda i: (0, 0))],
        out_specs=pl.BlockSpec((tm, d), lambda i: (i, 0)),
        out_shape=jax.ShapeDtypeStruct((n, d), F32),
        scratch_shapes=[pltpu.VMEM((tm + POOL_HALO, d), F32), pltpu.VMEM((tm, d), BF16)],
        compiler_params=_cparams("arbitrary"),
        name="pool",
    )(x, x, mod, g, win, wmix, scale, wo)


def _pack_bf16_pairs(v):
    w = v.shape[1] // 2
    bits = lambda a: lax.bitcast_convert_type(a.astype(BF16).astype(F32), jnp.int32)
    return bits(v[:, :w]) | lax.shift_right_logical(bits(v[:, w:]), 16)


def _unpack_bf16_pairs(p):
    hi = lax.bitcast_convert_type(p & jnp.int32(-65536), F32)
    lo = lax.bitcast_convert_type(lax.shift_left(p, 16), F32)
    return jnp.concatenate([hi, lo], axis=1)


def _router_kernel(x_ref, mod_ref, g_ref, rw_ref, rb_ref, h_ref, idx_ref, gate_ref, rank_ref,
                   cnt_ref, base_scr, before_scr):
    @pl.when(pl.program_id(0) == 0)
    def _():
        base_scr[...] = jnp.zeros_like(base_scr)
        shape = before_scr.shape
        before = lax.broadcasted_iota(jnp.int32, shape, 1) < lax.broadcasted_iota(jnp.int32, shape, 0)
        before_scr[...] = jnp.where(before, 1.0, 0.0).astype(BF16)

    m = mod_ref[0]
    h = _normmod(x_ref[...], g_ref[...], m[3:4], m[4:5])
    h_ref[...] = _pack_bf16_pairs(h)
    split = lambda a: (a.astype(BF16), (a - a.astype(BF16).astype(F32)).astype(BF16))
    h_hi, h_lo = split(h)
    w_hi, w_lo = split(rw_ref[...])
    logits = (_bdot(h_hi, w_hi) + _bdot(h_hi, w_lo) + _bdot(h_lo, w_hi)) + rb_ref[...]
    ne = float(logits.shape[1])
    lane = lax.broadcasted_iota(jnp.int32, logits.shape, 1).astype(F32)
    v1 = jnp.max(logits, axis=1, keepdims=True)
    i1 = jnp.min(jnp.where(logits == v1, lane, ne), axis=1, keepdims=True)
    rest = jnp.where(lane == i1, -jnp.inf, logits)
    v2 = jnp.max(rest, axis=1, keepdims=True)
    i2 = jnp.min(jnp.where(rest == v2, lane, ne), axis=1, keepdims=True)
    e2 = jnp.exp(v2 - v1)
    den = 1.0 + e2
    two = lax.broadcasted_iota(jnp.int32, idx_ref.shape, 1)
    idx_ref[...] = jnp.where(two == 0, i1, i2).astype(jnp.int32)
    gate_ref[...] = jnp.where(two == 0, 1.0 / den, e2 / den)
    tm = logits.shape[0]
    oh1 = jnp.where(lane == i1, 1.0, 0.0)
    oh2 = jnp.where(lane == i2, 1.0, 0.0)
    both = oh1 + oh2
    prior = _bdot(before_scr[...], both.astype(BF16)) + base_scr[...]
    r1 = jnp.sum(oh1 * prior, axis=1, keepdims=True)
    r2 = jnp.sum(oh2 * prior, axis=1, keepdims=True)
    rank_ref[...] = jnp.where(two == 0, r1, r2).astype(jnp.int32)
    total = base_scr[...] + jnp.sum(both, axis=0, keepdims=True)
    base_scr[...] = total
    cnt_ref[...] = total.astype(jnp.int32)


def _router(x, mod, g, rw, rb, seq, tm):
    n, d = x.shape
    ne = rw.shape[1]
    tps = seq // tm
    return pl.pallas_call(
        _router_kernel,
        grid=(n // tm,),
        in_specs=[pl.BlockSpec((tm, d), lambda i: (i, 0)),
                  pl.BlockSpec((1, 6, d), lambda i: (i // tps, 0, 0)),
                  pl.BlockSpec((1, d), lambda i: (0, 0)),
                  pl.BlockSpec((d, ne), lambda i: (0, 0)),
                  pl.BlockSpec((1, ne), lambda i: (0, 0))],
        out_specs=[pl.BlockSpec((tm, d // 2), lambda i: (i, 0)),
                   pl.BlockSpec((tm, TOP_K), lambda i: (i, 0)),
                   pl.BlockSpec((tm, TOP_K), lambda i: (i, 0)),
                   pl.BlockSpec((tm, TOP_K), lambda i: (i, 0)),
                   pl.BlockSpec((1, ne), lambda i: (0, 0))],
        out_shape=[jax.ShapeDtypeStruct((n, d // 2), jnp.int32),
                   jax.ShapeDtypeStruct((n, TOP_K), jnp.int32),
                   jax.ShapeDtypeStruct((n, TOP_K), F32),
                   jax.ShapeDtypeStruct((n, TOP_K), jnp.int32),
                   jax.ShapeDtypeStruct((1, ne), jnp.int32)],
        scratch_shapes=[pltpu.VMEM((1, ne), F32), pltpu.VMEM((tm, tm), BF16)],
        compiler_params=_cparams("arbitrary"),
        name="router",
    )(x, mod, g, rw, rb)


def _count_le(sorted_vals, queries):
    return jnp.sum((sorted_vals[None, :] <= queries[:, None]).astype(jnp.int32), axis=1)


def _routing_tables(idx, rank, cnt, ne, tile):
    n = idx.shape[0]
    padded = ((cnt + tile - 1) // tile) * tile
    off = jnp.cumsum(padded) - padded
    onehot = idx[:, :, None] == jnp.arange(ne, dtype=jnp.int32)[None, None, :]
    pos = rank + jnp.sum(jnp.where(onehot, off[None, None, :], 0), axis=2)
    n_tiles = (TOP_K * n) // tile + ne
    used = jnp.sum(padded) // tile
    tile_ids = jnp.arange(n_tiles, dtype=jnp.int32)
    tile_expert = _count_le(jnp.cumsum(padded), tile_ids * tile)
    last_used = jnp.maximum(used - 1, 0)
    tile_valid = (tile_ids < used).astype(jnp.int32)
    tile_src = jnp.minimum(tile_ids, last_used).astype(jnp.int32)
    tile_expert = jnp.minimum(tile_expert, ne - 1).astype(jnp.int32)
    tile_expert = jnp.where(tile_valid == 1, tile_expert, tile_expert[last_used])
    tile_rows = jnp.clip(cnt[tile_expert] - (tile_ids * tile - off[tile_expert]), 0, tile)
    tile_rows = jnp.where(tile_valid == 1, tile_rows, 0).astype(jnp.int32)
    return dict(pos=pos, tile_expert=tile_expert, tile_valid=tile_valid, tile_src=tile_src,
                tile_rows=tile_rows, n_tiles=n_tiles)


def _sc_mesh():
    return plsc.VectorSubcoreMesh(core_axis_name="c", subcore_axis_name="s",
                                  num_cores=SC_CORES, num_subcores=SC_SUBCORES)


def _dispatch(hp, pos0, pos1, rows_out):
    n, w = hp.shape
    per_w = n // SC_WORKERS
    ch = SC_ROWS
    nch = per_w // ch
    assert per_w * SC_WORKERS == n and nch * ch == per_w and nch % 2 == 0

    @functools.partial(
        pl.kernel, mesh=_sc_mesh(),
        out_type=jax.ShapeDtypeStruct((rows_out, w), hp.dtype),
        scratch_types=[pltpu.VMEM((per_w,), jnp.int32), pltpu.VMEM((per_w,), jnp.int32),
                       pltpu.VMEM((ch, w), hp.dtype), pltpu.VMEM((ch, w), hp.dtype)]
        + [pltpu.SemaphoreType.DMA] * 4,
        name="dispatch",
    )
    def k(hp_hbm, p0_hbm, p1_hbm, out_hbm, i0_v, i1_v, r0_v, r1_v, l0, l1, s0, s1):
        base = (lax.axis_index("s") * SC_CORES + lax.axis_index("c")) * per_w
        pltpu.sync_copy(p0_hbm.at[pl.ds(base, per_w)], i0_v)
        pltpu.sync_copy(p1_hbm.at[pl.ds(base, per_w)], i1_v)
        bufs, lsem, ssem = (r0_v, r1_v), (l0, l1), (s0, s1)

        def load(j, b):
            return pltpu.make_async_copy(hp_hbm.at[pl.ds(base + j * ch, ch)], bufs[b], lsem[b])

        def scatter(j, b, idx):
            return pltpu.make_async_copy(bufs[b], out_hbm.at[idx.at[pl.ds(j * ch, ch)]], ssem[b])

        load(0, 0).start()

        @pl.loop(0, nch, step=2)
        def _(j):
            for b in range(2):
                jj = j + b
                load(jj, b).wait()

                @pl.when(jj + 1 < nch)
                def _():
                    load(jj + 1, 1 - b).start()

                c0 = scatter(jj, b, i0_v)
                c1 = scatter(jj, b, i1_v)
                c0.start()
                c1.start()
                c0.wait()
                c1.wait()

    return k(hp, pos0, pos1)


def _gather_pairs(ys, pos0, pos1):
    n = pos0.shape[0]
    w = ys.shape[1]
    per_w = n // SC_WORKERS
    ch = SC_ROWS
    nch = per_w // ch
    assert per_w * SC_WORKERS == n and nch * ch == per_w
    out = jax.ShapeDtypeStruct((n, w), ys.dtype)

    @functools.partial(
        pl.kernel, mesh=_sc_mesh(), out_type=(out, out),
        scratch_types=[pltpu.VMEM((per_w,), jnp.int32), pltpu.VMEM((per_w,), jnp.int32),
                       pltpu.VMEM((ch, w), ys.dtype), pltpu.VMEM((ch, w), ys.dtype)]
        + [pltpu.SemaphoreType.DMA] * 4,
        name="gather_pairs",
    )
    def k(ys_hbm, p0_hbm, p1_hbm, a_hbm, b_hbm, i0_v, i1_v, ra_v, rb_v, ga, gb, wa, wb):
        base = (lax.axis_index("s") * SC_CORES + lax.axis_index("c")) * per_w
        pltpu.sync_copy(p0_hbm.at[pl.ds(base, per_w)], i0_v)
        pltpu.sync_copy(p1_hbm.at[pl.ds(base, per_w)], i1_v)

        def gather(j, idx, buf, sem):
            return pltpu.make_async_copy(ys_hbm.at[idx.at[pl.ds(j * ch, ch)]], buf, sem)

        def write(j, buf, dst, sem):
            return pltpu.make_async_copy(buf, dst.at[pl.ds(base + j * ch, ch)], sem)

        gather(0, i0_v, ra_v, ga).start()
        gather(0, i1_v, rb_v, gb).start()

        @pl.loop(0, nch)
        def _(j):
            gather(j, i0_v, ra_v, ga).wait()
            write(j, ra_v, a_hbm, wa).start()
            gather(j, i1_v, rb_v, gb).wait()
            write(j, rb_v, b_hbm, wb).start()
            write(j, ra_v, a_hbm, wa).wait()

            @pl.when(j + 1 < nch)
            def _():
                gather(j + 1, i0_v, ra_v, ga).start()

            write(j, rb_v, b_hbm, wb).wait()

            @pl.when(j + 1 < nch)
            def _():
                gather(j + 1, i1_v, rb_v, gb).start()

    return k(ys, pos0, pos1)


def _expert_kernel(te_ref, tv_ref, tsrc_ref, tr_ref, xs_ref, w1_ref, w3_ref, w2_ref,
                   ys_ref, *, sub):
    j = pl.program_id(0)

    @pl.when(tv_ref[j] == 1)
    def _():
        x = _unpack_bf16_pairs(xs_ref[...])
        row = lax.broadcasted_iota(jnp.int32, x.shape, 0)
        x = jnp.where(row < tr_ref[j], x, 0.0).astype(BF16)
        ys_ref[...] = _pack_bf16_pairs(_swiglu(x, w1_ref.at[0], w3_ref.at[0], w2_ref.at[0], sub))

    @pl.when(tv_ref[j] == 0)
    def _():
        ys_ref[...] = jnp.zeros_like(ys_ref)


def _experts(xs, w1, w3, w2, rt, tile):
    rows, wp = xs.shape
    ne, d, dff = w1.shape
    return pl.pallas_call(
        functools.partial(_expert_kernel, sub=512),
        grid_spec=pltpu.PrefetchScalarGridSpec(
            num_scalar_prefetch=4,
            grid=(rt['n_tiles'],),
            in_specs=[pl.BlockSpec((tile, wp), lambda j, te, tv, tsrc, tr: (tsrc[j], 0)),
                      pl.BlockSpec((1, d, dff), lambda j, te, tv, tsrc, tr: (te[j], 0, 0)),
                      pl.BlockSpec((1, d, dff), lambda j, te, tv, tsrc, tr: (te[j], 0, 0)),
                      pl.BlockSpec((1, dff, d), lambda j, te, tv, tsrc, tr: (te[j], 0, 0))],
            out_specs=pl.BlockSpec((tile, wp), lambda j, te, tv, tsrc, tr: (j, 0))),
        out_shape=jax.ShapeDtypeStruct((rows, wp), jnp.int32),
        compiler_params=_cparams("arbitrary"),
        name="experts",
    )(rt['tile_expert'], rt['tile_valid'], rt['tile_src'], rt['tile_rows'], xs, w1, w3, w2)


def _combine_kernel(a_ref, b_ref, gate_ref, x_ref, mod_ref, fg_ref, o_ref, *, final_norm):
    m = mod_ref[0]
    g = gate_ref[...]
    y = g[:, 0:1] * _unpack_bf16_pairs(a_ref[...]) + g[:, 1:2] * _unpack_bf16_pairs(b_ref[...])
    xn = x_ref[...] + m[5:6] * y
    if final_norm:
        ms = jnp.mean(xn * xn, axis=-1, keepdims=True)
        xn = (xn * lax.rsqrt(ms + EPS)) * fg_ref[...]
    o_ref[...] = xn


def _combine(a, b, gates, x, mod, fg, seq, tm, final_norm, tile0):
    n, d = x.shape
    tps = seq // tm
    part = lambda w: pl.BlockSpec((tm, w), lambda i: (i, 0))
    full = lambda w: pl.BlockSpec((tm, w), lambda i: (i + tile0, 0))
    return pl.pallas_call(
        functools.partial(_combine_kernel, final_norm=final_norm),
        grid=(a.shape[0] // tm,),
        in_specs=[part(d // 2), part(d // 2), full(TOP_K), full(d),
                  pl.BlockSpec((1, 6, d), lambda i: ((i + tile0) // tps, 0, 0)),
                  pl.BlockSpec((1, d), lambda i: (0, 0))],
        out_specs=full(d),
        out_shape=jax.ShapeDtypeStruct((n, d), F32),
        input_output_aliases={3: 0},
        compiler_params=_cparams("arbitrary"),
        name="combine",
    )(a, b, gates, x, mod, fg)


def _pick(n, prefs):
    for p in prefs:
        if n % p == 0:
            return p
    return n


def kernel(x, c, ada_w, ada_b, norm_g, ssm_in, ssm_log_dt, ssm_lam_re, ssm_lam_im,
           ssm_b_re, ssm_b_im, ssm_c_re, ssm_c_im, ssm_d, ssm_glu, ssm_out,
           pool_in, pool_mix, pool_scale, pool_out, ffn_w1, ffn_w3, ffn_w2,
           router_w, router_b, moe_w1, moe_w3, moe_w2, final_g):
    bsz, seq, d = x.shape
    depth = ada_w.shape[0]
    n = bsz * seq
    ne = router_w.shape[-1]
    g_, p_, h_ = ssm_b_re.shape[1:]
    t = SSM_T
    if seq % (LANES * t // bsz) != 0 or bsz != 8:
        raise NotImplementedError("state-space kernels assume batch 8 and seq % 256 == 0")
    if depth % 2 != 0:
        raise NotImplementedError("the final RMSNorm is fused into an expert layer's combine")
    tm = _pick(seq, (512, 256, 128, 64, 32, 16))
    tile = _pick(TOP_K * n, (512, 256, 128, 64, 32, 16))
    gb = _pick(g_, (4, 2, 1))

    stack = lambda w: w.astype(BF16).reshape((-1,) + w.shape[2:])
    ew1, ew3, ew2 = stack(moe_w1), stack(moe_w3), stack(moe_w2)
    fw1, fw3, fw2 = ffn_w1.astype(BF16), ffn_w3.astype(BF16), ffn_w2.astype(BF16)
    mod = _ada(c, ada_w, ada_b).reshape(depth, bsz, 6, d)
    xf = x.reshape(n, d)
    fg = final_g.reshape(1, d)
    for i in range(depth):
        j = i // 2
        mod_i = mod[i]
        g_a = norm_g[i, 0].reshape(1, d)
        g_b = norm_g[i, 1].reshape(1, d)
        if i % 2 == 0:
            ops = _ssm_operators(ssm_log_dt[j], ssm_lam_re[j], ssm_lam_im[j], ssm_b_re[j],
                                 ssm_b_im[j], ssm_c_re[j], ssm_c_im[j], ssm_d[j], t)
            ut = _ssm_in(xf.reshape(bsz, seq, d), mod_i, g_a, ssm_in[j].astype(BF16), t, h_, 4)
            yt = _ssm(ut, *ops, bsz, gb)
            xf = _glu_out(yt, xf.reshape(bsz, seq, d), mod_i, ssm_glu[j].astype(BF16),
                          ssm_out[j].astype(BF16), t, 4).reshape(n, d)
            xf = _ffn(xf, mod_i, g_b, fw1, fw3, fw2, seq, tm, layer=j)
        else:
            xf = _pool(xf, mod_i, g_a, pool_in[j].astype(BF16), pool_mix[j].astype(BF16),
                       pool_scale[j].reshape(1, d), pool_out[j].astype(BF16), seq, 2 * tm)
            h, idx, gates, rank, cnt = _router(xf, mod_i, g_b, router_w[j],
                                               router_b[j].reshape(1, ne), seq, tm)
            rt = _routing_tables(idx, rank, cnt[0], ne, tile)
            pos0, pos1 = rt['pos'][:, 0], rt['pos'][:, 1]
            xs = _dispatch(h, pos0, pos1, rt['n_tiles'] * tile)
            rt['tile_expert'] = rt['tile_expert'] + j * ne
            ys = _experts(xs, ew1, ew3, ew2, rt, tile)
            npc = n // COMBINE_PIECES
            for q in range(COMBINE_PIECES):
                piece = slice(q * npc, (q + 1) * npc)
                ya, yb = _gather_pairs(ys, pos0[piece], pos1[piece])
                xf = _combine(ya, yb, gates, xf, mod_i, fg, seq, 2 * tm,
                              final_norm=(i == depth - 1), tile0=q * (npc // (2 * tm)))
    return xf.reshape(bsz, seq, d)
```

```python
import functools
import math

import jax
import jax.numpy as jnp
from jax import lax
from jax.experimental import pallas as pl
from jax.experimental.pallas import tpu as pltpu
from jax.experimental.pallas import tpu_sc as plsc

F32 = jnp.float32
BF16 = jnp.bfloat16
EPS = 1e-6
POOL_WINDOWS = (2, 4, 8, 16)
POOL_HALO = 16
SSM_T = 16
LANES = 128
TOP_K = 2
SC_CORES, SC_SUBCORES = 2, 16
SC_WORKERS = SC_CORES * SC_SUBCORES
SC_ROWS = 64
COMBINE_PIECES = 2
VMEM_LIMIT = 58 * 1024 * 1024


def _cparams(*sem):
    return pltpu.CompilerParams(dimension_semantics=sem, vmem_limit_bytes=VMEM_LIMIT)


def _normmod(x, g, shift, scale):
    ms = jnp.mean(x * x, axis=-1, keepdims=True)
    y = x * lax.rsqrt(ms + EPS)
    return (y * g) * (1.0 + scale) + shift


def _bdot(a, b):
    return jnp.dot(a, b, preferred_element_type=F32)


def _ada_kernel(c_ref, w_ref, b_ref, o_ref):
    c = c_ref[...]
    cond = c * jax.nn.sigmoid(c)
    o_ref[0] = jnp.dot(cond, w_ref[0], precision=lax.Precision.HIGHEST,
                       preferred_element_type=F32) + b_ref[0]


def _ada(c, ada_w, ada_b):
    depth, d, d6 = ada_w.shape
    bsz = c.shape[0]
    tn = d6 // 4
    return pl.pallas_call(
        _ada_kernel,
        grid=(depth, d6 // tn),
        in_specs=[pl.BlockSpec((bsz, d), lambda l, j: (0, 0)),
                  pl.BlockSpec((1, d, tn), lambda l, j: (l, 0, j)),
                  pl.BlockSpec((1, 1, tn), lambda l, j: (l, 0, j))],
        out_specs=pl.BlockSpec((1, bsz, tn), lambda l, j: (l, 0, j)),
        out_shape=jax.ShapeDtypeStruct((depth, bsz, d6), F32),
        compiler_params=_cparams("arbitrary", "arbitrary"),
        name="ada",
    )(c, ada_w, ada_b.reshape(depth, 1, d6))


def _ssm_operators(log_dt, lam_re, lam_im, b_re, b_im, c_re, c_im, d_skip, t):
    g_, p_, h_ = b_re.shape
    dt = jnp.exp(log_dt.astype(F32))[:, None]
    lr = lam_re.astype(F32)
    li = lam_im.astype(F32)
    mag = jnp.exp(lr * dt)
    a_re = mag * jnp.cos(li * dt)
    a_im = mag * jnp.sin(li * dt)
    den = lr * lr + li * li
    nr = a_re - 1
    coef_re = (nr * lr + a_im * li) / den
    coef_im = (a_im * lr - nr * li) / den
    br = b_re.astype(F32)
    bi = b_im.astype(F32)
    bbar_re = coef_re[..., None] * br - coef_im[..., None] * bi
    bbar_im = coef_re[..., None] * bi + coef_im[..., None] * br
    steps = jnp.arange(0, t + 1, dtype=F32)[None, :, None]
    ang = (li * dt)[:, None, :] * steps
    pmag = jnp.exp((lr * dt)[:, None, :] * steps)
    pw_re = pmag * jnp.cos(ang)
    pw_im = pmag * jnp.sin(ang)
    cr = c_re.astype(F32)
    ci = c_im.astype(F32)
    er = cr[:, None] * pw_re[:, :, None, :] - ci[:, None] * pw_im[:, :, None, :]
    ei = cr[:, None] * pw_im[:, :, None, :] + ci[:, None] * pw_re[:, :, None, :]
    tg = lambda a: jnp.transpose(a, (1, 2, 0))
    pt_re, pt_im = tg(pw_re[:, :t])[:, None], tg(pw_im[:, :t])[:, None]
    ct_re, ct_im = tg(cr)[None], tg(ci)[None]
    er_l = (ct_re * pt_re - ct_im * pt_im)[..., None]
    ei_l = (ct_re * pt_im + ct_im * pt_re)[..., None]
    br_l = jnp.transpose(bbar_re, (1, 0, 2))[None, None]
    bi_l = jnp.transpose(bbar_im, (1, 0, 2))[None, None]
    kk = jnp.sum(er_l * br_l - ei_l * bi_l, axis=2)
    kk = jnp.transpose(kk, (2, 0, 1, 3))
    kz = jnp.pad(kk, ((0, 0), (t - 1, 0), (0, 0), (0, 0)))
    cols = [kz[:, t - 1 - j:2 * t - 1 - j] for j in range(t)]
    mt = jnp.stack(cols, axis=3).reshape(g_, t * h_, t * h_)
    rev_re = pw_re[:, :t][:, ::-1, None, :]
    rev_im = pw_im[:, :t][:, ::-1, None, :]
    bt_re = jnp.transpose(bbar_re, (0, 2, 1))[:, None]
    bt_im = jnp.transpose(bbar_im, (0, 2, 1))[:, None]
    w_re = (rev_re * bt_re - rev_im * bt_im).reshape(g_, t * h_, p_)
    w_im = (rev_re * bt_im + rev_im * bt_re).reshape(g_, t * h_, p_)
    w4 = jnp.concatenate([w_re, w_im, w_im, w_re], axis=-1)
    vt = jnp.concatenate([er[:, 1:].reshape(g_, t * h_, p_),
                          -ei[:, 1:].reshape(g_, t * h_, p_)], axis=-1)
    at_re, at_im = pw_re[:, t], pw_im[:, t]
    c1 = jnp.concatenate([at_re, at_re], axis=-1)
    c2 = jnp.concatenate([-at_im, at_im], axis=-1)
    coef = jnp.concatenate([c1[:, None], c2[:, None], jnp.zeros((g_, 6, 2 * p_), F32)], axis=1)
    dk = jnp.tile(d_skip.astype(F32)[:, None, :], (1, t, 1)).reshape(g_, t * h_, 1)
    return mt.astype(BF16), w4.astype(BF16), vt.astype(BF16), coef, dk


def _ssm_in_kernel(x_ref, mod_ref, g_ref, w_ref, ut_ref, h_scr, u_scr, *, t):
    q = pl.program_id(1)
    bsz, cs, _ = x_ref.shape
    nlt = u_scr.shape[0]
    for b in range(bsz):
        m = mod_ref[b]
        h = _normmod(x_ref[b], g_ref[...], m[0:1], m[1:2]).astype(BF16)
        for c in range(cs // t):
            h_scr[(c * bsz + b) * t:(c * bsz + b + 1) * t, :] = h[c * t:(c + 1) * t, :]
    u = _bdot(h_scr[...], w_ref[...])
    rows = pl.ds(pl.multiple_of(q * (bsz * cs), bsz * cs), bsz * cs)
    for l in range(nlt):
        u_scr[l, rows, :] = u[:, l * LANES:(l + 1) * LANES]

    @pl.when(q == pl.num_programs(1) - 1)
    def _():
        ng, th, nch = ut_ref.shape
        hh = th // t
        gl = LANES // hh
        for l in range(nlt):
            for k in range(t):
                uk = u_scr[l, pl.ds(k, nch, stride=t), :]
                ut_ref[l * gl:(l + 1) * gl, k * hh:(k + 1) * hh, :] = (
                    uk.T.reshape(gl, hh, nch).astype(BF16))


def _ssm_in(x, mod, g, w, t, hh, nq):
    bsz, seq, d = x.shape
    ng = d // hh
    cs = LANES * t // bsz // nq
    return pl.pallas_call(
        functools.partial(_ssm_in_kernel, t=t),
        grid=(seq // (cs * nq), nq),
        in_specs=[pl.BlockSpec((bsz, cs, d), lambda o, q: (0, o * nq + q, 0)),
                  pl.BlockSpec((bsz, 6, d), lambda o, q: (0, 0, 0)),
                  pl.BlockSpec((1, d), lambda o, q: (0, 0)),
                  pl.BlockSpec((d, d), lambda o, q: (0, 0))],
        out_specs=pl.BlockSpec((ng, t * hh, LANES), lambda o, q: (0, 0, o)),
        out_shape=jax.ShapeDtypeStruct((ng, t * hh, bsz * seq // t), BF16),
        scratch_shapes=[pltpu.VMEM((bsz * cs, d), BF16),
                        pltpu.VMEM((d // LANES, LANES * t, LANES), F32)],
        compiler_params=_cparams("arbitrary", "arbitrary"),
        name="ssm_in",
    )(x, mod, g, w)


def _ssm_kernel(ut_ref, mt_ref, w4_ref, vt_ref, coef_ref, dk_ref, yt_ref, z_scr, s_scr,
                *, gb, nchunks, bsz):
    p2 = s_scr.shape[-1]
    for gi in range(gb):
        u = ut_ref[gi].astype(F32).T.astype(BF16)
        z = _bdot(u, w4_ref[gi])
        z_scr[gi, 0] = z[:, :p2]
        z_scr[gi, 1] = z[:, p2:]
    c1 = [jnp.broadcast_to(coef_ref[gi, 0:1, :], (bsz, p2)) for gi in range(gb)]
    c2 = [jnp.broadcast_to(coef_ref[gi, 1:2, :], (bsz, p2)) for gi in range(gb)]

    def body(c, carry):
        new = []
        for gi in range(gb):
            s1, s2 = carry[gi]
            rows = pl.ds(pl.multiple_of(c * bsz, bsz), bsz)
            s_scr[gi, rows, :] = s1
            n1 = c1[gi] * s1 + c2[gi] * s2 + z_scr[gi, 0, rows, :]
            n2 = c1[gi] * s2 - c2[gi] * s1 + z_scr[gi, 1, rows, :]
            new.append((n1, n2))
        return tuple(new)

    zero = jnp.zeros((bsz, p2), F32)
    lax.fori_loop(0, nchunks, body, tuple((zero, zero) for _ in range(gb)))
    for gi in range(gb):
        ut = ut_ref[gi]
        st = _bdot(vt_ref[gi], s_scr[gi].T.astype(BF16))
        yt = _bdot(mt_ref[gi], ut) + st + dk_ref[gi] * ut.astype(F32)
        yt_ref[gi] = yt.astype(yt_ref.dtype)


def _ssm(ut, mt, w4, vt, coef, dk, bsz, gb):
    g_, th, nch = ut.shape
    p2 = vt.shape[2]
    kern = functools.partial(_ssm_kernel, gb=gb, nchunks=nch // bsz, bsz=bsz)
    spec3 = lambda a, b: pl.BlockSpec((gb, a, b), lambda i: (i, 0, 0))
    return pl.pallas_call(
        kern,
        grid=(g_ // gb,),
        in_specs=[spec3(th, nch), spec3(th, th), spec3(th, 2 * p2), spec3(th, p2),
                  spec3(8, p2), spec3(th, 1)],
        out_specs=spec3(th, nch),
        out_shape=jax.ShapeDtypeStruct((g_, th, nch), BF16),
        scratch_shapes=[pltpu.VMEM((gb, 2, nch, p2), F32), pltpu.VMEM((gb, nch, p2), F32)],
        compiler_params=_cparams("arbitrary"),
        name="ssm",
    )(ut, mt, w4, vt, coef, dk)


def _gelu_tanh(y):
    return y * (0.5 * (1.0 + jnp.tanh(math.sqrt(2.0 / math.pi) * (y + 0.044715 * (y * y * y)))))


def _glu_out_kernel(yt_ref, x_ref, mod_ref, wg_ref, wo_ref, o_ref, y_scr, *, t):
    q = pl.program_id(1)
    bsz, cs, _ = x_ref.shape
    nlt = y_scr.shape[0]

    @pl.when(q == 0)
    def _():
        ng, th, nch = yt_ref.shape
        hh = th // t
        gl = LANES // hh
        for l in range(nlt):
            for k in range(t):
                yk = yt_ref[l * gl:(l + 1) * gl, k * hh:(k + 1) * hh, :].astype(F32)
                y_scr[l, pl.ds(k, nch, stride=t), :] = yk.reshape(LANES, nch).T

    rows = pl.ds(pl.multiple_of(q * (bsz * cs), bsz * cs), bsz * cs)
    y = jnp.concatenate([y_scr[l, rows, :] for l in range(nlt)], axis=1)
    z = _gelu_tanh(y)
    z = z * jax.nn.sigmoid(_bdot(z.astype(BF16), wg_ref[...]))
    o = _bdot(z.astype(BF16), wo_ref[...])
    for b in range(bsz):
        gate = mod_ref[b][2:3]
        for c in range(cs // t):
            tok = slice(c * t, (c + 1) * t)
            o_ref[b, tok, :] = x_ref[b, tok, :] + gate * o[(c * bsz + b) * t:(c * bsz + b + 1) * t]


def _glu_out(yt, x, mod, wg, wo, t, nq):
    bsz, seq, d = x.shape
    ng, th, _ = yt.shape
    cs = LANES * t // bsz // nq
    return pl.pallas_call(
        functools.partial(_glu_out_kernel, t=t),
        grid=(seq // (cs * nq), nq),
        in_specs=[pl.BlockSpec((ng, th, LANES), lambda o, q: (0, 0, o)),
                  pl.BlockSpec((bsz, cs, d), lambda o, q: (0, o * nq + q, 0)),
                  pl.BlockSpec((bsz, 6, d), lambda o, q: (0, 0, 0)),
                  pl.BlockSpec((d, d), lambda o, q: (0, 0)),
                  pl.BlockSpec((d, d), lambda o, q: (0, 0))],
        out_specs=pl.BlockSpec((bsz, cs, d), lambda o, q: (0, o * nq + q, 0)),
        out_shape=jax.ShapeDtypeStruct((bsz, seq, d), F32),
        scratch_shapes=[pltpu.VMEM((d // LANES, LANES * t, LANES), F32)],
        compiler_params=_cparams("arbitrary", "arbitrary"),
        name="glu_out",
    )(yt, x, mod, wg, wo)


def _chunks(total, step):
    return [(s, min(step, total - s)) for s in range(0, total, step)]


def _swiglu(h, w1_ref, w3_ref, w2_ref, sub):
    total = None
    for s, n in _chunks(w1_ref.shape[-1], sub):
        a = _bdot(h, w1_ref[:, s:s + n])
        b = _bdot(h, w3_ref[:, s:s + n])
        act = (a * jax.nn.sigmoid(a) * b).astype(BF16)
        y = _bdot(act, w2_ref[s:s + n, :])
        total = y if total is None else total + y
    return total


def _ffn_kernel(x_ref, mod_ref, g_ref, w1_ref, w3_ref, w2_ref, o_ref, *, sub):
    m = mod_ref[0]
    x = x_ref[...]
    h = _normmod(x, g_ref[...], m[3:4], m[4:5]).astype(BF16)
    o_ref[...] = x + m[5:6] * _swiglu(h, w1_ref.at[0], w3_ref.at[0], w2_ref.at[0], sub)


def _ffn(x, mod, g, w1, w3, w2, seq, tm, layer):
    n, d = x.shape
    dff = w1.shape[2]
    tps = seq // tm
    const = lambda i: (0, 0)
    lay = lambda i: (layer, 0, 0)
    return pl.pallas_call(
        functools.partial(_ffn_kernel, sub=256),
        grid=(n // tm,),
        in_specs=[pl.BlockSpec((tm, d), lambda i: (i, 0)),
                  pl.BlockSpec((1, 6, d), lambda i: (i // tps, 0, 0)),
                  pl.BlockSpec((1, d), const),
                  pl.BlockSpec((1, d, dff), lay, pipeline_mode=pl.Buffered(1)),
                  pl.BlockSpec((1, d, dff), lay, pipeline_mode=pl.Buffered(1)),
                  pl.BlockSpec((1, dff, d), lay, pipeline_mode=pl.Buffered(1))],
        out_specs=pl.BlockSpec((tm, d), lambda i: (i, 0)),
        out_shape=jax.ShapeDtypeStruct((n, d), F32),
        compiler_params=_cparams("arbitrary"),
        name="ffn",
    )(x, mod, g, w1, w3, w2)


def _pool_kernel(x_ref, halo_ref, mod_ref, g_ref, win_ref, wmix_ref, scale_ref, wo_ref,
                 g2_ref, rw_ref, rb_ref, o_ref, h_ref, idx_ref, gate_ref, rank_ref, cnt_ref,
                 ext_ref, z_ref, base_scr, before_scr, *, tps):
    tm, d = x_ref.shape
    pg = d // len(POOL_WINDOWS)
    m = mod_ref[0]
    it = pl.program_id(0) % tps

    @pl.when(pl.program_id(0) == 0)
    def _():
        base_scr[...] = jnp.zeros_like(base_scr)
        shape = before_scr.shape
        before = lax.broadcasted_iota(jnp.int32, shape, 1) < lax.broadcasted_iota(jnp.int32, shape, 0)
        before_scr[...] = jnp.where(before, 1.0, 0.0).astype(BF16)

    def project(rows):
        h = _normmod(rows, g_ref[...], m[0:1], m[1:2])
        return _bdot(h.astype(BF16), win_ref[...])

    ext_ref[0:POOL_HALO, :] = jnp.where(it == 0, 0.0, project(halo_ref[...]))
    ext_ref[POOL_HALO:, :] = project(x_ref[...])
    tpos = (it * tm + 1 + lax.broadcasted_iota(jnp.int32, (tm, 1), 0)).astype(F32)
    for gi, w in enumerate(POOL_WINDOWS):
        cols = slice(gi * pg, (gi + 1) * pg)
        s = ext_ref[:, cols]
        span = 1
        while span < w:
            s = s + pltpu.roll(s, span, 0)
            span *= 2
        mean = s[POOL_HALO:, :] * (1.0 / jnp.minimum(tpos, float(w)))
        dlt = (mean - ext_ref[POOL_HALO:, cols]).astype(BF16)
        z_ref[:, cols] = (_bdot(dlt, wmix_ref[gi]) * scale_ref[:, cols]).astype(BF16)
    xn = x_ref[...] + m[2:3] * _bdot(z_ref[...], wo_ref[...])
    o_ref[...] = xn
    rt = before_scr.shape[0]
    for r in range(tm // rt):
        rows = slice(r * rt, (r + 1) * rt)
        hp, idx, gate, rank, total = _route_rows(
            xn[rows], g2_ref[...], m[3:4], m[4:5], rw_ref[...], rb_ref[...],
            before_scr[...], base_scr[...])
        h_ref[rows, :] = hp
        idx_ref[rows, :] = idx
        gate_ref[rows, :] = gate
        rank_ref[rows, :] = rank
        base_scr[...] = total
    cnt_ref[...] = base_scr[...].astype(jnp.int32)


def _pool(x, mod, g, win, wmix, scale, wo, g2, rw, rb, seq, tm, rt):
    n, d = x.shape
    tps = seq // tm
    hb = tm // POOL_HALO
    ng, pg, _ = wmix.shape
    ne = rw.shape[1]
    const = lambda i: (0, 0)
    row = lambda w: pl.BlockSpec((tm, w), lambda i: (i, 0))
    return pl.pallas_call(
        functools.partial(_pool_kernel, tps=tps),
        grid=(n // tm,),
        in_specs=[row(d),
                  pl.BlockSpec((POOL_HALO, d), lambda i: (jnp.maximum(i * hb - 1, 0), 0)),
                  pl.BlockSpec((1, 6, d), lambda i: (i // tps, 0, 0)),
                  pl.BlockSpec((1, d), const),
                  pl.BlockSpec((d, d), const),
                  pl.BlockSpec((ng, pg, pg), lambda i: (0, 0, 0)),
                  pl.BlockSpec((1, d), const),
                  pl.BlockSpec((d, d), const),
                  pl.BlockSpec((1, d), const),
                  pl.BlockSpec((d, ne), const),
                  pl.BlockSpec((1, ne), const)],
        out_specs=[row(d), row(d // 2), row(TOP_K), row(TOP_K), row(TOP_K),
                   pl.BlockSpec((1, ne), const)],
        out_shape=[jax.ShapeDtypeStruct((n, d), F32),
                   jax.ShapeDtypeStruct((n, d // 2), jnp.int32),
                   jax.ShapeDtypeStruct((n, TOP_K), jnp.int32),
                   jax.ShapeDtypeStruct((n, TOP_K), F32),
                   jax.ShapeDtypeStruct((n, TOP_K), jnp.int32),
                   jax.ShapeDtypeStruct((1, ne), jnp.int32)],
        scratch_shapes=[pltpu.VMEM((tm + POOL_HALO, d), F32), pltpu.VMEM((tm, d), BF16),
                        pltpu.VMEM((1, ne), F32), pltpu.VMEM((rt, rt), BF16)],
        compiler_params=_cparams("arbitrary"),
        name="pool_router",
    )(x, x, mod, g, win, wmix, scale, wo, g2, rw, rb)


def _pack_bf16_pairs(v):
    w = v.shape[1] // 2
    bits = lambda a: lax.bitcast_convert_type(a.astype(BF16).astype(F32), jnp.int32)
    return bits(v[:, :w]) | lax.shift_right_logical(bits(v[:, w:]), 16)


def _unpack_bf16_pairs(p):
    hi = lax.bitcast_convert_type(p & jnp.int32(-65536), F32)
    lo = lax.bitcast_convert_type(lax.shift_left(p, 16), F32)
    return jnp.concatenate([hi, lo], axis=1)


def _route_rows(x, g, shift, scale, rw, rb, before, base):
    h = _normmod(x, g, shift, scale)
    hp = _pack_bf16_pairs(h)
    split = lambda a: (a.astype(BF16), (a - a.astype(BF16).astype(F32)).astype(BF16))
    h_hi, h_lo = split(h)
    w_hi, w_lo = split(rw)
    logits = (_bdot(h_hi, w_hi) + _bdot(h_hi, w_lo) + _bdot(h_lo, w_hi)) + rb
    ne = float(logits.shape[1])
    lane = lax.broadcasted_iota(jnp.int32, logits.shape, 1).astype(F32)
    v1 = jnp.max(logits, axis=1, keepdims=True)
    i1 = jnp.min(jnp.where(logits == v1, lane, ne), axis=1, keepdims=True)
    rest = jnp.where(lane == i1, -jnp.inf, logits)
    v2 = jnp.max(rest, axis=1, keepdims=True)
    i2 = jnp.min(jnp.where(rest == v2, lane, ne), axis=1, keepdims=True)
    e2 = jnp.exp(v2 - v1)
    den = 1.0 + e2
    two = lax.broadcasted_iota(jnp.int32, (logits.shape[0], TOP_K), 1)
    idx = jnp.where(two == 0, i1, i2).astype(jnp.int32)
    gate = jnp.where(two == 0, 1.0 / den, e2 / den)
    oh1 = jnp.where(lane == i1, 1.0, 0.0)
    oh2 = jnp.where(lane == i2, 1.0, 0.0)
    both = oh1 + oh2
    prior = _bdot(before, both.astype(BF16)) + base
    r1 = jnp.sum(oh1 * prior, axis=1, keepdims=True)
    r2 = jnp.sum(oh2 * prior, axis=1, keepdims=True)
    rank = jnp.where(two == 0, r1, r2).astype(jnp.int32)
    return hp, idx, gate, rank, base + jnp.sum(both, axis=0, keepdims=True)


def _count_le(sorted_vals, queries):
    return jnp.sum((sorted_vals[None, :] <= queries[:, None]).astype(jnp.int32), axis=1)


def _routing_tables(idx, rank, cnt, ne, tile):
    n = idx.shape[0]
    padded = ((cnt + tile - 1) // tile) * tile
    off = jnp.cumsum(padded) - padded
    onehot = idx[:, :, None] == jnp.arange(ne, dtype=jnp.int32)[None, None, :]
    pos = rank + jnp.sum(jnp.where(onehot, off[None, None, :], 0), axis=2)
    n_tiles = (TOP_K * n) // tile + ne
    used = jnp.sum(padded) // tile
    tile_ids = jnp.arange(n_tiles, dtype=jnp.int32)
    tile_expert = _count_le(jnp.cumsum(padded), tile_ids * tile)
    last_used = jnp.maximum(used - 1, 0)
    tile_valid = (tile_ids < used).astype(jnp.int32)
    tile_src = jnp.minimum(tile_ids, last_used).astype(jnp.int32)
    tile_expert = jnp.minimum(tile_expert, ne - 1).astype(jnp.int32)
    tile_expert = jnp.where(tile_valid == 1, tile_expert, tile_expert[last_used])
    tile_rows = jnp.clip(cnt[tile_expert] - (tile_ids * tile - off[tile_expert]), 0, tile)
    tile_rows = jnp.where(tile_valid == 1, tile_rows, 0).astype(jnp.int32)
    return dict(pos=pos, tile_expert=tile_expert, tile_valid=tile_valid, tile_src=tile_src,
                tile_rows=tile_rows, n_tiles=n_tiles)


def _sc_mesh():
    return plsc.VectorSubcoreMesh(core_axis_name="c", subcore_axis_name="s",
                                  num_cores=SC_CORES, num_subcores=SC_SUBCORES)


def _dispatch(hp, pos0, pos1, rows_out):
    n, w = hp.shape
    per_w = n // SC_WORKERS
    ch = SC_ROWS
    nch = per_w // ch
    assert per_w * SC_WORKERS == n and nch * ch == per_w and nch % 2 == 0

    @functools.partial(
        pl.kernel, mesh=_sc_mesh(),
        out_type=jax.ShapeDtypeStruct((rows_out, w), hp.dtype),
        scratch_types=[pltpu.VMEM((per_w,), jnp.int32), pltpu.VMEM((per_w,), jnp.int32),
                       pltpu.VMEM((ch, w), hp.dtype), pltpu.VMEM((ch, w), hp.dtype)]
        + [pltpu.SemaphoreType.DMA] * 4,
        name="dispatch",
    )
    def k(hp_hbm, p0_hbm, p1_hbm, out_hbm, i0_v, i1_v, r0_v, r1_v, l0, l1, s0, s1):
        base = (lax.axis_index("s") * SC_CORES + lax.axis_index("c")) * per_w
        pltpu.sync_copy(p0_hbm.at[pl.ds(base, per_w)], i0_v)
        pltpu.sync_copy(p1_hbm.at[pl.ds(base, per_w)], i1_v)
        bufs, lsem, ssem = (r0_v, r1_v), (l0, l1), (s0, s1)

        def load(j, b):
            return pltpu.make_async_copy(hp_hbm.at[pl.ds(base + j * ch, ch)], bufs[b], lsem[b])

        def scatter(j, b, idx):
            return pltpu.make_async_copy(bufs[b], out_hbm.at[idx.at[pl.ds(j * ch, ch)]], ssem[b])

        load(0, 0).start()

        @pl.loop(0, nch, step=2)
        def _(j):
            for b in range(2):
                jj = j + b
                load(jj, b).wait()

                @pl.when(jj + 1 < nch)
                def _():
                    load(jj + 1, 1 - b).start()

                c0 = scatter(jj, b, i0_v)
                c1 = scatter(jj, b, i1_v)
                c0.start()
                c1.start()
                c0.wait()
                c1.wait()

    return k(hp, pos0, pos1)


def _gather_pairs(ys, pos0, pos1):
    n = pos0.shape[0]
    w = ys.shape[1]
    per_w = n // SC_WORKERS
    ch = SC_ROWS
    nch = per_w // ch
    assert per_w * SC_WORKERS == n and nch * ch == per_w
    out = jax.ShapeDtypeStruct((n, w), ys.dtype)

    @functools.partial(
        pl.kernel, mesh=_sc_mesh(), out_type=(out, out),
        scratch_types=[pltpu.VMEM((per_w,), jnp.int32), pltpu.VMEM((per_w,), jnp.int32),
                       pltpu.VMEM((ch, w), ys.dtype), pltpu.VMEM((ch, w), ys.dtype)]
        + [pltpu.SemaphoreType.DMA] * 4,
        name="gather_pairs",
    )
    def k(ys_hbm, p0_hbm, p1_hbm, a_hbm, b_hbm, i0_v, i1_v, ra_v, rb_v, ga, gb, wa, wb):
        base = (lax.axis_index("s") * SC_CORES + lax.axis_index("c")) * per_w
        pltpu.sync_copy(p0_hbm.at[pl.ds(base, per_w)], i0_v)
        pltpu.sync_copy(p1_hbm.at[pl.ds(base, per_w)], i1_v)

        def gather(j, idx, buf, sem):
            return pltpu.make_async_copy(ys_hbm.at[idx.at[pl.ds(j * ch, ch)]], buf, sem)

        def write(j, buf, dst, sem):
            return pltpu.make_async_copy(buf, dst.at[pl.ds(base + j * ch, ch)], sem)

        gather(0, i0_v, ra_v, ga).start()
        gather(0, i1_v, rb_v, gb).start()

        @pl.loop(0, nch)
        def _(j):
            gather(j, i0_v, ra_v, ga).wait()
            write(j, ra_v, a_hbm, wa).start()
            gather(j, i1_v, rb_v, gb).wait()
            write(j, rb_v, b_hbm, wb).start()
            write(j, ra_v, a_hbm, wa).wait()

            @pl.when(j + 1 < nch)
            def _():
                gather(j + 1, i0_v, ra_v, ga).start()

            write(j, rb_v, b_hbm, wb).wait()

            @pl.when(j + 1 < nch)
            def _():
                gather(j + 1, i1_v, rb_v, gb).start()

    return k(ys, pos0, pos1)


def _expert_kernel(te_ref, tv_ref, tsrc_ref, tr_ref, xs_ref, w1_ref, w3_ref, w2_ref,
                   ys_ref, *, sub):
    j = pl.program_id(0)

    @pl.when(tv_ref[j] == 1)
    def _():
        x = _unpack_bf16_pairs(xs_ref[...])
        row = lax.broadcasted_iota(jnp.int32, x.shape, 0)
        x = jnp.where(row < tr_ref[j], x, 0.0).astype(BF16)
        ys_ref[...] = _pack_bf16_pairs(_swiglu(x, w1_ref.at[0], w3_ref.at[0], w2_ref.at[0], sub))

    @pl.when(tv_ref[j] == 0)
    def _():
        ys_ref[...] = jnp.zeros_like(ys_ref)


def _experts(xs, w1, w3, w2, rt, tile):
    rows, wp = xs.shape
    ne, d, dff = w1.shape
    return pl.pallas_call(
        functools.partial(_expert_kernel, sub=512),
        grid_spec=pltpu.PrefetchScalarGridSpec(
            num_scalar_prefetch=4,
            grid=(rt['n_tiles'],),
            in_specs=[pl.BlockSpec((tile, wp), lambda j, te, tv, tsrc, tr: (tsrc[j], 0)),
                      pl.BlockSpec((1, d, dff), lambda j, te, tv, tsrc, tr: (te[j], 0, 0)),
                      pl.BlockSpec((1, d, dff), lambda j, te, tv, tsrc, tr: (te[j], 0, 0)),
                      pl.BlockSpec((1, dff, d), lambda j, te, tv, tsrc, tr: (te[j], 0, 0))],
            out_specs=pl.BlockSpec((tile, wp), lambda j, te, tv, tsrc, tr: (j, 0))),
        out_shape=jax.ShapeDtypeStruct((rows, wp), jnp.int32),
        compiler_params=_cparams("arbitrary"),
        name="experts",
    )(rt['tile_expert'], rt['tile_valid'], rt['tile_src'], rt['tile_rows'], xs, w1, w3, w2)


def _combine_kernel(a_ref, b_ref, gate_ref, x_ref, mod_ref, fg_ref, o_ref, *, final_norm):
    m = mod_ref[0]
    g = gate_ref[...]
    y = g[:, 0:1] * _unpack_bf16_pairs(a_ref[...]) + g[:, 1:2] * _unpack_bf16_pairs(b_ref[...])
    xn = x_ref[...] + m[5:6] * y
    if final_norm:
        ms = jnp.mean(xn * xn, axis=-1, keepdims=True)
        xn = (xn * lax.rsqrt(ms + EPS)) * fg_ref[...]
    o_ref[...] = xn


def _combine(a, b, gates, x, mod, fg, seq, tm, final_norm, tile0):
    n, d = x.shape
    tps = seq // tm
    part = lambda w: pl.BlockSpec((tm, w), lambda i: (i, 0))
    full = lambda w: pl.BlockSpec((tm, w), lambda i: (i + tile0, 0))
    return pl.pallas_call(
        functools.partial(_combine_kernel, final_norm=final_norm),
        grid=(a.shape[0] // tm,),
        in_specs=[part(d // 2), part(d // 2), full(TOP_K), full(d),
                  pl.BlockSpec((1, 6, d), lambda i: ((i + tile0) // tps, 0, 0)),
                  pl.BlockSpec((1, d), lambda i: (0, 0))],
        out_specs=full(d),
        out_shape=jax.ShapeDtypeStruct((n, d), F32),
        input_output_aliases={3: 0},
        compiler_params=_cparams("arbitrary"),
        name="combine",
    )(a, b, gates, x, mod, fg)


def _pick(n, prefs):
    for p in prefs:
        if n % p == 0:
            return p
    return n


def kernel(x, c, ada_w, ada_b, norm_g, ssm_in, ssm_log_dt, ssm_lam_re, ssm_lam_im,
           ssm_b_re, ssm_b_im, ssm_c_re, ssm_c_im, ssm_d, ssm_glu, ssm_out,
           pool_in, pool_mix, pool_scale, pool_out, ffn_w1, ffn_w3, ffn_w2,
           router_w, router_b, moe_w1, moe_w3, moe_w2, final_g):
    bsz, seq, d = x.shape
    depth = ada_w.shape[0]
    n = bsz * seq
    ne = router_w.shape[-1]
    g_, p_, h_ = ssm_b_re.shape[1:]
    t = SSM_T
    if seq % (LANES * t // bsz) != 0 or bsz != 8:
        raise NotImplementedError("state-space kernels assume batch 8 and seq % 256 == 0")
    if depth % 2 != 0:
        raise NotImplementedError("the final RMSNorm is fused into an expert layer's combine")
    tm = _pick(seq, (512, 256, 128, 64, 32, 16))
    tile = _pick(TOP_K * n, (512, 256, 128, 64, 32, 16))
    gb = _pick(g_, (4, 2, 1))

    stack = lambda w: w.astype(BF16).reshape((-1,) + w.shape[2:])
    ew1, ew3, ew2 = stack(moe_w1), stack(moe_w3), stack(moe_w2)
    fw1, fw3, fw2 = ffn_w1.astype(BF16), ffn_w3.astype(BF16), ffn_w2.astype(BF16)
    mod = _ada(c, ada_w, ada_b).reshape(depth, bsz, 6, d)
    xf = x.reshape(n, d)
    fg = final_g.reshape(1, d)
    for i in range(depth):
        j = i // 2
        mod_i = mod[i]
        g_a = norm_g[i, 0].reshape(1, d)
        g_b = norm_g[i, 1].reshape(1, d)
        if i % 2 == 0:
            ops = _ssm_operators(ssm_log_dt[j], ssm_lam_re[j], ssm_lam_im[j], ssm_b_re[j],
                                 ssm_b_im[j], ssm_c_re[j], ssm_c_im[j], ssm_d[j], t)
            ut = _ssm_in(xf.reshape(bsz, seq, d), mod_i, g_a, ssm_in[j].astype(BF16), t, h_, 4)
            yt = _ssm(ut, *ops, bsz, gb)
            xf = _glu_out(yt, xf.reshape(bsz, seq, d), mod_i, ssm_glu[j].astype(BF16),
                          ssm_out[j].astype(BF16), t, 4).reshape(n, d)
            xf = _ffn(xf, mod_i, g_b, fw1, fw3, fw2, seq, tm, layer=j)
        else:
            xf, h, idx, gates, rank, cnt = _pool(
                xf, mod_i, g_a, pool_in[j].astype(BF16), pool_mix[j].astype(BF16),
                pool_scale[j].reshape(1, d), pool_out[j].astype(BF16),
                g_b, router_w[j], router_b[j].reshape(1, ne), seq, 2 * tm, tm)
            rt = _routing_tables(idx, rank, cnt[0], ne, tile)
            pos0, pos1 = rt['pos'][:, 0], rt['pos'][:, 1]
            xs = _dispatch(h, pos0, pos1, rt['n_tiles'] * tile)
            rt['tile_expert'] = rt['tile_expert'] + j * ne
            ys = _experts(xs, ew1, ew3, ew2, rt, tile)
            npc = n // COMBINE_PIECES
            for q in range(COMBINE_PIECES):
                piece = slice(q * npc, (q + 1) * npc)
                ya, yb = _gather_pairs(ys, pos0[piece], pos1[piece])
                xf = _combine(ya, yb, gates, xf, mod_i, fg, seq, 2 * tm,
                              final_norm=(i == depth - 1), tile0=q * (npc // (2 * tm)))
    return xf.reshape(bsz, seq, d)
```

```python
import functools
import math

import jax
import jax.numpy as jnp
from jax import lax
from jax.experimental import pallas as pl
from jax.experimental.pallas import tpu as pltpu
from jax.experimental.pallas import tpu_sc as plsc

F32 = jnp.float32
BF16 = jnp.bfloat16
EPS = 1e-6
POOL_WINDOWS = (2, 4, 8, 16)
POOL_HALO = 16
SSM_T = 16
LANES = 128
TOP_K = 2
SC_CORES, SC_SUBCORES = 2, 16
SC_WORKERS = SC_CORES * SC_SUBCORES
SC_ROWS = 64
COMBINE_PIECES = 2
VMEM_LIMIT = 58 * 1024 * 1024


def _cparams(*sem):
    return pltpu.CompilerParams(dimension_semantics=sem, vmem_limit_bytes=VMEM_LIMIT)


def _normmod(x, g, shift, scale):
    ms = jnp.mean(x * x, axis=-1, keepdims=True)
    y = x * lax.rsqrt(ms + EPS)
    return (y * g) * (1.0 + scale) + shift


def _bdot(a, b):
    return jnp.dot(a, b, preferred_element_type=F32)


def _ada_kernel(c_ref, w_ref, b_ref, o_ref):
    c = c_ref[...]
    cond = c * jax.nn.sigmoid(c)
    o_ref[0] = jnp.dot(cond, w_ref[0], precision=lax.Precision.HIGHEST,
                       preferred_element_type=F32) + b_ref[0]


def _ada(c, ada_w, ada_b):
    depth, d, d6 = ada_w.shape
    bsz = c.shape[0]
    tn = d6 // 4
    return pl.pallas_call(
        _ada_kernel,
        grid=(depth, d6 // tn),
        in_specs=[pl.BlockSpec((bsz, d), lambda l, j: (0, 0)),
                  pl.BlockSpec((1, d, tn), lambda l, j: (l, 0, j)),
                  pl.BlockSpec((1, 1, tn), lambda l, j: (l, 0, j))],
        out_specs=pl.BlockSpec((1, bsz, tn), lambda l, j: (l, 0, j)),
        out_shape=jax.ShapeDtypeStruct((depth, bsz, d6), F32),
        compiler_params=_cparams("arbitrary", "arbitrary"),
        name="ada",
    )(c, ada_w, ada_b.reshape(depth, 1, d6))


def _ssm_operators(log_dt, lam_re, lam_im, b_re, b_im, c_re, c_im, d_skip, t):
    g_, p_, h_ = b_re.shape
    dt = jnp.exp(log_dt.astype(F32))[:, None]
    lr = lam_re.astype(F32)
    li = lam_im.astype(F32)
    mag = jnp.exp(lr * dt)
    a_re = mag * jnp.cos(li * dt)
    a_im = mag * jnp.sin(li * dt)
    den = lr * lr + li * li
    nr = a_re - 1
    coef_re = (nr * lr + a_im * li) / den
    coef_im = (a_im * lr - nr * li) / den
    br = b_re.astype(F32)
    bi = b_im.astype(F32)
    bbar_re = coef_re[..., None] * br - coef_im[..., None] * bi
    bbar_im = coef_re[..., None] * bi + coef_im[..., None] * br
    steps = jnp.arange(0, t + 1, dtype=F32)[None, :, None]
    ang = (li * dt)[:, None, :] * steps
    pmag = jnp.exp((lr * dt)[:, None, :] * steps)
    pw_re = pmag * jnp.cos(ang)
    pw_im = pmag * jnp.sin(ang)
    cr = c_re.astype(F32)
    ci = c_im.astype(F32)
    er = cr[:, None] * pw_re[:, :, None, :] - ci[:, None] * pw_im[:, :, None, :]
    ei = cr[:, None] * pw_im[:, :, None, :] + ci[:, None] * pw_re[:, :, None, :]
    tg = lambda a: jnp.transpose(a, (1, 2, 0))
    pt_re, pt_im = tg(pw_re[:, :t])[:, None], tg(pw_im[:, :t])[:, None]
    ct_re, ct_im = tg(cr)[None], tg(ci)[None]
    er_l = (ct_re * pt_re - ct_im * pt_im)[..., None]
    ei_l = (ct_re * pt_im + ct_im * pt_re)[..., None]
    br_l = jnp.transpose(bbar_re, (1, 0, 2))[None, None]
    bi_l = jnp.transpose(bbar_im, (1, 0, 2))[None, None]
    kk = jnp.sum(er_l * br_l - ei_l * bi_l, axis=2)
    kk = jnp.transpose(kk, (2, 0, 1, 3))
    kz = jnp.pad(kk, ((0, 0), (t - 1, 0), (0, 0), (0, 0)))
    cols = [kz[:, t - 1 - j:2 * t - 1 - j] for j in range(t)]
    mt = jnp.stack(cols, axis=3).reshape(g_, t * h_, t * h_)
    rev_re = pw_re[:, :t][:, ::-1, None, :]
    rev_im = pw_im[:, :t][:, ::-1, None, :]
    bt_re = jnp.transpose(bbar_re, (0, 2, 1))[:, None]
    bt_im = jnp.transpose(bbar_im, (0, 2, 1))[:, None]
    w_re = (rev_re * bt_re - rev_im * bt_im).reshape(g_, t * h_, p_)
    w_im = (rev_re * bt_im + rev_im * bt_re).reshape(g_, t * h_, p_)
    w4 = jnp.concatenate([w_re, w_im, w_im, w_re], axis=-1)
    vt = jnp.concatenate([er[:, 1:].reshape(g_, t * h_, p_),
                          -ei[:, 1:].reshape(g_, t * h_, p_)], axis=-1)
    at_re, at_im = pw_re[:, t], pw_im[:, t]
    c1 = jnp.concatenate([at_re, at_re], axis=-1)
    c2 = jnp.concatenate([-at_im, at_im], axis=-1)
    coef = jnp.concatenate([c1[:, None], c2[:, None], jnp.zeros((g_, 6, 2 * p_), F32)], axis=1)
    dk = jnp.tile(d_skip.astype(F32)[:, None, :], (1, t, 1)).reshape(g_, t * h_, 1)
    return mt.astype(BF16), w4.astype(BF16), vt.astype(BF16), coef, dk


def _ssm_in_kernel(x_ref, mod_ref, g_ref, w_ref, ut_ref, h_scr, u_scr, *, t):
    q = pl.program_id(1)
    bsz, cs, _ = x_ref.shape
    nlt = u_scr.shape[0]
    for b in range(bsz):
        m = mod_ref[b]
        h = _normmod(x_ref[b], g_ref[...], m[0:1], m[1:2]).astype(BF16)
        for c in range(cs // t):
            h_scr[(c * bsz + b) * t:(c * bsz + b + 1) * t, :] = h[c * t:(c + 1) * t, :]
    u = _bdot(h_scr[...], w_ref[...])
    rows = pl.ds(pl.multiple_of(q * (bsz * cs), bsz * cs), bsz * cs)
    for l in range(nlt):
        u_scr[l, rows, :] = u[:, l * LANES:(l + 1) * LANES]

    @pl.when(q == pl.num_programs(1) - 1)
    def _():
        ng, th, nch = ut_ref.shape
        hh = th // t
        gl = LANES // hh
        for l in range(nlt):
            for k in range(t):
                uk = u_scr[l, pl.ds(k, nch, stride=t), :]
                ut_ref[l * gl:(l + 1) * gl, k * hh:(k + 1) * hh, :] = (
                    uk.T.reshape(gl, hh, nch).astype(BF16))


def _ssm_in(x, mod, g, w, t, hh, nq):
    bsz, seq, d = x.shape
    ng = d // hh
    cs = LANES * t // bsz // nq
    return pl.pallas_call(
        functools.partial(_ssm_in_kernel, t=t),
        grid=(seq // (cs * nq), nq),
        in_specs=[pl.BlockSpec((bsz, cs, d), lambda o, q: (0, o * nq + q, 0)),
                  pl.BlockSpec((bsz, 6, d), lambda o, q: (0, 0, 0)),
                  pl.BlockSpec((1, d), lambda o, q: (0, 0)),
                  pl.BlockSpec((d, d), lambda o, q: (0, 0))],
        out_specs=pl.BlockSpec((ng, t * hh, LANES), lambda o, q: (0, 0, o)),
        out_shape=jax.ShapeDtypeStruct((ng, t * hh, bsz * seq // t), BF16),
        scratch_shapes=[pltpu.VMEM((bsz * cs, d), BF16),
                        pltpu.VMEM((d // LANES, LANES * t, LANES), F32)],
        compiler_params=_cparams("arbitrary", "arbitrary"),
        name="ssm_in",
    )(x, mod, g, w)


def _ssm_kernel(ut_ref, mt_ref, w4_ref, vt_ref, coef_ref, dk_ref, yt_ref, z_scr, s_scr,
                *, gb, nchunks, bsz):
    p2 = s_scr.shape[-1]
    for gi in range(gb):
        u = ut_ref[gi].astype(F32).T.astype(BF16)
        z = _bdot(u, w4_ref[gi])
        z_scr[gi, 0] = z[:, :p2]
        z_scr[gi, 1] = z[:, p2:]
    c1 = [jnp.broadcast_to(coef_ref[gi, 0:1, :], (bsz, p2)) for gi in range(gb)]
    c2 = [jnp.broadcast_to(coef_ref[gi, 1:2, :], (bsz, p2)) for gi in range(gb)]

    def body(c, carry):
        new = []
        for gi in range(gb):
            s1, s2 = carry[gi]
            rows = pl.ds(pl.multiple_of(c * bsz, bsz), bsz)
            s_scr[gi, rows, :] = s1
            n1 = c1[gi] * s1 + c2[gi] * s2 + z_scr[gi, 0, rows, :]
            n2 = c1[gi] * s2 - c2[gi] * s1 + z_scr[gi, 1, rows, :]
            new.append((n1, n2))
        return tuple(new)

    zero = jnp.zeros((bsz, p2), F32)
    lax.fori_loop(0, nchunks, body, tuple((zero, zero) for _ in range(gb)))
    for gi in range(gb):
        ut = ut_ref[gi]
        st = _bdot(vt_ref[gi], s_scr[gi].T.astype(BF16))
        yt = _bdot(mt_ref[gi], ut) + st + dk_ref[gi] * ut.astype(F32)
        yt_ref[gi] = yt.astype(yt_ref.dtype)


def _ssm(ut, mt, w4, vt, coef, dk, bsz, gb):
    g_, th, nch = ut.shape
    p2 = vt.shape[2]
    kern = functools.partial(_ssm_kernel, gb=gb, nchunks=nch // bsz, bsz=bsz)
    spec3 = lambda a, b: pl.BlockSpec((gb, a, b), lambda i: (i, 0, 0))
    return pl.pallas_call(
        kern,
        grid=(g_ // gb,),
        in_specs=[spec3(th, nch), spec3(th, th), spec3(th, 2 * p2), spec3(th, p2),
                  spec3(8, p2), spec3(th, 1)],
        out_specs=spec3(th, nch),
        out_shape=jax.ShapeDtypeStruct((g_, th, nch), BF16),
        scratch_shapes=[pltpu.VMEM((gb, 2, nch, p2), F32), pltpu.VMEM((gb, nch, p2), F32)],
        compiler_params=_cparams("arbitrary"),
        name="ssm",
    )(ut, mt, w4, vt, coef, dk)


def _gelu_tanh(y):
    return y * (0.5 * (1.0 + jnp.tanh(math.sqrt(2.0 / math.pi) * (y + 0.044715 * (y * y * y)))))


def _glu_ffn_kernel(yt_ref, x_ref, mod_ref, wg_ref, wo_ref, g2_ref, w1_ref, w3_ref, w2_ref,
                    o_ref, y_scr, xm_scr, h2_scr, *, t, sub):
    q = pl.program_id(1)
    bsz, cs, _ = x_ref.shape
    nlt = y_scr.shape[0]

    @pl.when(q == 0)
    def _():
        ng, th, nch = yt_ref.shape
        hh = th // t
        gl = LANES // hh
        for l in range(nlt):
            for k in range(t):
                yk = yt_ref[l * gl:(l + 1) * gl, k * hh:(k + 1) * hh, :].astype(F32)
                y_scr[l, pl.ds(k, nch, stride=t), :] = yk.reshape(LANES, nch).T

    rows = pl.ds(pl.multiple_of(q * (bsz * cs), bsz * cs), bsz * cs)
    y = jnp.concatenate([y_scr[l, rows, :] for l in range(nlt)], axis=1)
    z = _gelu_tanh(y)
    z = z * jax.nn.sigmoid(_bdot(z.astype(BF16), wg_ref[...]))
    o = _bdot(z.astype(BF16), wo_ref[...])
    pieces = [(b, c, slice(c * t, (c + 1) * t), slice((c * bsz + b) * t, (c * bsz + b + 1) * t))
              for b in range(bsz) for c in range(cs // t)]
    for b, c, tok, rws in pieces:
        m = mod_ref[b]
        xm = x_ref[b, tok, :] + m[2:3] * o[rws]
        xm_scr[rws, :] = xm
        h2_scr[rws, :] = _normmod(xm, g2_ref[...], m[3:4], m[4:5]).astype(BF16)
    f = _swiglu(h2_scr[...], w1_ref.at[0], w3_ref.at[0], w2_ref.at[0], sub)
    for b, c, tok, rws in pieces:
        o_ref[b, tok, :] = xm_scr[rws, :] + mod_ref[b][5:6] * f[rws]


def _glu_ffn(yt, x, mod, wg, wo, g2, w1, w3, w2, t, nq, layer):
    bsz, seq, d = x.shape
    ng, th, _ = yt.shape
    dff = w1.shape[2]
    cs = LANES * t // bsz // nq
    once = pl.Buffered(1)
    const = lambda o, q: (0, 0)
    lay = lambda o, q: (layer, 0, 0)
    return pl.pallas_call(
        functools.partial(_glu_ffn_kernel, t=t, sub=256),
        grid=(seq // (cs * nq), nq),
        in_specs=[pl.BlockSpec((ng, th, LANES), lambda o, q: (0, 0, o), pipeline_mode=once),
                  pl.BlockSpec((bsz, cs, d), lambda o, q: (0, o * nq + q, 0)),
                  pl.BlockSpec((bsz, 6, d), lambda o, q: (0, 0, 0)),
                  pl.BlockSpec((d, d), const, pipeline_mode=once),
                  pl.BlockSpec((d, d), const, pipeline_mode=once),
                  pl.BlockSpec((1, d), const),
                  pl.BlockSpec((1, d, dff), lay, pipeline_mode=once),
                  pl.BlockSpec((1, d, dff), lay, pipeline_mode=once),
                  pl.BlockSpec((1, dff, d), lay, pipeline_mode=once)],
        out_specs=pl.BlockSpec((bsz, cs, d), lambda o, q: (0, o * nq + q, 0)),
        out_shape=jax.ShapeDtypeStruct((bsz, seq, d), F32),
        scratch_shapes=[pltpu.VMEM((d // LANES, LANES * t, LANES), F32),
                        pltpu.VMEM((bsz * cs, d), F32), pltpu.VMEM((bsz * cs, d), BF16)],
        compiler_params=_cparams("arbitrary", "arbitrary"),
        name="glu_ffn",
    )(yt, x, mod, wg, wo, g2, w1, w3, w2)


def _chunks(total, step):
    return [(s, min(step, total - s)) for s in range(0, total, step)]


def _swiglu(h, w1_ref, w3_ref, w2_ref, sub):
    total = None
    for s, n in _chunks(w1_ref.shape[-1], sub):
        a = _bdot(h, w1_ref[:, s:s + n])
        b = _bdot(h, w3_ref[:, s:s + n])
        act = (a * jax.nn.sigmoid(a) * b).astype(BF16)
        y = _bdot(act, w2_ref[s:s + n, :])
        total = y if total is None else total + y
    return total


def _pool_kernel(x_ref, halo_ref, mod_ref, g_ref, win_ref, wmix_ref, scale_ref, wo_ref,
                 g2_ref, rw_ref, rb_ref, o_ref, h_ref, idx_ref, gate_ref, rank_ref, cnt_ref,
                 ext_ref, z_ref, base_scr, before_scr, *, tps):
    tm, d = x_ref.shape
    pg = d // len(POOL_WINDOWS)
    m = mod_ref[0]
    it = pl.program_id(0) % tps

    @pl.when(pl.program_id(0) == 0)
    def _():
        base_scr[...] = jnp.zeros_like(base_scr)
        shape = before_scr.shape
        before = lax.broadcasted_iota(jnp.int32, shape, 1) < lax.broadcasted_iota(jnp.int32, shape, 0)
        before_scr[...] = jnp.where(before, 1.0, 0.0).astype(BF16)

    def project(rows):
        h = _normmod(rows, g_ref[...], m[0:1], m[1:2])
        return _bdot(h.astype(BF16), win_ref[...])

    ext_ref[0:POOL_HALO, :] = jnp.where(it == 0, 0.0, project(halo_ref[...]))
    ext_ref[POOL_HALO:, :] = project(x_ref[...])
    tpos = (it * tm + 1 + lax.broadcasted_iota(jnp.int32, (tm, 1), 0)).astype(F32)
    for gi, w in enumerate(POOL_WINDOWS):
        cols = slice(gi * pg, (gi + 1) * pg)
        s = ext_ref[:, cols]
        span = 1
        while span < w:
            s = s + pltpu.roll(s, span, 0)
            span *= 2
        mean = s[POOL_HALO:, :] * (1.0 / jnp.minimum(tpos, float(w)))
        dlt = (mean - ext_ref[POOL_HALO:, cols]).astype(BF16)
        z_ref[:, cols] = (_bdot(dlt, wmix_ref[gi]) * scale_ref[:, cols]).astype(BF16)
    xn = x_ref[...] + m[2:3] * _bdot(z_ref[...], wo_ref[...])
    o_ref[...] = xn
    rt = before_scr.shape[0]
    for r in range(tm // rt):
        rows = slice(r * rt, (r + 1) * rt)
        hp, idx, gate, rank, total = _route_rows(
            xn[rows], g2_ref[...], m[3:4], m[4:5], rw_ref[...], rb_ref[...],
            before_scr[...], base_scr[...])
        h_ref[rows, :] = hp
        idx_ref[rows, :] = idx
        gate_ref[rows, :] = gate
        rank_ref[rows, :] = rank
        base_scr[...] = total
    cnt_ref[...] = base_scr[...].astype(jnp.int32)


def _pool(x, mod, g, win, wmix, scale, wo, g2, rw, rb, seq, tm, rt):
    n, d = x.shape
    tps = seq // tm
    hb = tm // POOL_HALO
    ng, pg, _ = wmix.shape
    ne = rw.shape[1]
    const = lambda i: (0, 0)
    row = lambda w: pl.BlockSpec((tm, w), lambda i: (i, 0))
    return pl.pallas_call(
        functools.partial(_pool_kernel, tps=tps),
        grid=(n // tm,),
        in_specs=[row(d),
                  pl.BlockSpec((POOL_HALO, d), lambda i: (jnp.maximum(i * hb - 1, 0), 0)),
                  pl.BlockSpec((1, 6, d), lambda i: (i // tps, 0, 0)),
                  pl.BlockSpec((1, d), const),
                  pl.BlockSpec((d, d), const),
                  pl.BlockSpec((ng, pg, pg), lambda i: (0, 0, 0)),
                  pl.BlockSpec((1, d), const),
                  pl.BlockSpec((d, d), const),
                  pl.BlockSpec((1, d), const),
                  pl.BlockSpec((d, ne), const),
                  pl.BlockSpec((1, ne), const)],
        out_specs=[row(d), row(d // 2), row(TOP_K), row(TOP_K), row(TOP_K),
                   pl.BlockSpec((1, ne), const)],
        out_shape=[jax.ShapeDtypeStruct((n, d), F32),
                   jax.ShapeDtypeStruct((n, d // 2), jnp.int32),
                   jax.ShapeDtypeStruct((n, TOP_K), jnp.int32),
                   jax.ShapeDtypeStruct((n, TOP_K), F32),
                   jax.ShapeDtypeStruct((n, TOP_K), jnp.int32),
                   jax.ShapeDtypeStruct((1, ne), jnp.int32)],
        scratch_shapes=[pltpu.VMEM((tm + POOL_HALO, d), F32), pltpu.VMEM((tm, d), BF16),
                        pltpu.VMEM((1, ne), F32), pltpu.VMEM((rt, rt), BF16)],
        compiler_params=_cparams("arbitrary"),
        name="pool_router",
    )(x, x, mod, g, win, wmix, scale, wo, g2, rw, rb)


def _pack_bf16_pairs(v):
    w = v.shape[1] // 2
    bits = lambda a: lax.bitcast_convert_type(a.astype(BF16).astype(F32), jnp.int32)
    return bits(v[:, :w]) | lax.shift_right_logical(bits(v[:, w:]), 16)


def _unpack_bf16_pairs(p):
    hi = lax.bitcast_convert_type(p & jnp.int32(-65536), F32)
    lo = lax.bitcast_convert_type(lax.shift_left(p, 16), F32)
    return jnp.concatenate([hi, lo], axis=1)


def _route_rows(x, g, shift, scale, rw, rb, before, base):
    h = _normmod(x, g, shift, scale)
    hp = _pack_bf16_pairs(h)
    split = lambda a: (a.astype(BF16), (a - a.astype(BF16).astype(F32)).astype(BF16))
    h_hi, h_lo = split(h)
    w_hi, w_lo = split(rw)
    logits = (_bdot(h_hi, w_hi) + _bdot(h_hi, w_lo) + _bdot(h_lo, w_hi)) + rb
    ne = float(logits.shape[1])
    lane = lax.broadcasted_iota(jnp.int32, logits.shape, 1).astype(F32)
    v1 = jnp.max(logits, axis=1, keepdims=True)
    i1 = jnp.min(jnp.where(logits == v1, lane, ne), axis=1, keepdims=True)
    rest = jnp.where(lane == i1, -jnp.inf, logits)
    v2 = jnp.max(rest, axis=1, keepdims=True)
    i2 = jnp.min(jnp.where(rest == v2, lane, ne), axis=1, keepdims=True)
    e2 = jnp.exp(v2 - v1)
    den = 1.0 + e2
    two = lax.broadcasted_iota(jnp.int32, (logits.shape[0], TOP_K), 1)
    idx = jnp.where(two == 0, i1, i2).astype(jnp.int32)
    gate = jnp.where(two == 0, 1.0 / den, e2 / den)
    oh1 = jnp.where(lane == i1, 1.0, 0.0)
    oh2 = jnp.where(lane == i2, 1.0, 0.0)
    both = oh1 + oh2
    prior = _bdot(before, both.astype(BF16)) + base
    r1 = jnp.sum(oh1 * prior, axis=1, keepdims=True)
    r2 = jnp.sum(oh2 * prior, axis=1, keepdims=True)
    rank = jnp.where(two == 0, r1, r2).astype(jnp.int32)
    return hp, idx, gate, rank, base + jnp.sum(both, axis=0, keepdims=True)


def _count_le(sorted_vals, queries):
    return jnp.sum((sorted_vals[None, :] <= queries[:, None]).astype(jnp.int32), axis=1)


def _routing_tables(idx, rank, cnt, ne, tile):
    n = idx.shape[0]
    padded = ((cnt + tile - 1) // tile) * tile
    off = jnp.cumsum(padded) - padded
    onehot = idx[:, :, None] == jnp.arange(ne, dtype=jnp.int32)[None, None, :]
    pos = rank + jnp.sum(jnp.where(onehot, off[None, None, :], 0), axis=2)
    n_tiles = (TOP_K * n) // tile + ne
    used = jnp.sum(padded) // tile
    tile_ids = jnp.arange(n_tiles, dtype=jnp.int32)
    tile_expert = _count_le(jnp.cumsum(padded), tile_ids * tile)
    last_used = jnp.maximum(used - 1, 0)
    tile_valid = (tile_ids < used).astype(jnp.int32)
    tile_src = jnp.minimum(tile_ids, last_used).astype(jnp.int32)
    tile_expert = jnp.minimum(tile_expert, ne - 1).astype(jnp.int32)
    tile_expert = jnp.where(tile_valid == 1, tile_expert, tile_expert[last_used])
    tile_rows = jnp.clip(cnt[tile_expert] - (tile_ids * tile - off[tile_expert]), 0, tile)
    tile_rows = jnp.where(tile_valid == 1, tile_rows, 0).astype(jnp.int32)
    return dict(pos=pos, tile_expert=tile_expert, tile_valid=tile_valid, tile_src=tile_src,
                tile_rows=tile_rows, n_tiles=n_tiles)


def _sc_mesh():
    return plsc.VectorSubcoreMesh(core_axis_name="c", subcore_axis_name="s",
                                  num_cores=SC_CORES, num_subcores=SC_SUBCORES)


def _dispatch(hp, pos0, pos1, rows_out):
    n, w = hp.shape
    per_w = n // SC_WORKERS
    ch = SC_ROWS
    nch = per_w // ch
    assert per_w * SC_WORKERS == n and nch * ch == per_w and nch % 2 == 0

    @functools.partial(
        pl.kernel, mesh=_sc_mesh(),
        out_type=jax.ShapeDtypeStruct((rows_out, w), hp.dtype),
        scratch_types=[pltpu.VMEM((per_w,), jnp.int32), pltpu.VMEM((per_w,), jnp.int32),
                       pltpu.VMEM((ch, w), hp.dtype), pltpu.VMEM((ch, w), hp.dtype)]
        + [pltpu.SemaphoreType.DMA] * 4,
        name="dispatch",
    )
    def k(hp_hbm, p0_hbm, p1_hbm, out_hbm, i0_v, i1_v, r0_v, r1_v, l0, l1, s0, s1):
        base = (lax.axis_index("s") * SC_CORES + lax.axis_index("c")) * per_w
        pltpu.sync_copy(p0_hbm.at[pl.ds(base, per_w)], i0_v)
        pltpu.sync_copy(p1_hbm.at[pl.ds(base, per_w)], i1_v)
        bufs, lsem, ssem = (r0_v, r1_v), (l0, l1), (s0, s1)

        def load(j, b):
            return pltpu.make_async_copy(hp_hbm.at[pl.ds(base + j * ch, ch)], bufs[b], lsem[b])

        def scatter(j, b, idx):
            return pltpu.make_async_copy(bufs[b], out_hbm.at[idx.at[pl.ds(j * ch, ch)]], ssem[b])

        load(0, 0).start()

        @pl.loop(0, nch, step=2)
        def _(j):
            for b in range(2):
                jj = j + b
                load(jj, b).wait()

                @pl.when(jj + 1 < nch)
                def _():
                    load(jj + 1, 1 - b).start()

                c0 = scatter(jj, b, i0_v)
                c1 = scatter(jj, b, i1_v)
                c0.start()
                c1.start()
                c0.wait()
                c1.wait()

    return k(hp, pos0, pos1)


def _gather_pairs(ys, pos0, pos1):
    n = pos0.shape[0]
    w = ys.shape[1]
    per_w = n // SC_WORKERS
    ch = SC_ROWS
    nch = per_w // ch
    assert per_w * SC_WORKERS == n and nch * ch == per_w
    out = jax.ShapeDtypeStruct((n, w), ys.dtype)

    @functools.partial(
        pl.kernel, mesh=_sc_mesh(), out_type=(out, out),
        scratch_types=[pltpu.VMEM((per_w,), jnp.int32), pltpu.VMEM((per_w,), jnp.int32),
                       pltpu.VMEM((ch, w), ys.dtype), pltpu.VMEM((ch, w), ys.dtype)]
        + [pltpu.SemaphoreType.DMA] * 4,
        name="gather_pairs",
    )
    def k(ys_hbm, p0_hbm, p1_hbm, a_hbm, b_hbm, i0_v, i1_v, ra_v, rb_v, ga, gb, wa, wb):
        base = (lax.axis_index("s") * SC_CORES + lax.axis_index("c")) * per_w
        pltpu.sync_copy(p0_hbm.at[pl.ds(base, per_w)], i0_v)
        pltpu.sync_copy(p1_hbm.at[pl.ds(base, per_w)], i1_v)

        def gather(j, idx, buf, sem):
            return pltpu.make_async_copy(ys_hbm.at[idx.at[pl.ds(j * ch, ch)]], buf, sem)

        def write(j, buf, dst, sem):
            return pltpu.make_async_copy(buf, dst.at[pl.ds(base + j * ch, ch)], sem)

        gather(0, i0_v, ra_v, ga).start()
        gather(0, i1_v, rb_v, gb).start()

        @pl.loop(0, nch)
        def _(j):
            gather(j, i0_v, ra_v, ga).wait()
            write(j, ra_v, a_hbm, wa).start()
            gather(j, i1_v, rb_v, gb).wait()
            write(j, rb_v, b_hbm, wb).start()
            write(j, ra_v, a_hbm, wa).wait()

            @pl.when(j + 1 < nch)
            def _():
                gather(j + 1, i0_v, ra_v, ga).start()

            write(j, rb_v, b_hbm, wb).wait()

            @pl.when(j + 1 < nch)
            def _():
                gather(j + 1, i1_v, rb_v, gb).start()

    return k(ys, pos0, pos1)


def _expert_kernel(te_ref, tv_ref, tsrc_ref, tr_ref, xs_ref, w1_ref, w3_ref, w2_ref,
                   ys_ref, *, sub):
    j = pl.program_id(0)

    @pl.when(tv_ref[j] == 1)
    def _():
        x = _unpack_bf16_pairs(xs_ref[...])
        row = lax.broadcasted_iota(jnp.int32, x.shape, 0)
        x = jnp.where(row < tr_ref[j], x, 0.0).astype(BF16)
        ys_ref[...] = _pack_bf16_pairs(_swiglu(x, w1_ref.at[0], w3_ref.at[0], w2_ref.at[0], sub))

    @pl.when(tv_ref[j] == 0)
    def _():
        ys_ref[...] = jnp.zeros_like(ys_ref)


def _experts(xs, w1, w3, w2, rt, tile):
    rows, wp = xs.shape
    ne, d, dff = w1.shape
    return pl.pallas_call(
        functools.partial(_expert_kernel, sub=512),
        grid_spec=pltpu.PrefetchScalarGridSpec(
            num_scalar_prefetch=4,
            grid=(rt['n_tiles'],),
            in_specs=[pl.BlockSpec((tile, wp), lambda j, te, tv, tsrc, tr: (tsrc[j], 0)),
                      pl.BlockSpec((1, d, dff), lambda j, te, tv, tsrc, tr: (te[j], 0, 0)),
                      pl.BlockSpec((1, d, dff), lambda j, te, tv, tsrc, tr: (te[j], 0, 0)),
                      pl.BlockSpec((1, dff, d), lambda j, te, tv, tsrc, tr: (te[j], 0, 0))],
            out_specs=pl.BlockSpec((tile, wp), lambda j, te, tv, tsrc, tr: (j, 0))),
        out_shape=jax.ShapeDtypeStruct((rows, wp), jnp.int32),
        compiler_params=_cparams("arbitrary"),
        name="experts",
    )(rt['tile_expert'], rt['tile_valid'], rt['tile_src'], rt['tile_rows'], xs, w1, w3, w2)


def _combine_kernel(a_ref, b_ref, gate_ref, x_ref, mod_ref, fg_ref, o_ref, *, final_norm):
    m = mod_ref[0]
    g = gate_ref[...]
    y = g[:, 0:1] * _unpack_bf16_pairs(a_ref[...]) + g[:, 1:2] * _unpack_bf16_pairs(b_ref[...])
    xn = x_ref[...] + m[5:6] * y
    if final_norm:
        ms = jnp.mean(xn * xn, axis=-1, keepdims=True)
        xn = (xn * lax.rsqrt(ms + EPS)) * fg_ref[...]
    o_ref[...] = xn


def _combine(a, b, gates, x, mod, fg, seq, tm, final_norm, tile0):
    n, d = x.shape
    tps = seq // tm
    part = lambda w: pl.BlockSpec((tm, w), lambda i: (i, 0))
    full = lambda w: pl.BlockSpec((tm, w), lambda i: (i + tile0, 0))
    return pl.pallas_call(
        functools.partial(_combine_kernel, final_norm=final_norm),
        grid=(a.shape[0] // tm,),
        in_specs=[part(d // 2), part(d // 2), full(TOP_K), full(d),
                  pl.BlockSpec((1, 6, d), lambda i: ((i + tile0) // tps, 0, 0)),
                  pl.BlockSpec((1, d), lambda i: (0, 0))],
        out_specs=full(d),
        out_shape=jax.ShapeDtypeStruct((n, d), F32),
        input_output_aliases={3: 0},
        compiler_params=_cparams("arbitrary"),
        name="combine",
    )(a, b, gates, x, mod, fg)


def _pick(n, prefs):
    for p in prefs:
        if n % p == 0:
            return p
    return n


def kernel(x, c, ada_w, ada_b, norm_g, ssm_in, ssm_log_dt, ssm_lam_re, ssm_lam_im,
           ssm_b_re, ssm_b_im, ssm_c_re, ssm_c_im, ssm_d, ssm_glu, ssm_out,
           pool_in, pool_mix, pool_scale, pool_out, ffn_w1, ffn_w3, ffn_w2,
           router_w, router_b, moe_w1, moe_w3, moe_w2, final_g):
    bsz, seq, d = x.shape
    depth = ada_w.shape[0]
    n = bsz * seq
    ne = router_w.shape[-1]
    g_, p_, h_ = ssm_b_re.shape[1:]
    t = SSM_T
    if seq % (LANES * t // bsz) != 0 or bsz != 8:
        raise NotImplementedError("state-space kernels assume batch 8 and seq % 256 == 0")
    if depth % 2 != 0:
        raise NotImplementedError("the final RMSNorm is fused into an expert layer's combine")
    tm = _pick(seq, (512, 256, 128, 64, 32, 16))
    tile = _pick(TOP_K * n, (512, 256, 128, 64, 32, 16))
    gb = _pick(g_, (4, 2, 1))

    stack = lambda w: w.astype(BF16).reshape((-1,) + w.shape[2:])
    ew1, ew3, ew2 = stack(moe_w1), stack(moe_w3), stack(moe_w2)
    fw1, fw3, fw2 = ffn_w1.astype(BF16), ffn_w3.astype(BF16), ffn_w2.astype(BF16)
    mod = _ada(c, ada_w, ada_b).reshape(depth, bsz, 6, d)
    xf = x.reshape(n, d)
    fg = final_g.reshape(1, d)
    for i in range(depth):
        j = i // 2
        mod_i = mod[i]
        g_a = norm_g[i, 0].reshape(1, d)
        g_b = norm_g[i, 1].reshape(1, d)
        if i % 2 == 0:
            ops = _ssm_operators(ssm_log_dt[j], ssm_lam_re[j], ssm_lam_im[j], ssm_b_re[j],
                                 ssm_b_im[j], ssm_c_re[j], ssm_c_im[j], ssm_d[j], t)
            ut = _ssm_in(xf.reshape(bsz, seq, d), mod_i, g_a, ssm_in[j].astype(BF16), t, h_, 4)
            yt = _ssm(ut, *ops, bsz, gb)
            xf = _glu_ffn(yt, xf.reshape(bsz, seq, d), mod_i, ssm_glu[j].astype(BF16),
                          ssm_out[j].astype(BF16), g_b, fw1, fw3, fw2, t, 4,
                          layer=j).reshape(n, d)
        else:
            xf, h, idx, gates, rank, cnt = _pool(
                xf, mod_i, g_a, pool_in[j].astype(BF16), pool_mix[j].astype(BF16),
                pool_scale[j].reshape(1, d), pool_out[j].astype(BF16),
                g_b, router_w[j], router_b[j].reshape(1, ne), seq, 2 * tm, tm)
            rt = _routing_tables(idx, rank, cnt[0], ne, tile)
            pos0, pos1 = rt['pos'][:, 0], rt['pos'][:, 1]
            xs = _dispatch(h, pos0, pos1, rt['n_tiles'] * tile)
            rt['tile_expert'] = rt['tile_expert'] + j * ne
            ys = _experts(xs, ew1, ew3, ew2, rt, tile)
            npc = n // COMBINE_PIECES
            for q in range(COMBINE_PIECES):
                piece = slice(q * npc, (q + 1) * npc)
                ya, yb = _gather_pairs(ys, pos0[piece], pos1[piece])
                xf = _combine(ya, yb, gates, xf, mod_i, fg, seq, 2 * tm,
                              final_norm=(i == depth - 1), tile0=q * (npc // (2 * tm)))
    return xf.reshape(bsz, seq, d)
```

```python
import functools
import math

import jax
import jax.numpy as jnp
from jax import lax
from jax.experimental import pallas as pl
from jax.experimental.pallas import tpu as pltpu
from jax.experimental.pallas import tpu_sc as plsc

F32 = jnp.float32
BF16 = jnp.bfloat16
EPS = 1e-6
POOL_WINDOWS = (2, 4, 8, 16)
POOL_HALO = 16
SSM_T = 16
LANES = 128
TOP_K = 2
SC_CORES, SC_SUBCORES = 2, 16
SC_WORKERS = SC_CORES * SC_SUBCORES
SC_ROWS = 64
COMBINE_PIECES = 2
VMEM_LIMIT = 58 * 1024 * 1024


def _cparams(*sem):
    return pltpu.CompilerParams(dimension_semantics=sem, vmem_limit_bytes=VMEM_LIMIT)


def _normmod(x, g, shift, scale):
    ms = jnp.mean(x * x, axis=-1, keepdims=True)
    y = x * lax.rsqrt(ms + EPS)
    return (y * g) * (1.0 + scale) + shift


def _bdot(a, b):
    return jnp.dot(a, b, preferred_element_type=F32)


def _ada_kernel(c_ref, w_ref, b_ref, o_ref):
    c = c_ref[...]
    cond = c * jax.nn.sigmoid(c)
    o_ref[0] = jnp.dot(cond, w_ref[0], precision=lax.Precision.HIGHEST,
                       preferred_element_type=F32) + b_ref[0]


def _ada(c, ada_w, ada_b):
    depth, d, d6 = ada_w.shape
    bsz = c.shape[0]
    tn = d6 // 4
    return pl.pallas_call(
        _ada_kernel,
        grid=(depth, d6 // tn),
        in_specs=[pl.BlockSpec((bsz, d), lambda l, j: (0, 0)),
                  pl.BlockSpec((1, d, tn), lambda l, j: (l, 0, j)),
                  pl.BlockSpec((1, 1, tn), lambda l, j: (l, 0, j))],
        out_specs=pl.BlockSpec((1, bsz, tn), lambda l, j: (l, 0, j)),
        out_shape=jax.ShapeDtypeStruct((depth, bsz, d6), F32),
        compiler_params=_cparams("arbitrary", "arbitrary"),
        name="ada",
    )(c, ada_w, ada_b.reshape(depth, 1, d6))


def _ssm_operators(log_dt, lam_re, lam_im, b_re, b_im, c_re, c_im, d_skip, t):
    g_, p_, h_ = b_re.shape
    dt = jnp.exp(log_dt.astype(F32))[:, None]
    lr = lam_re.astype(F32)
    li = lam_im.astype(F32)
    mag = jnp.exp(lr * dt)
    a_re = mag * jnp.cos(li * dt)
    a_im = mag * jnp.sin(li * dt)
    den = lr * lr + li * li
    nr = a_re - 1
    coef_re = (nr * lr + a_im * li) / den
    coef_im = (a_im * lr - nr * li) / den
    br = b_re.astype(F32)
    bi = b_im.astype(F32)
    bbar_re = coef_re[..., None] * br - coef_im[..., None] * bi
    bbar_im = coef_re[..., None] * bi + coef_im[..., None] * br
    steps = jnp.arange(0, t + 1, dtype=F32)[None, :, None]
    ang = (li * dt)[:, None, :] * steps
    pmag = jnp.exp((lr * dt)[:, None, :] * steps)
    pw_re = pmag * jnp.cos(ang)
    pw_im = pmag * jnp.sin(ang)
    cr = c_re.astype(F32)
    ci = c_im.astype(F32)
    er = cr[:, None] * pw_re[:, :, None, :] - ci[:, None] * pw_im[:, :, None, :]
    ei = cr[:, None] * pw_im[:, :, None, :] + ci[:, None] * pw_re[:, :, None, :]
    tg = lambda a: jnp.transpose(a, (1, 2, 0))
    pt_re, pt_im = tg(pw_re[:, :t])[:, None], tg(pw_im[:, :t])[:, None]
    ct_re, ct_im = tg(cr)[None], tg(ci)[None]
    er_l = (ct_re * pt_re - ct_im * pt_im)[..., None]
    ei_l = (ct_re * pt_im + ct_im * pt_re)[..., None]
    br_l = jnp.transpose(bbar_re, (1, 0, 2))[None, None]
    bi_l = jnp.transpose(bbar_im, (1, 0, 2))[None, None]
    kk = jnp.sum(er_l * br_l - ei_l * bi_l, axis=2)
    kk = jnp.transpose(kk, (2, 0, 1, 3))
    kz = jnp.pad(kk, ((0, 0), (t - 1, 0), (0, 0), (0, 0)))
    cols = [kz[:, t - 1 - j:2 * t - 1 - j] for j in range(t)]
    mt = jnp.stack(cols, axis=3).reshape(g_, t * h_, t * h_)
    rev_re = pw_re[:, :t][:, ::-1, None, :]
    rev_im = pw_im[:, :t][:, ::-1, None, :]
    bt_re = jnp.transpose(bbar_re, (0, 2, 1))[:, None]
    bt_im = jnp.transpose(bbar_im, (0, 2, 1))[:, None]
    w_re = (rev_re * bt_re - rev_im * bt_im).reshape(g_, t * h_, p_)
    w_im = (rev_re * bt_im + rev_im * bt_re).reshape(g_, t * h_, p_)
    w4 = jnp.concatenate([w_re, w_im, w_im, w_re], axis=-1)
    vt = jnp.concatenate([er[:, 1:].reshape(g_, t * h_, p_),
                          -ei[:, 1:].reshape(g_, t * h_, p_)], axis=-1)
    at_re, at_im = pw_re[:, t], pw_im[:, t]
    c1 = jnp.concatenate([at_re, at_re], axis=-1)
    c2 = jnp.concatenate([-at_im, at_im], axis=-1)
    coef = jnp.concatenate([c1[:, None], c2[:, None], jnp.zeros((g_, 6, 2 * p_), F32)], axis=1)
    dk = jnp.tile(d_skip.astype(F32)[:, None, :], (1, t, 1)).reshape(g_, t * h_, 1)
    return mt.astype(BF16), w4.astype(BF16), vt.astype(BF16), coef, dk


def _ssm_in_kernel(x_ref, mod_ref, g_ref, w_ref, ut_ref, h_scr, u_scr, *, t):
    q = pl.program_id(1)
    bsz, cs, _ = x_ref.shape
    nlt = u_scr.shape[0]
    for b in range(bsz):
        m = mod_ref[b]
        h = _normmod(x_ref[b], g_ref[...], m[0:1], m[1:2]).astype(BF16)
        for c in range(cs // t):
            h_scr[(c * bsz + b) * t:(c * bsz + b + 1) * t, :] = h[c * t:(c + 1) * t, :]
    u = _bdot(h_scr[...], w_ref[...])
    rows = pl.ds(pl.multiple_of(q * (bsz * cs), bsz * cs), bsz * cs)
    for l in range(nlt):
        u_scr[l, rows, :] = u[:, l * LANES:(l + 1) * LANES]

    @pl.when(q == pl.num_programs(1) - 1)
    def _():
        ng, th, nch = ut_ref.shape
        hh = th // t
        gl = LANES // hh
        for l in range(nlt):
            for k in range(t):
                uk = u_scr[l, pl.ds(k, nch, stride=t), :]
                ut_ref[l * gl:(l + 1) * gl, k * hh:(k + 1) * hh, :] = (
                    uk.T.reshape(gl, hh, nch).astype(BF16))


def _ssm_in(x, mod, g, w, t, hh, nq):
    bsz, seq, d = x.shape
    ng = d // hh
    cs = LANES * t // bsz // nq
    return pl.pallas_call(
        functools.partial(_ssm_in_kernel, t=t),
        grid=(seq // (cs * nq), nq),
        in_specs=[pl.BlockSpec((bsz, cs, d), lambda o, q: (0, o * nq + q, 0)),
                  pl.BlockSpec((bsz, 6, d), lambda o, q: (0, 0, 0)),
                  pl.BlockSpec((1, d), lambda o, q: (0, 0)),
                  pl.BlockSpec((d, d), lambda o, q: (0, 0))],
        out_specs=pl.BlockSpec((ng, t * hh, LANES), lambda o, q: (0, 0, o)),
        out_shape=jax.ShapeDtypeStruct((ng, t * hh, bsz * seq // t), BF16),
        scratch_shapes=[pltpu.VMEM((bsz * cs, d), BF16),
                        pltpu.VMEM((d // LANES, LANES * t, LANES), F32)],
        compiler_params=_cparams("arbitrary", "arbitrary"),
        name="ssm_in",
    )(x, mod, g, w)


def _ssm_kernel(ut_ref, mt_ref, w4_ref, vt_ref, coef_ref, dk_ref, yt_ref, z_scr, s_scr,
                *, gb, nchunks, bsz):
    p2 = s_scr.shape[-1]
    for gi in range(gb):
        u = ut_ref[gi].astype(F32).T.astype(BF16)
        z = _bdot(u, w4_ref[gi])
        z_scr[gi, 0] = z[:, :p2]
        z_scr[gi, 1] = z[:, p2:]
    c1 = [jnp.broadcast_to(coef_ref[gi, 0:1, :], (bsz, p2)) for gi in range(gb)]
    c2 = [jnp.broadcast_to(coef_ref[gi, 1:2, :], (bsz, p2)) for gi in range(gb)]

    def body(c, carry):
        new = []
        for gi in range(gb):
            s1, s2 = carry[gi]
            rows = pl.ds(pl.multiple_of(c * bsz, bsz), bsz)
            s_scr[gi, rows, :] = s1
            n1 = c1[gi] * s1 + c2[gi] * s2 + z_scr[gi, 0, rows, :]
            n2 = c1[gi] * s2 - c2[gi] * s1 + z_scr[gi, 1, rows, :]
            new.append((n1, n2))
        return tuple(new)

    zero = jnp.zeros((bsz, p2), F32)
    lax.fori_loop(0, nchunks, body, tuple((zero, zero) for _ in range(gb)))
    for gi in range(gb):
        ut = ut_ref[gi]
        st = _bdot(vt_ref[gi], s_scr[gi].T.astype(BF16))
        yt = _bdot(mt_ref[gi], ut) + st + dk_ref[gi] * ut.astype(F32)
        yt_ref[gi] = yt.astype(yt_ref.dtype)


def _ssm(ut, mt, w4, vt, coef, dk, bsz, gb):
    g_, th, nch = ut.shape
    p2 = vt.shape[2]
    kern = functools.partial(_ssm_kernel, gb=gb, nchunks=nch // bsz, bsz=bsz)
    spec3 = lambda a, b: pl.BlockSpec((gb, a, b), lambda i: (i, 0, 0))
    return pl.pallas_call(
        kern,
        grid=(g_ // gb,),
        in_specs=[spec3(th, nch), spec3(th, th), spec3(th, 2 * p2), spec3(th, p2),
                  spec3(8, p2), spec3(th, 1)],
        out_specs=spec3(th, nch),
        out_shape=jax.ShapeDtypeStruct((g_, th, nch), BF16),
        scratch_shapes=[pltpu.VMEM((gb, 2, nch, p2), F32), pltpu.VMEM((gb, nch, p2), F32)],
        compiler_params=_cparams("arbitrary"),
        name="ssm",
    )(ut, mt, w4, vt, coef, dk)


def _gelu_tanh(y):
    return y * (0.5 * (1.0 + jnp.tanh(math.sqrt(2.0 / math.pi) * (y + 0.044715 * (y * y * y)))))


def _glu_out_kernel(yt_ref, x_ref, mod_ref, wg_ref, wo_ref, o_ref, y_scr, *, t):
    q = pl.program_id(1)
    bsz, cs, _ = x_ref.shape
    nlt = y_scr.shape[0]

    @pl.when(q == 0)
    def _():
        ng, th, nch = yt_ref.shape
        hh = th // t
        gl = LANES // hh
        for l in range(nlt):
            for k in range(t):
                yk = yt_ref[l * gl:(l + 1) * gl, k * hh:(k + 1) * hh, :].astype(F32)
                y_scr[l, pl.ds(k, nch, stride=t), :] = yk.reshape(LANES, nch).T

    rows = pl.ds(pl.multiple_of(q * (bsz * cs), bsz * cs), bsz * cs)
    y = jnp.concatenate([y_scr[l, rows, :] for l in range(nlt)], axis=1)
    z = _gelu_tanh(y)
    z = z * jax.nn.sigmoid(_bdot(z.astype(BF16), wg_ref[...]))
    o = _bdot(z.astype(BF16), wo_ref[...])
    for b in range(bsz):
        gate = mod_ref[b][2:3]
        for c in range(cs // t):
            tok = slice(c * t, (c + 1) * t)
            o_ref[b, tok, :] = x_ref[b, tok, :] + gate * o[(c * bsz + b) * t:(c * bsz + b + 1) * t]


def _glu_out(yt, x, mod, wg, wo, t, nq):
    bsz, seq, d = x.shape
    ng, th, _ = yt.shape
    cs = LANES * t // bsz // nq
    return pl.pallas_call(
        functools.partial(_glu_out_kernel, t=t),
        grid=(seq // (cs * nq), nq),
        in_specs=[pl.BlockSpec((ng, th, LANES), lambda o, q: (0, 0, o)),
                  pl.BlockSpec((bsz, cs, d), lambda o, q: (0, o * nq + q, 0)),
                  pl.BlockSpec((bsz, 6, d), lambda o, q: (0, 0, 0)),
                  pl.BlockSpec((d, d), lambda o, q: (0, 0)),
                  pl.BlockSpec((d, d), lambda o, q: (0, 0))],
        out_specs=pl.BlockSpec((bsz, cs, d), lambda o, q: (0, o * nq + q, 0)),
        out_shape=jax.ShapeDtypeStruct((bsz, seq, d), F32),
        scratch_shapes=[pltpu.VMEM((d // LANES, LANES * t, LANES), F32)],
        compiler_params=_cparams("arbitrary", "arbitrary"),
        name="glu_out",
    )(yt, x, mod, wg, wo)


def _chunks(total, step):
    return [(s, min(step, total - s)) for s in range(0, total, step)]


def _swiglu(h, w1_ref, w3_ref, w2_ref, sub):
    total = None
    for s, n in _chunks(w1_ref.shape[-1], sub):
        a = _bdot(h, w1_ref[:, s:s + n])
        b = _bdot(h, w3_ref[:, s:s + n])
        act = (a * jax.nn.sigmoid(a) * b).astype(BF16)
        y = _bdot(act, w2_ref[s:s + n, :])
        total = y if total is None else total + y
    return total


def _ffn_kernel(x_ref, mod_ref, g_ref, w1_ref, w3_ref, w2_ref, o_ref, *, sub):
    m = mod_ref[0]
    x = x_ref[...]
    h = _normmod(x, g_ref[...], m[3:4], m[4:5]).astype(BF16)
    o_ref[...] = x + m[5:6] * _swiglu(h, w1_ref.at[0], w3_ref.at[0], w2_ref.at[0], sub)


def _ffn(x, mod, g, w1, w3, w2, seq, tm, layer):
    n, d = x.shape
    dff = w1.shape[2]
    tps = seq // tm
    const = lambda i: (0, 0)
    lay = lambda i: (layer, 0, 0)
    return pl.pallas_call(
        functools.partial(_ffn_kernel, sub=256),
        grid=(n // tm,),
        in_specs=[pl.BlockSpec((tm, d), lambda i: (i, 0)),
                  pl.BlockSpec((1, 6, d), lambda i: (i // tps, 0, 0)),
                  pl.BlockSpec((1, d), const),
                  pl.BlockSpec((1, d, dff), lay, pipeline_mode=pl.Buffered(1)),
                  pl.BlockSpec((1, d, dff), lay, pipeline_mode=pl.Buffered(1)),
                  pl.BlockSpec((1, dff, d), lay, pipeline_mode=pl.Buffered(1))],
        out_specs=pl.BlockSpec((tm, d), lambda i: (i, 0)),
        out_shape=jax.ShapeDtypeStruct((n, d), F32),
        compiler_params=_cparams("arbitrary"),
        name="ffn",
    )(x, mod, g, w1, w3, w2)


def _pool_kernel(x_ref, halo_ref, mod_ref, g_ref, win_ref, wmix_ref, scale_ref, wo_ref,
                 g2_ref, rw_ref, rb_ref, o_ref, h_ref, idx_ref, gate_ref, rank_ref, cnt_ref,
                 ext_ref, z_ref, base_scr, before_scr, *, tps):
    tm, d = x_ref.shape
    pg = d // len(POOL_WINDOWS)
    m = mod_ref[0]
    it = pl.program_id(0) % tps

    @pl.when(pl.program_id(0) == 0)
    def _():
        base_scr[...] = jnp.zeros_like(base_scr)
        shape = before_scr.shape
        before = lax.broadcasted_iota(jnp.int32, shape, 1) < lax.broadcasted_iota(jnp.int32, shape, 0)
        before_scr[...] = jnp.where(before, 1.0, 0.0).astype(BF16)

    def project(rows):
        h = _normmod(rows, g_ref[...], m[0:1], m[1:2])
        return _bdot(h.astype(BF16), win_ref[...])

    ext_ref[0:POOL_HALO, :] = jnp.where(it == 0, 0.0, project(halo_ref[...]))
    ext_ref[POOL_HALO:, :] = project(x_ref[...])
    tpos = (it * tm + 1 + lax.broadcasted_iota(jnp.int32, (tm, 1), 0)).astype(F32)
    for gi, w in enumerate(POOL_WINDOWS):
        cols = slice(gi * pg, (gi + 1) * pg)
        s = ext_ref[:, cols]
        span = 1
        while span < w:
            s = s + pltpu.roll(s, span, 0)
            span *= 2
        mean = s[POOL_HALO:, :] * (1.0 / jnp.minimum(tpos, float(w)))
        dlt = (mean - ext_ref[POOL_HALO:, cols]).astype(BF16)
        z_ref[:, cols] = (_bdot(dlt, wmix_ref[gi]) * scale_ref[:, cols]).astype(BF16)
    xn = x_ref[...] + m[2:3] * _bdot(z_ref[...], wo_ref[...])
    o_ref[...] = xn
    rt = before_scr.shape[0]
    for r in range(tm // rt):
        rows = slice(r * rt, (r + 1) * rt)
        hp, idx, gate, rank, total = _route_rows(
            xn[rows], g2_ref[...], m[3:4], m[4:5], rw_ref[...], rb_ref[...],
            before_scr[...], base_scr[...])
        h_ref[rows, :] = hp
        idx_ref[rows, :] = idx
        gate_ref[rows, :] = gate
        rank_ref[rows, :] = rank
        base_scr[...] = total
    cnt_ref[...] = base_scr[...].astype(jnp.int32)


def _pool(x, mod, g, win, wmix, scale, wo, g2, rw, rb, seq, tm, rt):
    n, d = x.shape
    tps = seq // tm
    hb = tm // POOL_HALO
    ng, pg, _ = wmix.shape
    ne = rw.shape[1]
    const = lambda i: (0, 0)
    row = lambda w: pl.BlockSpec((tm, w), lambda i: (i, 0))
    return pl.pallas_call(
        functools.partial(_pool_kernel, tps=tps),
        grid=(n // tm,),
        in_specs=[row(d),
                  pl.BlockSpec((POOL_HALO, d), lambda i: (jnp.maximum(i * hb - 1, 0), 0)),
                  pl.BlockSpec((1, 6, d), lambda i: (i // tps, 0, 0)),
                  pl.BlockSpec((1, d), const),
                  pl.BlockSpec((d, d), const),
                  pl.BlockSpec((ng, pg, pg), lambda i: (0, 0, 0)),
                  pl.BlockSpec((1, d), const),
                  pl.BlockSpec((d, d), const),
                  pl.BlockSpec((1, d), const),
                  pl.BlockSpec((d, ne), const),
                  pl.BlockSpec((1, ne), const)],
        out_specs=[row(d), row(d // 2), row(TOP_K), row(TOP_K), row(TOP_K),
                   pl.BlockSpec((1, ne), const)],
        out_shape=[jax.ShapeDtypeStruct((n, d), F32),
                   jax.ShapeDtypeStruct((n, d // 2), jnp.int32),
                   jax.ShapeDtypeStruct((n, TOP_K), jnp.int32),
                   jax.ShapeDtypeStruct((n, TOP_K), F32),
                   jax.ShapeDtypeStruct((n, TOP_K), jnp.int32),
                   jax.ShapeDtypeStruct((1, ne), jnp.int32)],
        scratch_shapes=[pltpu.VMEM((tm + POOL_HALO, d), F32), pltpu.VMEM((tm, d), BF16),
                        pltpu.VMEM((1, ne), F32), pltpu.VMEM((rt, rt), BF16)],
        compiler_params=_cparams("arbitrary"),
        name="pool_router",
    )(x, x, mod, g, win, wmix, scale, wo, g2, rw, rb)


def _pack_bf16_pairs(v):
    w = v.shape[1] // 2
    bits = lambda a: lax.bitcast_convert_type(a.astype(BF16).astype(F32), jnp.int32)
    return bits(v[:, :w]) | lax.shift_right_logical(bits(v[:, w:]), 16)


def _unpack_bf16_pairs(p):
    hi = lax.bitcast_convert_type(p & jnp.int32(-65536), F32)
    lo = lax.bitcast_convert_type(lax.shift_left(p, 16), F32)
    return jnp.concatenate([hi, lo], axis=1)


def _route_rows(x, g, shift, scale, rw, rb, before, base):
    h = _normmod(x, g, shift, scale)
    hp = _pack_bf16_pairs(h)
    split = lambda a: (a.astype(BF16), (a - a.astype(BF16).astype(F32)).astype(BF16))
    h_hi, h_lo = split(h)
    w_hi, w_lo = split(rw)
    logits = (_bdot(h_hi, w_hi) + _bdot(h_hi, w_lo) + _bdot(h_lo, w_hi)) + rb
    ne = float(logits.shape[1])
    lane = lax.broadcasted_iota(jnp.int32, logits.shape, 1).astype(F32)
    v1 = jnp.max(logits, axis=1, keepdims=True)
    i1 = jnp.min(jnp.where(logits == v1, lane, ne), axis=1, keepdims=True)
    rest = jnp.where(lane == i1, -jnp.inf, logits)
    v2 = jnp.max(rest, axis=1, keepdims=True)
    i2 = jnp.min(jnp.where(rest == v2, lane, ne), axis=1, keepdims=True)
    e2 = jnp.exp(v2 - v1)
    den = 1.0 + e2
    two = lax.broadcasted_iota(jnp.int32, (logits.shape[0], TOP_K), 1)
    idx = jnp.where(two == 0, i1, i2).astype(jnp.int32)
    gate = jnp.where(two == 0, 1.0 / den, e2 / den)
    oh1 = jnp.where(lane == i1, 1.0, 0.0)
    oh2 = jnp.where(lane == i2, 1.0, 0.0)
    both = oh1 + oh2
    prior = _bdot(before, both.astype(BF16)) + base
    r1 = jnp.sum(oh1 * prior, axis=1, keepdims=True)
    r2 = jnp.sum(oh2 * prior, axis=1, keepdims=True)
    rank = jnp.where(two == 0, r1, r2).astype(jnp.int32)
    return hp, idx, gate, rank, base + jnp.sum(both, axis=0, keepdims=True)


def _count_le(sorted_vals, queries):
    return jnp.sum((sorted_vals[None, :] <= queries[:, None]).astype(jnp.int32), axis=1)


def _routing_tables(idx, rank, cnt, ne, tile):
    n = idx.shape[0]
    padded = ((cnt + tile - 1) // tile) * tile
    off = jnp.cumsum(padded) - padded
    onehot = idx[:, :, None] == jnp.arange(ne, dtype=jnp.int32)[None, None, :]
    pos = rank + jnp.sum(jnp.where(onehot, off[None, None, :], 0), axis=2)
    n_tiles = (TOP_K * n) // tile + ne
    used = jnp.sum(padded) // tile
    tile_ids = jnp.arange(n_tiles, dtype=jnp.int32)
    tile_expert = _count_le(jnp.cumsum(padded), tile_ids * tile)
    last_used = jnp.maximum(used - 1, 0)
    tile_valid = (tile_ids < used).astype(jnp.int32)
    tile_src = jnp.minimum(tile_ids, last_used).astype(jnp.int32)
    tile_expert = jnp.minimum(tile_expert, ne - 1).astype(jnp.int32)
    tile_expert = jnp.where(tile_valid == 1, tile_expert, tile_expert[last_used])
    tile_rows = jnp.clip(cnt[tile_expert] - (tile_ids * tile - off[tile_expert]), 0, tile)
    tile_rows = jnp.where(tile_valid == 1, tile_rows, 0).astype(jnp.int32)
    return dict(pos=pos, tile_expert=tile_expert, tile_valid=tile_valid, tile_src=tile_src,
                tile_rows=tile_rows, n_tiles=n_tiles)


def _sc_mesh():
    return plsc.VectorSubcoreMesh(core_axis_name="c", subcore_axis_name="s",
                                  num_cores=SC_CORES, num_subcores=SC_SUBCORES)


def _dispatch(hp, pos0, pos1, rows_out):
    n, w = hp.shape
    per_w = n // SC_WORKERS
    ch = SC_ROWS
    nch = per_w // ch
    assert per_w * SC_WORKERS == n and nch * ch == per_w and nch % 2 == 0

    @functools.partial(
        pl.kernel, mesh=_sc_mesh(),
        out_type=jax.ShapeDtypeStruct((rows_out, w), hp.dtype),
        scratch_types=[pltpu.VMEM((per_w,), jnp.int32), pltpu.VMEM((per_w,), jnp.int32),
                       pltpu.VMEM((ch, w), hp.dtype), pltpu.VMEM((ch, w), hp.dtype)]
        + [pltpu.SemaphoreType.DMA] * 4,
        name="dispatch",
    )
    def k(hp_hbm, p0_hbm, p1_hbm, out_hbm, i0_v, i1_v, r0_v, r1_v, l0, l1, s0, s1):
        base = (lax.axis_index("s") * SC_CORES + lax.axis_index("c")) * per_w
        pltpu.sync_copy(p0_hbm.at[pl.ds(base, per_w)], i0_v)
        pltpu.sync_copy(p1_hbm.at[pl.ds(base, per_w)], i1_v)
        bufs, lsem, ssem = (r0_v, r1_v), (l0, l1), (s0, s1)

        def load(j, b):
            return pltpu.make_async_copy(hp_hbm.at[pl.ds(base + j * ch, ch)], bufs[b], lsem[b])

        def scatter(j, b, idx):
            return pltpu.make_async_copy(bufs[b], out_hbm.at[idx.at[pl.ds(j * ch, ch)]], ssem[b])

        load(0, 0).start()

        @pl.loop(0, nch, step=2)
        def _(j):
            for b in range(2):
                jj = j + b
                load(jj, b).wait()

                @pl.when(jj + 1 < nch)
                def _():
                    load(jj + 1, 1 - b).start()

                c0 = scatter(jj, b, i0_v)
                c1 = scatter(jj, b, i1_v)
                c0.start()
                c1.start()
                c0.wait()
                c1.wait()

    return k(hp, pos0, pos1)


def _gather_pairs(ys, pos0, pos1):
    n = pos0.shape[0]
    w = ys.shape[1]
    per_w = n // SC_WORKERS
    ch = SC_ROWS
    nch = per_w // ch
    assert per_w * SC_WORKERS == n and nch * ch == per_w
    out = jax.ShapeDtypeStruct((n, w), ys.dtype)

    @functools.partial(
        pl.kernel, mesh=_sc_mesh(), out_type=(out, out),
        scratch_types=[pltpu.VMEM((per_w,), jnp.int32), pltpu.VMEM((per_w,), jnp.int32),
                       pltpu.VMEM((ch, w), ys.dtype), pltpu.VMEM((ch, w), ys.dtype)]
        + [pltpu.SemaphoreType.DMA] * 4,
        name="gather_pairs",
    )
    def k(ys_hbm, p0_hbm, p1_hbm, a_hbm, b_hbm, i0_v, i1_v, ra_v, rb_v, ga, gb, wa, wb):
        base = (lax.axis_index("s") * SC_CORES + lax.axis_index("c")) * per_w
        pltpu.sync_copy(p0_hbm.at[pl.ds(base, per_w)], i0_v)
        pltpu.sync_copy(p1_hbm.at[pl.ds(base, per_w)], i1_v)

        def gather(j, idx, buf, sem):
            return pltpu.make_async_copy(ys_hbm.at[idx.at[pl.ds(j * ch, ch)]], buf, sem)

        def write(j, buf, dst, sem):
            return pltpu.make_async_copy(buf, dst.at[pl.ds(base + j * ch, ch)], sem)

        gather(0, i0_v, ra_v, ga).start()
        gather(0, i1_v, rb_v, gb).start()

        @pl.loop(0, nch)
        def _(j):
            gather(j, i0_v, ra_v, ga).wait()
            write(j, ra_v, a_hbm, wa).start()
            gather(j, i1_v, rb_v, gb).wait()
            write(j, rb_v, b_hbm, wb).start()
            write(j, ra_v, a_hbm, wa).wait()

            @pl.when(j + 1 < nch)
            def _():
                gather(j + 1, i0_v, ra_v, ga).start()

            write(j, rb_v, b_hbm, wb).wait()

            @pl.when(j + 1 < nch)
            def _():
                gather(j + 1, i1_v, rb_v, gb).start()

    return k(ys, pos0, pos1)


def _expert_kernel(te_ref, tv_ref, tsrc_ref, tr_ref, xs_ref, w1_ref, w3_ref, w2_ref,
                   ys_ref, *, sub):
    j = pl.program_id(0)
    half = xs_ref.shape[0] // 2

    def run(rows):
        x = _unpack_bf16_pairs(xs_ref[rows, :])
        row = lax.broadcasted_iota(jnp.int32, x.shape, 0)
        x = jnp.where(row < tr_ref[j], x, 0.0).astype(BF16)
        ys_ref[rows, :] = _pack_bf16_pairs(
            _swiglu(x, w1_ref.at[0], w3_ref.at[0], w2_ref.at[0], sub))

    @pl.when(tr_ref[j] > half)
    def _():
        run(slice(None))

    @pl.when((tr_ref[j] > 0) & (tr_ref[j] <= half))
    def _():
        run(slice(0, half))
        ys_ref[half:, :] = jnp.zeros((half, ys_ref.shape[1]), ys_ref.dtype)

    @pl.when(tr_ref[j] == 0)
    def _():
        ys_ref[...] = jnp.zeros_like(ys_ref)


def _experts(xs, w1, w3, w2, rt, tile):
    rows, wp = xs.shape
    ne, d, dff = w1.shape
    return pl.pallas_call(
        functools.partial(_expert_kernel, sub=512),
        grid_spec=pltpu.PrefetchScalarGridSpec(
            num_scalar_prefetch=4,
            grid=(rt['n_tiles'],),
            in_specs=[pl.BlockSpec((tile, wp), lambda j, te, tv, tsrc, tr: (tsrc[j], 0)),
                      pl.BlockSpec((1, d, dff), lambda j, te, tv, tsrc, tr: (te[j], 0, 0)),
                      pl.BlockSpec((1, d, dff), lambda j, te, tv, tsrc, tr: (te[j], 0, 0)),
                      pl.BlockSpec((1, dff, d), lambda j, te, tv, tsrc, tr: (te[j], 0, 0))],
            out_specs=pl.BlockSpec((tile, wp), lambda j, te, tv, tsrc, tr: (j, 0))),
        out_shape=jax.ShapeDtypeStruct((rows, wp), jnp.int32),
        compiler_params=_cparams("arbitrary"),
        name="experts",
    )(rt['tile_expert'], rt['tile_valid'], rt['tile_src'], rt['tile_rows'], xs, w1, w3, w2)


def _combine_kernel(a_ref, b_ref, gate_ref, x_ref, mod_ref, fg_ref, o_ref, *, final_norm):
    m = mod_ref[0]
    g = gate_ref[...]
    y = g[:, 0:1] * _unpack_bf16_pairs(a_ref[...]) + g[:, 1:2] * _unpack_bf16_pairs(b_ref[...])
    xn = x_ref[...] + m[5:6] * y
    if final_norm:
        ms = jnp.mean(xn * xn, axis=-1, keepdims=True)
        xn = (xn * lax.rsqrt(ms + EPS)) * fg_ref[...]
    o_ref[...] = xn


def _combine(a, b, gates, x, mod, fg, seq, tm, final_norm, tile0):
    n, d = x.shape
    tps = seq // tm
    part = lambda w: pl.BlockSpec((tm, w), lambda i: (i, 0))
    full = lambda w: pl.BlockSpec((tm, w), lambda i: (i + tile0, 0))
    return pl.pallas_call(
        functools.partial(_combine_kernel, final_norm=final_norm),
        grid=(a.shape[0] // tm,),
        in_specs=[part(d // 2), part(d // 2), full(TOP_K), full(d),
                  pl.BlockSpec((1, 6, d), lambda i: ((i + tile0) // tps, 0, 0)),
                  pl.BlockSpec((1, d), lambda i: (0, 0))],
        out_specs=full(d),
        out_shape=jax.ShapeDtypeStruct((n, d), F32),
        input_output_aliases={3: 0},
        compiler_params=_cparams("arbitrary"),
        name="combine",
    )(a, b, gates, x, mod, fg)


def _pick(n, prefs):
    for p in prefs:
        if n % p == 0:
            return p
    return n


def kernel(x, c, ada_w, ada_b, norm_g, ssm_in, ssm_log_dt, ssm_lam_re, ssm_lam_im,
           ssm_b_re, ssm_b_im, ssm_c_re, ssm_c_im, ssm_d, ssm_glu, ssm_out,
           pool_in, pool_mix, pool_scale, pool_out, ffn_w1, ffn_w3, ffn_w2,
           router_w, router_b, moe_w1, moe_w3, moe_w2, final_g):
    bsz, seq, d = x.shape
    depth = ada_w.shape[0]
    n = bsz * seq
    ne = router_w.shape[-1]
    g_, p_, h_ = ssm_b_re.shape[1:]
    t = SSM_T
    if seq % (LANES * t // bsz) != 0 or bsz != 8:
        raise NotImplementedError("state-space kernels assume batch 8 and seq % 256 == 0")
    if depth % 2 != 0:
        raise NotImplementedError("the final RMSNorm is fused into an expert layer's combine")
    tm = _pick(seq, (512, 256, 128, 64, 32, 16))
    tile = _pick(TOP_K * n, (512, 256, 128, 64, 32, 16))
    gb = _pick(g_, (4, 2, 1))

    stack = lambda w: w.astype(BF16).reshape((-1,) + w.shape[2:])
    ew1, ew3, ew2 = stack(moe_w1), stack(moe_w3), stack(moe_w2)
    fw1, fw3, fw2 = ffn_w1.astype(BF16), ffn_w3.astype(BF16), ffn_w2.astype(BF16)
    mod = _ada(c, ada_w, ada_b).reshape(depth, bsz, 6, d)
    xf = x.reshape(n, d)
    fg = final_g.reshape(1, d)
    for i in range(depth):
        j = i // 2
        mod_i = mod[i]
        g_a = norm_g[i, 0].reshape(1, d)
        g_b = norm_g[i, 1].reshape(1, d)
        if i % 2 == 0:
            ops = _ssm_operators(ssm_log_dt[j], ssm_lam_re[j], ssm_lam_im[j], ssm_b_re[j],
                                 ssm_b_im[j], ssm_c_re[j], ssm_c_im[j], ssm_d[j], t)
            ut = _ssm_in(xf.reshape(bsz, seq, d), mod_i, g_a, ssm_in[j].astype(BF16), t, h_, 4)
            yt = _ssm(ut, *ops, bsz, gb)
            xf = _glu_out(yt, xf.reshape(bsz, seq, d), mod_i, ssm_glu[j].astype(BF16),
                          ssm_out[j].astype(BF16), t, 4).reshape(n, d)
            xf = _ffn(xf, mod_i, g_b, fw1, fw3, fw2, seq, tm, layer=j)
        else:
            xf, h, idx, gates, rank, cnt = _pool(
                xf, mod_i, g_a, pool_in[j].astype(BF16), pool_mix[j].astype(BF16),
                pool_scale[j].reshape(1, d), pool_out[j].astype(BF16),
                g_b, router_w[j], router_b[j].reshape(1, ne), seq, 2 * tm, tm)
            rt = _routing_tables(idx, rank, cnt[0], ne, tile)
            pos0, pos1 = rt['pos'][:, 0], rt['pos'][:, 1]
            xs = _dispatch(h, pos0, pos1, rt['n_tiles'] * tile)
            rt['tile_expert'] = rt['tile_expert'] + j * ne
            ys = _experts(xs, ew1, ew3, ew2, rt, tile)
            npc = n // COMBINE_PIECES
            for q in range(COMBINE_PIECES):
                piece = slice(q * npc, (q + 1) * npc)
                ya, yb = _gather_pairs(ys, pos0[piece], pos1[piece])
                xf = _combine(ya, yb, gates, xf, mod_i, fg, seq, 2 * tm,
                              final_norm=(i == depth - 1), tile0=q * (npc // (2 * tm)))
    return xf.reshape(bsz, seq, d)
```

```python
import functools
import math

import jax
import jax.numpy as jnp
from jax import lax
from jax.experimental import pallas as pl
from jax.experimental.pallas import tpu as pltpu
from jax.experimental.pallas import tpu_sc as plsc

F32 = jnp.float32
BF16 = jnp.bfloat16
EPS = 1e-6
POOL_WINDOWS = (2, 4, 8, 16)
POOL_HALO = 16
SSM_T = 16
LANES = 128
TOP_K = 2
SC_CORES, SC_SUBCORES = 2, 16
SC_WORKERS = SC_CORES * SC_SUBCORES
SC_ROWS = 64
COMBINE_PIECES = 2
VMEM_LIMIT = 58 * 1024 * 1024


def _cparams(*sem):
    return pltpu.CompilerParams(dimension_semantics=sem, vmem_limit_bytes=VMEM_LIMIT)


def _normmod(x, g, shift, scale):
    ms = jnp.mean(x * x, axis=-1, keepdims=True)
    y = x * lax.rsqrt(ms + EPS)
    return (y * g) * (1.0 + scale) + shift


def _bdot(a, b):
    return jnp.dot(a, b, preferred_element_type=F32)


def _ada_kernel(c_ref, w_ref, b_ref, o_ref):
    c = c_ref[...]
    cond = c * jax.nn.sigmoid(c)
    o_ref[0] = jnp.dot(cond, w_ref[0], precision=lax.Precision.HIGHEST,
                       preferred_element_type=F32) + b_ref[0]


def _ada(c, ada_w, ada_b):
    depth, d, d6 = ada_w.shape
    bsz = c.shape[0]
    tn = d6 // 4
    return pl.pallas_call(
        _ada_kernel,
        grid=(depth, d6 // tn),
        in_specs=[pl.BlockSpec((bsz, d), lambda l, j: (0, 0)),
                  pl.BlockSpec((1, d, tn), lambda l, j: (l, 0, j)),
                  pl.BlockSpec((1, 1, tn), lambda l, j: (l, 0, j))],
        out_specs=pl.BlockSpec((1, bsz, tn), lambda l, j: (l, 0, j)),
        out_shape=jax.ShapeDtypeStruct((depth, bsz, d6), F32),
        compiler_params=_cparams("arbitrary", "arbitrary"),
        name="ada",
    )(c, ada_w, ada_b.reshape(depth, 1, d6))


def _ssm_operators(log_dt, lam_re, lam_im, b_re, b_im, c_re, c_im, d_skip, t):
    g_, p_, h_ = b_re.shape
    dt = jnp.exp(log_dt.astype(F32))[:, None]
    lr = lam_re.astype(F32)
    li = lam_im.astype(F32)
    mag = jnp.exp(lr * dt)
    a_re = mag * jnp.cos(li * dt)
    a_im = mag * jnp.sin(li * dt)
    den = lr * lr + li * li
    nr = a_re - 1
    coef_re = (nr * lr + a_im * li) / den
    coef_im = (a_im * lr - nr * li) / den
    br = b_re.astype(F32)
    bi = b_im.astype(F32)
    bbar_re = coef_re[..., None] * br - coef_im[..., None] * bi
    bbar_im = coef_re[..., None] * bi + coef_im[..., None] * br
    steps = jnp.arange(0, t + 1, dtype=F32)[None, :, None]
    ang = (li * dt)[:, None, :] * steps
    pmag = jnp.exp((lr * dt)[:, None, :] * steps)
    pw_re = pmag * jnp.cos(ang)
    pw_im = pmag * jnp.sin(ang)
    cr = c_re.astype(F32)
    ci = c_im.astype(F32)
    er = cr[:, None] * pw_re[:, :, None, :] - ci[:, None] * pw_im[:, :, None, :]
    ei = cr[:, None] * pw_im[:, :, None, :] + ci[:, None] * pw_re[:, :, None, :]
    tg = lambda a: jnp.transpose(a, (1, 2, 0))
    pt_re, pt_im = tg(pw_re[:, :t])[:, None], tg(pw_im[:, :t])[:, None]
    ct_re, ct_im = tg(cr)[None], tg(ci)[None]
    er_l = (ct_re * pt_re - ct_im * pt_im)[..., None]
    ei_l = (ct_re * pt_im + ct_im * pt_re)[..., None]
    br_l = jnp.transpose(bbar_re, (1, 0, 2))[None, None]
    bi_l = jnp.transpose(bbar_im, (1, 0, 2))[None, None]
    kk = jnp.sum(er_l * br_l - ei_l * bi_l, axis=2)
    kk = jnp.transpose(kk, (2, 0, 1, 3))
    kz = jnp.pad(kk, ((0, 0), (t - 1, 0), (0, 0), (0, 0)))
    cols = [kz[:, t - 1 - j:2 * t - 1 - j] for j in range(t)]
    mt = jnp.stack(cols, axis=3).reshape(g_, t * h_, t * h_)
    rev_re = pw_re[:, :t][:, ::-1, None, :]
    rev_im = pw_im[:, :t][:, ::-1, None, :]
    bt_re = jnp.transpose(bbar_re, (0, 2, 1))[:, None]
    bt_im = jnp.transpose(bbar_im, (0, 2, 1))[:, None]
    w_re = (rev_re * bt_re - rev_im * bt_im).reshape(g_, t * h_, p_)
    w_im = (rev_re * bt_im + rev_im * bt_re).reshape(g_, t * h_, p_)
    w4 = jnp.concatenate([w_re, w_im, w_im, w_re], axis=-1)
    vt = jnp.concatenate([er[:, 1:].reshape(g_, t * h_, p_),
                          -ei[:, 1:].reshape(g_, t * h_, p_)], axis=-1)
    at_re, at_im = pw_re[:, t], pw_im[:, t]
    c1 = jnp.concatenate([at_re, at_re], axis=-1)
    c2 = jnp.concatenate([-at_im, at_im], axis=-1)
    coef = jnp.concatenate([c1[:, None], c2[:, None], jnp.zeros((g_, 6, 2 * p_), F32)], axis=1)
    dk = jnp.tile(d_skip.astype(F32)[:, None, :], (1, t, 1)).reshape(g_, t * h_, 1)
    return mt.astype(BF16), w4.astype(BF16), vt.astype(BF16), coef, dk


def _ssm_in_kernel(x_ref, mod_ref, g_ref, w_ref, ut_ref, h_scr, u_scr, *, t):
    q = pl.program_id(1)
    bsz, cs, _ = x_ref.shape
    nlt = u_scr.shape[0]
    for b in range(bsz):
        m = mod_ref[b]
        h = _normmod(x_ref[b], g_ref[...], m[0:1], m[1:2]).astype(BF16)
        for c in range(cs // t):
            h_scr[(c * bsz + b) * t:(c * bsz + b + 1) * t, :] = h[c * t:(c + 1) * t, :]
    u = _bdot(h_scr[...], w_ref[0])
    rows = pl.ds(pl.multiple_of(q * (bsz * cs), bsz * cs), bsz * cs)
    for l in range(nlt):
        u_scr[l, rows, :] = u[:, l * LANES:(l + 1) * LANES]

    @pl.when(q == pl.num_programs(1) - 1)
    def _():
        ng, th, nch = ut_ref.shape
        hh = th // t
        gl = LANES // hh
        for l in range(nlt):
            for k in range(t):
                uk = u_scr[l, pl.ds(k, nch, stride=t), :]
                ut_ref[l * gl:(l + 1) * gl, k * hh:(k + 1) * hh, :] = (
                    uk.T.reshape(gl, hh, nch).astype(BF16))


def _ssm_in(x, mod, g, w, t, hh, nq, layer):
    bsz, seq, d = x.shape
    ng = d // hh
    cs = LANES * t // bsz // nq
    return pl.pallas_call(
        functools.partial(_ssm_in_kernel, t=t),
        grid=(seq // (cs * nq), nq),
        in_specs=[pl.BlockSpec((bsz, cs, d), lambda o, q: (0, o * nq + q, 0)),
                  pl.BlockSpec((bsz, 6, d), lambda o, q: (0, 0, 0)),
                  pl.BlockSpec((1, d), lambda o, q: (0, 0)),
                  pl.BlockSpec((1, d, d), lambda o, q: (layer, 0, 0))],
        out_specs=pl.BlockSpec((ng, t * hh, LANES), lambda o, q: (0, 0, o)),
        out_shape=jax.ShapeDtypeStruct((ng, t * hh, bsz * seq // t), BF16),
        scratch_shapes=[pltpu.VMEM((bsz * cs, d), BF16),
                        pltpu.VMEM((d // LANES, LANES * t, LANES), F32)],
        compiler_params=_cparams("arbitrary", "arbitrary"),
        name="ssm_in",
    )(x, mod, g, w)


def _ssm_kernel(ut_ref, mt_ref, w4_ref, vt_ref, coef_ref, dk_ref, yt_ref, z_scr, s_scr,
                *, gb, nchunks, bsz):
    p2 = s_scr.shape[-1]
    for gi in range(gb):
        u = ut_ref[gi].astype(F32).T.astype(BF16)
        z = _bdot(u, w4_ref[gi])
        z_scr[gi, 0] = z[:, :p2]
        z_scr[gi, 1] = z[:, p2:]
    c1 = [jnp.broadcast_to(coef_ref[gi, 0:1, :], (bsz, p2)) for gi in range(gb)]
    c2 = [jnp.broadcast_to(coef_ref[gi, 1:2, :], (bsz, p2)) for gi in range(gb)]

    def body(c, carry):
        new = []
        for gi in range(gb):
            s1, s2 = carry[gi]
            rows = pl.ds(pl.multiple_of(c * bsz, bsz), bsz)
            s_scr[gi, rows, :] = s1
            n1 = c1[gi] * s1 + c2[gi] * s2 + z_scr[gi, 0, rows, :]
            n2 = c1[gi] * s2 - c2[gi] * s1 + z_scr[gi, 1, rows, :]
            new.append((n1, n2))
        return tuple(new)

    zero = jnp.zeros((bsz, p2), F32)
    lax.fori_loop(0, nchunks, body, tuple((zero, zero) for _ in range(gb)))
    for gi in range(gb):
        ut = ut_ref[gi]
        st = _bdot(vt_ref[gi], s_scr[gi].T.astype(BF16))
        yt = _bdot(mt_ref[gi], ut) + st + dk_ref[gi] * ut.astype(F32)
        yt_ref[gi] = yt.astype(yt_ref.dtype)


def _ssm(ut, mt, w4, vt, coef, dk, bsz, gb):
    g_, th, nch = ut.shape
    p2 = vt.shape[2]
    kern = functools.partial(_ssm_kernel, gb=gb, nchunks=nch // bsz, bsz=bsz)
    spec3 = lambda a, b: pl.BlockSpec((gb, a, b), lambda i: (i, 0, 0))
    return pl.pallas_call(
        kern,
        grid=(g_ // gb,),
        in_specs=[spec3(th, nch), spec3(th, th), spec3(th, 2 * p2), spec3(th, p2),
                  spec3(8, p2), spec3(th, 1)],
        out_specs=spec3(th, nch),
        out_shape=jax.ShapeDtypeStruct((g_, th, nch), BF16),
        scratch_shapes=[pltpu.VMEM((gb, 2, nch, p2), F32), pltpu.VMEM((gb, nch, p2), F32)],
        compiler_params=_cparams("arbitrary"),
        name="ssm",
    )(ut, mt, w4, vt, coef, dk)


def _gelu_tanh(y):
    return y * (0.5 * (1.0 + jnp.tanh(math.sqrt(2.0 / math.pi) * (y + 0.044715 * (y * y * y)))))


def _glu_out_kernel(yt_ref, x_ref, mod_ref, wg_ref, wo_ref, o_ref, y_scr, *, t):
    q = pl.program_id(1)
    bsz, cs, _ = x_ref.shape
    nlt = y_scr.shape[0]

    @pl.when(q == 0)
    def _():
        ng, th, nch = yt_ref.shape
        hh = th // t
        gl = LANES // hh
        for l in range(nlt):
            for k in range(t):
                yk = yt_ref[l * gl:(l + 1) * gl, k * hh:(k + 1) * hh, :].astype(F32)
                y_scr[l, pl.ds(k, nch, stride=t), :] = yk.reshape(LANES, nch).T

    rows = pl.ds(pl.multiple_of(q * (bsz * cs), bsz * cs), bsz * cs)
    y = jnp.concatenate([y_scr[l, rows, :] for l in range(nlt)], axis=1)
    z = _gelu_tanh(y)
    z = z * jax.nn.sigmoid(_bdot(z.astype(BF16), wg_ref[0]))
    o = _bdot(z.astype(BF16), wo_ref[0])
    for b in range(bsz):
        gate = mod_ref[b][2:3]
        for c in range(cs // t):
            tok = slice(c * t, (c + 1) * t)
            o_ref[b, tok, :] = x_ref[b, tok, :] + gate * o[(c * bsz + b) * t:(c * bsz + b + 1) * t]


def _glu_out(yt, x, mod, wg, wo, t, nq, layer):
    bsz, seq, d = x.shape
    ng, th, _ = yt.shape
    cs = LANES * t // bsz // nq
    return pl.pallas_call(
        functools.partial(_glu_out_kernel, t=t),
        grid=(seq // (cs * nq), nq),
        in_specs=[pl.BlockSpec((ng, th, LANES), lambda o, q: (0, 0, o)),
                  pl.BlockSpec((bsz, cs, d), lambda o, q: (0, o * nq + q, 0)),
                  pl.BlockSpec((bsz, 6, d), lambda o, q: (0, 0, 0)),
                  pl.BlockSpec((1, d, d), lambda o, q: (layer, 0, 0)),
                  pl.BlockSpec((1, d, d), lambda o, q: (layer, 0, 0))],
        out_specs=pl.BlockSpec((bsz, cs, d), lambda o, q: (0, o * nq + q, 0)),
        out_shape=jax.ShapeDtypeStruct((bsz, seq, d), F32),
        scratch_shapes=[pltpu.VMEM((d // LANES, LANES * t, LANES), F32)],
        compiler_params=_cparams("arbitrary", "arbitrary"),
        name="glu_out",
    )(yt, x, mod, wg, wo)


def _chunks(total, step):
    return [(s, min(step, total - s)) for s in range(0, total, step)]


def _swiglu(h, w1_ref, w3_ref, w2_ref, sub):
    total = None
    for s, n in _chunks(w1_ref.shape[-1], sub):
        a = _bdot(h, w1_ref[:, s:s + n])
        b = _bdot(h, w3_ref[:, s:s + n])
        act = (a * jax.nn.sigmoid(a) * b).astype(BF16)
        y = _bdot(act, w2_ref[s:s + n, :])
        total = y if total is None else total + y
    return total


def _ffn_kernel(x_ref, mod_ref, g_ref, w1_ref, w3_ref, w2_ref, o_ref, *, sub):
    m = mod_ref[0]
    x = x_ref[...]
    h = _normmod(x, g_ref[...], m[3:4], m[4:5]).astype(BF16)
    o_ref[...] = x + m[5:6] * _swiglu(h, w1_ref.at[0], w3_ref.at[0], w2_ref.at[0], sub)


def _ffn(x, mod, g, w1, w3, w2, seq, tm, layer):
    n, d = x.shape
    dff = w1.shape[2]
    tps = seq // tm
    const = lambda i: (0, 0)
    lay = lambda i: (layer, 0, 0)
    return pl.pallas_call(
        functools.partial(_ffn_kernel, sub=256),
        grid=(n // tm,),
        in_specs=[pl.BlockSpec((tm, d), lambda i: (i, 0)),
                  pl.BlockSpec((1, 6, d), lambda i: (i // tps, 0, 0)),
                  pl.BlockSpec((1, d), const),
                  pl.BlockSpec((1, d, dff), lay, pipeline_mode=pl.Buffered(1)),
                  pl.BlockSpec((1, d, dff), lay, pipeline_mode=pl.Buffered(1)),
                  pl.BlockSpec((1, dff, d), lay, pipeline_mode=pl.Buffered(1))],
        out_specs=pl.BlockSpec((tm, d), lambda i: (i, 0)),
        out_shape=jax.ShapeDtypeStruct((n, d), F32),
        compiler_params=_cparams("arbitrary"),
        name="ffn",
    )(x, mod, g, w1, w3, w2)


def _pool_kernel(x_ref, halo_ref, mod_ref, g_ref, win_ref, wmix_ref, scale_ref, wo_ref,
                 g2_ref, rw_ref, rb_ref, o_ref, h_ref, idx_ref, gate_ref, rank_ref, cnt_ref,
                 ext_ref, z_ref, base_scr, before_scr, *, tps):
    tm, d = x_ref.shape
    pg = d // len(POOL_WINDOWS)
    m = mod_ref[0]
    it = pl.program_id(0) % tps

    @pl.when(pl.program_id(0) == 0)
    def _():
        base_scr[...] = jnp.zeros_like(base_scr)
        shape = before_scr.shape
        before = lax.broadcasted_iota(jnp.int32, shape, 1) < lax.broadcasted_iota(jnp.int32, shape, 0)
        before_scr[...] = jnp.where(before, 1.0, 0.0).astype(BF16)

    def project(rows):
        h = _normmod(rows, g_ref[...], m[0:1], m[1:2])
        return _bdot(h.astype(BF16), win_ref[0])

    ext_ref[0:POOL_HALO, :] = jnp.where(it == 0, 0.0, project(halo_ref[...]))
    ext_ref[POOL_HALO:, :] = project(x_ref[...])
    tpos = (it * tm + 1 + lax.broadcasted_iota(jnp.int32, (tm, 1), 0)).astype(F32)
    for gi, w in enumerate(POOL_WINDOWS):
        cols = slice(gi * pg, (gi + 1) * pg)
        s = ext_ref[:, cols]
        span = 1
        while span < w:
            s = s + pltpu.roll(s, span, 0)
            span *= 2
        mean = s[POOL_HALO:, :] * (1.0 / jnp.minimum(tpos, float(w)))
        dlt = (mean - ext_ref[POOL_HALO:, cols]).astype(BF16)
        z_ref[:, cols] = (_bdot(dlt, wmix_ref[0, gi]) * scale_ref[:, cols]).astype(BF16)
    xn = x_ref[...] + m[2:3] * _bdot(z_ref[...], wo_ref[0])
    o_ref[...] = xn
    rt = before_scr.shape[0]
    for r in range(tm // rt):
        rows = slice(r * rt, (r + 1) * rt)
        hp, idx, gate, rank, total = _route_rows(
            xn[rows], g2_ref[...], m[3:4], m[4:5], rw_ref[...], rb_ref[...],
            before_scr[...], base_scr[...])
        h_ref[rows, :] = hp
        idx_ref[rows, :] = idx
        gate_ref[rows, :] = gate
        rank_ref[rows, :] = rank
        base_scr[...] = total
    cnt_ref[...] = base_scr[...].astype(jnp.int32)


def _pool(x, mod, g, win, wmix, scale, wo, g2, rw, rb, seq, tm, rt, layer):
    n, d = x.shape
    tps = seq // tm
    hb = tm // POOL_HALO
    _, ng, pg, _ = wmix.shape
    ne = rw.shape[1]
    const = lambda i: (0, 0)
    lay = lambda i: (layer, 0, 0)
    row = lambda w: pl.BlockSpec((tm, w), lambda i: (i, 0))
    return pl.pallas_call(
        functools.partial(_pool_kernel, tps=tps),
        grid=(n // tm,),
        in_specs=[row(d),
                  pl.BlockSpec((POOL_HALO, d), lambda i: (jnp.maximum(i * hb - 1, 0), 0)),
                  pl.BlockSpec((1, 6, d), lambda i: (i // tps, 0, 0)),
                  pl.BlockSpec((1, d), const),
                  pl.BlockSpec((1, d, d), lay),
                  pl.BlockSpec((1, ng, pg, pg), lambda i: (layer, 0, 0, 0)),
                  pl.BlockSpec((1, d), const),
                  pl.BlockSpec((1, d, d), lay),
                  pl.BlockSpec((1, d), const),
                  pl.BlockSpec((d, ne), const),
                  pl.BlockSpec((1, ne), const)],
        out_specs=[row(d), row(d // 2), row(TOP_K), row(TOP_K), row(TOP_K),
                   pl.BlockSpec((1, ne), const)],
        out_shape=[jax.ShapeDtypeStruct((n, d), F32),
                   jax.ShapeDtypeStruct((n, d // 2), jnp.int32),
                   jax.ShapeDtypeStruct((n, TOP_K), jnp.int32),
                   jax.ShapeDtypeStruct((n, TOP_K), F32),
                   jax.ShapeDtypeStruct((n, TOP_K), jnp.int32),
                   jax.ShapeDtypeStruct((1, ne), jnp.int32)],
        scratch_shapes=[pltpu.VMEM((tm + POOL_HALO, d), F32), pltpu.VMEM((tm, d), BF16),
                        pltpu.VMEM((1, ne), F32), pltpu.VMEM((rt, rt), BF16)],
        compiler_params=_cparams("arbitrary"),
        name="pool_router",
    )(x, x, mod, g, win, wmix, scale, wo, g2, rw, rb)


def _pack_bf16_pairs(v):
    w = v.shape[1] // 2
    bits = lambda a: lax.bitcast_convert_type(a.astype(BF16).astype(F32), jnp.int32)
    return bits(v[:, :w]) | lax.shift_right_logical(bits(v[:, w:]), 16)


def _unpack_bf16_pairs(p):
    hi = lax.bitcast_convert_type(p & jnp.int32(-65536), F32)
    lo = lax.bitcast_convert_type(lax.shift_left(p, 16), F32)
    return jnp.concatenate([hi, lo], axis=1)


def _route_rows(x, g, shift, scale, rw, rb, before, base):
    h = _normmod(x, g, shift, scale)
    hp = _pack_bf16_pairs(h)
    split = lambda a: (a.astype(BF16), (a - a.astype(BF16).astype(F32)).astype(BF16))
    h_hi, h_lo = split(h)
    w_hi, w_lo = split(rw)
    logits = (_bdot(h_hi, w_hi) + _bdot(h_hi, w_lo) + _bdot(h_lo, w_hi)) + rb
    ne = float(logits.shape[1])
    lane = lax.broadcasted_iota(jnp.int32, logits.shape, 1).astype(F32)
    v1 = jnp.max(logits, axis=1, keepdims=True)
    i1 = jnp.min(jnp.where(logits == v1, lane, ne), axis=1, keepdims=True)
    rest = jnp.where(lane == i1, -jnp.inf, logits)
    v2 = jnp.max(rest, axis=1, keepdims=True)
    i2 = jnp.min(jnp.where(rest == v2, lane, ne), axis=1, keepdims=True)
    e2 = jnp.exp(v2 - v1)
    den = 1.0 + e2
    two = lax.broadcasted_iota(jnp.int32, (logits.shape[0], TOP_K), 1)
    idx = jnp.where(two == 0, i1, i2).astype(jnp.int32)
    gate = jnp.where(two == 0, 1.0 / den, e2 / den)
    oh1 = jnp.where(lane == i1, 1.0, 0.0)
    oh2 = jnp.where(lane == i2, 1.0, 0.0)
    both = oh1 + oh2
    prior = _bdot(before, both.astype(BF16)) + base
    r1 = jnp.sum(oh1 * prior, axis=1, keepdims=True)
    r2 = jnp.sum(oh2 * prior, axis=1, keepdims=True)
    rank = jnp.where(two == 0, r1, r2).astype(jnp.int32)
    return hp, idx, gate, rank, base + jnp.sum(both, axis=0, keepdims=True)


def _count_le(sorted_vals, queries):
    return jnp.sum((sorted_vals[None, :] <= queries[:, None]).astype(jnp.int32), axis=1)


def _routing_tables(idx, rank, cnt, ne, tile):
    n = idx.shape[0]
    padded = ((cnt + tile - 1) // tile) * tile
    off = jnp.cumsum(padded) - padded
    onehot = idx[:, :, None] == jnp.arange(ne, dtype=jnp.int32)[None, None, :]
    pos = rank + jnp.sum(jnp.where(onehot, off[None, None, :], 0), axis=2)
    n_tiles = (TOP_K * n) // tile + ne
    used = jnp.sum(padded) // tile
    tile_ids = jnp.arange(n_tiles, dtype=jnp.int32)
    tile_expert = _count_le(jnp.cumsum(padded), tile_ids * tile)
    last_used = jnp.maximum(used - 1, 0)
    tile_valid = (tile_ids < used).astype(jnp.int32)
    tile_src = jnp.minimum(tile_ids, last_used).astype(jnp.int32)
    tile_expert = jnp.minimum(tile_expert, ne - 1).astype(jnp.int32)
    tile_expert = jnp.where(tile_valid == 1, tile_expert, tile_expert[last_used])
    tile_rows = jnp.clip(cnt[tile_expert] - (tile_ids * tile - off[tile_expert]), 0, tile)
    tile_rows = jnp.where(tile_valid == 1, tile_rows, 0).astype(jnp.int32)
    return dict(pos=pos, tile_expert=tile_expert, tile_valid=tile_valid, tile_src=tile_src,
                tile_rows=tile_rows, n_tiles=n_tiles)


def _sc_mesh():
    return plsc.VectorSubcoreMesh(core_axis_name="c", subcore_axis_name="s",
                                  num_cores=SC_CORES, num_subcores=SC_SUBCORES)


def _dispatch(hp, pos0, pos1, rows_out):
    n, w = hp.shape
    per_w = n // SC_WORKERS
    ch = SC_ROWS
    nch = per_w // ch
    assert per_w * SC_WORKERS == n and nch * ch == per_w and nch % 2 == 0

    @functools.partial(
        pl.kernel, mesh=_sc_mesh(),
        out_type=jax.ShapeDtypeStruct((rows_out, w), hp.dtype),
        scratch_types=[pltpu.VMEM((per_w,), jnp.int32), pltpu.VMEM((per_w,), jnp.int32),
                       pltpu.VMEM((ch, w), hp.dtype), pltpu.VMEM((ch, w), hp.dtype)]
        + [pltpu.SemaphoreType.DMA] * 4,
        name="dispatch",
    )
    def k(hp_hbm, p0_hbm, p1_hbm, out_hbm, i0_v, i1_v, r0_v, r1_v, l0, l1, s0, s1):
        base = (lax.axis_index("s") * SC_CORES + lax.axis_index("c")) * per_w
        pltpu.sync_copy(p0_hbm.at[pl.ds(base, per_w)], i0_v)
        pltpu.sync_copy(p1_hbm.at[pl.ds(base, per_w)], i1_v)
        bufs, lsem, ssem = (r0_v, r1_v), (l0, l1), (s0, s1)

        def load(j, b):
            return pltpu.make_async_copy(hp_hbm.at[pl.ds(base + j * ch, ch)], bufs[b], lsem[b])

        def scatter(j, b, idx):
            return pltpu.make_async_copy(bufs[b], out_hbm.at[idx.at[pl.ds(j * ch, ch)]], ssem[b])

        load(0, 0).start()

        @pl.loop(0, nch, step=2)
        def _(j):
            for b in range(2):
                jj = j + b
                load(jj, b).wait()

                @pl.when(jj + 1 < nch)
                def _():
                    load(jj + 1, 1 - b).start()

                c0 = scatter(jj, b, i0_v)
                c1 = scatter(jj, b, i1_v)
                c0.start()
                c1.start()
                c0.wait()
                c1.wait()

    return k(hp, pos0, pos1)


def _gather_pairs(ys, pos0, pos1):
    n = pos0.shape[0]
    w = ys.shape[1]
    per_w = n // SC_WORKERS
    ch = SC_ROWS
    nch = per_w // ch
    assert per_w * SC_WORKERS == n and nch * ch == per_w
    out = jax.ShapeDtypeStruct((n, w), ys.dtype)

    @functools.partial(
        pl.kernel, mesh=_sc_mesh(), out_type=(out, out),
        scratch_types=[pltpu.VMEM((per_w,), jnp.int32), pltpu.VMEM((per_w,), jnp.int32),
                       pltpu.VMEM((ch, w), ys.dtype), pltpu.VMEM((ch, w), ys.dtype)]
        + [pltpu.SemaphoreType.DMA] * 4,
        name="gather_pairs",
    )
    def k(ys_hbm, p0_hbm, p1_hbm, a_hbm, b_hbm, i0_v, i1_v, ra_v, rb_v, ga, gb, wa, wb):
        base = (lax.axis_index("s") * SC_CORES + lax.axis_index("c")) * per_w
        pltpu.sync_copy(p0_hbm.at[pl.ds(base, per_w)], i0_v)
        pltpu.sync_copy(p1_hbm.at[pl.ds(base, per_w)], i1_v)

        def gather(j, idx, buf, sem):
            return pltpu.make_async_copy(ys_hbm.at[idx.at[pl.ds(j * ch, ch)]], buf, sem)

        def write(j, buf, dst, sem):
            return pltpu.make_async_copy(buf, dst.at[pl.ds(base + j * ch, ch)], sem)

        gather(0, i0_v, ra_v, ga).start()
        gather(0, i1_v, rb_v, gb).start()

        @pl.loop(0, nch)
        def _(j):
            gather(j, i0_v, ra_v, ga).wait()
            write(j, ra_v, a_hbm, wa).start()
            gather(j, i1_v, rb_v, gb).wait()
            write(j, rb_v, b_hbm, wb).start()
            write(j, ra_v, a_hbm, wa).wait()

            @pl.when(j + 1 < nch)
            def _():
                gather(j + 1, i0_v, ra_v, ga).start()

            write(j, rb_v, b_hbm, wb).wait()

            @pl.when(j + 1 < nch)
            def _():
                gather(j + 1, i1_v, rb_v, gb).start()

    return k(ys, pos0, pos1)


def _expert_kernel(te_ref, tv_ref, tsrc_ref, tr_ref, xs_ref, w1_ref, w3_ref, w2_ref,
                   ys_ref, *, sub):
    j = pl.program_id(0)
    half = xs_ref.shape[0] // 2

    def run(rows):
        x = _unpack_bf16_pairs(xs_ref[rows, :])
        row = lax.broadcasted_iota(jnp.int32, x.shape, 0)
        x = jnp.where(row < tr_ref[j], x, 0.0).astype(BF16)
        ys_ref[rows, :] = _pack_bf16_pairs(
            _swiglu(x, w1_ref.at[0], w3_ref.at[0], w2_ref.at[0], sub))

    @pl.when(tr_ref[j] > half)
    def _():
        run(slice(None))

    @pl.when((tr_ref[j] > 0) & (tr_ref[j] <= half))
    def _():
        run(slice(0, half))
        ys_ref[half:, :] = jnp.zeros((half, ys_ref.shape[1]), ys_ref.dtype)

    @pl.when(tr_ref[j] == 0)
    def _():
        ys_ref[...] = jnp.zeros_like(ys_ref)


def _experts(xs, w1, w3, w2, rt, tile):
    rows, wp = xs.shape
    ne, d, dff = w1.shape
    return pl.pallas_call(
        functools.partial(_expert_kernel, sub=512),
        grid_spec=pltpu.PrefetchScalarGridSpec(
            num_scalar_prefetch=4,
            grid=(rt['n_tiles'],),
            in_specs=[pl.BlockSpec((tile, wp), lambda j, te, tv, tsrc, tr: (tsrc[j], 0)),
                      pl.BlockSpec((1, d, dff), lambda j, te, tv, tsrc, tr: (te[j], 0, 0)),
                      pl.BlockSpec((1, d, dff), lambda j, te, tv, tsrc, tr: (te[j], 0, 0)),
                      pl.BlockSpec((1, dff, d), lambda j, te, tv, tsrc, tr: (te[j], 0, 0))],
            out_specs=pl.BlockSpec((tile, wp), lambda j, te, tv, tsrc, tr: (j, 0))),
        out_shape=jax.ShapeDtypeStruct((rows, wp), jnp.int32),
        compiler_params=_cparams("arbitrary"),
        name="experts",
    )(rt['tile_expert'], rt['tile_valid'], rt['tile_src'], rt['tile_rows'], xs, w1, w3, w2)


def _combine_kernel(a_ref, b_ref, gate_ref, x_ref, mod_ref, fg_ref, o_ref, *, final_norm):
    m = mod_ref[0]
    g = gate_ref[...]
    y = g[:, 0:1] * _unpack_bf16_pairs(a_ref[...]) + g[:, 1:2] * _unpack_bf16_pairs(b_ref[...])
    xn = x_ref[...] + m[5:6] * y
    if final_norm:
        ms = jnp.mean(xn * xn, axis=-1, keepdims=True)
        xn = (xn * lax.rsqrt(ms + EPS)) * fg_ref[...]
    o_ref[...] = xn


def _combine(a, b, gates, x, mod, fg, seq, tm, final_norm, tile0):
    n, d = x.shape
    tps = seq // tm
    part = lambda w: pl.BlockSpec((tm, w), lambda i: (i, 0))
    full = lambda w: pl.BlockSpec((tm, w), lambda i: (i + tile0, 0))
    return pl.pallas_call(
        functools.partial(_combine_kernel, final_norm=final_norm),
        grid=(a.shape[0] // tm,),
        in_specs=[part(d // 2), part(d // 2), full(TOP_K), full(d),
                  pl.BlockSpec((1, 6, d), lambda i: ((i + tile0) // tps, 0, 0)),
                  pl.BlockSpec((1, d), lambda i: (0, 0))],
        out_specs=full(d),
        out_shape=jax.ShapeDtypeStruct((n, d), F32),
        input_output_aliases={3: 0},
        compiler_params=_cparams("arbitrary"),
        name="combine",
    )(a, b, gates, x, mod, fg)


def _pick(n, prefs):
    for p in prefs:
        if n % p == 0:
            return p
    return n


def kernel(x, c, ada_w, ada_b, norm_g, ssm_in, ssm_log_dt, ssm_lam_re, ssm_lam_im,
           ssm_b_re, ssm_b_im, ssm_c_re, ssm_c_im, ssm_d, ssm_glu, ssm_out,
           pool_in, pool_mix, pool_scale, pool_out, ffn_w1, ffn_w3, ffn_w2,
           router_w, router_b, moe_w1, moe_w3, moe_w2, final_g):
    bsz, seq, d = x.shape
    depth = ada_w.shape[0]
    n = bsz * seq
    ne = router_w.shape[-1]
    g_, p_, h_ = ssm_b_re.shape[1:]
    t = SSM_T
    if seq % (LANES * t // bsz) != 0 or bsz != 8:
        raise NotImplementedError("state-space kernels assume batch 8 and seq % 256 == 0")
    if depth % 2 != 0:
        raise NotImplementedError("the final RMSNorm is fused into an expert layer's combine")
    tm = _pick(seq, (512, 256, 128, 64, 32, 16))
    tile = _pick(TOP_K * n, (512, 256, 128, 64, 32, 16))
    gb = _pick(g_, (4, 2, 1))

    stack = lambda w: w.astype(BF16).reshape((-1,) + w.shape[2:])
    ew1, ew3, ew2 = stack(moe_w1), stack(moe_w3), stack(moe_w2)
    fw1, fw3, fw2 = ffn_w1.astype(BF16), ffn_w3.astype(BF16), ffn_w2.astype(BF16)
    sw_in, sw_glu, sw_out = ssm_in.astype(BF16), ssm_glu.astype(BF16), ssm_out.astype(BF16)
    pw_in, pw_mix, pw_out = pool_in.astype(BF16), pool_mix.astype(BF16), pool_out.astype(BF16)
    mod = _ada(c, ada_w, ada_b).reshape(depth, bsz, 6, d)
    xf = x.reshape(n, d)
    fg = final_g.reshape(1, d)
    for i in range(depth):
        j = i // 2
        mod_i = mod[i]
        g_a = norm_g[i, 0].reshape(1, d)
        g_b = norm_g[i, 1].reshape(1, d)
        if i % 2 == 0:
            ops = _ssm_operators(ssm_log_dt[j], ssm_lam_re[j], ssm_lam_im[j], ssm_b_re[j],
                                 ssm_b_im[j], ssm_c_re[j], ssm_c_im[j], ssm_d[j], t)
            ut = _ssm_in(xf.reshape(bsz, seq, d), mod_i, g_a, sw_in, t, h_, 4, layer=j)
            yt = _ssm(ut, *ops, bsz, gb)
            xf = _glu_out(yt, xf.reshape(bsz, seq, d), mod_i, sw_glu, sw_out, t, 4,
                          layer=j).reshape(n, d)
            xf = _ffn(xf, mod_i, g_b, fw1, fw3, fw2, seq, tm, layer=j)
        else:
            xf, h, idx, gates, rank, cnt = _pool(
                xf, mod_i, g_a, pw_in, pw_mix, pool_scale[j].reshape(1, d), pw_out,
                g_b, router_w[j], router_b[j].reshape(1, ne), seq, 2 * tm, tm, layer=j)
            rt = _routing_tables(idx, rank, cnt[0], ne, tile)
            pos0, pos1 = rt['pos'][:, 0], rt['pos'][:, 1]
            xs = _dispatch(h, pos0, pos1, rt['n_tiles'] * tile)
            rt['tile_expert'] = rt['tile_expert'] + j * ne
            ys = _experts(xs, ew1, ew3, ew2, rt, tile)
            npc = n // COMBINE_PIECES
            for q in range(COMBINE_PIECES):
                piece = slice(q * npc, (q + 1) * npc)
                ya, yb = _gather_pairs(ys, pos0[piece], pos1[piece])
                xf = _combine(ya, yb, gates, xf, mod_i, fg, seq, 2 * tm,
                              final_norm=(i == depth - 1), tile0=q * (npc // (2 * tm)))
    return xf.reshape(bsz, seq, d)
```

```python
import functools
import math

import jax
import jax.numpy as jnp
from jax import lax
from jax.experimental import pallas as pl
from jax.experimental.pallas import tpu as pltpu
from jax.experimental.pallas import tpu_sc as plsc

F32 = jnp.float32
BF16 = jnp.bfloat16
EPS = 1e-6
POOL_WINDOWS = (2, 4, 8, 16)
POOL_HALO = 16
SSM_T = 16
LANES = 128
TOP_K = 2
SC_CORES, SC_SUBCORES = 2, 16
SC_WORKERS = SC_CORES * SC_SUBCORES
SC_ROWS = 64
COMBINE_PIECES = 2
VMEM_LIMIT = 58 * 1024 * 1024


def _cparams(*sem):
    return pltpu.CompilerParams(dimension_semantics=sem, vmem_limit_bytes=VMEM_LIMIT)


def _normmod(x, g, shift, scale):
    ms = jnp.mean(x * x, axis=-1, keepdims=True)
    y = x * lax.rsqrt(ms + EPS)
    return (y * g) * (1.0 + scale) + shift


def _bdot(a, b):
    return jnp.dot(a, b, preferred_element_type=F32)


def _ada_kernel(c_ref, w_ref, b_ref, o_ref):
    c = c_ref[...]
    cond = c * jax.nn.sigmoid(c)
    o_ref[0] = jnp.dot(cond, w_ref[0], precision=lax.Precision.HIGHEST,
                       preferred_element_type=F32) + b_ref[0]


def _ada(c, ada_w, ada_b):
    depth, d, d6 = ada_w.shape
    bsz = c.shape[0]
    tn = d6 // 4
    return pl.pallas_call(
        _ada_kernel,
        grid=(depth, d6 // tn),
        in_specs=[pl.BlockSpec((bsz, d), lambda l, j: (0, 0)),
                  pl.BlockSpec((1, d, tn), lambda l, j: (l, 0, j)),
                  pl.BlockSpec((1, 1, tn), lambda l, j: (l, 0, j))],
        out_specs=pl.BlockSpec((1, bsz, tn), lambda l, j: (l, 0, j)),
        out_shape=jax.ShapeDtypeStruct((depth, bsz, d6), F32),
        compiler_params=_cparams("arbitrary", "arbitrary"),
        name="ada",
    )(c, ada_w, ada_b.reshape(depth, 1, d6))


def _ssm_operators(log_dt, lam_re, lam_im, b_re, b_im, c_re, c_im, d_skip, t):
    g_, p_, h_ = b_re.shape
    dt = jnp.exp(log_dt.astype(F32))[:, None]
    lr = lam_re.astype(F32)
    li = lam_im.astype(F32)
    mag = jnp.exp(lr * dt)
    a_re = mag * jnp.cos(li * dt)
    a_im = mag * jnp.sin(li * dt)
    den = lr * lr + li * li
    nr = a_re - 1
    coef_re = (nr * lr + a_im * li) / den
    coef_im = (a_im * lr - nr * li) / den
    br = b_re.astype(F32)
    bi = b_im.astype(F32)
    bbar_re = coef_re[..., None] * br - coef_im[..., None] * bi
    bbar_im = coef_re[..., None] * bi + coef_im[..., None] * br
    steps = jnp.arange(0, t + 1, dtype=F32)[None, :, None]
    ang = (li * dt)[:, None, :] * steps
    pmag = jnp.exp((lr * dt)[:, None, :] * steps)
    pw_re = pmag * jnp.cos(ang)
    pw_im = pmag * jnp.sin(ang)
    cr = c_re.astype(F32)
    ci = c_im.astype(F32)
    er = cr[:, None] * pw_re[:, :, None, :] - ci[:, None] * pw_im[:, :, None, :]
    ei = cr[:, None] * pw_im[:, :, None, :] + ci[:, None] * pw_re[:, :, None, :]
    tg = lambda a: jnp.transpose(a, (1, 2, 0))
    pt_re, pt_im = tg(pw_re[:, :t])[:, None], tg(pw_im[:, :t])[:, None]
    ct_re, ct_im = tg(cr)[None], tg(ci)[None]
    er_l = (ct_re * pt_re - ct_im * pt_im)[..., None]
    ei_l = (ct_re * pt_im + ct_im * pt_re)[..., None]
    br_l = jnp.transpose(bbar_re, (1, 0, 2))[None, None]
    bi_l = jnp.transpose(bbar_im, (1, 0, 2))[None, None]
    kk = jnp.sum(er_l * br_l - ei_l * bi_l, axis=2)
    kk = jnp.transpose(kk, (2, 0, 1, 3))
    kz = jnp.pad(kk, ((0, 0), (t - 1, 0), (0, 0), (0, 0)))
    cols = [kz[:, t - 1 - j:2 * t - 1 - j] for j in range(t)]
    mt = jnp.stack(cols, axis=3).reshape(g_, t * h_, t * h_)
    rev_re = pw_re[:, :t][:, ::-1, None, :]
    rev_im = pw_im[:, :t][:, ::-1, None, :]
    bt_re = jnp.transpose(bbar_re, (0, 2, 1))[:, None]
    bt_im = jnp.transpose(bbar_im, (0, 2, 1))[:, None]
    w_re = (rev_re * bt_re - rev_im * bt_im).reshape(g_, t * h_, p_)
    w_im = (rev_re * bt_im + rev_im * bt_re).reshape(g_, t * h_, p_)
    w4 = jnp.concatenate([w_re, w_im, w_im, w_re], axis=-1)
    vt = jnp.concatenate([er[:, 1:].reshape(g_, t * h_, p_),
                          -ei[:, 1:].reshape(g_, t * h_, p_)], axis=-1)
    at_re, at_im = pw_re[:, t], pw_im[:, t]
    c1 = jnp.concatenate([at_re, at_re], axis=-1)
    c2 = jnp.concatenate([-at_im, at_im], axis=-1)
    coef = jnp.concatenate([c1[:, None], c2[:, None], jnp.zeros((g_, 6, 2 * p_), F32)], axis=1)
    dk = jnp.tile(d_skip.astype(F32)[:, None, :], (1, t, 1)).reshape(g_, t * h_, 1)
    return mt.astype(BF16), w4.astype(BF16), vt.astype(BF16), coef, dk


def _ssm_in_kernel(x_ref, mod_ref, g_ref, w_ref, ut_ref, h_scr, u_scr, *, t):
    q = pl.program_id(1)
    bsz, cs, _ = x_ref.shape
    nlt = u_scr.shape[0]
    for b in range(bsz):
        m = mod_ref[b]
        h = _normmod(x_ref[b], g_ref[...], m[0:1], m[1:2]).astype(BF16)
        for c in range(cs // t):
            h_scr[(c * bsz + b) * t:(c * bsz + b + 1) * t, :] = h[c * t:(c + 1) * t, :]
    u = _bdot(h_scr[...], w_ref[0])
    rows = pl.ds(pl.multiple_of(q * (bsz * cs), bsz * cs), bsz * cs)
    for l in range(nlt):
        u_scr[l, rows, :] = u[:, l * LANES:(l + 1) * LANES]

    @pl.when(q == pl.num_programs(1) - 1)
    def _():
        ng, th, nch = ut_ref.shape
        hh = th // t
        gl = LANES // hh
        for l in range(nlt):
            uq = pltpu.einshape("ckl->kcl", u_scr[l].reshape(nch, t, LANES))
            for k in range(t):
                ut_ref[l * gl:(l + 1) * gl, k * hh:(k + 1) * hh, :] = (
                    uq[k].T.reshape(gl, hh, nch).astype(BF16))


def _ssm_in(x, mod, g, w, t, hh, nq, layer):
    bsz, seq, d = x.shape
    ng = d // hh
    cs = LANES * t // bsz // nq
    return pl.pallas_call(
        functools.partial(_ssm_in_kernel, t=t),
        grid=(seq // (cs * nq), nq),
        in_specs=[pl.BlockSpec((bsz, cs, d), lambda o, q: (0, o * nq + q, 0)),
                  pl.BlockSpec((bsz, 6, d), lambda o, q: (0, 0, 0)),
                  pl.BlockSpec((1, d), lambda o, q: (0, 0)),
                  pl.BlockSpec((1, d, d), lambda o, q: (layer, 0, 0))],
        out_specs=pl.BlockSpec((ng, t * hh, LANES), lambda o, q: (0, 0, o)),
        out_shape=jax.ShapeDtypeStruct((ng, t * hh, bsz * seq // t), BF16),
        scratch_shapes=[pltpu.VMEM((bsz * cs, d), BF16),
                        pltpu.VMEM((d // LANES, LANES * t, LANES), F32)],
        compiler_params=_cparams("arbitrary", "arbitrary"),
        name="ssm_in",
    )(x, mod, g, w)


def _ssm_kernel(ut_ref, mt_ref, w4_ref, vt_ref, coef_ref, dk_ref, yt_ref, z_scr, s_scr,
                *, gb, nchunks, bsz):
    p2 = s_scr.shape[-1]
    for gi in range(gb):
        u = ut_ref[gi].astype(F32).T.astype(BF16)
        z = _bdot(u, w4_ref[gi])
        z_scr[gi, 0] = z[:, :p2]
        z_scr[gi, 1] = z[:, p2:]
    c1 = [jnp.broadcast_to(coef_ref[gi, 0:1, :], (bsz, p2)) for gi in range(gb)]
    c2 = [jnp.broadcast_to(coef_ref[gi, 1:2, :], (bsz, p2)) for gi in range(gb)]

    def body(c, carry):
        new = []
        for gi in range(gb):
            s1, s2 = carry[gi]
            rows = pl.ds(pl.multiple_of(c * bsz, bsz), bsz)
            s_scr[gi, rows, :] = s1
            n1 = c1[gi] * s1 + c2[gi] * s2 + z_scr[gi, 0, rows, :]
            n2 = c1[gi] * s2 - c2[gi] * s1 + z_scr[gi, 1, rows, :]
            new.append((n1, n2))
        return tuple(new)

    zero = jnp.zeros((bsz, p2), F32)
    lax.fori_loop(0, nchunks, body, tuple((zero, zero) for _ in range(gb)))
    for gi in range(gb):
        ut = ut_ref[gi]
        st = _bdot(vt_ref[gi], s_scr[gi].T.astype(BF16))
        yt = _bdot(mt_ref[gi], ut) + st + dk_ref[gi] * ut.astype(F32)
        yt_ref[gi] = yt.astype(yt_ref.dtype)


def _ssm(ut, mt, w4, vt, coef, dk, bsz, gb):
    g_, th, nch = ut.shape
    p2 = vt.shape[2]
    kern = functools.partial(_ssm_kernel, gb=gb, nchunks=nch // bsz, bsz=bsz)
    spec3 = lambda a, b: pl.BlockSpec((gb, a, b), lambda i: (i, 0, 0))
    return pl.pallas_call(
        kern,
        grid=(g_ // gb,),
        in_specs=[spec3(th, nch), spec3(th, th), spec3(th, 2 * p2), spec3(th, p2),
                  spec3(8, p2), spec3(th, 1)],
        out_specs=spec3(th, nch),
        out_shape=jax.ShapeDtypeStruct((g_, th, nch), BF16),
        scratch_shapes=[pltpu.VMEM((gb, 2, nch, p2), F32), pltpu.VMEM((gb, nch, p2), F32)],
        compiler_params=_cparams("arbitrary"),
        name="ssm",
    )(ut, mt, w4, vt, coef, dk)


def _gelu_tanh(y):
    return y * (0.5 * (1.0 + jnp.tanh(math.sqrt(2.0 / math.pi) * (y + 0.044715 * (y * y * y)))))


def _glu_out_kernel(yt_ref, x_ref, mod_ref, wg_ref, wo_ref, o_ref, y_scr, *, t):
    q = pl.program_id(1)
    bsz, cs, _ = x_ref.shape
    nlt = y_scr.shape[0]

    @pl.when(q == 0)
    def _():
        ng, th, nch = yt_ref.shape
        hh = th // t
        gl = LANES // hh
        for l in range(nlt):
            parts = [yt_ref[l * gl:(l + 1) * gl, k * hh:(k + 1) * hh, :].astype(F32)
                     .reshape(LANES, nch).T for k in range(t)]
            yq = pltpu.einshape("kcl->ckl", jnp.stack(parts, axis=0))
            y_scr[l] = yq.reshape(nch * t, LANES)

    rows = pl.ds(pl.multiple_of(q * (bsz * cs), bsz * cs), bsz * cs)
    y = jnp.concatenate([y_scr[l, rows, :] for l in range(nlt)], axis=1)
    z = _gelu_tanh(y)
    z = z * jax.nn.sigmoid(_bdot(z.astype(BF16), wg_ref[0]))
    o = _bdot(z.astype(BF16), wo_ref[0])
    for b in range(bsz):
        gate = mod_ref[b][2:3]
        for c in range(cs // t):
            tok = slice(c * t, (c + 1) * t)
            o_ref[b, tok, :] = x_ref[b, tok, :] + gate * o[(c * bsz + b) * t:(c * bsz + b + 1) * t]


def _glu_out(yt, x, mod, wg, wo, t, nq, layer):
    bsz, seq, d = x.shape
    ng, th, _ = yt.shape
    cs = LANES * t // bsz // nq
    return pl.pallas_call(
        functools.partial(_glu_out_kernel, t=t),
        grid=(seq // (cs * nq), nq),
        in_specs=[pl.BlockSpec((ng, th, LANES), lambda o, q: (0, 0, o)),
                  pl.BlockSpec((bsz, cs, d), lambda o, q: (0, o * nq + q, 0)),
                  pl.BlockSpec((bsz, 6, d), lambda o, q: (0, 0, 0)),
                  pl.BlockSpec((1, d, d), lambda o, q: (layer, 0, 0)),
                  pl.BlockSpec((1, d, d), lambda o, q: (layer, 0, 0))],
        out_specs=pl.BlockSpec((bsz, cs, d), lambda o, q: (0, o * nq + q, 0)),
        out_shape=jax.ShapeDtypeStruct((bsz, seq, d), F32),
        scratch_shapes=[pltpu.VMEM((d // LANES, LANES * t, LANES), F32)],
        compiler_params=_cparams("arbitrary", "arbitrary"),
        name="glu_out",
    )(yt, x, mod, wg, wo)


def _chunks(total, step):
    return [(s, min(step, total - s)) for s in range(0, total, step)]


def _swiglu(h, w1_ref, w3_ref, w2_ref, sub):
    total = None
    for s, n in _chunks(w1_ref.shape[-1], sub):
        a = _bdot(h, w1_ref[:, s:s + n])
        b = _bdot(h, w3_ref[:, s:s + n])
        act = (a * jax.nn.sigmoid(a) * b).astype(BF16)
        y = _bdot(act, w2_ref[s:s + n, :])
        total = y if total is None else total + y
    return total


def _ffn_kernel(x_ref, mod_ref, g_ref, w1_ref, w3_ref, w2_ref, o_ref, *, sub):
    m = mod_ref[0]
    x = x_ref[...]
    h = _normmod(x, g_ref[...], m[3:4], m[4:5]).astype(BF16)
    o_ref[...] = x + m[5:6] * _swiglu(h, w1_ref.at[0], w3_ref.at[0], w2_ref.at[0], sub)


def _ffn(x, mod, g, w1, w3, w2, seq, tm, layer):
    n, d = x.shape
    dff = w1.shape[2]
    tps = seq // tm
    const = lambda i: (0, 0)
    lay = lambda i: (layer, 0, 0)
    return pl.pallas_call(
        functools.partial(_ffn_kernel, sub=256),
        grid=(n // tm,),
        in_specs=[pl.BlockSpec((tm, d), lambda i: (i, 0)),
                  pl.BlockSpec((1, 6, d), lambda i: (i // tps, 0, 0)),
                  pl.BlockSpec((1, d), const),
                  pl.BlockSpec((1, d, dff), lay, pipeline_mode=pl.Buffered(1)),
                  pl.BlockSpec((1, d, dff), lay, pipeline_mode=pl.Buffered(1)),
                  pl.BlockSpec((1, dff, d), lay, pipeline_mode=pl.Buffered(1))],
        out_specs=pl.BlockSpec((tm, d), lambda i: (i, 0)),
        out_shape=jax.ShapeDtypeStruct((n, d), F32),
        compiler_params=_cparams("arbitrary"),
        name="ffn",
    )(x, mod, g, w1, w3, w2)


def _pool_kernel(x_ref, halo_ref, mod_ref, g_ref, win_ref, wmix_ref, scale_ref, wo_ref,
                 g2_ref, rw_ref, rb_ref, o_ref, h_ref, idx_ref, gate_ref, rank_ref, cnt_ref,
                 ext_ref, z_ref, base_scr, before_scr, *, tps):
    tm, d = x_ref.shape
    pg = d // len(POOL_WINDOWS)
    m = mod_ref[0]
    it = pl.program_id(0) % tps

    @pl.when(pl.program_id(0) == 0)
    def _():
        base_scr[...] = jnp.zeros_like(base_scr)
        shape = before_scr.shape
        before = lax.broadcasted_iota(jnp.int32, shape, 1) < lax.broadcasted_iota(jnp.int32, shape, 0)
        before_scr[...] = jnp.where(before, 1.0, 0.0).astype(BF16)

    def project(rows):
        h = _normmod(rows, g_ref[...], m[0:1], m[1:2])
        return _bdot(h.astype(BF16), win_ref[0])

    ext_ref[0:POOL_HALO, :] = jnp.where(it == 0, 0.0, project(halo_ref[...]))
    ext_ref[POOL_HALO:, :] = project(x_ref[...])
    tpos = (it * tm + 1 + lax.broadcasted_iota(jnp.int32, (tm, 1), 0)).astype(F32)
    for gi, w in enumerate(POOL_WINDOWS):
        cols = slice(gi * pg, (gi + 1) * pg)
        s = ext_ref[:, cols]
        span = 1
        while span < w:
            s = s + pltpu.roll(s, span, 0)
            span *= 2
        mean = s[POOL_HALO:, :] * (1.0 / jnp.minimum(tpos, float(w)))
        dlt = (mean - ext_ref[POOL_HALO:, cols]).astype(BF16)
        z_ref[:, cols] = (_bdot(dlt, wmix_ref[0, gi]) * scale_ref[:, cols]).astype(BF16)
    xn = x_ref[...] + m[2:3] * _bdot(z_ref[...], wo_ref[0])
    o_ref[...] = xn
    rt = before_scr.shape[0]
    for r in range(tm // rt):
        rows = slice(r * rt, (r + 1) * rt)
        hp, idx, gate, rank, total = _route_rows(
            xn[rows], g2_ref[...], m[3:4], m[4:5], rw_ref[...], rb_ref[...],
            before_scr[...], base_scr[...])
        h_ref[rows, :] = hp
        idx_ref[rows, :] = idx
        gate_ref[rows, :] = gate
        rank_ref[rows, :] = rank
        base_scr[...] = total
    cnt_ref[...] = base_scr[...].astype(jnp.int32)


def _pool(x, mod, g, win, wmix, scale, wo, g2, rw, rb, seq, tm, rt, layer):
    n, d = x.shape
    tps = seq // tm
    hb = tm // POOL_HALO
    _, ng, pg, _ = wmix.shape
    ne = rw.shape[1]
    const = lambda i: (0, 0)
    lay = lambda i: (layer, 0, 0)
    row = lambda w: pl.BlockSpec((tm, w), lambda i: (i, 0))
    return pl.pallas_call(
        functools.partial(_pool_kernel, tps=tps),
        grid=(n // tm,),
        in_specs=[row(d),
                  pl.BlockSpec((POOL_HALO, d), lambda i: (jnp.maximum(i * hb - 1, 0), 0)),
                  pl.BlockSpec((1, 6, d), lambda i: (i // tps, 0, 0)),
                  pl.BlockSpec((1, d), const),
                  pl.BlockSpec((1, d, d), lay),
                  pl.BlockSpec((1, ng, pg, pg), lambda i: (layer, 0, 0, 0)),
                  pl.BlockSpec((1, d), const),
                  pl.BlockSpec((1, d, d), lay),
                  pl.BlockSpec((1, d), const),
                  pl.BlockSpec((d, ne), const),
                  pl.BlockSpec((1, ne), const)],
        out_specs=[row(d), row(d // 2), row(TOP_K), row(TOP_K), row(TOP_K),
                   pl.BlockSpec((1, ne), const)],
        out_shape=[jax.ShapeDtypeStruct((n, d), F32),
                   jax.ShapeDtypeStruct((n, d // 2), jnp.int32),
                   jax.ShapeDtypeStruct((n, TOP_K), jnp.int32),
                   jax.ShapeDtypeStruct((n, TOP_K), F32),
                   jax.ShapeDtypeStruct((n, TOP_K), jnp.int32),
                   jax.ShapeDtypeStruct((1, ne), jnp.int32)],
        scratch_shapes=[pltpu.VMEM((tm + POOL_HALO, d), F32), pltpu.VMEM((tm, d), BF16),
                        pltpu.VMEM((1, ne), F32), pltpu.VMEM((rt, rt), BF16)],
        compiler_params=_cparams("arbitrary"),
        name="pool_router",
    )(x, x, mod, g, win, wmix, scale, wo, g2, rw, rb)


def _pack_bf16_pairs(v):
    w = v.shape[1] // 2
    bits = lambda a: lax.bitcast_convert_type(a.astype(BF16).astype(F32), jnp.int32)
    return bits(v[:, :w]) | lax.shift_right_logical(bits(v[:, w:]), 16)


def _unpack_bf16_pairs(p):
    hi = lax.bitcast_convert_type(p & jnp.int32(-65536), F32)
    lo = lax.bitcast_convert_type(lax.shift_left(p, 16), F32)
    return jnp.concatenate([hi, lo], axis=1)


def _route_rows(x, g, shift, scale, rw, rb, before, base):
    h = _normmod(x, g, shift, scale)
    hp = _pack_bf16_pairs(h)
    split = lambda a: (a.astype(BF16), (a - a.astype(BF16).astype(F32)).astype(BF16))
    h_hi, h_lo = split(h)
    w_hi, w_lo = split(rw)
    logits = (_bdot(h_hi, w_hi) + _bdot(h_hi, w_lo) + _bdot(h_lo, w_hi)) + rb
    ne = float(logits.shape[1])
    lane = lax.broadcasted_iota(jnp.int32, logits.shape, 1).astype(F32)
    v1 = jnp.max(logits, axis=1, keepdims=True)
    i1 = jnp.min(jnp.where(logits == v1, lane, ne), axis=1, keepdims=True)
    rest = jnp.where(lane == i1, -jnp.inf, logits)
    v2 = jnp.max(rest, axis=1, keepdims=True)
    i2 = jnp.min(jnp.where(rest == v2, lane, ne), axis=1, keepdims=True)
    e2 = jnp.exp(v2 - v1)
    den = 1.0 + e2
    two = lax.broadcasted_iota(jnp.int32, (logits.shape[0], TOP_K), 1)
    idx = jnp.where(two == 0, i1, i2).astype(jnp.int32)
    gate = jnp.where(two == 0, 1.0 / den, e2 / den)
    oh1 = jnp.where(lane == i1, 1.0, 0.0)
    oh2 = jnp.where(lane == i2, 1.0, 0.0)
    both = oh1 + oh2
    prior = _bdot(before, both.astype(BF16)) + base
    r1 = jnp.sum(oh1 * prior, axis=1, keepdims=True)
    r2 = jnp.sum(oh2 * prior, axis=1, keepdims=True)
    rank = jnp.where(two == 0, r1, r2).astype(jnp.int32)
    return hp, idx, gate, rank, base + jnp.sum(both, axis=0, keepdims=True)


def _count_le(sorted_vals, queries):
    return jnp.sum((sorted_vals[None, :] <= queries[:, None]).astype(jnp.int32), axis=1)


def _routing_tables(idx, rank, cnt, ne, tile):
    n = idx.shape[0]
    padded = ((cnt + tile - 1) // tile) * tile
    off = jnp.cumsum(padded) - padded
    onehot = idx[:, :, None] == jnp.arange(ne, dtype=jnp.int32)[None, None, :]
    pos = rank + jnp.sum(jnp.where(onehot, off[None, None, :], 0), axis=2)
    n_tiles = (TOP_K * n) // tile + ne
    used = jnp.sum(padded) // tile
    tile_ids = jnp.arange(n_tiles, dtype=jnp.int32)
    tile_expert = _count_le(jnp.cumsum(padded), tile_ids * tile)
    last_used = jnp.maximum(used - 1, 0)
    tile_valid = (tile_ids < used).astype(jnp.int32)
    tile_src = jnp.minimum(tile_ids, last_used).astype(jnp.int32)
    tile_expert = jnp.minimum(tile_expert, ne - 1).astype(jnp.int32)
    tile_expert = jnp.where(tile_valid == 1, tile_expert, tile_expert[last_used])
    tile_rows = jnp.clip(cnt[tile_expert] - (tile_ids * tile - off[tile_expert]), 0, tile)
    tile_rows = jnp.where(tile_valid == 1, tile_rows, 0).astype(jnp.int32)
    return dict(pos=pos, tile_expert=tile_expert, tile_valid=tile_valid, tile_src=tile_src,
                tile_rows=tile_rows, n_tiles=n_tiles)


def _sc_mesh():
    return plsc.VectorSubcoreMesh(core_axis_name="c", subcore_axis_name="s",
                                  num_cores=SC_CORES, num_subcores=SC_SUBCORES)


def _dispatch(hp, pos0, pos1, rows_out):
    n, w = hp.shape
    per_w = n // SC_WORKERS
    ch = SC_ROWS
    nch = per_w // ch
    assert per_w * SC_WORKERS == n and nch * ch == per_w and nch % 2 == 0

    @functools.partial(
        pl.kernel, mesh=_sc_mesh(),
        out_type=jax.ShapeDtypeStruct((rows_out, w), hp.dtype),
        scratch_types=[pltpu.VMEM((per_w,), jnp.int32), pltpu.VMEM((per_w,), jnp.int32),
                       pltpu.VMEM((ch, w), hp.dtype), pltpu.VMEM((ch, w), hp.dtype)]
        + [pltpu.SemaphoreType.DMA] * 4,
        name="dispatch",
    )
    def k(hp_hbm, p0_hbm, p1_hbm, out_hbm, i0_v, i1_v, r0_v, r1_v, l0, l1, s0, s1):
        base = (lax.axis_index("s") * SC_CORES + lax.axis_index("c")) * per_w
        pltpu.sync_copy(p0_hbm.at[pl.ds(base, per_w)], i0_v)
        pltpu.sync_copy(p1_hbm.at[pl.ds(base, per_w)], i1_v)
        bufs, lsem, ssem = (r0_v, r1_v), (l0, l1), (s0, s1)

        def load(j, b):
            return pltpu.make_async_copy(hp_hbm.at[pl.ds(base + j * ch, ch)], bufs[b], lsem[b])

        def scatter(j, b, idx):
            return pltpu.make_async_copy(bufs[b], out_hbm.at[idx.at[pl.ds(j * ch, ch)]], ssem[b])

        load(0, 0).start()

        @pl.loop(0, nch, step=2)
        def _(j):
            for b in range(2):
                jj = j + b
                load(jj, b).wait()

                @pl.when(jj + 1 < nch)
                def _():
                    load(jj + 1, 1 - b).start()

                c0 = scatter(jj, b, i0_v)
                c1 = scatter(jj, b, i1_v)
                c0.start()
                c1.start()
                c0.wait()
                c1.wait()

    return k(hp, pos0, pos1)


def _gather_pairs(ys, pos0, pos1):
    n = pos0.shape[0]
    w = ys.shape[1]
    per_w = n // SC_WORKERS
    ch = SC_ROWS
    nch = per_w // ch
    assert per_w * SC_WORKERS == n and nch * ch == per_w
    out = jax.ShapeDtypeStruct((n, w), ys.dtype)

    @functools.partial(
        pl.kernel, mesh=_sc_mesh(), out_type=(out, out),
        scratch_types=[pltpu.VMEM((per_w,), jnp.int32), pltpu.VMEM((per_w,), jnp.int32),
                       pltpu.VMEM((ch, w), ys.dtype), pltpu.VMEM((ch, w), ys.dtype)]
        + [pltpu.SemaphoreType.DMA] * 4,
        name="gather_pairs",
    )
    def k(ys_hbm, p0_hbm, p1_hbm, a_hbm, b_hbm, i0_v, i1_v, ra_v, rb_v, ga, gb, wa, wb):
        base = (lax.axis_index("s") * SC_CORES + lax.axis_index("c")) * per_w
        pltpu.sync_copy(p0_hbm.at[pl.ds(base, per_w)], i0_v)
        pltpu.sync_copy(p1_hbm.at[pl.ds(base, per_w)], i1_v)

        def gather(j, idx, buf, sem):
            return pltpu.make_async_copy(ys_hbm.at[idx.at[pl.ds(j * ch, ch)]], buf, sem)

        def write(j, buf, dst, sem):
            return pltpu.make_async_copy(buf, dst.at[pl.ds(base + j * ch, ch)], sem)

        gather(0, i0_v, ra_v, ga).start()
        gather(0, i1_v, rb_v, gb).start()

        @pl.loop(0, nch)
        def _(j):
            gather(j, i0_v, ra_v, ga).wait()
            write(j, ra_v, a_hbm, wa).start()
            gather(j, i1_v, rb_v, gb).wait()
            write(j, rb_v, b_hbm, wb).start()
            write(j, ra_v, a_hbm, wa).wait()

            @pl.when(j + 1 < nch)
            def _():
                gather(j + 1, i0_v, ra_v, ga).start()

            write(j, rb_v, b_hbm, wb).wait()

            @pl.when(j + 1 < nch)
            def _():
                gather(j + 1, i1_v, rb_v, gb).start()

    return k(ys, pos0, pos1)


def _expert_kernel(te_ref, tv_ref, tsrc_ref, tr_ref, xs_ref, w1_ref, w3_ref, w2_ref,
                   ys_ref, *, sub):
    j = pl.program_id(0)
    half = xs_ref.shape[0] // 2

    def run(rows):
        x = _unpack_bf16_pairs(xs_ref[rows, :])
        row = lax.broadcasted_iota(jnp.int32, x.shape, 0)
        x = jnp.where(row < tr_ref[j], x, 0.0).astype(BF16)
        ys_ref[rows, :] = _pack_bf16_pairs(
            _swiglu(x, w1_ref.at[0], w3_ref.at[0], w2_ref.at[0], sub))

    @pl.when(tr_ref[j] > half)
    def _():
        run(slice(None))

    @pl.when((tr_ref[j] > 0) & (tr_ref[j] <= half))
    def _():
        run(slice(0, half))
        ys_ref[half:, :] = jnp.zeros((half, ys_ref.shape[1]), ys_ref.dtype)

    @pl.when(tr_ref[j] == 0)
    def _():
        ys_ref[...] = jnp.zeros_like(ys_ref)


def _experts(xs, w1, w3, w2, rt, tile):
    rows, wp = xs.shape
    ne, d, dff = w1.shape
    return pl.pallas_call(
        functools.partial(_expert_kernel, sub=512),
        grid_spec=pltpu.PrefetchScalarGridSpec(
            num_scalar_prefetch=4,
            grid=(rt['n_tiles'],),
            in_specs=[pl.BlockSpec((tile, wp), lambda j, te, tv, tsrc, tr: (tsrc[j], 0)),
                      pl.BlockSpec((1, d, dff), lambda j, te, tv, tsrc, tr: (te[j], 0, 0)),
                      pl.BlockSpec((1, d, dff), lambda j, te, tv, tsrc, tr: (te[j], 0, 0)),
                      pl.BlockSpec((1, dff, d), lambda j, te, tv, tsrc, tr: (te[j], 0, 0))],
            out_specs=pl.BlockSpec((tile, wp), lambda j, te, tv, tsrc, tr: (j, 0))),
        out_shape=jax.ShapeDtypeStruct((rows, wp), jnp.int32),
        compiler_params=_cparams("arbitrary"),
        name="experts",
    )(rt['tile_expert'], rt['tile_valid'], rt['tile_src'], rt['tile_rows'], xs, w1, w3, w2)


def _combine_kernel(a_ref, b_ref, gate_ref, x_ref, mod_ref, fg_ref, o_ref, *, final_norm):
    m = mod_ref[0]
    g = gate_ref[...]
    y = g[:, 0:1] * _unpack_bf16_pairs(a_ref[...]) + g[:, 1:2] * _unpack_bf16_pairs(b_ref[...])
    xn = x_ref[...] + m[5:6] * y
    if final_norm:
        ms = jnp.mean(xn * xn, axis=-1, keepdims=True)
        xn = (xn * lax.rsqrt(ms + EPS)) * fg_ref[...]
    o_ref[...] = xn


def _combine(a, b, gates, x, mod, fg, seq, tm, final_norm, tile0):
    n, d = x.shape
    tps = seq // tm
    part = lambda w: pl.BlockSpec((tm, w), lambda i: (i, 0))
    full = lambda w: pl.BlockSpec((tm, w), lambda i: (i + tile0, 0))
    return pl.pallas_call(
        functools.partial(_combine_kernel, final_norm=final_norm),
        grid=(a.shape[0] // tm,),
        in_specs=[part(d // 2), part(d // 2), full(TOP_K), full(d),
                  pl.BlockSpec((1, 6, d), lambda i: ((i + tile0) // tps, 0, 0)),
                  pl.BlockSpec((1, d), lambda i: (0, 0))],
        out_specs=full(d),
        out_shape=jax.ShapeDtypeStruct((n, d), F32),
        input_output_aliases={3: 0},
        compiler_params=_cparams("arbitrary"),
        name="combine",
    )(a, b, gates, x, mod, fg)


def _pick(n, prefs):
    for p in prefs:
        if n % p == 0:
            return p
    return n


def kernel(x, c, ada_w, ada_b, norm_g, ssm_in, ssm_log_dt, ssm_lam_re, ssm_lam_im,
           ssm_b_re, ssm_b_im, ssm_c_re, ssm_c_im, ssm_d, ssm_glu, ssm_out,
           pool_in, pool_mix, pool_scale, pool_out, ffn_w1, ffn_w3, ffn_w2,
           router_w, router_b, moe_w1, moe_w3, moe_w2, final_g):
    bsz, seq, d = x.shape
    depth = ada_w.shape[0]
    n = bsz * seq
    ne = router_w.shape[-1]
    g_, p_, h_ = ssm_b_re.shape[1:]
    t = SSM_T
    if seq % (LANES * t // bsz) != 0 or bsz != 8:
        raise NotImplementedError("state-space kernels assume batch 8 and seq % 256 == 0")
    if depth % 2 != 0:
        raise NotImplementedError("the final RMSNorm is fused into an expert layer's combine")
    tm = _pick(seq, (512, 256, 128, 64, 32, 16))
    tile = _pick(TOP_K * n, (512, 256, 128, 64, 32, 16))
    gb = _pick(g_, (4, 2, 1))

    stack = lambda w: w.astype(BF16).reshape((-1,) + w.shape[2:])
    ew1, ew3, ew2 = stack(moe_w1), stack(moe_w3), stack(moe_w2)
    fw1, fw3, fw2 = ffn_w1.astype(BF16), ffn_w3.astype(BF16), ffn_w2.astype(BF16)
    sw_in, sw_glu, sw_out = ssm_in.astype(BF16), ssm_glu.astype(BF16), ssm_out.astype(BF16)
    pw_in, pw_mix, pw_out = pool_in.astype(BF16), pool_mix.astype(BF16), pool_out.astype(BF16)
    mod = _ada(c, ada_w, ada_b).reshape(depth, bsz, 6, d)
    xf = x.reshape(n, d)
    fg = final_g.reshape(1, d)
    for i in range(depth):
        j = i // 2
        mod_i = mod[i]
        g_a = norm_g[i, 0].reshape(1, d)
        g_b = norm_g[i, 1].reshape(1, d)
        if i % 2 == 0:
            ops = _ssm_operators(ssm_log_dt[j], ssm_lam_re[j], ssm_lam_im[j], ssm_b_re[j],
                                 ssm_b_im[j], ssm_c_re[j], ssm_c_im[j], ssm_d[j], t)
            ut = _ssm_in(xf.reshape(bsz, seq, d), mod_i, g_a, sw_in, t, h_, 4, layer=j)
            yt = _ssm(ut, *ops, bsz, gb)
            xf = _glu_out(yt, xf.reshape(bsz, seq, d), mod_i, sw_glu, sw_out, t, 4,
                          layer=j).reshape(n, d)
            xf = _ffn(xf, mod_i, g_b, fw1, fw3, fw2, seq, tm, layer=j)
        else:
            xf, h, idx, gates, rank, cnt = _pool(
                xf, mod_i, g_a, pw_in, pw_mix, pool_scale[j].reshape(1, d), pw_out,
                g_b, router_w[j], router_b[j].reshape(1, ne), seq, 2 * tm, tm, layer=j)
            rt = _routing_tables(idx, rank, cnt[0], ne, tile)
            pos0, pos1 = rt['pos'][:, 0], rt['pos'][:, 1]
            xs = _dispatch(h, pos0, pos1, rt['n_tiles'] * tile)
            rt['tile_expert'] = rt['tile_expert'] + j * ne
            ys = _experts(xs, ew1, ew3, ew2, rt, tile)
            npc = n // COMBINE_PIECES
            for q in range(COMBINE_PIECES):
                piece = slice(q * npc, (q + 1) * npc)
                ya, yb = _gather_pairs(ys, pos0[piece], pos1[piece])
                xf = _combine(ya, yb, gates, xf, mod_i, fg, seq, 2 * tm,
                              final_norm=(i == depth - 1), tile0=q * (npc // (2 * tm)))
    return xf.reshape(bsz, seq, d)
```

```python
import functools
import math

import jax
import jax.numpy as jnp
from jax import lax
from jax.experimental import pallas as pl
from jax.experimental.pallas import tpu as pltpu
from jax.experimental.pallas import tpu_sc as plsc

F32 = jnp.float32
BF16 = jnp.bfloat16
EPS = 1e-6
POOL_WINDOWS = (2, 4, 8, 16)
POOL_HALO = 16
SSM_T = 16
LANES = 128
TOP_K = 2
SC_CORES, SC_SUBCORES = 2, 16
SC_WORKERS = SC_CORES * SC_SUBCORES
SC_ROWS = 64
COMBINE_PIECES = 2
VMEM_LIMIT = 58 * 1024 * 1024


def _cparams(*sem):
    return pltpu.CompilerParams(dimension_semantics=sem, vmem_limit_bytes=VMEM_LIMIT)


def _normmod(x, g, shift, scale):
    ms = jnp.mean(x * x, axis=-1, keepdims=True)
    y = x * lax.rsqrt(ms + EPS)
    return (y * g) * (1.0 + scale) + shift


def _bdot(a, b):
    return jnp.dot(a, b, preferred_element_type=F32)


def _ada_kernel(c_ref, w_ref, b_ref, o_ref):
    c = c_ref[...]
    cond = c * jax.nn.sigmoid(c)
    o_ref[0] = jnp.dot(cond, w_ref[0], precision=lax.Precision.HIGHEST,
                       preferred_element_type=F32) + b_ref[0]


def _ada(c, ada_w, ada_b):
    depth, d, d6 = ada_w.shape
    bsz = c.shape[0]
    tn = d6 // 4
    return pl.pallas_call(
        _ada_kernel,
        grid=(depth, d6 // tn),
        in_specs=[pl.BlockSpec((bsz, d), lambda l, j: (0, 0)),
                  pl.BlockSpec((1, d, tn), lambda l, j: (l, 0, j)),
                  pl.BlockSpec((1, 1, tn), lambda l, j: (l, 0, j))],
        out_specs=pl.BlockSpec((1, bsz, tn), lambda l, j: (l, 0, j)),
        out_shape=jax.ShapeDtypeStruct((depth, bsz, d6), F32),
        compiler_params=_cparams("arbitrary", "arbitrary"),
        name="ada",
    )(c, ada_w, ada_b.reshape(depth, 1, d6))


def _ssm_operators(log_dt, lam_re, lam_im, b_re, b_im, c_re, c_im, d_skip, t):
    g_, p_, h_ = b_re.shape
    dt = jnp.exp(log_dt.astype(F32))[:, None]
    lr = lam_re.astype(F32)
    li = lam_im.astype(F32)
    mag = jnp.exp(lr * dt)
    a_re = mag * jnp.cos(li * dt)
    a_im = mag * jnp.sin(li * dt)
    den = lr * lr + li * li
    nr = a_re - 1
    coef_re = (nr * lr + a_im * li) / den
    coef_im = (a_im * lr - nr * li) / den
    br = b_re.astype(F32)
    bi = b_im.astype(F32)
    bbar_re = coef_re[..., None] * br - coef_im[..., None] * bi
    bbar_im = coef_re[..., None] * bi + coef_im[..., None] * br
    steps = jnp.arange(0, t + 1, dtype=F32)[None, :, None]
    ang = (li * dt)[:, None, :] * steps
    pmag = jnp.exp((lr * dt)[:, None, :] * steps)
    pw_re = pmag * jnp.cos(ang)
    pw_im = pmag * jnp.sin(ang)
    cr = c_re.astype(F32)
    ci = c_im.astype(F32)
    er = cr[:, None] * pw_re[:, :, None, :] - ci[:, None] * pw_im[:, :, None, :]
    ei = cr[:, None] * pw_im[:, :, None, :] + ci[:, None] * pw_re[:, :, None, :]
    tg = lambda a: jnp.transpose(a, (1, 2, 0))
    pt_re, pt_im = tg(pw_re[:, :t])[:, None], tg(pw_im[:, :t])[:, None]
    ct_re, ct_im = tg(cr)[None], tg(ci)[None]
    er_l = (ct_re * pt_re - ct_im * pt_im)[..., None]
    ei_l = (ct_re * pt_im + ct_im * pt_re)[..., None]
    br_l = jnp.transpose(bbar_re, (1, 0, 2))[None, None]
    bi_l = jnp.transpose(bbar_im, (1, 0, 2))[None, None]
    kk = jnp.sum(er_l * br_l - ei_l * bi_l, axis=2)
    kk = jnp.transpose(kk, (2, 0, 1, 3))
    kz = jnp.pad(kk, ((0, 0), (t - 1, 0), (0, 0), (0, 0)))
    cols = [kz[:, t - 1 - j:2 * t - 1 - j] for j in range(t)]
    mt = jnp.stack(cols, axis=3).reshape(g_, t * h_, t * h_)
    rev_re = pw_re[:, :t][:, ::-1, None, :]
    rev_im = pw_im[:, :t][:, ::-1, None, :]
    bt_re = jnp.transpose(bbar_re, (0, 2, 1))[:, None]
    bt_im = jnp.transpose(bbar_im, (0, 2, 1))[:, None]
    w_re = (rev_re * bt_re - rev_im * bt_im).reshape(g_, t * h_, p_)
    w_im = (rev_re * bt_im + rev_im * bt_re).reshape(g_, t * h_, p_)
    w4 = jnp.concatenate([w_re, w_im, w_im, w_re], axis=-1)
    vt = jnp.concatenate([er[:, 1:].reshape(g_, t * h_, p_),
                          -ei[:, 1:].reshape(g_, t * h_, p_)], axis=-1)
    at_re, at_im = pw_re[:, t], pw_im[:, t]
    c1 = jnp.concatenate([at_re, at_re], axis=-1)
    c2 = jnp.concatenate([-at_im, at_im], axis=-1)
    coef = jnp.concatenate([c1[:, None], c2[:, None], jnp.zeros((g_, 6, 2 * p_), F32)], axis=1)
    dk = jnp.tile(d_skip.astype(F32)[:, None, :], (1, t, 1)).reshape(g_, t * h_, 1)
    return mt.astype(BF16), w4.astype(BF16), vt.astype(BF16), coef, dk


def _ssm_in_kernel(x_ref, mod_ref, g_ref, w_ref, ut_ref, h_scr, u_scr, *, t):
    q = pl.program_id(1)
    bsz, cs, _ = x_ref.shape
    nlt = u_scr.shape[0]
    for b in range(bsz):
        m = mod_ref[b]
        h = _normmod(x_ref[b], g_ref[...], m[0:1], m[1:2]).astype(BF16)
        for c in range(cs // t):
            h_scr[(c * bsz + b) * t:(c * bsz + b + 1) * t, :] = h[c * t:(c + 1) * t, :]
    u = _bdot(h_scr[...], w_ref[0])
    rows = pl.ds(pl.multiple_of(q * (bsz * cs), bsz * cs), bsz * cs)
    for l in range(nlt):
        u_scr[l, rows, :] = u[:, l * LANES:(l + 1) * LANES]

    @pl.when(q == pl.num_programs(1) - 1)
    def _():
        ng, th, nch = ut_ref.shape
        hh = th // t
        gl = LANES // hh
        for l in range(nlt):
            uq = pltpu.einshape("ckl->kcl", u_scr[l].reshape(nch, t, LANES))
            for k in range(t):
                ut_ref[l * gl:(l + 1) * gl, k * hh:(k + 1) * hh, :] = (
                    uq[k].T.reshape(gl, hh, nch).astype(BF16))


def _ssm_in(x, mod, g, w, t, hh, nq, layer):
    bsz, seq, d = x.shape
    ng = d // hh
    cs = LANES * t // bsz // nq
    return pl.pallas_call(
        functools.partial(_ssm_in_kernel, t=t),
        grid=(seq // (cs * nq), nq),
        in_specs=[pl.BlockSpec((bsz, cs, d), lambda o, q: (0, o * nq + q, 0)),
                  pl.BlockSpec((bsz, 6, d), lambda o, q: (0, 0, 0)),
                  pl.BlockSpec((1, d), lambda o, q: (0, 0)),
                  pl.BlockSpec((1, d, d), lambda o, q: (layer, 0, 0))],
        out_specs=pl.BlockSpec((ng, t * hh, LANES), lambda o, q: (0, 0, o)),
        out_shape=jax.ShapeDtypeStruct((ng, t * hh, bsz * seq // t), BF16),
        scratch_shapes=[pltpu.VMEM((bsz * cs, d), BF16),
                        pltpu.VMEM((d // LANES, LANES * t, LANES), F32)],
        compiler_params=_cparams("arbitrary", "arbitrary"),
        name="ssm_in",
    )(x, mod, g, w)


def _ssm_kernel(ut_ref, mt_ref, w4_ref, vt_ref, coef_ref, dk_ref, yt_ref, z_scr, s_scr,
                *, gb, nchunks, bsz):
    p2 = s_scr.shape[-1]
    for gi in range(gb):
        u = ut_ref[gi].astype(F32).T.astype(BF16)
        z = _bdot(u, w4_ref[gi])
        z_scr[gi, 0] = z[:, :p2]
        z_scr[gi, 1] = z[:, p2:]
    c1 = [jnp.broadcast_to(coef_ref[gi, 0:1, :], (bsz, p2)) for gi in range(gb)]
    c2 = [jnp.broadcast_to(coef_ref[gi, 1:2, :], (bsz, p2)) for gi in range(gb)]

    def body(c, carry):
        new = []
        for gi in range(gb):
            s1, s2 = carry[gi]
            rows = pl.ds(pl.multiple_of(c * bsz, bsz), bsz)
            s_scr[gi, rows, :] = s1
            n1 = c1[gi] * s1 + c2[gi] * s2 + z_scr[gi, 0, rows, :]
            n2 = c1[gi] * s2 - c2[gi] * s1 + z_scr[gi, 1, rows, :]
            new.append((n1, n2))
        return tuple(new)

    zero = jnp.zeros((bsz, p2), F32)
    lax.fori_loop(0, nchunks, body, tuple((zero, zero) for _ in range(gb)))
    for gi in range(gb):
        ut = ut_ref[gi]
        st = _bdot(vt_ref[gi], s_scr[gi].T.astype(BF16))
        yt = _bdot(mt_ref[gi], ut) + st + dk_ref[gi] * ut.astype(F32)
        yt_ref[gi] = yt.astype(yt_ref.dtype)


def _ssm(ut, mt, w4, vt, coef, dk, bsz, gb):
    g_, th, nch = ut.shape
    p2 = vt.shape[2]
    kern = functools.partial(_ssm_kernel, gb=gb, nchunks=nch // bsz, bsz=bsz)
    spec3 = lambda a, b: pl.BlockSpec((gb, a, b), lambda i: (i, 0, 0))
    return pl.pallas_call(
        kern,
        grid=(g_ // gb,),
        in_specs=[spec3(th, nch), spec3(th, th), spec3(th, 2 * p2), spec3(th, p2),
                  spec3(8, p2), spec3(th, 1)],
        out_specs=spec3(th, nch),
        out_shape=jax.ShapeDtypeStruct((g_, th, nch), BF16),
        scratch_shapes=[pltpu.VMEM((gb, 2, nch, p2), F32), pltpu.VMEM((gb, nch, p2), F32)],
        compiler_params=_cparams("arbitrary"),
        name="ssm",
    )(ut, mt, w4, vt, coef, dk)


def _gelu_tanh(y):
    return y * (0.5 * (1.0 + jnp.tanh(math.sqrt(2.0 / math.pi) * (y + 0.044715 * (y * y * y)))))


def _glu_out_kernel(yt_ref, x_ref, mod_ref, wg_ref, wo_ref, o_ref, y_scr, *, t):
    q = pl.program_id(1)
    bsz, cs, _ = x_ref.shape
    nlt = y_scr.shape[0]

    @pl.when(q == 0)
    def _():
        ng, th, nch = yt_ref.shape
        hh = th // t
        gl = LANES // hh
        for l in range(nlt):
            parts = [yt_ref[l * gl:(l + 1) * gl, k * hh:(k + 1) * hh, :].astype(F32)
                     .reshape(LANES, nch).T for k in range(t)]
            yq = pltpu.einshape("kcl->ckl", jnp.stack(parts, axis=0))
            y_scr[l] = yq.reshape(nch * t, LANES)

    rows = pl.ds(pl.multiple_of(q * (bsz * cs), bsz * cs), bsz * cs)
    y = jnp.concatenate([y_scr[l, rows, :] for l in range(nlt)], axis=1)
    z = _gelu_tanh(y)
    z = z * jax.nn.sigmoid(_bdot(z.astype(BF16), wg_ref[0]))
    o = _bdot(z.astype(BF16), wo_ref[0])
    for b in range(bsz):
        gate = mod_ref[b][2:3]
        for c in range(cs // t):
            tok = slice(c * t, (c + 1) * t)
            o_ref[b, tok, :] = x_ref[b, tok, :] + gate * o[(c * bsz + b) * t:(c * bsz + b + 1) * t]


def _glu_out(yt, x, mod, wg, wo, t, nq, layer):
    bsz, seq, d = x.shape
    ng, th, _ = yt.shape
    cs = LANES * t // bsz // nq
    return pl.pallas_call(
        functools.partial(_glu_out_kernel, t=t),
        grid=(seq // (cs * nq), nq),
        in_specs=[pl.BlockSpec((ng, th, LANES), lambda o, q: (0, 0, o)),
                  pl.BlockSpec((bsz, cs, d), lambda o, q: (0, o * nq + q, 0)),
                  pl.BlockSpec((bsz, 6, d), lambda o, q: (0, 0, 0)),
                  pl.BlockSpec((1, d, d), lambda o, q: (layer, 0, 0)),
                  pl.BlockSpec((1, d, d), lambda o, q: (layer, 0, 0))],
        out_specs=pl.BlockSpec((bsz, cs, d), lambda o, q: (0, o * nq + q, 0)),
        out_shape=jax.ShapeDtypeStruct((bsz, seq, d), F32),
        scratch_shapes=[pltpu.VMEM((d // LANES, LANES * t, LANES), F32)],
        compiler_params=_cparams("arbitrary", "arbitrary"),
        name="glu_out",
    )(yt, x, mod, wg, wo)


def _chunks(total, step):
    return [(s, min(step, total - s)) for s in range(0, total, step)]


def _swiglu(h, w1_ref, w3_ref, w2_ref, sub):
    total = None
    for s, n in _chunks(w1_ref.shape[-1], sub):
        a = _bdot(h, w1_ref[:, s:s + n])
        b = _bdot(h, w3_ref[:, s:s + n])
        act = (a * jax.nn.sigmoid(a) * b).astype(BF16)
        y = _bdot(act, w2_ref[s:s + n, :])
        total = y if total is None else total + y
    return total


def _ffn_kernel(x_ref, mod_ref, g_ref, w1_ref, w3_ref, w2_ref, o_ref, *, sub):
    m = mod_ref[0]
    x = x_ref[...]
    h = _normmod(x, g_ref[...], m[3:4], m[4:5]).astype(BF16)
    o_ref[...] = x + m[5:6] * _swiglu(h, w1_ref.at[0], w3_ref.at[0], w2_ref.at[0], sub)


def _ffn(x, mod, g, w1, w3, w2, seq, tm, layer):
    n, d = x.shape
    dff = w1.shape[2]
    tps = seq // tm
    const = lambda i: (0, 0)
    lay = lambda i: (layer, 0, 0)
    return pl.pallas_call(
        functools.partial(_ffn_kernel, sub=256),
        grid=(n // tm,),
        in_specs=[pl.BlockSpec((tm, d), lambda i: (i, 0)),
                  pl.BlockSpec((1, 6, d), lambda i: (i // tps, 0, 0)),
                  pl.BlockSpec((1, d), const),
                  pl.BlockSpec((1, d, dff), lay, pipeline_mode=pl.Buffered(1)),
                  pl.BlockSpec((1, d, dff), lay, pipeline_mode=pl.Buffered(1)),
                  pl.BlockSpec((1, dff, d), lay, pipeline_mode=pl.Buffered(1))],
        out_specs=pl.BlockSpec((tm, d), lambda i: (i, 0)),
        out_shape=jax.ShapeDtypeStruct((n, d), F32),
        compiler_params=_cparams("arbitrary"),
        name="ffn",
    )(x, mod, g, w1, w3, w2)


def _pool_kernel(x_ref, halo_ref, mod_ref, g_ref, win_ref, wmix_ref, scale_ref, wo_ref,
                 g2_ref, rw_ref, rb_ref, o_ref, h_ref, idx_ref, gate_ref, rank_ref, cnt_ref,
                 ext_ref, z_ref, base_scr, before_scr, *, tps):
    tm, d = x_ref.shape
    pg = d // len(POOL_WINDOWS)
    m = mod_ref[0]
    it = pl.program_id(0) % tps

    @pl.when(pl.program_id(0) == 0)
    def _():
        base_scr[...] = jnp.zeros_like(base_scr)
        shape = before_scr.shape
        before = lax.broadcasted_iota(jnp.int32, shape, 1) < lax.broadcasted_iota(jnp.int32, shape, 0)
        before_scr[...] = jnp.where(before, 1.0, 0.0).astype(BF16)

    def project(rows):
        h = _normmod(rows, g_ref[...], m[0:1], m[1:2])
        return _bdot(h.astype(BF16), win_ref[0])

    ext_ref[0:POOL_HALO, :] = jnp.where(it == 0, 0.0, project(halo_ref[...]))
    ext_ref[POOL_HALO:, :] = project(x_ref[...])
    tpos = (it * tm + 1 + lax.broadcasted_iota(jnp.int32, (tm, 1), 0)).astype(F32)
    for gi, w in enumerate(POOL_WINDOWS):
        cols = slice(gi * pg, (gi + 1) * pg)
        s = ext_ref[:, cols]
        span = 1
        while span < w:
            s = s + pltpu.roll(s, span, 0)
            span *= 2
        mean = s[POOL_HALO:, :] * (1.0 / jnp.minimum(tpos, float(w)))
        dlt = (mean - ext_ref[POOL_HALO:, cols]).astype(BF16)
        z_ref[:, cols] = (_bdot(dlt, wmix_ref[0, gi]) * scale_ref[:, cols]).astype(BF16)
    xn = x_ref[...] + m[2:3] * _bdot(z_ref[...], wo_ref[0])
    o_ref[...] = xn
    rt = before_scr.shape[0]
    for r in range(tm // rt):
        rows = slice(r * rt, (r + 1) * rt)
        hp, idx, gate, rank, total = _route_rows(
            xn[rows], g2_ref[...], m[3:4], m[4:5], rw_ref[...], rb_ref[...],
            before_scr[...], base_scr[...])
        h_ref[rows, :] = hp
        idx_ref[rows, :] = idx
        gate_ref[rows, :] = gate
        rank_ref[rows, :] = rank
        base_scr[...] = total
    cnt_ref[...] = base_scr[...].astype(jnp.int32)


def _pool(x, mod, g, win, wmix, scale, wo, g2, rw, rb, seq, tm, rt, layer):
    n, d = x.shape
    tps = seq // tm
    hb = tm // POOL_HALO
    _, ng, pg, _ = wmix.shape
    ne = rw.shape[1]
    const = lambda i: (0, 0)
    lay = lambda i: (layer, 0, 0)
    row = lambda w: pl.BlockSpec((tm, w), lambda i: (i, 0))
    return pl.pallas_call(
        functools.partial(_pool_kernel, tps=tps),
        grid=(n // tm,),
        in_specs=[row(d),
                  pl.BlockSpec((POOL_HALO, d), lambda i: (jnp.maximum(i * hb - 1, 0), 0)),
                  pl.BlockSpec((1, 6, d), lambda i: (i // tps, 0, 0)),
                  pl.BlockSpec((1, d), const),
                  pl.BlockSpec((1, d, d), lay),
                  pl.BlockSpec((1, ng, pg, pg), lambda i: (layer, 0, 0, 0)),
                  pl.BlockSpec((1, d), const),
                  pl.BlockSpec((1, d, d), lay),
                  pl.BlockSpec((1, d), const),
                  pl.BlockSpec((d, ne), const),
                  pl.BlockSpec((1, ne), const)],
        out_specs=[row(d), row(d // 2), row(TOP_K), row(TOP_K), row(TOP_K),
                   pl.BlockSpec((1, ne), const)],
        out_shape=[jax.ShapeDtypeStruct((n, d), F32),
                   jax.ShapeDtypeStruct((n, d // 2), jnp.int32),
                   jax.ShapeDtypeStruct((n, TOP_K), jnp.int32),
                   jax.ShapeDtypeStruct((n, TOP_K), F32),
                   jax.ShapeDtypeStruct((n, TOP_K), jnp.int32),
                   jax.ShapeDtypeStruct((1, ne), jnp.int32)],
        scratch_shapes=[pltpu.VMEM((tm + POOL_HALO, d), F32), pltpu.VMEM((tm, d), BF16),
                        pltpu.VMEM((1, ne), F32), pltpu.VMEM((rt, rt), BF16)],
        compiler_params=_cparams("arbitrary"),
        name="pool_router",
    )(x, x, mod, g, win, wmix, scale, wo, g2, rw, rb)


def _pack_bf16_pairs(v):
    w = v.shape[1] // 2
    bits = lambda a: lax.bitcast_convert_type(a.astype(BF16).astype(F32), jnp.int32)
    return bits(v[:, :w]) | lax.shift_right_logical(bits(v[:, w:]), 16)


def _unpack_bf16_pairs(p):
    hi = lax.bitcast_convert_type(p & jnp.int32(-65536), F32)
    lo = lax.bitcast_convert_type(lax.shift_left(p, 16), F32)
    return jnp.concatenate([hi, lo], axis=1)


def _route_rows(x, g, shift, scale, rw, rb, before, base):
    h = _normmod(x, g, shift, scale)
    hp = _pack_bf16_pairs(h)
    split = lambda a: (a.astype(BF16), (a - a.astype(BF16).astype(F32)).astype(BF16))
    h_hi, h_lo = split(h)
    w_hi, w_lo = split(rw)
    logits = (_bdot(h_hi, w_hi) + _bdot(h_hi, w_lo) + _bdot(h_lo, w_hi)) + rb
    ne = float(logits.shape[1])
    lane = lax.broadcasted_iota(jnp.int32, logits.shape, 1).astype(F32)
    v1 = jnp.max(logits, axis=1, keepdims=True)
    i1 = jnp.min(jnp.where(logits == v1, lane, ne), axis=1, keepdims=True)
    rest = jnp.where(lane == i1, -jnp.inf, logits)
    v2 = jnp.max(rest, axis=1, keepdims=True)
    i2 = jnp.min(jnp.where(rest == v2, lane, ne), axis=1, keepdims=True)
    e2 = jnp.exp(v2 - v1)
    den = 1.0 + e2
    two = lax.broadcasted_iota(jnp.int32, (logits.shape[0], TOP_K), 1)
    idx = jnp.where(two == 0, i1, i2).astype(jnp.int32)
    gate = jnp.where(two == 0, 1.0 / den, e2 / den)
    oh1 = jnp.where(lane == i1, 1.0, 0.0)
    oh2 = jnp.where(lane == i2, 1.0, 0.0)
    both = oh1 + oh2
    prior = _bdot(before, both.astype(BF16)) + base
    r1 = jnp.sum(oh1 * prior, axis=1, keepdims=True)
    r2 = jnp.sum(oh2 * prior, axis=1, keepdims=True)
    rank = jnp.where(two == 0, r1, r2).astype(jnp.int32)
    return hp, idx, gate, rank, base + jnp.sum(both, axis=0, keepdims=True)


def _count_le(sorted_vals, queries):
    return jnp.sum((sorted_vals[None, :] <= queries[:, None]).astype(jnp.int32), axis=1)


def _routing_tables(idx, rank, cnt, ne, tile):
    n = idx.shape[0]
    padded = ((cnt + tile - 1) // tile) * tile
    off = jnp.cumsum(padded) - padded
    onehot = idx[:, :, None] == jnp.arange(ne, dtype=jnp.int32)[None, None, :]
    pos = rank + jnp.sum(jnp.where(onehot, off[None, None, :], 0), axis=2)
    n_tiles = (TOP_K * n) // tile + ne
    used = jnp.sum(padded) // tile
    tile_ids = jnp.arange(n_tiles, dtype=jnp.int32)
    tile_expert = _count_le(jnp.cumsum(padded), tile_ids * tile)
    last_used = jnp.maximum(used - 1, 0)
    tile_valid = (tile_ids < used).astype(jnp.int32)
    tile_src = jnp.minimum(tile_ids, last_used).astype(jnp.int32)
    tile_expert = jnp.minimum(tile_expert, ne - 1).astype(jnp.int32)
    tile_expert = jnp.where(tile_valid == 1, tile_expert, tile_expert[last_used])
    tile_rows = jnp.clip(cnt[tile_expert] - (tile_ids * tile - off[tile_expert]), 0, tile)
    tile_rows = jnp.where(tile_valid == 1, tile_rows, 0).astype(jnp.int32)
    return dict(pos=pos, tile_expert=tile_expert, tile_valid=tile_valid, tile_src=tile_src,
                tile_rows=tile_rows, n_tiles=n_tiles)


def _sc_mesh():
    return plsc.VectorSubcoreMesh(core_axis_name="c", subcore_axis_name="s",
                                  num_cores=SC_CORES, num_subcores=SC_SUBCORES)


def _dispatch(hp, pos0, pos1, rows_out):
    n, w = hp.shape
    per_w = n // SC_WORKERS
    ch = SC_ROWS
    nch = per_w // ch
    assert per_w * SC_WORKERS == n and nch * ch == per_w and nch % 2 == 0

    @functools.partial(
        pl.kernel, mesh=_sc_mesh(),
        out_type=jax.ShapeDtypeStruct((rows_out, w), hp.dtype),
        scratch_types=[pltpu.VMEM((per_w,), jnp.int32), pltpu.VMEM((per_w,), jnp.int32),
                       pltpu.VMEM((ch, w), hp.dtype), pltpu.VMEM((ch, w), hp.dtype)]
        + [pltpu.SemaphoreType.DMA] * 4,
        name="dispatch",
    )
    def k(hp_hbm, p0_hbm, p1_hbm, out_hbm, i0_v, i1_v, r0_v, r1_v, l0, l1, s0, s1):
        base = (lax.axis_index("s") * SC_CORES + lax.axis_index("c")) * per_w
        pltpu.sync_copy(p0_hbm.at[pl.ds(base, per_w)], i0_v)
        pltpu.sync_copy(p1_hbm.at[pl.ds(base, per_w)], i1_v)
        bufs, lsem, ssem = (r0_v, r1_v), (l0, l1), (s0, s1)

        def load(j, b):
            return pltpu.make_async_copy(hp_hbm.at[pl.ds(base + j * ch, ch)], bufs[b], lsem[b])

        def scatter(j, b, idx):
            return pltpu.make_async_copy(bufs[b], out_hbm.at[idx.at[pl.ds(j * ch, ch)]], ssem[b])

        load(0, 0).start()

        @pl.loop(0, nch, step=2)
        def _(j):
            for b in range(2):
                jj = j + b
                load(jj, b).wait()

                @pl.when(jj + 1 < nch)
                def _():
                    load(jj + 1, 1 - b).start()

                c0 = scatter(jj, b, i0_v)
                c1 = scatter(jj, b, i1_v)
                c0.start()
                c1.start()
                c0.wait()
                c1.wait()

    return k(hp, pos0, pos1)


def _gather_pairs(ys, pos0, pos1):
    n = pos0.shape[0]
    w = ys.shape[1]
    per_w = n // SC_WORKERS
    ch = SC_ROWS
    nch = per_w // ch
    assert per_w * SC_WORKERS == n and nch * ch == per_w
    out = jax.ShapeDtypeStruct((n, w), ys.dtype)

    @functools.partial(
        pl.kernel, mesh=_sc_mesh(), out_type=(out, out),
        scratch_types=[pltpu.VMEM((per_w,), jnp.int32), pltpu.VMEM((per_w,), jnp.int32),
                       pltpu.VMEM((ch, w), ys.dtype), pltpu.VMEM((ch, w), ys.dtype)]
        + [pltpu.SemaphoreType.DMA] * 4,
        name="gather_pairs",
    )
    def k(ys_hbm, p0_hbm, p1_hbm, a_hbm, b_hbm, i0_v, i1_v, ra_v, rb_v, ga, gb, wa, wb):
        base = (lax.axis_index("s") * SC_CORES + lax.axis_index("c")) * per_w
        pltpu.sync_copy(p0_hbm.at[pl.ds(base, per_w)], i0_v)
        pltpu.sync_copy(p1_hbm.at[pl.ds(base, per_w)], i1_v)

        def gather(j, idx, buf, sem):
            return pltpu.make_async_copy(ys_hbm.at[idx.at[pl.ds(j * ch, ch)]], buf, sem)

        def write(j, buf, dst, sem):
            return pltpu.make_async_copy(buf, dst.at[pl.ds(base + j * ch, ch)], sem)

        gather(0, i0_v, ra_v, ga).start()
        gather(0, i1_v, rb_v, gb).start()

        @pl.loop(0, nch)
        def _(j):
            gather(j, i0_v, ra_v, ga).wait()
            write(j, ra_v, a_hbm, wa).start()
            gather(j, i1_v, rb_v, gb).wait()
            write(j, rb_v, b_hbm, wb).start()
            write(j, ra_v, a_hbm, wa).wait()

            @pl.when(j + 1 < nch)
            def _():
                gather(j + 1, i0_v, ra_v, ga).start()

            write(j, rb_v, b_hbm, wb).wait()

            @pl.when(j + 1 < nch)
            def _():
                gather(j + 1, i1_v, rb_v, gb).start()

    return k(ys, pos0, pos1)


def _expert_kernel(te_ref, tsrc_ref, tr_ref, xs_ref, w1_ref, w3_ref, w2_ref, ys_ref, *, sub):
    j = pl.program_id(0)
    half = xs_ref.shape[0] // 2

    def run(rows):
        x = _unpack_bf16_pairs(xs_ref[rows, :])
        row = lax.broadcasted_iota(jnp.int32, x.shape, 0)
        x = jnp.where(row < tr_ref[j], x, 0.0).astype(BF16)
        ys_ref[rows, :] = _pack_bf16_pairs(
            _swiglu(x, w1_ref.at[0], w3_ref.at[0], w2_ref.at[0], sub))

    @pl.when(tr_ref[j] > half)
    def _():
        run(slice(None))

    @pl.when((tr_ref[j] > 0) & (tr_ref[j] <= half))
    def _():
        run(slice(0, half))
        ys_ref[half:, :] = jnp.zeros((half, ys_ref.shape[1]), ys_ref.dtype)

    @pl.when(tr_ref[j] == 0)
    def _():
        ys_ref[...] = jnp.zeros_like(ys_ref)


def _experts(xs, w1, w3, w2, rt, tile):
    rows, wp = xs.shape
    ne, d, dff = w1.shape
    return pl.pallas_call(
        functools.partial(_expert_kernel, sub=512),
        grid_spec=pltpu.PrefetchScalarGridSpec(
            num_scalar_prefetch=3,
            grid=(rt['n_tiles'],),
            in_specs=[pl.BlockSpec((tile, wp), lambda j, te, tsrc, tr: (tsrc[j], 0)),
                      pl.BlockSpec((1, d, dff), lambda j, te, tsrc, tr: (te[j], 0, 0)),
                      pl.BlockSpec((1, d, dff), lambda j, te, tsrc, tr: (te[j], 0, 0)),
                      pl.BlockSpec((1, dff, d), lambda j, te, tsrc, tr: (te[j], 0, 0))],
            out_specs=pl.BlockSpec((tile, wp), lambda j, te, tsrc, tr: (j, 0))),
        out_shape=jax.ShapeDtypeStruct((rows, wp), jnp.int32),
        compiler_params=_cparams("arbitrary"),
        name="experts",
    )(rt['tile_expert'], rt['tile_src'], rt['tile_rows'], xs, w1, w3, w2)


def _combine_kernel(a_ref, b_ref, gate_ref, x_ref, mod_ref, fg_ref, o_ref, *, final_norm):
    m = mod_ref[0]
    g = gate_ref[...]
    y = g[:, 0:1] * _unpack_bf16_pairs(a_ref[...]) + g[:, 1:2] * _unpack_bf16_pairs(b_ref[...])
    xn = x_ref[...] + m[5:6] * y
    if final_norm:
        ms = jnp.mean(xn * xn, axis=-1, keepdims=True)
        xn = (xn * lax.rsqrt(ms + EPS)) * fg_ref[...]
    o_ref[...] = xn


def _combine(a, b, gates, x, mod, fg, seq, tm, final_norm, tile0):
    n, d = x.shape
    tps = seq // tm
    part = lambda w: pl.BlockSpec((tm, w), lambda i: (i, 0))
    full = lambda w: pl.BlockSpec((tm, w), lambda i: (i + tile0, 0))
    return pl.pallas_call(
        functools.partial(_combine_kernel, final_norm=final_norm),
        grid=(a.shape[0] // tm,),
        in_specs=[part(d // 2), part(d // 2), full(TOP_K), full(d),
                  pl.BlockSpec((1, 6, d), lambda i: ((i + tile0) // tps, 0, 0)),
                  pl.BlockSpec((1, d), lambda i: (0, 0))],
        out_specs=full(d),
        out_shape=jax.ShapeDtypeStruct((n, d), F32),
        input_output_aliases={3: 0},
        compiler_params=_cparams("arbitrary"),
        name="combine",
    )(a, b, gates, x, mod, fg)


def _pick(n, prefs):
    for p in prefs:
        if n % p == 0:
            return p
    return n


def kernel(x, c, ada_w, ada_b, norm_g, ssm_in, ssm_log_dt, ssm_lam_re, ssm_lam_im,
           ssm_b_re, ssm_b_im, ssm_c_re, ssm_c_im, ssm_d, ssm_glu, ssm_out,
           pool_in, pool_mix, pool_scale, pool_out, ffn_w1, ffn_w3, ffn_w2,
           router_w, router_b, moe_w1, moe_w3, moe_w2, final_g):
    bsz, seq, d = x.shape
    depth = ada_w.shape[0]
    n = bsz * seq
    ne = router_w.shape[-1]
    g_, p_, h_ = ssm_b_re.shape[1:]
    t = SSM_T
    if seq % (LANES * t // bsz) != 0 or bsz != 8:
        raise NotImplementedError("state-space kernels assume batch 8 and seq % 256 == 0")
    if depth % 2 != 0:
        raise NotImplementedError("the final RMSNorm is fused into an expert layer's combine")
    tm = _pick(seq, (512, 256, 128, 64, 32, 16))
    tile = _pick(TOP_K * n, (512, 256, 128, 64, 32, 16))
    gb = _pick(g_, (4, 2, 1))

    stack = lambda w: w.astype(BF16).reshape((-1,) + w.shape[2:])
    ew1, ew3, ew2 = stack(moe_w1), stack(moe_w3), stack(moe_w2)
    fw1, fw3, fw2 = ffn_w1.astype(BF16), ffn_w3.astype(BF16), ffn_w2.astype(BF16)
    sw_in, sw_glu, sw_out = ssm_in.astype(BF16), ssm_glu.astype(BF16), ssm_out.astype(BF16)
    pw_in, pw_mix, pw_out = pool_in.astype(BF16), pool_mix.astype(BF16), pool_out.astype(BF16)
    mod = _ada(c, ada_w, ada_b).reshape(depth, bsz, 6, d)
    xf = x.reshape(n, d)
    fg = final_g.reshape(1, d)
    for i in range(depth):
        j = i // 2
        mod_i = mod[i]
        g_a = norm_g[i, 0].reshape(1, d)
        g_b = norm_g[i, 1].reshape(1, d)
        if i % 2 == 0:
            ops = _ssm_operators(ssm_log_dt[j], ssm_lam_re[j], ssm_lam_im[j], ssm_b_re[j],
                                 ssm_b_im[j], ssm_c_re[j], ssm_c_im[j], ssm_d[j], t)
            ut = _ssm_in(xf.reshape(bsz, seq, d), mod_i, g_a, sw_in, t, h_, 4, layer=j)
            yt = _ssm(ut, *ops, bsz, gb)
            xf = _glu_out(yt, xf.reshape(bsz, seq, d), mod_i, sw_glu, sw_out, t, 4,
                          layer=j).reshape(n, d)
            xf = _ffn(xf, mod_i, g_b, fw1, fw3, fw2, seq, tm, layer=j)
        else:
            xf, h, idx, gates, rank, cnt = _pool(
                xf, mod_i, g_a, pw_in, pw_mix, pool_scale[j].reshape(1, d), pw_out,
                g_b, router_w[j], router_b[j].reshape(1, ne), seq, 2 * tm, tm, layer=j)
            rt = _routing_tables(idx, rank, cnt[0], ne, tile)
            pos0, pos1 = rt['pos'][:, 0], rt['pos'][:, 1]
            xs = _dispatch(h, pos0, pos1, rt['n_tiles'] * tile)
            rt['tile_expert'] = rt['tile_expert'] + j * ne
            ys = _experts(xs, ew1, ew3, ew2, rt, tile)
            npc = n // COMBINE_PIECES
            for q in range(COMBINE_PIECES):
                piece = slice(q * npc, (q + 1) * npc)
                ya, yb = _gather_pairs(ys, pos0[piece], pos1[piece])
                xf = _combine(ya, yb, gates, xf, mod_i, fg, seq, 2 * tm,
                              final_norm=(i == depth - 1), tile0=q * (npc // (2 * tm)))
    return xf.reshape(bsz, seq, d)
```

```python
import functools
import math

import jax
import jax.numpy as jnp
from jax import lax
from jax.experimental import pallas as pl
from jax.experimental.pallas import tpu as pltpu
from jax.experimental.pallas import tpu_sc as plsc

F32 = jnp.float32
BF16 = jnp.bfloat16
EPS = 1e-6
POOL_WINDOWS = (2, 4, 8, 16)
POOL_HALO = 16
SSM_T = 16
LANES = 128
TOP_K = 2
SC_CORES, SC_SUBCORES = 2, 16
SC_WORKERS = SC_CORES * SC_SUBCORES
SC_ROWS = 64
COMBINE_PIECES = 2
VMEM_LIMIT = 58 * 1024 * 1024


def _cparams(*sem):
    return pltpu.CompilerParams(dimension_semantics=sem, vmem_limit_bytes=VMEM_LIMIT)


def _normmod(x, g, shift, scale):
    ms = jnp.mean(x * x, axis=-1, keepdims=True)
    y = x * lax.rsqrt(ms + EPS)
    return (y * g) * (1.0 + scale) + shift


def _bdot(a, b):
    return jnp.dot(a, b, preferred_element_type=F32)


def _ada_kernel(c_ref, w_ref, b_ref, o_ref):
    c = c_ref[...]
    cond = c * jax.nn.sigmoid(c)
    o_ref[0] = jnp.dot(cond, w_ref[0], precision=lax.Precision.HIGHEST,
                       preferred_element_type=F32) + b_ref[0]


def _ada(c, ada_w, ada_b):
    depth, d, d6 = ada_w.shape
    bsz = c.shape[0]
    tn = d6 // 4
    return pl.pallas_call(
        _ada_kernel,
        grid=(depth, d6 // tn),
        in_specs=[pl.BlockSpec((bsz, d), lambda l, j: (0, 0)),
                  pl.BlockSpec((1, d, tn), lambda l, j: (l, 0, j)),
                  pl.BlockSpec((1, 1, tn), lambda l, j: (l, 0, j))],
        out_specs=pl.BlockSpec((1, bsz, tn), lambda l, j: (l, 0, j)),
        out_shape=jax.ShapeDtypeStruct((depth, bsz, d6), F32),
        compiler_params=_cparams("arbitrary", "arbitrary"),
        name="ada",
    )(c, ada_w, ada_b.reshape(depth, 1, d6))


def _ssm_operators(log_dt, lam_re, lam_im, b_re, b_im, c_re, c_im, d_skip, t):
    g_, p_, h_ = b_re.shape
    dt = jnp.exp(log_dt.astype(F32))[:, None]
    lr = lam_re.astype(F32)
    li = lam_im.astype(F32)
    mag = jnp.exp(lr * dt)
    a_re = mag * jnp.cos(li * dt)
    a_im = mag * jnp.sin(li * dt)
    den = lr * lr + li * li
    nr = a_re - 1
    coef_re = (nr * lr + a_im * li) / den
    coef_im = (a_im * lr - nr * li) / den
    br = b_re.astype(F32)
    bi = b_im.astype(F32)
    bbar_re = coef_re[..., None] * br - coef_im[..., None] * bi
    bbar_im = coef_re[..., None] * bi + coef_im[..., None] * br
    steps = jnp.arange(0, t + 1, dtype=F32)[None, :, None]
    ang = (li * dt)[:, None, :] * steps
    pmag = jnp.exp((lr * dt)[:, None, :] * steps)
    pw_re = pmag * jnp.cos(ang)
    pw_im = pmag * jnp.sin(ang)
    cr = c_re.astype(F32)
    ci = c_im.astype(F32)
    er = cr[:, None] * pw_re[:, :, None, :] - ci[:, None] * pw_im[:, :, None, :]
    ei = cr[:, None] * pw_im[:, :, None, :] + ci[:, None] * pw_re[:, :, None, :]
    tg = lambda a: jnp.transpose(a, (1, 2, 0))
    pt_re, pt_im = tg(pw_re[:, :t])[:, None], tg(pw_im[:, :t])[:, None]
    ct_re, ct_im = tg(cr)[None], tg(ci)[None]
    er_l = (ct_re * pt_re - ct_im * pt_im)[..., None]
    ei_l = (ct_re * pt_im + ct_im * pt_re)[..., None]
    br_l = jnp.transpose(bbar_re, (1, 0, 2))[None, None]
    bi_l = jnp.transpose(bbar_im, (1, 0, 2))[None, None]
    kk = jnp.sum(er_l * br_l - ei_l * bi_l, axis=2)
    kk = jnp.transpose(kk, (2, 0, 1, 3))
    kz = jnp.pad(kk, ((0, 0), (t - 1, 0), (0, 0), (0, 0)))
    cols = [kz[:, t - 1 - j:2 * t - 1 - j] for j in range(t)]
    mt = jnp.stack(cols, axis=3).reshape(g_, t * h_, t * h_)
    rev_re = pw_re[:, :t][:, ::-1, None, :]
    rev_im = pw_im[:, :t][:, ::-1, None, :]
    bt_re = jnp.transpose(bbar_re, (0, 2, 1))[:, None]
    bt_im = jnp.transpose(bbar_im, (0, 2, 1))[:, None]
    w_re = (rev_re * bt_re - rev_im * bt_im).reshape(g_, t * h_, p_)
    w_im = (rev_re * bt_im + rev_im * bt_re).reshape(g_, t * h_, p_)
    w4 = jnp.concatenate([w_re, w_im, w_im, w_re], axis=-1)
    vt = jnp.concatenate([er[:, 1:].reshape(g_, t * h_, p_),
                          -ei[:, 1:].reshape(g_, t * h_, p_)], axis=-1)
    at_re, at_im = pw_re[:, t], pw_im[:, t]
    c1 = jnp.concatenate([at_re, at_re], axis=-1)
    c2 = jnp.concatenate([-at_im, at_im], axis=-1)
    coef = jnp.concatenate([c1[:, None], c2[:, None], jnp.zeros((g_, 6, 2 * p_), F32)], axis=1)
    dk = jnp.tile(d_skip.astype(F32)[:, None, :], (1, t, 1)).reshape(g_, t * h_, 1)
    return mt.astype(BF16), w4.astype(BF16), vt.astype(BF16), coef, dk


def _ssm_in_kernel(x_ref, mod_ref, g_ref, w_ref, ut_ref, h_scr, u_scr, *, t):
    q = pl.program_id(1)
    bsz, cs, _ = x_ref.shape
    nlt = u_scr.shape[0]
    for b in range(bsz):
        m = mod_ref[b]
        h = _normmod(x_ref[b], g_ref[...], m[0:1], m[1:2]).astype(BF16)
        for c in range(cs // t):
            h_scr[(c * bsz + b) * t:(c * bsz + b + 1) * t, :] = h[c * t:(c + 1) * t, :]
    u = _bdot(h_scr[...], w_ref[0])
    rows = pl.ds(pl.multiple_of(q * (bsz * cs), bsz * cs), bsz * cs)
    for l in range(nlt):
        u_scr[l, rows, :] = u[:, l * LANES:(l + 1) * LANES]

    @pl.when(q == pl.num_programs(1) - 1)
    def _():
        ng, th, nch = ut_ref.shape
        hh = th // t
        gl = LANES // hh
        for l in range(nlt):
            uq = pltpu.einshape("ckl->kcl", u_scr[l].reshape(nch, t, LANES))
            for k in range(t):
                ut_ref[l * gl:(l + 1) * gl, k * hh:(k + 1) * hh, :] = (
                    uq[k].T.reshape(gl, hh, nch).astype(BF16))


def _ssm_in(x, mod, g, w, t, hh, nq, layer):
    bsz, seq, d = x.shape
    ng = d // hh
    cs = LANES * t // bsz // nq
    return pl.pallas_call(
        functools.partial(_ssm_in_kernel, t=t),
        grid=(seq // (cs * nq), nq),
        in_specs=[pl.BlockSpec((bsz, cs, d), lambda o, q: (0, o * nq + q, 0)),
                  pl.BlockSpec((bsz, 6, d), lambda o, q: (0, 0, 0)),
                  pl.BlockSpec((1, d), lambda o, q: (0, 0)),
                  pl.BlockSpec((1, d, d), lambda o, q: (layer, 0, 0))],
        out_specs=pl.BlockSpec((ng, t * hh, LANES), lambda o, q: (0, 0, o)),
        out_shape=jax.ShapeDtypeStruct((ng, t * hh, bsz * seq // t), BF16),
        scratch_shapes=[pltpu.VMEM((bsz * cs, d), BF16),
                        pltpu.VMEM((d // LANES, LANES * t, LANES), F32)],
        compiler_params=_cparams("arbitrary", "arbitrary"),
        name="ssm_in",
    )(x, mod, g, w)


def _ssm_kernel(ut_ref, mt_ref, w4_ref, vt_ref, coef_ref, dk_ref, yt_ref, z_scr, s_scr,
                *, gb, nchunks, bsz):
    p2 = s_scr.shape[-1]
    for gi in range(gb):
        u = ut_ref[gi].astype(F32).T.astype(BF16)
        z = _bdot(u, w4_ref[gi])
        z_scr[gi, 0] = z[:, :p2]
        z_scr[gi, 1] = z[:, p2:]
    c1 = [jnp.broadcast_to(coef_ref[gi, 0:1, :], (bsz, p2)) for gi in range(gb)]
    c2 = [jnp.broadcast_to(coef_ref[gi, 1:2, :], (bsz, p2)) for gi in range(gb)]

    def body(c, carry):
        new = []
        for gi in range(gb):
            s1, s2 = carry[gi]
            rows = pl.ds(pl.multiple_of(c * bsz, bsz), bsz)
            s_scr[gi, rows, :] = s1
            n1 = c1[gi] * s1 + c2[gi] * s2 + z_scr[gi, 0, rows, :]
            n2 = c1[gi] * s2 - c2[gi] * s1 + z_scr[gi, 1, rows, :]
            new.append((n1, n2))
        return tuple(new)

    zero = jnp.zeros((bsz, p2), F32)
    lax.fori_loop(0, nchunks, body, tuple((zero, zero) for _ in range(gb)), unroll=4)
    for gi in range(gb):
        ut = ut_ref[gi]
        st = _bdot(vt_ref[gi], s_scr[gi].T.astype(BF16))
        yt = _bdot(mt_ref[gi], ut) + st + dk_ref[gi] * ut.astype(F32)
        yt_ref[gi] = yt.astype(yt_ref.dtype)


def _ssm(ut, mt, w4, vt, coef, dk, bsz, gb):
    g_, th, nch = ut.shape
    p2 = vt.shape[2]
    kern = functools.partial(_ssm_kernel, gb=gb, nchunks=nch // bsz, bsz=bsz)
    spec3 = lambda a, b: pl.BlockSpec((gb, a, b), lambda i: (i, 0, 0))
    return pl.pallas_call(
        kern,
        grid=(g_ // gb,),
        in_specs=[spec3(th, nch), spec3(th, th), spec3(th, 2 * p2), spec3(th, p2),
                  spec3(8, p2), spec3(th, 1)],
        out_specs=spec3(th, nch),
        out_shape=jax.ShapeDtypeStruct((g_, th, nch), BF16),
        scratch_shapes=[pltpu.VMEM((gb, 2, nch, p2), F32), pltpu.VMEM((gb, nch, p2), F32)],
        compiler_params=_cparams("arbitrary"),
        name="ssm",
    )(ut, mt, w4, vt, coef, dk)


def _gelu_tanh(y):
    return y * (0.5 * (1.0 + jnp.tanh(math.sqrt(2.0 / math.pi) * (y + 0.044715 * (y * y * y)))))


def _glu_out_kernel(yt_ref, x_ref, mod_ref, wg_ref, wo_ref, o_ref, y_scr, *, t):
    q = pl.program_id(1)
    bsz, cs, _ = x_ref.shape
    nlt = y_scr.shape[0]

    @pl.when(q == 0)
    def _():
        ng, th, nch = yt_ref.shape
        hh = th // t
        gl = LANES // hh
        for l in range(nlt):
            parts = [yt_ref[l * gl:(l + 1) * gl, k * hh:(k + 1) * hh, :].astype(F32)
                     .reshape(LANES, nch).T for k in range(t)]
            yq = pltpu.einshape("kcl->ckl", jnp.stack(parts, axis=0))
            y_scr[l] = yq.reshape(nch * t, LANES)

    rows = pl.ds(pl.multiple_of(q * (bsz * cs), bsz * cs), bsz * cs)
    y = jnp.concatenate([y_scr[l, rows, :] for l in range(nlt)], axis=1)
    z = _gelu_tanh(y)
    z = z * jax.nn.sigmoid(_bdot(z.astype(BF16), wg_ref[0]))
    o = _bdot(z.astype(BF16), wo_ref[0])
    for b in range(bsz):
        gate = mod_ref[b][2:3]
        for c in range(cs // t):
            tok = slice(c * t, (c + 1) * t)
            o_ref[b, tok, :] = x_ref[b, tok, :] + gate * o[(c * bsz + b) * t:(c * bsz + b + 1) * t]


def _glu_out(yt, x, mod, wg, wo, t, nq, layer):
    bsz, seq, d = x.shape
    ng, th, _ = yt.shape
    cs = LANES * t // bsz // nq
    return pl.pallas_call(
        functools.partial(_glu_out_kernel, t=t),
        grid=(seq // (cs * nq), nq),
        in_specs=[pl.BlockSpec((ng, th, LANES), lambda o, q: (0, 0, o)),
                  pl.BlockSpec((bsz, cs, d), lambda o, q: (0, o * nq + q, 0)),
                  pl.BlockSpec((bsz, 6, d), lambda o, q: (0, 0, 0)),
                  pl.BlockSpec((1, d, d), lambda o, q: (layer, 0, 0)),
                  pl.BlockSpec((1, d, d), lambda o, q: (layer, 0, 0))],
        out_specs=pl.BlockSpec((bsz, cs, d), lambda o, q: (0, o * nq + q, 0)),
        out_shape=jax.ShapeDtypeStruct((bsz, seq, d), F32),
        scratch_shapes=[pltpu.VMEM((d // LANES, LANES * t, LANES), F32)],
        compiler_params=_cparams("arbitrary", "arbitrary"),
        name="glu_out",
    )(yt, x, mod, wg, wo)


def _chunks(total, step):
    return [(s, min(step, total - s)) for s in range(0, total, step)]


def _swiglu(h, w1_ref, w3_ref, w2_ref, sub):
    total = None
    for s, n in _chunks(w1_ref.shape[-1], sub):
        a = _bdot(h, w1_ref[:, s:s + n])
        b = _bdot(h, w3_ref[:, s:s + n])
        act = (a * jax.nn.sigmoid(a) * b).astype(BF16)
        y = _bdot(act, w2_ref[s:s + n, :])
        total = y if total is None else total + y
    return total


def _ffn_kernel(x_ref, mod_ref, g_ref, w1_ref, w3_ref, w2_ref, o_ref, *, sub):
    m = mod_ref[0]
    x = x_ref[...]
    h = _normmod(x, g_ref[...], m[3:4], m[4:5]).astype(BF16)
    o_ref[...] = x + m[5:6] * _swiglu(h, w1_ref.at[0], w3_ref.at[0], w2_ref.at[0], sub)


def _ffn(x, mod, g, w1, w3, w2, seq, tm, layer):
    n, d = x.shape
    dff = w1.shape[2]
    tps = seq // tm
    const = lambda i: (0, 0)
    lay = lambda i: (layer, 0, 0)
    return pl.pallas_call(
        functools.partial(_ffn_kernel, sub=256),
        grid=(n // tm,),
        in_specs=[pl.BlockSpec((tm, d), lambda i: (i, 0)),
                  pl.BlockSpec((1, 6, d), lambda i: (i // tps, 0, 0)),
                  pl.BlockSpec((1, d), const),
                  pl.BlockSpec((1, d, dff), lay, pipeline_mode=pl.Buffered(1)),
                  pl.BlockSpec((1, d, dff), lay, pipeline_mode=pl.Buffered(1)),
                  pl.BlockSpec((1, dff, d), lay, pipeline_mode=pl.Buffered(1))],
        out_specs=pl.BlockSpec((tm, d), lambda i: (i, 0)),
        out_shape=jax.ShapeDtypeStruct((n, d), F32),
        compiler_params=_cparams("arbitrary"),
        name="ffn",
    )(x, mod, g, w1, w3, w2)


def _pool_kernel(x_ref, halo_ref, mod_ref, g_ref, win_ref, wmix_ref, scale_ref, wo_ref,
                 g2_ref, rw_ref, rb_ref, o_ref, h_ref, idx_ref, gate_ref, rank_ref, cnt_ref,
                 ext_ref, z_ref, base_scr, before_scr, *, tps):
    tm, d = x_ref.shape
    pg = d // len(POOL_WINDOWS)
    m = mod_ref[0]
    it = pl.program_id(0) % tps

    @pl.when(pl.program_id(0) == 0)
    def _():
        base_scr[...] = jnp.zeros_like(base_scr)
        shape = before_scr.shape
        before = lax.broadcasted_iota(jnp.int32, shape, 1) < lax.broadcasted_iota(jnp.int32, shape, 0)
        before_scr[...] = jnp.where(before, 1.0, 0.0).astype(BF16)

    def project(rows):
        h = _normmod(rows, g_ref[...], m[0:1], m[1:2])
        return _bdot(h.astype(BF16), win_ref[0])

    ext_ref[0:POOL_HALO, :] = jnp.where(it == 0, 0.0, project(halo_ref[...]))
    ext_ref[POOL_HALO:, :] = project(x_ref[...])
    tpos = (it * tm + 1 + lax.broadcasted_iota(jnp.int32, (tm, 1), 0)).astype(F32)
    for gi, w in enumerate(POOL_WINDOWS):
        cols = slice(gi * pg, (gi + 1) * pg)
        s = ext_ref[:, cols]
        span = 1
        while span < w:
            s = s + pltpu.roll(s, span, 0)
            span *= 2
        mean = s[POOL_HALO:, :] * (1.0 / jnp.minimum(tpos, float(w)))
        dlt = (mean - ext_ref[POOL_HALO:, cols]).astype(BF16)
        z_ref[:, cols] = (_bdot(dlt, wmix_ref[0, gi]) * scale_ref[:, cols]).astype(BF16)
    xn = x_ref[...] + m[2:3] * _bdot(z_ref[...], wo_ref[0])
    o_ref[...] = xn
    rt = before_scr.shape[0]
    for r in range(tm // rt):
        rows = slice(r * rt, (r + 1) * rt)
        hp, idx, gate, rank, total = _route_rows(
            xn[rows], g2_ref[...], m[3:4], m[4:5], rw_ref[...], rb_ref[...],
            before_scr[...], base_scr[...])
        h_ref[rows, :] = hp
        idx_ref[rows, :] = idx
        gate_ref[rows, :] = gate
        rank_ref[rows, :] = rank
        base_scr[...] = total
    cnt_ref[...] = base_scr[...].astype(jnp.int32)


def _pool(x, mod, g, win, wmix, scale, wo, g2, rw, rb, seq, tm, rt, layer):
    n, d = x.shape
    tps = seq // tm
    hb = tm // POOL_HALO
    _, ng, pg, _ = wmix.shape
    ne = rw.shape[1]
    const = lambda i: (0, 0)
    lay = lambda i: (layer, 0, 0)
    row = lambda w: pl.BlockSpec((tm, w), lambda i: (i, 0))
    return pl.pallas_call(
        functools.partial(_pool_kernel, tps=tps),
        grid=(n // tm,),
        in_specs=[row(d),
                  pl.BlockSpec((POOL_HALO, d), lambda i: (jnp.maximum(i * hb - 1, 0), 0)),
                  pl.BlockSpec((1, 6, d), lambda i: (i // tps, 0, 0)),
                  pl.BlockSpec((1, d), const),
                  pl.BlockSpec((1, d, d), lay),
                  pl.BlockSpec((1, ng, pg, pg), lambda i: (layer, 0, 0, 0)),
                  pl.BlockSpec((1, d), const),
                  pl.BlockSpec((1, d, d), lay),
                  pl.BlockSpec((1, d), const),
                  pl.BlockSpec((d, ne), const),
                  pl.BlockSpec((1, ne), const)],
        out_specs=[row(d), row(d // 2), row(TOP_K), row(TOP_K), row(TOP_K),
                   pl.BlockSpec((1, ne), const)],
        out_shape=[jax.ShapeDtypeStruct((n, d), F32),
                   jax.ShapeDtypeStruct((n, d // 2), jnp.int32),
                   jax.ShapeDtypeStruct((n, TOP_K), jnp.int32),
                   jax.ShapeDtypeStruct((n, TOP_K), F32),
                   jax.ShapeDtypeStruct((n, TOP_K), jnp.int32),
                   jax.ShapeDtypeStruct((1, ne), jnp.int32)],
        scratch_shapes=[pltpu.VMEM((tm + POOL_HALO, d), F32), pltpu.VMEM((tm, d), BF16),
                        pltpu.VMEM((1, ne), F32), pltpu.VMEM((rt, rt), BF16)],
        compiler_params=_cparams("arbitrary"),
        name="pool_router",
    )(x, x, mod, g, win, wmix, scale, wo, g2, rw, rb)


def _pack_bf16_pairs(v):
    w = v.shape[1] // 2
    bits = lambda a: lax.bitcast_convert_type(a.astype(BF16).astype(F32), jnp.int32)
    return bits(v[:, :w]) | lax.shift_right_logical(bits(v[:, w:]), 16)


def _unpack_bf16_pairs(p):
    hi = lax.bitcast_convert_type(p & jnp.int32(-65536), F32)
    lo = lax.bitcast_convert_type(lax.shift_left(p, 16), F32)
    return jnp.concatenate([hi, lo], axis=1)


def _route_rows(x, g, shift, scale, rw, rb, before, base):
    h = _normmod(x, g, shift, scale)
    hp = _pack_bf16_pairs(h)
    split = lambda a: (a.astype(BF16), (a - a.astype(BF16).astype(F32)).astype(BF16))
    h_hi, h_lo = split(h)
    w_hi, w_lo = split(rw)
    logits = (_bdot(h_hi, w_hi) + _bdot(h_hi, w_lo) + _bdot(h_lo, w_hi)) + rb
    ne = float(logits.shape[1])
    lane = lax.broadcasted_iota(jnp.int32, logits.shape, 1).astype(F32)
    v1 = jnp.max(logits, axis=1, keepdims=True)
    i1 = jnp.min(jnp.where(logits == v1, lane, ne), axis=1, keepdims=True)
    rest = jnp.where(lane == i1, -jnp.inf, logits)
    v2 = jnp.max(rest, axis=1, keepdims=True)
    i2 = jnp.min(jnp.where(rest == v2, lane, ne), axis=1, keepdims=True)
    e2 = jnp.exp(v2 - v1)
    den = 1.0 + e2
    two = lax.broadcasted_iota(jnp.int32, (logits.shape[0], TOP_K), 1)
    idx = jnp.where(two == 0, i1, i2).astype(jnp.int32)
    gate = jnp.where(two == 0, 1.0 / den, e2 / den)
    oh1 = jnp.where(lane == i1, 1.0, 0.0)
    oh2 = jnp.where(lane == i2, 1.0, 0.0)
    both = oh1 + oh2
    prior = _bdot(before, both.astype(BF16)) + base
    r1 = jnp.sum(oh1 * prior, axis=1, keepdims=True)
    r2 = jnp.sum(oh2 * prior, axis=1, keepdims=True)
    rank = jnp.where(two == 0, r1, r2).astype(jnp.int32)
    return hp, idx, gate, rank, base + jnp.sum(both, axis=0, keepdims=True)


def _count_le(sorted_vals, queries):
    return jnp.sum((sorted_vals[None, :] <= queries[:, None]).astype(jnp.int32), axis=1)


def _routing_tables(idx, rank, cnt, ne, tile):
    n = idx.shape[0]
    padded = ((cnt + tile - 1) // tile) * tile
    off = jnp.cumsum(padded) - padded
    onehot = idx[:, :, None] == jnp.arange(ne, dtype=jnp.int32)[None, None, :]
    pos = rank + jnp.sum(jnp.where(onehot, off[None, None, :], 0), axis=2)
    n_tiles = (TOP_K * n) // tile + ne
    used = jnp.sum(padded) // tile
    tile_ids = jnp.arange(n_tiles, dtype=jnp.int32)
    tile_expert = _count_le(jnp.cumsum(padded), tile_ids * tile)
    last_used = jnp.maximum(used - 1, 0)
    tile_valid = (tile_ids < used).astype(jnp.int32)
    tile_src = jnp.minimum(tile_ids, last_used).astype(jnp.int32)
    tile_expert = jnp.minimum(tile_expert, ne - 1).astype(jnp.int32)
    tile_expert = jnp.where(tile_valid == 1, tile_expert, tile_expert[last_used])
    tile_rows = jnp.clip(cnt[tile_expert] - (tile_ids * tile - off[tile_expert]), 0, tile)
    tile_rows = jnp.where(tile_valid == 1, tile_rows, 0).astype(jnp.int32)
    return dict(pos=pos, tile_expert=tile_expert, tile_valid=tile_valid, tile_src=tile_src,
                tile_rows=tile_rows, n_tiles=n_tiles)


def _sc_mesh():
    return plsc.VectorSubcoreMesh(core_axis_name="c", subcore_axis_name="s",
                                  num_cores=SC_CORES, num_subcores=SC_SUBCORES)


def _dispatch(hp, pos0, pos1, rows_out):
    n, w = hp.shape
    per_w = n // SC_WORKERS
    ch = SC_ROWS
    nch = per_w // ch
    assert per_w * SC_WORKERS == n and nch * ch == per_w and nch % 2 == 0

    @functools.partial(
        pl.kernel, mesh=_sc_mesh(),
        out_type=jax.ShapeDtypeStruct((rows_out, w), hp.dtype),
        scratch_types=[pltpu.VMEM((per_w,), jnp.int32), pltpu.VMEM((per_w,), jnp.int32),
                       pltpu.VMEM((ch, w), hp.dtype), pltpu.VMEM((ch, w), hp.dtype)]
        + [pltpu.SemaphoreType.DMA] * 4,
        name="dispatch",
    )
    def k(hp_hbm, p0_hbm, p1_hbm, out_hbm, i0_v, i1_v, r0_v, r1_v, l0, l1, s0, s1):
        base = (lax.axis_index("s") * SC_CORES + lax.axis_index("c")) * per_w
        pltpu.sync_copy(p0_hbm.at[pl.ds(base, per_w)], i0_v)
        pltpu.sync_copy(p1_hbm.at[pl.ds(base, per_w)], i1_v)
        bufs, lsem, ssem = (r0_v, r1_v), (l0, l1), (s0, s1)

        def load(j, b):
            return pltpu.make_async_copy(hp_hbm.at[pl.ds(base + j * ch, ch)], bufs[b], lsem[b])

        def scatter(j, b, idx):
            return pltpu.make_async_copy(bufs[b], out_hbm.at[idx.at[pl.ds(j * ch, ch)]], ssem[b])

        load(0, 0).start()

        @pl.loop(0, nch, step=2)
        def _(j):
            for b in range(2):
                jj = j + b
                load(jj, b).wait()

                @pl.when(jj + 1 < nch)
                def _():
                    load(jj + 1, 1 - b).start()

                c0 = scatter(jj, b, i0_v)
                c1 = scatter(jj, b, i1_v)
                c0.start()
                c1.start()
                c0.wait()
                c1.wait()

    return k(hp, pos0, pos1)


def _gather_pairs(ys, pos0, pos1):
    n = pos0.shape[0]
    w = ys.shape[1]
    per_w = n // SC_WORKERS
    ch = SC_ROWS
    nch = per_w // ch
    assert per_w * SC_WORKERS == n and nch * ch == per_w
    out = jax.ShapeDtypeStruct((n, w), ys.dtype)

    @functools.partial(
        pl.kernel, mesh=_sc_mesh(), out_type=(out, out),
        scratch_types=[pltpu.VMEM((per_w,), jnp.int32), pltpu.VMEM((per_w,), jnp.int32),
                       pltpu.VMEM((ch, w), ys.dtype), pltpu.VMEM((ch, w), ys.dtype)]
        + [pltpu.SemaphoreType.DMA] * 4,
        name="gather_pairs",
    )
    def k(ys_hbm, p0_hbm, p1_hbm, a_hbm, b_hbm, i0_v, i1_v, ra_v, rb_v, ga, gb, wa, wb):
        base = (lax.axis_index("s") * SC_CORES + lax.axis_index("c")) * per_w
        pltpu.sync_copy(p0_hbm.at[pl.ds(base, per_w)], i0_v)
        pltpu.sync_copy(p1_hbm.at[pl.ds(base, per_w)], i1_v)

        def gather(j, idx, buf, sem):
            return pltpu.make_async_copy(ys_hbm.at[idx.at[pl.ds(j * ch, ch)]], buf, sem)

        def write(j, buf, dst, sem):
            return pltpu.make_async_copy(buf, dst.at[pl.ds(base + j * ch, ch)], sem)

        gather(0, i0_v, ra_v, ga).start()
        gather(0, i1_v, rb_v, gb).start()

        @pl.loop(0, nch)
        def _(j):
            gather(j, i0_v, ra_v, ga).wait()
            write(j, ra_v, a_hbm, wa).start()
            gather(j, i1_v, rb_v, gb).wait()
            write(j, rb_v, b_hbm, wb).start()
            write(j, ra_v, a_hbm, wa).wait()

            @pl.when(j + 1 < nch)
            def _():
                gather(j + 1, i0_v, ra_v, ga).start()

            write(j, rb_v, b_hbm, wb).wait()

            @pl.when(j + 1 < nch)
            def _():
                gather(j + 1, i1_v, rb_v, gb).start()

    return k(ys, pos0, pos1)


def _expert_kernel(te_ref, tsrc_ref, tr_ref, xs_ref, w1_ref, w3_ref, w2_ref, ys_ref, *, sub):
    j = pl.program_id(0)
    half = xs_ref.shape[0] // 2

    def run(rows):
        x = _unpack_bf16_pairs(xs_ref[rows, :])
        row = lax.broadcasted_iota(jnp.int32, x.shape, 0)
        x = jnp.where(row < tr_ref[j], x, 0.0).astype(BF16)
        ys_ref[rows, :] = _pack_bf16_pairs(
            _swiglu(x, w1_ref.at[0], w3_ref.at[0], w2_ref.at[0], sub))

    @pl.when(tr_ref[j] > half)
    def _():
        run(slice(None))

    @pl.when((tr_ref[j] > 0) & (tr_ref[j] <= half))
    def _():
        run(slice(0, half))
        ys_ref[half:, :] = jnp.zeros((half, ys_ref.shape[1]), ys_ref.dtype)

    @pl.when(tr_ref[j] == 0)
    def _():
        ys_ref[...] = jnp.zeros_like(ys_ref)


def _experts(xs, w1, w3, w2, rt, tile):
    rows, wp = xs.shape
    ne, d, dff = w1.shape
    return pl.pallas_call(
        functools.partial(_expert_kernel, sub=512),
        grid_spec=pltpu.PrefetchScalarGridSpec(
            num_scalar_prefetch=3,
            grid=(rt['n_tiles'],),
            in_specs=[pl.BlockSpec((tile, wp), lambda j, te, tsrc, tr: (tsrc[j], 0)),
                      pl.BlockSpec((1, d, dff), lambda j, te, tsrc, tr: (te[j], 0, 0)),
                      pl.BlockSpec((1, d, dff), lambda j, te, tsrc, tr: (te[j], 0, 0)),
                      pl.BlockSpec((1, dff, d), lambda j, te, tsrc, tr: (te[j], 0, 0))],
            out_specs=pl.BlockSpec((tile, wp), lambda j, te, tsrc, tr: (j, 0))),
        out_shape=jax.ShapeDtypeStruct((rows, wp), jnp.int32),
        compiler_params=_cparams("arbitrary"),
        name="experts",
    )(rt['tile_expert'], rt['tile_src'], rt['tile_rows'], xs, w1, w3, w2)


def _combine_kernel(a_ref, b_ref, gate_ref, x_ref, mod_ref, fg_ref, o_ref, *, final_norm):
    m = mod_ref[0]
    g = gate_ref[...]
    y = g[:, 0:1] * _unpack_bf16_pairs(a_ref[...]) + g[:, 1:2] * _unpack_bf16_pairs(b_ref[...])
    xn = x_ref[...] + m[5:6] * y
    if final_norm:
        ms = jnp.mean(xn * xn, axis=-1, keepdims=True)
        xn = (xn * lax.rsqrt(ms + EPS)) * fg_ref[...]
    o_ref[...] = xn


def _combine(a, b, gates, x, mod, fg, seq, tm, final_norm, tile0):
    n, d = x.shape
    tps = seq // tm
    part = lambda w: pl.BlockSpec((tm, w), lambda i: (i, 0))
    full = lambda w: pl.BlockSpec((tm, w), lambda i: (i + tile0, 0))
    return pl.pallas_call(
        functools.partial(_combine_kernel, final_norm=final_norm),
        grid=(a.shape[0] // tm,),
        in_specs=[part(d // 2), part(d // 2), full(TOP_K), full(d),
                  pl.BlockSpec((1, 6, d), lambda i: ((i + tile0) // tps, 0, 0)),
                  pl.BlockSpec((1, d), lambda i: (0, 0))],
        out_specs=full(d),
        out_shape=jax.ShapeDtypeStruct((n, d), F32),
        input_output_aliases={3: 0},
        compiler_params=_cparams("arbitrary"),
        name="combine",
    )(a, b, gates, x, mod, fg)


def _pick(n, prefs):
    for p in prefs:
        if n % p == 0:
            return p
    return n


def kernel(x, c, ada_w, ada_b, norm_g, ssm_in, ssm_log_dt, ssm_lam_re, ssm_lam_im,
           ssm_b_re, ssm_b_im, ssm_c_re, ssm_c_im, ssm_d, ssm_glu, ssm_out,
           pool_in, pool_mix, pool_scale, pool_out, ffn_w1, ffn_w3, ffn_w2,
           router_w, router_b, moe_w1, moe_w3, moe_w2, final_g):
    bsz, seq, d = x.shape
    depth = ada_w.shape[0]
    n = bsz * seq
    ne = router_w.shape[-1]
    g_, p_, h_ = ssm_b_re.shape[1:]
    t = SSM_T
    if seq % (LANES * t // bsz) != 0 or bsz != 8:
        raise NotImplementedError("state-space kernels assume batch 8 and seq % 256 == 0")
    if depth % 2 != 0:
        raise NotImplementedError("the final RMSNorm is fused into an expert layer's combine")
    tm = _pick(seq, (512, 256, 128, 64, 32, 16))
    tile = _pick(TOP_K * n, (512, 256, 128, 64, 32, 16))
    gb = _pick(g_, (4, 2, 1))

    stack = lambda w: w.astype(BF16).reshape((-1,) + w.shape[2:])
    ew1, ew3, ew2 = stack(moe_w1), stack(moe_w3), stack(moe_w2)
    fw1, fw3, fw2 = ffn_w1.astype(BF16), ffn_w3.astype(BF16), ffn_w2.astype(BF16)
    sw_in, sw_glu, sw_out = ssm_in.astype(BF16), ssm_glu.astype(BF16), ssm_out.astype(BF16)
    pw_in, pw_mix, pw_out = pool_in.astype(BF16), pool_mix.astype(BF16), pool_out.astype(BF16)
    mod = _ada(c, ada_w, ada_b).reshape(depth, bsz, 6, d)
    xf = x.reshape(n, d)
    fg = final_g.reshape(1, d)
    for i in range(depth):
        j = i // 2
        mod_i = mod[i]
        g_a = norm_g[i, 0].reshape(1, d)
        g_b = norm_g[i, 1].reshape(1, d)
        if i % 2 == 0:
            ops = _ssm_operators(ssm_log_dt[j], ssm_lam_re[j], ssm_lam_im[j], ssm_b_re[j],
                                 ssm_b_im[j], ssm_c_re[j], ssm_c_im[j], ssm_d[j], t)
            ut = _ssm_in(xf.reshape(bsz, seq, d), mod_i, g_a, sw_in, t, h_, 4, layer=j)
            yt = _ssm(ut, *ops, bsz, gb)
            xf = _glu_out(yt, xf.reshape(bsz, seq, d), mod_i, sw_glu, sw_out, t, 4,
                          layer=j).reshape(n, d)
            xf = _ffn(xf, mod_i, g_b, fw1, fw3, fw2, seq, tm, layer=j)
        else:
            xf, h, idx, gates, rank, cnt = _pool(
                xf, mod_i, g_a, pw_in, pw_mix, pool_scale[j].reshape(1, d), pw_out,
                g_b, router_w[j], router_b[j].reshape(1, ne), seq, 2 * tm, tm, layer=j)
            rt = _routing_tables(idx, rank, cnt[0], ne, tile)
            pos0, pos1 = rt['pos'][:, 0], rt['pos'][:, 1]
            xs = _dispatch(h, pos0, pos1, rt['n_tiles'] * tile)
            rt['tile_expert'] = rt['tile_expert'] + j * ne
            ys = _experts(xs, ew1, ew3, ew2, rt, tile)
            npc = n // COMBINE_PIECES
            for q in range(COMBINE_PIECES):
                piece = slice(q * npc, (q + 1) * npc)
                ya, yb = _gather_pairs(ys, pos0[piece], pos1[piece])
                xf = _combine(ya, yb, gates, xf, mod_i, fg, seq, 2 * tm,
                              final_norm=(i == depth - 1), tile0=q * (npc // (2 * tm)))
    return xf.reshape(bsz, seq, d)
```
